```python
import math
import jax, jax.numpy as jnp
from jax import lax
import numpy as np

D_MODEL = 1024
BATCH = 4
SEQ = 4096
DEPTH = 2

GRID_W = 64
CTX_LEN = 256
N_EVEN = (DEPTH + 1) // 2
N_ODD = DEPTH // 2

LRU_WIDTH = D_MODEL // 2
LRU_BLOCKS = 8
LRU_BLOCK = LRU_WIDTH // LRU_BLOCKS
CONV_W = 4
LRU_C = 8.0

MLA_HEADS = 8
MLA_NOPE = 64
MLA_ROPE = 32
MLA_QK = MLA_NOPE + MLA_ROPE
MLA_V = 64
Q_LORA = D_MODEL // 4
KV_LORA = D_MODEL // 8

EVEN_SPLITS = (LRU_WIDTH, 2 * LRU_WIDTH, 2 * LRU_WIDTH + Q_LORA, 2 * LRU_WIDTH + Q_LORA + KV_LORA)
EVEN_IN = 2 * LRU_WIDTH + Q_LORA + KV_LORA + MLA_ROPE
EVEN_MIX = LRU_WIDTH + MLA_HEADS * MLA_V

GQA_HEADS = 16
GQA_KV_HEADS = 4
GQA_DIM = 64
WINDOW = 128
ODD_IN = (GQA_HEADS + 2 * GQA_KV_HEADS) * GQA_DIM
ODD_MIX = GQA_HEADS * GQA_DIM

Q_BLOCK = 128
ROPE_THETA = 10000.0
NEG_INF = -1e30
EPS = 1e-6

N_EXPERTS = 32
TOP_K = 4
D_FF = D_MODEL
SWIGLU_LIMIT = 7.0
SWIGLU_ALPHA = 1.702
MOE_BLOCK = 128

kernel_name = 'hybrid_rglru_mla_swa_moe_diffusion_block'


def rms_norm(x, g):
    xf = x.astype(jnp.float32)
    y = xf * lax.rsqrt(jnp.mean(xf * xf, axis=-1, keepdims=True) + EPS)
    return (y * g.astype(jnp.float32)).astype(x.dtype)


def axial_rope(n_rows, rot_dim):
    row = jnp.repeat(jnp.arange(n_rows, dtype=jnp.float32), GRID_W)
    col = jnp.tile(jnp.arange(GRID_W, dtype=jnp.float32), n_rows)
    n = rot_dim // 4
    freqs = ROPE_THETA ** (-jnp.arange(n, dtype=jnp.float32) / n)
    ang = jnp.concatenate([row[:, None] * freqs, col[:, None] * freqs], axis=-1)
    return jnp.cos(ang), jnp.sin(ang)


def apply_rope(x, cos, sin):
    xf = x.astype(jnp.float32)
    x1, x2 = jnp.split(xf, 2, axis=-1)
    cs, sn = cos[None, :, None, :], sin[None, :, None, :]
    return jnp.concatenate([x1 * cs - x2 * sn, x2 * cs + x1 * sn], axis=-1).astype(x.dtype)


def ctx_attention(q, k, v, scale, sink=None):
    B, L, H, d = q.shape
    KH = k.shape[2]
    G = H // KH
    qg = q.reshape(B, L, KH, G, d)
    s = jnp.einsum('bqhgd,bkhd->bhgqk', qg, k, preferred_element_type=jnp.float32) * scale
    if sink is not None:
        sink_logit = jnp.broadcast_to(sink.astype(jnp.float32).reshape(1, KH, G, 1, 1), (B, KH, G, L, 1))
        s = jnp.concatenate([s, sink_logit], axis=-1)
    p = jax.nn.softmax(s, axis=-1)[..., :L].astype(v.dtype)
    o = jnp.einsum('bhgqk,bkhd->bqhgd', p, v)
    return o.reshape(B, L, H, -1)


def dense_latent_attention(q, k, v, scale):
    B, S, H, dq = q.shape
    nb = S // Q_BLOCK
    q_blocks = q.reshape(B, nb, Q_BLOCK, H, dq).swapaxes(0, 1)

    def one_block(q_blk):
        s = jnp.einsum('bqhd,bkhd->bhqk', q_blk, k, preferred_element_type=jnp.float32) * scale
        p = jax.nn.softmax(s, axis=-1).astype(v.dtype)
        return jnp.einsum('bhqk,bkhd->bqhd', p, v)

    o = lax.map(one_block, q_blocks)
    return o.swapaxes(0, 1).reshape(B, S, H, -1)


def window_attention(q, k, v, k_ctx, v_ctx, sink, scale):
    B, S, H, d = q.shape
    KH = k.shape[2]
    G = H // KH
    L = k_ctx.shape[1]
    W3 = 3 * Q_BLOCK
    nb = S // Q_BLOCK
    pad = ((0, 0), (Q_BLOCK, Q_BLOCK), (0, 0), (0, 0))
    k_pad = jnp.pad(k, pad)
    v_pad = jnp.pad(v, pad)
    q_blocks = q.reshape(B, nb, Q_BLOCK, KH, G, d).swapaxes(0, 1)
    sink_logit = jnp.broadcast_to(sink.astype(jnp.float32).reshape(1, KH, G, 1, 1), (B, KH, G, Q_BLOCK, 1))

    def one_block(args):
        i, q_blk = args
        k_win = lax.dynamic_slice_in_dim(k_pad, i * Q_BLOCK, W3, axis=1)
        v_win = lax.dynamic_slice_in_dim(v_pad, i * Q_BLOCK, W3, axis=1)
        q_pos = i * Q_BLOCK + jnp.arange(Q_BLOCK)
        k_pos = (i - 1) * Q_BLOCK + jnp.arange(W3)
        valid = (jnp.abs(q_pos[:, None] - k_pos[None, :]) <= WINDOW) & (k_pos >= 0) & (k_pos < S)
        s_win = jnp.einsum('bqhgd,bkhd->bhgqk', q_blk, k_win, preferred_element_type=jnp.float32) * scale
        s_win = jnp.where(valid, s_win, NEG_INF)
        s_ctx = jnp.einsum('bqhgd,bkhd->bhgqk', q_blk, k_ctx, preferred_element_type=jnp.float32) * scale
        p = jax.nn.softmax(jnp.concatenate([s_win, s_ctx, sink_logit], axis=-1), axis=-1)
        p_win = p[..., :W3].astype(v.dtype)
        p_ctx = p[..., W3:W3 + L].astype(v.dtype)
        return (jnp.einsum('bhgqk,bkhd->bqhgd', p_win, v_win)
                + jnp.einsum('bhgqk,bkhd->bqhgd', p_ctx, v_ctx))

    o = lax.map(one_block, (jnp.arange(nb), q_blocks))
    return o.swapaxes(0, 1).reshape(B, S, H, d)


def _linear_recurrence_combine(e1, e2):
    a1, b1 = e1
    a2, b2 = e2
    return a1 * a2, a2 * b1 + b2


def rglru_scan(u, conv_w, conv_b, w_r, b_r, w_i, b_i, lam, h0):
    B, T, C = u.shape
    xc = lax.conv_general_dilated(
        u, conv_w.astype(u.dtype)[:, None, :], window_strides=(1,), padding=[(CONV_W - 1, 0)],
        dimension_numbers=('NWC', 'WIO', 'NWC'), feature_group_count=C)
    xf = (xc + conv_b.astype(u.dtype)).astype(jnp.float32)
    xb = xf.reshape(B, T, LRU_BLOCKS, LRU_BLOCK)
    r = jax.nn.sigmoid(jnp.einsum('btnc,ncd->btnd', xb, w_r.astype(jnp.float32)).reshape(B, T, C)
                       + b_r.astype(jnp.float32))
    gi = jax.nn.sigmoid(jnp.einsum('btnc,ncd->btnd', xb, w_i.astype(jnp.float32)).reshape(B, T, C)
                        + b_i.astype(jnp.float32))
    log_a = -LRU_C * r * jax.nn.softplus(-lam.astype(jnp.float32))
    a = jnp.exp(log_a)
    b = jnp.sqrt(-jnp.expm1(2.0 * log_a)) * (gi * xf)
    b = b.at[:, 0].add(a[:, 0] * h0)
    _, h = lax.associative_scan(_linear_recurrence_combine, (a, b), axis=1)
    return h, h[:, -1]


def even_mixer(h_lat, h_ctx, w_in, conv_w, conv_b, w_r, b_r, w_i, b_i, lam,
               q_a_norm, w_q_b, kv_a_norm, w_kv_b, nope_norm, rope_norm, w_out, rope_cs, need_ctx):
    B, S, _ = h_lat.shape
    L = h_ctx.shape[1]
    xa_l, ga_l, qa_l, kva_l, kr_l = jnp.split(h_lat @ w_in, EVEN_SPLITS, axis=-1)
    xa_c, ga_c, qa_c, kva_c, kr_c = jnp.split(h_ctx @ w_in, EVEN_SPLITS, axis=-1)

    rec_l, rec_c = [], []
    for d in range(2):
        flip = (lambda t: jnp.flip(t, axis=1)) if d == 1 else (lambda t: t)
        h0 = jnp.zeros((B, LRU_WIDTH), jnp.float32)
        hc, hc_last = rglru_scan(flip(xa_c), conv_w[d], conv_b[d], w_r[d], b_r[d], w_i[d], b_i[d], lam[d], h0)
        hl, _ = rglru_scan(flip(xa_l), conv_w[d], conv_b[d], w_r[d], b_r[d], w_i[d], b_i[d], lam[d], hc_last)
        rec_l.append(flip(hl))
        rec_c.append(flip(hc))
    ya_l = ((rec_l[0] + rec_l[1]) * jax.nn.gelu(ga_l.astype(jnp.float32))).astype(h_lat.dtype)

    def mla_qkv(qa, kva, kr, cs):
        Bq, T, _ = qa.shape
        q = (rms_norm(qa, q_a_norm) @ w_q_b).reshape(Bq, T, MLA_HEADS, MLA_QK)
        kv = (rms_norm(kva, kv_a_norm) @ w_kv_b).reshape(Bq, T, MLA_HEADS, MLA_NOPE + MLA_V)
        q_nope = rms_norm(q[..., :MLA_NOPE], nope_norm[0])
        q_rope = rms_norm(q[..., MLA_NOPE:], rope_norm[0])
        k_nope = rms_norm(kv[..., :MLA_NOPE], nope_norm[1])
        v = kv[..., MLA_NOPE:]
        k_rope = rms_norm(kr[:, :, None, :], rope_norm[1])
        if cs is not None:
            q_rope = apply_rope(q_rope, *cs)
            k_rope = apply_rope(k_rope, *cs)
        k_rope = jnp.broadcast_to(k_rope, (Bq, T, MLA_HEADS, MLA_ROPE))
        return (jnp.concatenate([q_nope, q_rope], axis=-1),
                jnp.concatenate([k_nope, k_rope], axis=-1), v)

    scale = MLA_QK ** -0.5
    q_c, k_c, v_c = mla_qkv(qa_c, kva_c, kr_c, None)
    q_l, k_l, v_l = mla_qkv(qa_l, kva_l, kr_l, rope_cs)
    yb_l = dense_latent_attention(q_l, jnp.concatenate([k_l, k_c], axis=1),
                                  jnp.concatenate([v_l, v_c], axis=1), scale)
    out_l = jnp.concatenate([ya_l, yb_l.reshape(B, S, -1)], axis=-1) @ w_out
    out_c = None
    if need_ctx:
        ya_c = ((rec_c[0] + rec_c[1]) * jax.nn.gelu(ga_c.astype(jnp.float32))).astype(h_ctx.dtype)
        yb_c = ctx_attention(q_c, k_c, v_c, scale)
        out_c = jnp.concatenate([ya_c, yb_c.reshape(B, L, -1)], axis=-1) @ w_out
    return out_l, out_c


def odd_mixer(h_lat, h_ctx, w_qkv, qk_norm, sink, w_out, rope_cs, need_ctx):
    B, S, _ = h_lat.shape
    L = h_ctx.shape[1]
    nq = GQA_HEADS * GQA_DIM
    nk = GQA_KV_HEADS * GQA_DIM

    def qkv(h, cs):
        Bq, T, _ = h.shape
        z = h @ w_qkv
        q = rms_norm(z[..., :nq].reshape(Bq, T, GQA_HEADS, GQA_DIM), qk_norm[0])
        k = rms_norm(z[..., nq:nq + nk].reshape(Bq, T, GQA_KV_HEADS, GQA_DIM), qk_norm[1])
        v = z[..., nq + nk:].reshape(Bq, T, GQA_KV_HEADS, GQA_DIM)
        if cs is not None:
            q = apply_rope(q, *cs)
            k = apply_rope(k, *cs)
        return q, k, v

    scale = GQA_DIM ** -0.5
    q_c, k_c, v_c = qkv(h_ctx, None)
    q_l, k_l, v_l = qkv(h_lat, rope_cs)
    o_l = window_attention(q_l, k_l, v_l, k_c, v_c, sink, scale)
    out_l = o_l.reshape(B, S, -1) @ w_out
    out_c = None
    if need_ctx:
        out_c = ctx_attention(q_c, k_c, v_c, scale, sink).reshape(B, L, -1) @ w_out
    return out_l, out_c


def moe_ffn(t, w_router, b_router, w_gu, b_gu, w_dn, b_dn):
    N, D = t.shape
    logits = jnp.dot(t, w_router, preferred_element_type=jnp.float32) + b_router.astype(jnp.float32)
    top_logit, top_e = lax.top_k(logits, TOP_K)
    gate = jax.nn.softmax(top_logit, axis=-1)
    n_assign = N * TOP_K
    flat_e = top_e.reshape(-1)
    order = jnp.argsort(flat_e)
    e_sorted = flat_e[order]
    tok_sorted = (order // TOP_K).astype(jnp.int32)
    gate_sorted = gate.reshape(-1)[order]
    counts = jnp.bincount(flat_e, length=N_EXPERTS)
    padded = (counts + MOE_BLOCK - 1) // MOE_BLOCK * MOE_BLOCK
    start = jnp.cumsum(counts) - counts
    padded_end = jnp.cumsum(padded)
    padded_start = padded_end - padded
    dest = padded_start[e_sorted] + jnp.arange(n_assign) - start[e_sorted]
    n_blocks = -(-n_assign // MOE_BLOCK) + N_EXPERTS
    rows = n_blocks * MOE_BLOCK
    row_tok = jnp.full((rows,), N, jnp.int32).at[dest].set(tok_sorted)
    row_gate = jnp.zeros((rows,), jnp.float32).at[dest].set(gate_sorted)
    block_e = jnp.minimum(jnp.searchsorted(padded_end, jnp.arange(n_blocks) * MOE_BLOCK, side='right'),
                          N_EXPERTS - 1)
    t_pad = jnp.concatenate([t, jnp.zeros((1, D), t.dtype)], axis=0)
    xb = t_pad[row_tok].reshape(n_blocks, MOE_BLOCK, D)

    def expert_block(args):
        x_blk, e = args
        hgu = jnp.dot(x_blk, w_gu[e], preferred_element_type=jnp.float32) + b_gu[e].astype(jnp.float32)
        hg, hu = jnp.split(hgu, 2, axis=-1)
        hg = jnp.minimum(hg, SWIGLU_LIMIT)
        hu = jnp.clip(hu, -SWIGLU_LIMIT, SWIGLU_LIMIT)
        act = hg * jax.nn.sigmoid(SWIGLU_ALPHA * hg) * (hu + 1.0)
        return jnp.dot(act.astype(t.dtype), w_dn[e], preferred_element_type=jnp.float32) + b_dn[e].astype(jnp.float32)

    yb = lax.map(expert_block, (xb, block_e))
    y = yb.reshape(rows, D) * row_gate[:, None]
    out = jnp.zeros((N + 1, D), jnp.float32).at[row_tok].add(y)[:N]
    return out.astype(t.dtype)


def setup_inputs(seed: int = 0) -> dict:
    key = jax.random.key(seed)
    keys = iter(jax.random.split(key, 48))
    f32 = jnp.float32
    D = D_MODEL

    def nrm(shape, scale):
        return jax.random.normal(next(keys), shape, f32) * scale

    def gain(shape):
        return 1.0 + 0.1 * jax.random.normal(next(keys), shape, f32)

    x = nrm((BATCH, SEQ, D), 1.0)
    c = nrm((BATCH, D), 1.0)
    ctx = nrm((BATCH, CTX_LEN, D), 1.0)
    c_ctx = nrm((D,), 1.0)
    w_mod = nrm((DEPTH, D, 6 * D), 0.5 / math.sqrt(D))
    b_mod = nrm((DEPTH, 6 * D), 0.02)
    norm_mix = gain((DEPTH, D))
    norm_ffn = gain((DEPTH, D))
    w_in_even = nrm((N_EVEN, D, EVEN_IN), D ** -0.5)
    lru_conv_w = nrm((N_EVEN, 2, CONV_W, LRU_WIDTH), 0.5)
    lru_conv_b = nrm((N_EVEN, 2, LRU_WIDTH), 0.02)
    lru_w_r = nrm((N_EVEN, 2, LRU_BLOCKS, LRU_BLOCK, LRU_BLOCK), LRU_BLOCK ** -0.5)
    lru_b_r = nrm((N_EVEN, 2, LRU_WIDTH), 0.02)
    lru_w_i = nrm((N_EVEN, 2, LRU_BLOCKS, LRU_BLOCK, LRU_BLOCK), LRU_BLOCK ** -0.5)
    lru_b_i = nrm((N_EVEN, 2, LRU_WIDTH), 0.02)
    a0 = jax.random.uniform(next(keys), (N_EVEN, 2, LRU_WIDTH), f32, 0.9, 0.999)
    a_base = a0 ** (1.0 / LRU_C)
    lru_lambda = jnp.log(a_base) - jnp.log1p(-a_base)
    mla_q_a_norm = gain((N_EVEN, Q_LORA))
    mla_w_q_b = nrm((N_EVEN, Q_LORA, MLA_HEADS * MLA_QK), Q_LORA ** -0.5)
    mla_kv_a_norm = gain((N_EVEN, KV_LORA))
    mla_w_kv_b = nrm((N_EVEN, KV_LORA, MLA_HEADS * (MLA_NOPE + MLA_V)), KV_LORA ** -0.5)
    mla_nope_norm = gain((N_EVEN, 2, MLA_NOPE))
    mla_rope_norm = gain((N_EVEN, 2, MLA_ROPE))
    w_out_even = nrm((N_EVEN, EVEN_MIX, D), EVEN_MIX ** -0.5)
    w_qkv_odd = nrm((N_ODD, D, ODD_IN), D ** -0.5)
    gqa_qk_norm = gain((N_ODD, 2, GQA_DIM))
    gqa_sink = nrm((N_ODD, GQA_HEADS), 0.5)
    w_out_odd = nrm((N_ODD, ODD_MIX, D), ODD_MIX ** -0.5)
    w_router = nrm((DEPTH, D, N_EXPERTS), D ** -0.5)
    b_router = nrm((DEPTH, N_EXPERTS), 0.01)
    w_gate_up = nrm((DEPTH, N_EXPERTS, D, 2 * D_FF), D ** -0.5)
    b_gate_up = nrm((DEPTH, N_EXPERTS, 2 * D_FF), 0.02)
    w_down = nrm((DEPTH, N_EXPERTS, D_FF, D), D_FF ** -0.5)
    b_down = nrm((DEPTH, N_EXPERTS, D), 0.02)
    return {'x': x, 'c': c, 'ctx': ctx, 'c_ctx': c_ctx,
            'w_mod': w_mod, 'b_mod': b_mod, 'norm_mix': norm_mix, 'norm_ffn': norm_ffn,
            'w_in_even': w_in_even, 'lru_conv_w': lru_conv_w, 'lru_conv_b': lru_conv_b,
            'lru_w_r': lru_w_r, 'lru_b_r': lru_b_r, 'lru_w_i': lru_w_i, 'lru_b_i': lru_b_i,
            'lru_lambda': lru_lambda, 'mla_q_a_norm': mla_q_a_norm, 'mla_w_q_b': mla_w_q_b,
            'mla_kv_a_norm': mla_kv_a_norm, 'mla_w_kv_b': mla_w_kv_b, 'mla_nope_norm': mla_nope_norm,
            'mla_rope_norm': mla_rope_norm, 'w_out_even': w_out_even,
            'w_qkv_odd': w_qkv_odd, 'gqa_qk_norm': gqa_qk_norm, 'gqa_sink': gqa_sink, 'w_out_odd': w_out_odd,
            'w_router': w_router, 'b_router': b_router, 'w_gate_up': w_gate_up, 'b_gate_up': b_gate_up,
            'w_down': w_down, 'b_down': b_down}


def reference(x, c, ctx, c_ctx, w_mod, b_mod, norm_mix, norm_ffn,
              w_in_even, lru_conv_w, lru_conv_b, lru_w_r, lru_b_r, lru_w_i, lru_b_i, lru_lambda,
              mla_q_a_norm, mla_w_q_b, mla_kv_a_norm, mla_w_kv_b, mla_nope_norm, mla_rope_norm, w_out_even,
              w_qkv_odd, gqa_qk_norm, gqa_sink, w_out_odd,
              w_router, b_router, w_gate_up, b_gate_up, w_down, b_down):
    B, S, D = x.shape
    L = ctx.shape[1]
    n_rows = S // GRID_W
    rope_mla = axial_rope(n_rows, MLA_ROPE)
    rope_gqa = axial_rope(n_rows, GQA_DIM)
    x_l, x_c = x, ctx
    for layer in range(DEPTH):
        last = layer == DEPTH - 1
        mod_l = (jax.nn.silu(c) @ w_mod[layer] + b_mod[layer])[:, None, :]
        mod_c = jax.nn.silu(c_ctx) @ w_mod[layer] + b_mod[layer]
        sh1_l, sc1_l, g1_l, sh2_l, sc2_l, g2_l = jnp.split(mod_l, 6, axis=-1)
        sh1_c, sc1_c, g1_c, sh2_c, sc2_c, g2_c = jnp.split(mod_c, 6, axis=-1)
        h_l = rms_norm(x_l, norm_mix[layer]) * (1.0 + sc1_l) + sh1_l
        h_c = rms_norm(x_c, norm_mix[layer]) * (1.0 + sc1_c) + sh1_c
        if layer % 2 == 0:
            e = layer // 2
            m_l, m_c = even_mixer(h_l, h_c, w_in_even[e], lru_conv_w[e], lru_conv_b[e], lru_w_r[e], lru_b_r[e],
                                  lru_w_i[e], lru_b_i[e], lru_lambda[e], mla_q_a_norm[e], mla_w_q_b[e],
                                  mla_kv_a_norm[e], mla_w_kv_b[e], mla_nope_norm[e], mla_rope_norm[e],
                                  w_out_even[e], rope_mla, not last)
        else:
            o = layer // 2
            m_l, m_c = odd_mixer(h_l, h_c, w_qkv_odd[o], gqa_qk_norm[o], gqa_sink[o], w_out_odd[o],
                                 rope_gqa, not last)
        x_l = x_l + g1_l * m_l
        f_l = rms_norm(x_l, norm_ffn[layer]) * (1.0 + sc2_l) + sh2_l
        moe_args = (w_router[layer], b_router[layer], w_gate_up[layer], b_gate_up[layer],
                    w_down[layer], b_down[layer])
        if last:
            y_l = moe_ffn(f_l.reshape(B * S, D), *moe_args).reshape(B, S, D)
        else:
            x_c = x_c + g1_c * m_c
            f_c = rms_norm(x_c, norm_ffn[layer]) * (1.0 + sc2_c) + sh2_c
            y = moe_ffn(jnp.concatenate([f_l.reshape(B * S, D), f_c.reshape(B * L, D)], axis=0), *moe_args)
            y_l = y[:B * S].reshape(B, S, D)
            x_c = x_c + g2_c * y[B * S:].reshape(B, L, D)
        x_l = x_l + g2_l * y_l
    return x_l
```

```python
import functools
import math

import jax
import jax.numpy as jnp
from jax import lax
from jax.experimental import pallas as pl
from jax.experimental.pallas import tpu as pltpu

F32 = jnp.float32
BF16 = jnp.bfloat16
I32 = jnp.int32

GRID_W = 64
LRU_BLOCKS = 8
LRU_C = 8.0
CONV_W = 4
MLA_HEADS = 8
MLA_NOPE = 64
MLA_ROPE = 32
MLA_V = 64
GQA_HEADS = 16
GQA_KV_HEADS = 4
GQA_DIM = 64
WINDOW = 128
ROPE_THETA = 10000.0
NEG_INF = -1e30
EPS = 1e-6
N_EXPERTS = 32
TOP_K = 4
SWIGLU_LIMIT = 7.0
SWIGLU_ALPHA = 1.702

LANES = 128
SUBLANES = 8
TOK_TILE = 256
LRU_CHUNK = 128
WIN_BLOCK = 128
MOE_ROWS = 256
VMEM_LIMIT = 48 * 1024 * 1024


def _cparams(sem, **kw):
    return pltpu.CompilerParams(dimension_semantics=sem, vmem_limit_bytes=VMEM_LIMIT, **kw)


def _dot(a, b):
    return jnp.dot(a, b, preferred_element_type=F32)


def _dot_nt(a, b):
    return lax.dot_general(a, b, (((1,), (1,)), ((), ())), preferred_element_type=F32)


def _split_bf16(x):
    hi = x.astype(BF16)
    lo = (x - hi.astype(F32)).astype(BF16)
    return hi, lo


def _dot3(a, w):
    ah, al = _split_bf16(a)
    wh, wl = _split_bf16(w)
    return _dot(ah, wh) + _dot(al, wh) + _dot(ah, wl)


def _rms(x):
    return x * lax.rsqrt(jnp.mean(x * x, axis=-1, keepdims=True) + EPS)


def _prenorm(x, g, scale, shift):
    return (_rms(x) * g) * (1.0 + scale) + shift


def _rope(x, cos, sin_lo, sin_hi, half):
    w = x.shape[-1]
    return x * cos + pltpu.roll(x, w - half, axis=1) * sin_lo + pltpu.roll(x, half, axis=1) * sin_hi


def _one_minus_exp(t, exp_t):
    series = -t * (1.0 + t * (1 / 2) * (1.0 + t * (1 / 3) * (1.0 + t * (1 / 4) * (1.0 + t * (1 / 5) * (1.0 + t * (1 / 6))))))
    return jnp.where(t > -0.25, series, 1.0 - exp_t)


def _tile_lanes(t, reps):
    return jnp.concatenate([t] * reps, axis=1) if reps > 1 else t


def _mod_kernel(c_ref, w_ref, b_ref, o_ref):
    c = c_ref[...]
    o_ref[...] = _dot3(c * jax.nn.sigmoid(c), w_ref[...]) + b_ref[...]


def _modulation(cvec, w_mod, b_mod):
    depth, d, n = w_mod.shape
    tn = 1536
    return pl.pallas_call(
        _mod_kernel,
        out_shape=jax.ShapeDtypeStruct((depth, SUBLANES, n), F32),
        grid=(depth, n // tn),
        in_specs=[pl.BlockSpec((SUBLANES, d), lambda l, j: (0, 0)),
                  pl.BlockSpec((None, d, tn), lambda l, j: (l, 0, j)),
                  pl.BlockSpec((None, 1, tn), lambda l, j: (l, 0, j))],
        out_specs=pl.BlockSpec((None, SUBLANES, tn), lambda l, j: (l, 0, j)),
        compiler_params=_cparams(("arbitrary", "arbitrary")),
        name="modulation",
    )(cvec, w_mod, b_mod.reshape(depth, 1, n))


def _mod_row(b, t, ctx_row):
    return jnp.where(t == 0, ctx_row, b)


def _even_in_kernel(x_ref, mod_ref, g_ref, w_ref, xa_ref, ga_ref, mla_ref):
    mod = mod_ref[...]
    h = _prenorm(x_ref[...], g_ref[...], mod[1:2], mod[0:1])
    z = _dot(h.astype(BF16), w_ref[...])
    c = xa_ref.shape[-1]
    xa_ref[...] = z[:, :c]
    ga_ref[...] = z[:, c:2 * c]
    mla_ref[...] = z[:, 2 * c:]


def _even_in(xs, mods, g, w_pad, lru_w):
    b, t, d = xs.shape
    nt = t // TOK_TILE
    n_out = w_pad.shape[1]
    n_mla = n_out - 2 * lru_w
    row = functools.partial(_mod_row, ctx_row=b)
    tok = lambda bi, ti: (bi, ti, 0)
    return pl.pallas_call(
        _even_in_kernel,
        out_shape=(jax.ShapeDtypeStruct((b, t, lru_w), F32),
                   jax.ShapeDtypeStruct((b, t, lru_w), F32),
                   jax.ShapeDtypeStruct((b, t, n_mla), F32)),
        grid=(b, nt),
        in_specs=[pl.BlockSpec((None, TOK_TILE, d), tok),
                  pl.BlockSpec((None, 6, d), lambda bi, ti: (row(bi, ti), 0, 0)),
                  pl.BlockSpec((1, d), lambda bi, ti: (0, 0)),
                  pl.BlockSpec((d, n_out), lambda bi, ti: (0, 0))],
        out_specs=(pl.BlockSpec((None, TOK_TILE, lru_w), tok),
                   pl.BlockSpec((None, TOK_TILE, lru_w), tok),
                   pl.BlockSpec((None, TOK_TILE, n_mla), tok)),
        compiler_params=_cparams(("arbitrary", "arbitrary")),
        name="even_in",
    )(xs, mods, g, w_pad)


def _lru_kernel(xa_ref, ga_ref, cw_ref, cb_ref, wg_ref, br_ref, bi_ref, sp_ref, o_ref, pad_ref, rec_ref, *, ctx_len):
    t, c = xa_ref.shape
    tc = LRU_CHUNK
    halo = SUBLANES
    n_chunks = t // tc
    n_ctx = ctx_len // tc
    groups = tc // SUBLANES

    pad_ref[0:halo, :] = jnp.zeros((halo, c), F32)
    pad_ref[t + halo:t + 2 * halo, :] = jnp.zeros((halo, c), F32)
    pad_ref[halo:t + halo, :] = xa_ref[...]

    rid = lax.broadcasted_iota(I32, (tc, 1), 0)
    sub = rid % SUBLANES

    for d in range(2):
        cw = cw_ref[d]
        cb = cb_ref[d]
        wg = wg_ref[d]
        b_r = br_ref[d]
        b_i = bi_ref[d]
        sp = sp_ref[d]

        def chunk(i, h, d=d, cw=cw, cb=cb, wg=wg, b_r=b_r, b_i=b_i, sp=sp):
            if d == 0:
                ci = i
            else:
                ci = jnp.where(i < n_ctx, n_ctx - 1 - i, n_chunks - 1 - (i - n_ctx))
            r0 = pl.multiple_of(ci * tc, tc)
            win = pad_ref[pl.ds(r0, tc + 2 * halo), :]
            pos = r0 + rid
            is_lat = pos >= ctx_len
            xc = jnp.zeros((tc, c), F32) + cb
            for k in range(CONV_W):
                off = (k - (CONV_W - 1)) if d == 0 else ((CONV_W - 1) - k)
                if off == 0:
                    src = win[halo:halo + tc]
                else:
                    src = pltpu.roll(win, (-off) % (tc + 2 * halo), axis=0)[halo:halo + tc]
                    same = ((pos + off) >= ctx_len) == is_lat
                    src = jnp.where(same, src, 0.0)
                xc = xc + cw[k:k + 1] * src
            gz = _dot(xc.astype(BF16), wg)
            r = jax.nn.sigmoid(gz[:, :c] + b_r)
            gi = jax.nn.sigmoid(gz[:, c:] + b_i)
            log_a = (-LRU_C) * r * sp
            a = jnp.exp(log_a)
            bb = jnp.sqrt(_one_minus_exp(2.0 * log_a, a * a)) * (gi * xc)
            for s in (1, 2, 4):
                if d == 0:
                    ok = sub >= s
                    sh = s
                else:
                    ok = sub <= (SUBLANES - 1 - s)
                    sh = tc - s
                a_prev = jnp.where(ok, pltpu.roll(a, sh, axis=0), 1.0)
                b_prev = jnp.where(ok, pltpu.roll(bb, sh, axis=0), 0.0)
                bb = a * b_prev + bb
                a = a * a_prev
            outs = [None] * groups
            order = range(groups) if d == 0 else range(groups - 1, -1, -1)
            for g in order:
                lo = g * SUBLANES
                hg = a[lo:lo + SUBLANES] * h + bb[lo:lo + SUBLANES]
                outs[g] = hg
                h = hg[SUBLANES - 1:SUBLANES] if d == 0 else hg[0:1]
            hs = jnp.concatenate(outs, axis=0)
            if d == 0:
                rec_ref[pl.ds(r0, tc), :] = hs
            else:
                tot = rec_ref[pl.ds(r0, tc), :] + hs
                gate = jax.nn.gelu(ga_ref[pl.ds(r0, tc), :], approximate=True)
                o_ref[pl.ds(r0, tc), :] = (tot * gate).astype(o_ref.dtype)
            return h

        lax.fori_loop(0, n_chunks, chunk, jnp.zeros((1, c), F32))


def _lru(xa, ga, conv_w, conv_b, w_gates, b_r, b_i, sp, ctx_len):
    b, t, w = xa.shape
    c = 2 * LANES
    nh = w // c
    tok = lambda bi, hi: (bi, 0, hi)
    par = lambda bi, hi: (0, 0, hi)
    return pl.pallas_call(
        functools.partial(_lru_kernel, ctx_len=ctx_len),
        out_shape=jax.ShapeDtypeStruct((b, t, w), BF16),
        grid=(b, nh),
        in_specs=[pl.BlockSpec((None, t, c), tok),
                  pl.BlockSpec((None, t, c), tok),
                  pl.BlockSpec((2, CONV_W, c), par),
                  pl.BlockSpec((2, 1, c), par),
                  pl.BlockSpec((2, None, c, 2 * c), lambda bi, hi: (0, hi, 0, 0)),
                  pl.BlockSpec((2, 1, c), par),
                  pl.BlockSpec((2, 1, c), par),
                  pl.BlockSpec((2, 1, c), par)],
        out_specs=pl.BlockSpec((None, t, c), tok),
        scratch_shapes=[pltpu.VMEM((t + 2 * SUBLANES, c), F32), pltpu.VMEM((t, c), F32)],
        compiler_params=_cparams(("arbitrary", "arbitrary")),
        name="rglru",
    )(xa, ga, conv_w, conv_b, w_gates, b_r, b_i, sp)


def _mla_prep_kernel(in_ref, qan_ref, kvn_ref, wq_ref, wk_ref, wv_ref, mq_ref, mk_ref, gq_ref, gk_ref, gkr_ref,
                     cos_ref, slo_ref, shi_ref, q_ref, k_ref, v_ref, *, q_lora, kv_lora, scale):
    z = in_ref[...]
    heads = q_ref.shape[-1] // LANES
    cos = cos_ref[...]
    slo = slo_ref[...]
    shi = shi_ref[...]
    half = MLA_ROPE // 2

    qan = (_rms(z[:, :q_lora]) * qan_ref[...]).astype(BF16)
    q = _dot(qan, wq_ref[...])
    q = q * lax.rsqrt(_dot((q * q).astype(BF16), mq_ref[...]) + EPS) * gq_ref[...]
    q = _rope(q, _tile_lanes(cos, heads), _tile_lanes(slo, heads), _tile_lanes(shi, heads), half)
    q_ref[...] = (q * scale).astype(BF16)

    kvn = (_rms(z[:, q_lora:q_lora + kv_lora]) * kvn_ref[...]).astype(BF16)
    kk = _dot(kvn, wk_ref[...])
    kk = kk * lax.rsqrt(_dot((kk * kk).astype(BF16), mk_ref[...]) + EPS) * gk_ref[...]
    v_ref[...] = _dot(kvn, wv_ref[...]).astype(BF16)

    kr = z[:, q_lora + kv_lora:]
    kr = kr * lax.rsqrt(jnp.sum(kr * kr, axis=-1, keepdims=True) * (1.0 / MLA_ROPE) + EPS) * gkr_ref[...]
    kr = pltpu.roll(kr, MLA_NOPE, axis=1)
    kr = _rope(kr, cos, slo, shi, half)
    k_ref[...] = (kk + _tile_lanes(kr, heads)).astype(BF16)


def _mla_prep(mla_in, p, scale):
    b, t, w = mla_in.shape
    hq = MLA_HEADS * LANES
    hv = MLA_HEADS * MLA_V
    tok = lambda bi, ti: (bi, ti, 0)
    full = lambda a: pl.BlockSpec(a.shape, lambda bi, ti: (0,) * a.ndim)
    pos = pl.BlockSpec((TOK_TILE, LANES), lambda bi, ti: (ti, 0))
    consts = [p["qan"], p["kvn"], p["wq"], p["wk"], p["wv"], p["mq"], p["mk"], p["gq"], p["gk"], p["gkr"]]
    return pl.pallas_call(
        functools.partial(_mla_prep_kernel, q_lora=p["qan"].shape[1], kv_lora=p["kvn"].shape[1], scale=scale),
        out_shape=(jax.ShapeDtypeStruct((b, t, hq), BF16),
                   jax.ShapeDtypeStruct((b, t, hq), BF16),
                   jax.ShapeDtypeStruct((b, t, hv), BF16)),
        grid=(b, t // TOK_TILE),
        in_specs=[pl.BlockSpec((None, TOK_TILE, w), tok)] + [full(a) for a in consts] + [pos, pos, pos],
        out_specs=(pl.BlockSpec((None, TOK_TILE, hq), tok),
                   pl.BlockSpec((None, TOK_TILE, hq), tok),
                   pl.BlockSpec((None, TOK_TILE, hv), tok)),
        compiler_params=_cparams(("arbitrary", "arbitrary")),
        name="mla_prep",
    )(mla_in, *consts, p["cos"], p["slo"], p["shi"])


def _mla_attn_kernel(q_ref, k_ref, v_ref, o_ref, *, ctx_len):
    tq = q_ref.shape[0]
    t = k_ref.shape[0]
    lane = lax.broadcasted_iota(I32, (tq, LANES), 1)

    def attend(nk):
        v = v_ref[0:nk, :]
        outs = []
        for hh in range(2):
            q = q_ref[:, hh * LANES:(hh + 1) * LANES]
            k = k_ref[0:nk, hh * LANES:(hh + 1) * LANES]
            s = _dot_nt(q, k)
            p = jnp.exp(s - jnp.max(s, axis=-1, keepdims=True))
            l = jnp.sum(p, axis=-1, keepdims=True)
            outs.append(_dot(p.astype(BF16), v) / l)
        o_ref[...] = jnp.where(lane < MLA_V, outs[0], outs[1]).astype(o_ref.dtype)

    @pl.when(pl.program_id(2) == 0)
    def _():
        attend(ctx_len)

    @pl.when(pl.program_id(2) > 0)
    def _():
        attend(t)


def _mla_attn(q, k, v, ctx_len):
    b, t, hq = q.shape
    pairs = hq // (2 * LANES)
    return pl.pallas_call(
        functools.partial(_mla_attn_kernel, ctx_len=ctx_len),
        out_shape=jax.ShapeDtypeStruct((b, t, v.shape[-1]), BF16),
        grid=(b, pairs, t // TOK_TILE),
        in_specs=[pl.BlockSpec((None, TOK_TILE, 2 * LANES), lambda bi, hi, ti: (bi, ti, hi)),
                  pl.BlockSpec((None, t, 2 * LANES), lambda bi, hi, ti: (bi, 0, hi)),
                  pl.BlockSpec((None, t, LANES), lambda bi, hi, ti: (bi, 0, hi))],
        out_specs=pl.BlockSpec((None, TOK_TILE, LANES), lambda bi, hi, ti: (bi, ti, hi)),
        compiler_params=_cparams(("arbitrary", "arbitrary", "arbitrary")),
        name="mla_attn",
    )(q, k, v)


def _odd_in_kernel(x_ref, mod_ref, g_ref, w_ref, m_ref, gq_ref, gk_ref, cos_ref, slo_ref, shi_ref,
                   q_ref, k_ref, v_ref, *, scale):
    mod = mod_ref[...]
    h = _prenorm(x_ref[...], g_ref[...], mod[1:2], mod[0:1]).astype(BF16)
    n = q_ref.shape[-1]
    reps = n // LANES
    cos = _tile_lanes(cos_ref[...], reps)
    slo = _tile_lanes(slo_ref[...], reps)
    shi = _tile_lanes(shi_ref[...], reps)
    half = GQA_DIM // 2
    m = m_ref[...]

    q = _dot(h, w_ref[:, 0:n])
    q = q * lax.rsqrt(_dot((q * q).astype(BF16), m) + EPS) * gq_ref[...]
    q_ref[...] = (_rope(q, cos, slo, shi, half) * scale).astype(BF16)
    k = _dot(h, w_ref[:, n:2 * n])
    k = k * lax.rsqrt(_dot((k * k).astype(BF16), m) + EPS) * gk_ref[...]
    k_ref[...] = _rope(k, cos, slo, shi, half).astype(BF16)
    v_ref[...] = _dot(h, w_ref[:, 2 * n:3 * n]).astype(BF16)


def _odd_in(xs, mods, g, p, scale):
    b, t, d = xs.shape
    n = p["w"].shape[1] // 3
    row = functools.partial(_mod_row, ctx_row=b)
    tok = lambda bi, ti: (bi, ti, 0)
    full = lambda a: pl.BlockSpec(a.shape, lambda bi, ti: (0,) * a.ndim)
    pos = pl.BlockSpec((TOK_TILE, LANES), lambda bi, ti: (ti, 0))
    out = jax.ShapeDtypeStruct((b, t, n), BF16)
    return pl.pallas_call(
        functools.partial(_odd_in_kernel, scale=scale),
        out_shape=(out, out, out),
        grid=(b, t // TOK_TILE),
        in_specs=[pl.BlockSpec((None, TOK_TILE, d), tok),
                  pl.BlockSpec((None, 6, d), lambda bi, ti: (row(bi, ti), 0, 0)),
                  pl.BlockSpec((1, d), lambda bi, ti: (0, 0)),
                  full(p["w"]), full(p["m"]), full(p["gq"]), full(p["gk"]), pos, pos, pos],
        out_specs=(pl.BlockSpec((None, TOK_TILE, n), tok),) * 3,
        compiler_params=_cparams(("arbitrary", "arbitrary")),
        name="odd_in",
    )(xs, mods, g, p["w"], p["m"], p["gq"], p["gk"], p["cos"], p["slo"], p["shi"])


def _win_attn_kernel(sink_ref, q_ref, k_ref, v_ref, o_ref, *, ctx_len):
    t = k_ref.shape[0]
    wb = WIN_BLOCK
    w3 = 3 * wb
    group = GQA_HEADS // GQA_KV_HEADS
    kh = pl.program_id(1)
    i = pl.program_id(2)
    q0 = ctx_len + i * wb
    ws = pl.multiple_of(jnp.clip(q0 - wb, ctx_len, t - w3), wb)
    q = q_ref[...]
    kcat = jnp.concatenate([k_ref[pl.ds(ws, w3), :], k_ref[0:ctx_len, :]], axis=0)
    vcat = jnp.concatenate([v_ref[pl.ds(ws, w3), :], v_ref[0:ctx_len, :]], axis=0)
    nk = w3 + ctx_len
    q_pos = q0 + lax.broadcasted_iota(I32, (wb, nk), 0)
    col = lax.broadcasted_iota(I32, (wb, nk), 1)
    valid = (col >= w3) | (jnp.abs(q_pos - (ws + col)) <= WINDOW)
    head_of_lane = lax.broadcasted_iota(I32, (nk, group * GQA_DIM), 1) // GQA_DIM
    acc = jnp.zeros((wb, group * GQA_DIM), F32)
    for h in range(group):
        sel = head_of_lane == h
        s = _dot_nt(q, jnp.where(sel, kcat, jnp.zeros_like(kcat)))
        s = jnp.where(valid, s, NEG_INF)
        sink = sink_ref[kh * group + h]
        m = jnp.maximum(jnp.max(s, axis=-1, keepdims=True), sink)
        p = jnp.exp(s - m)
        l = jnp.sum(p, axis=-1, keepdims=True) + jnp.exp(sink - m)
        acc = acc + _dot(p.astype(BF16), jnp.where(sel, vcat, jnp.zeros_like(vcat))) / l
    o_ref[...] = acc.astype(o_ref.dtype)


def _win_attn(q, k_rep, v_rep, sink, ctx_len):
    b, t, n = q.shape
    gw = n // GQA_KV_HEADS
    s_len = t - ctx_len
    off = ctx_len // WIN_BLOCK
    return pl.pallas_call(
        functools.partial(_win_attn_kernel, ctx_len=ctx_len),
        out_shape=jax.ShapeDtypeStruct((b, s_len, n), BF16),
        grid=(b, GQA_KV_HEADS, s_len // WIN_BLOCK),
        in_specs=[pl.BlockSpec(memory_space=pltpu.SMEM),
                  pl.BlockSpec((None, WIN_BLOCK, gw), lambda bi, hi, ti: (bi, ti + off, hi)),
                  pl.BlockSpec((None, t, gw), lambda bi, hi, ti: (bi, 0, hi)),
                  pl.BlockSpec((None, t, gw), lambda bi, hi, ti: (bi, 0, hi))],
        out_specs=pl.BlockSpec((None, WIN_BLOCK, gw), lambda bi, hi, ti: (bi, ti, hi)),
        compiler_params=_cparams(("arbitrary", "arbitrary", "arbitrary")),
        name="win_attn",
    )(sink, q, k_rep, v_rep)


def _post_mix_kernel(a1_ref, a2_ref, x_ref, mod_ref, g_ref, w_ref, wr_ref, br_ref,
                     xo_ref, f_ref, e_ref, gate_ref, msk_ref):
    half = a1_ref.shape[-1]
    mod = mod_ref[...]
    m = _dot(a1_ref[...], w_ref[0:half, :]) + _dot(a2_ref[...], w_ref[half:2 * half, :])
    x = x_ref[...] + mod[2:3] * m
    xo_ref[...] = x
    f = _prenorm(x, g_ref[...], mod[4:5], mod[3:4])
    for j in range(f_ref.shape[1]):
        f_ref[:, j, :] = f[:, j * LANES:(j + 1) * LANES]

    logit = _dot3(f, wr_ref[...]) + br_ref[...]
    tm = logit.shape[0]
    lane = lax.broadcasted_iota(I32, (tm, LANES), 1)
    lane_f = lane.astype(F32)
    vals, idxs = [], []
    for _ in range(TOP_K):
        mx = jnp.max(logit, axis=-1, keepdims=True)
        ix = jnp.min(jnp.where(logit == mx, lane_f, float(LANES)), axis=-1, keepdims=True)
        vals.append(mx)
        idxs.append(ix)
        logit = jnp.where(lane_f == ix, -jnp.inf, logit)
    exps = [jnp.exp(v - vals[0]) for v in vals]
    den = exps[0]
    for e in exps[1:]:
        den = den + e
    e_out = jnp.zeros((tm, LANES), F32)
    g_out = jnp.zeros((tm, LANES), F32)
    m_out = jnp.zeros((tm, LANES), F32)
    for k in range(TOP_K):
        e_out = jnp.where(lane == k, idxs[k], e_out)
        g_out = jnp.where(lane == k, exps[k] / den, g_out)
        m_out = jnp.where(lane_f == idxs[k], 1.0, m_out)
    e_ref[...] = e_out.astype(I32)
    gate_ref[...] = g_out
    msk_ref[...] = m_out.astype(I32)


def _post_mix(a1, a2, lane_blk2, xs, mods, g, w_out, w_router, b_router, t_off, a_off):
    b, t, d = xs.shape
    half = w_out.shape[0] // 2
    nt = t // TOK_TILE - t_off
    t_out = nt * TOK_TILE
    row = functools.partial(_mod_row, ctx_row=b)
    tok = lambda bi, ti: (bi, ti, 0)
    slab = jax.ShapeDtypeStruct((b, t_out, LANES), F32)
    slab_i = jax.ShapeDtypeStruct((b, t_out, LANES), I32)
    return pl.pallas_call(
        _post_mix_kernel,
        out_shape=(jax.ShapeDtypeStruct((b, t_out, d), F32),
                   jax.ShapeDtypeStruct((b * t_out, d // LANES, LANES), F32),
                   slab_i, slab, slab_i),
        grid=(b, nt),
        in_specs=[pl.BlockSpec((None, TOK_TILE, half), lambda bi, ti: (bi, ti + a_off, 0)),
                  pl.BlockSpec((None, TOK_TILE, half), lambda bi, ti: (bi, ti + a_off, lane_blk2)),
                  pl.BlockSpec((None, TOK_TILE, d), lambda bi, ti: (bi, ti + t_off, 0)),
                  pl.BlockSpec((None, 6, d), lambda bi, ti: (row(bi, ti + t_off), 0, 0)),
                  pl.BlockSpec((1, d), lambda bi, ti: (0, 0)),
                  pl.BlockSpec(w_out.shape, lambda bi, ti: (0, 0)),
                  pl.BlockSpec(w_router.shape, lambda bi, ti: (0, 0)),
                  pl.BlockSpec((1, LANES), lambda bi, ti: (0, 0))],
        out_specs=(pl.BlockSpec((None, TOK_TILE, d), tok),
                   pl.BlockSpec((TOK_TILE, d // LANES, LANES), lambda bi, ti: (bi * nt + ti, 0, 0)),
                   pl.BlockSpec((None, TOK_TILE, LANES), tok),
                   pl.BlockSpec((None, TOK_TILE, LANES), tok),
                   pl.BlockSpec((None, TOK_TILE, LANES), tok)),
        compiler_params=_cparams(("arbitrary", "arbitrary")),
        name="post_mix",
    )(a1, a2, xs, mods, g, w_out, w_router, b_router)


def _dispatch_kernel(dest_ref, f_ref, xs_in_ref, xs_ref, sem):
    del xs_in_ref
    n = dest_ref.shape[0]
    base = pl.program_id(0) * (n // TOP_K)

    def issue(j, c):
        pltpu.make_async_copy(f_ref.at[base + j // TOP_K], xs_ref.at[dest_ref[j]], sem).start()
        return c

    lax.fori_loop(0, n, issue, 0)

    def drain(j, c):
        pltpu.make_async_copy(f_ref.at[0], xs_ref.at[0], sem).wait()
        return c

    lax.fori_loop(0, n, drain, 0)


def _dispatch(f3, dest, rows):
    n, s, l = f3.shape
    per = TOK_TILE * TOP_K
    xs0 = jnp.zeros((rows, s, l), F32)
    return pl.pallas_call(
        _dispatch_kernel,
        out_shape=jax.ShapeDtypeStruct((rows, s, l), F32),
        grid=(n // TOK_TILE,),
        in_specs=[pl.BlockSpec((per,), lambda i: (i,), memory_space=pltpu.SMEM),
                  pl.BlockSpec(memory_space=pl.ANY),
                  pl.BlockSpec(memory_space=pl.ANY)],
        out_specs=pl.BlockSpec(memory_space=pl.ANY),
        scratch_shapes=[pltpu.SemaphoreType.DMA(())],
        input_output_aliases={2: 0},
        compiler_params=_cparams(("arbitrary",), has_side_effects=True),
        name="moe_dispatch",
    )(dest, f3, xs0)


def _experts_kernel(be_ref, nu_ref, xs_ref, wgu_ref, bgu_ref, wdn_ref, bdn_ref, y_ref):
    del be_ref
    i = pl.program_id(0)
    nj = xs_ref.shape[1]

    @pl.when(i < nu_ref[0])
    def _():
        x = jnp.concatenate([xs_ref[:, j, :] for j in range(nj)], axis=1).astype(BF16)
        h = _dot(x, wgu_ref[...]) + bgu_ref[...]
        ff = h.shape[1] // 2
        hg = jnp.minimum(h[:, :ff], SWIGLU_LIMIT)
        hu = jnp.clip(h[:, ff:], -SWIGLU_LIMIT, SWIGLU_LIMIT)
        act = hg * jax.nn.sigmoid(SWIGLU_ALPHA * hg) * (hu + 1.0)
        y = _dot(act.astype(BF16), wdn_ref[...]) + bdn_ref[...]
        for j in range(nj):
            y_ref[:, j, :] = y[:, j * LANES:(j + 1) * LANES]

    @pl.when(i >= nu_ref[0])
    def _():
        y_ref[...] = jnp.zeros(y_ref.shape, F32)


def _experts(xs, block_e, n_used, wgu, bgu, wdn, bdn):
    rows, s, l = xs.shape
    d = s * l
    ff2 = wgu.shape[2]
    nb = rows // MOE_ROWS
    return pl.pallas_call(
        _experts_kernel,
        out_shape=jax.ShapeDtypeStruct((rows, s, l), F32),
        grid_spec=pltpu.PrefetchScalarGridSpec(
            num_scalar_prefetch=2,
            grid=(nb,),
            in_specs=[pl.BlockSpec((MOE_ROWS, s, l), lambda i, be, nu: (i, 0, 0)),
                      pl.BlockSpec((None, d, ff2), lambda i, be, nu: (be[i], 0, 0)),
                      pl.BlockSpec((None, 1, ff2), lambda i, be, nu: (be[i], 0, 0)),
                      pl.BlockSpec((None, ff2 // 2, d), lambda i, be, nu: (be[i], 0, 0)),
                      pl.BlockSpec((None, 1, d), lambda i, be, nu: (be[i], 0, 0))],
            out_specs=pl.BlockSpec((MOE_ROWS, s, l), lambda i, be, nu: (i, 0, 0))),
        compiler_params=_cparams(("arbitrary",)),
        name="moe_experts",
    )(block_e, n_used, xs, wgu, bgu, wdn, bdn)


def _combine_kernel(dest_ref, y_ref, gate_ref, x_ref, mod_ref, o_ref, buf_ref, sem):
    n = dest_ref.shape[0]
    tm = n // TOP_K

    def issue(j, c):
        pltpu.make_async_copy(y_ref.at[dest_ref[j]], buf_ref.at[j % TOP_K, j // TOP_K], sem).start()
        return c

    lax.fori_loop(0, n, issue, 0)

    def drain(j, c):
        pltpu.make_async_copy(y_ref.at[0], buf_ref.at[0, 0], sem).wait()
        return c

    lax.fori_loop(0, n, drain, 0)

    gates = gate_ref[...]
    g2 = mod_ref[5:6, :]
    for j in range(buf_ref.shape[2]):
        acc = jnp.zeros((tm, LANES), F32)
        for k in range(TOP_K):
            acc = acc + gates[:, k:k + 1] * buf_ref[k, :, j, :]
        sl = slice(j * LANES, (j + 1) * LANES)
        o_ref[:, sl] = x_ref[:, sl] + g2[:, sl] * acc


def _combine(y3, dest, gates, xs, mods, t_off):
    b, t_out, d = xs.shape
    nt = t_out // TOK_TILE
    per = TOK_TILE * TOP_K
    s, l = y3.shape[1:]
    row = functools.partial(_mod_row, ctx_row=b)
    tok = lambda bi, ti: (bi, ti, 0)
    return pl.pallas_call(
        _combine_kernel,
        out_shape=jax.ShapeDtypeStruct((b, t_out, d), F32),
        grid=(b, nt),
        in_specs=[pl.BlockSpec((per,), lambda bi, ti: (bi * nt + ti,), memory_space=pltpu.SMEM),
                  pl.BlockSpec(memory_space=pl.ANY),
                  pl.BlockSpec((None, TOK_TILE, LANES), tok),
                  pl.BlockSpec((None, TOK_TILE, d), tok),
                  pl.BlockSpec((None, 6, d), lambda bi, ti: (row(bi, ti + t_off), 0, 0))],
        out_specs=pl.BlockSpec((None, TOK_TILE, d), tok),
        scratch_shapes=[pltpu.VMEM((TOP_K, TOK_TILE, s, l), F32), pltpu.SemaphoreType.DMA(())],
        compiler_params=_cparams(("arbitrary", "arbitrary")),
        name="moe_combine",
    )(dest, y3, gates, xs, mods)


def _routing(e_sel, msk):
    n = e_sel.shape[0]
    counts = jnp.sum(msk, axis=0)
    padded = (counts + MOE_ROWS - 1) // MOE_ROWS * MOE_ROWS
    pend = jnp.cumsum(padded)
    pstart = pend - padded
    rank = jnp.cumsum(msk, axis=0) - msk
    dest = pstart[e_sel] + jnp.take_along_axis(rank, e_sel, axis=1)
    n_blocks = n * TOP_K // MOE_ROWS + N_EXPERTS
    block_e = jnp.minimum(jnp.searchsorted(pend, jnp.arange(n_blocks) * MOE_ROWS, side="right"), N_EXPERTS - 1)
    n_used = (pend[-1] // MOE_ROWS).reshape(1)
    return dest.reshape(-1).astype(I32), block_e.astype(I32), n_used.astype(I32), n_blocks * MOE_ROWS


def _moe(f3, e_out, gates, msk, x_mid, mods, t_off, wgu, bgu, wdn, bdn):
    n = f3.shape[0]
    e_sel = e_out.reshape(n, LANES)[:, :TOP_K]
    dest, block_e, n_used, rows = _routing(e_sel, msk.reshape(n, LANES)[:, :N_EXPERTS])
    xs = _dispatch(f3, dest, rows)
    y3 = _experts(xs, block_e, n_used, wgu, bgu, wdn, bdn)
    return _combine(y3, dest, gates, x_mid, mods, t_off)


def _axial_angles(n_rows, rot_dim):
    row = jnp.repeat(jnp.arange(n_rows, dtype=F32), GRID_W)
    col = jnp.tile(jnp.arange(GRID_W, dtype=F32), n_rows)
    n = rot_dim // 4
    freqs = ROPE_THETA ** (-jnp.arange(n, dtype=F32) / n)
    return jnp.concatenate([row[:, None] * freqs, col[:, None] * freqs], axis=-1)


def _rope_tables(ctx_len, s_len, rot_dim, lane_base, reps):
    ang = _axial_angles(s_len // GRID_W, rot_dim)
    half = rot_dim // 2
    period = LANES // reps
    cos = jnp.ones((s_len, period), F32)
    cos = cos.at[:, lane_base:lane_base + rot_dim].set(jnp.tile(jnp.cos(ang), (1, 2)))
    slo = jnp.zeros((s_len, period), F32).at[:, lane_base:lane_base + half].set(-jnp.sin(ang))
    shi = jnp.zeros((s_len, period), F32).at[:, lane_base + half:lane_base + rot_dim].set(jnp.sin(ang))
    ident = [jnp.ones((ctx_len, period), F32), jnp.zeros((ctx_len, period), F32), jnp.zeros((ctx_len, period), F32)]
    return [jnp.tile(jnp.concatenate([i, tb], axis=0), (1, reps)) for i, tb in zip(ident, (cos, slo, shi))]


def _segment_mean_matrix(width, segs):
    lane = jnp.arange(width)
    seg_id = jnp.full((width,), -1, I32)
    seg_w = jnp.zeros((width,), F32)
    for start, length, period in segs:
        inside = ((lane % period) >= start) & ((lane % period) < start + length)
        seg_id = jnp.where(inside, (lane // period) * 8 + start // 32, seg_id)
        seg_w = jnp.where(inside, 1.0 / length, seg_w)
    same = (seg_id[:, None] == seg_id[None, :]) & (seg_id[:, None] >= 0)
    return jnp.where(same, seg_w[None, :], 0.0).astype(BF16)


def _block_diag(w):
    n, c, _ = w.shape
    eye = jnp.eye(n, dtype=w.dtype)
    return (eye[:, None, :, None] * w[:, :, None, :]).reshape(n * c, n * c)


def kernel(x, c, ctx, c_ctx, w_mod, b_mod, norm_mix, norm_ffn, w_in_even, lru_conv_w, lru_conv_b, lru_w_r, lru_b_r, lru_w_i, lru_b_i, lru_lambda, mla_q_a_norm, mla_w_q_b, mla_kv_a_norm, mla_w_kv_b, mla_nope_norm, mla_rope_norm, w_out_even, w_qkv_odd, gqa_qk_norm, gqa_sink, w_out_odd, w_router, b_router, w_gate_up, b_gate_up, w_down, b_down):
    b, s_len, d = x.shape
    ctx_len = ctx.shape[1]
    depth = w_mod.shape[0]
    assert depth == 2 and ctx_len == TOK_TILE and s_len % TOK_TILE == 0 and b + 1 <= SUBLANES
    lru_w = lru_conv_w.shape[-1]
    q_lora = mla_q_a_norm.shape[-1]
    kv_lora = mla_kv_a_norm.shape[-1]

    xs = jnp.concatenate([ctx, x], axis=1)
    cvec = jnp.concatenate([c, c_ctx[None], jnp.zeros((SUBLANES - b - 1, d), F32)], axis=0)
    mods = _modulation(cvec, w_mod, b_mod).reshape(depth, SUBLANES, 6, d)

    wr_pad = jnp.zeros((depth, d, LANES), F32).at[:, :, :N_EXPERTS].set(w_router)
    br_pad = jnp.full((depth, 1, LANES), NEG_INF, F32).at[:, 0, :N_EXPERTS].set(b_router)
    wgu = w_gate_up.astype(BF16)
    wdn = w_down.astype(BF16)
    bgu = b_gate_up[:, :, None, :]
    bdn = b_down[:, :, None, :]

    n_in = w_in_even.shape[-1]
    n_in_pad = -(-n_in // LANES) * LANES
    w_in = jnp.zeros((d, n_in_pad), F32).at[:, :n_in].set(w_in_even[0]).astype(BF16)
    xa, ga, mla_in = _even_in(xs, mods[0], norm_mix[0][None], w_in, lru_w)

    nh = lru_w // (2 * LANES)
    per = LRU_BLOCKS // nh
    blk = lru_w // LRU_BLOCKS
    w_gates = jnp.stack([
        jnp.stack([jnp.concatenate([_block_diag(lru_w_r[0, dd, h * per:(h + 1) * per]),
                                    _block_diag(lru_w_i[0, dd, h * per:(h + 1) * per])], axis=1)
                   for h in range(nh)]) for dd in range(2)]).astype(BF16)
    assert blk * per == 2 * LANES
    ya = _lru(xa, ga, lru_conv_w[0], lru_conv_b[0][:, None, :], w_gates, lru_b_r[0][:, None, :],
              lru_b_i[0][:, None, :], jax.nn.softplus(-lru_lambda[0])[:, None, :], ctx_len)

    qk = MLA_NOPE + MLA_ROPE
    hq = MLA_HEADS * LANES
    wq = jnp.zeros((q_lora, MLA_HEADS, LANES), F32).at[:, :, :qk].set(
        mla_w_q_b[0].reshape(q_lora, MLA_HEADS, qk)).reshape(q_lora, hq).astype(BF16)
    wkv = mla_w_kv_b[0].reshape(kv_lora, MLA_HEADS, MLA_NOPE + MLA_V)
    wk = jnp.zeros((kv_lora, MLA_HEADS, LANES), F32).at[:, :, :MLA_NOPE].set(
        wkv[:, :, :MLA_NOPE]).reshape(kv_lora, hq).astype(BF16)
    wv = wkv[:, :, MLA_NOPE:].reshape(kv_lora, MLA_HEADS * MLA_V).astype(BF16)
    zpad = jnp.zeros((LANES - qk,), F32)
    gq = jnp.tile(jnp.concatenate([mla_nope_norm[0, 0], mla_rope_norm[0, 0], zpad]), MLA_HEADS)[None]
    gk = jnp.tile(jnp.concatenate([mla_nope_norm[0, 1], jnp.zeros((LANES - MLA_NOPE,), F32)]), MLA_HEADS)[None]
    gkr = jnp.concatenate([mla_rope_norm[0, 1], jnp.zeros((LANES - MLA_ROPE,), F32)])[None]
    cos, slo, shi = _rope_tables(ctx_len, s_len, MLA_ROPE, MLA_NOPE, 1)
    mla_p = dict(qan=mla_q_a_norm[0][None], kvn=mla_kv_a_norm[0][None], wq=wq, wk=wk, wv=wv,
                 mq=_segment_mean_matrix(hq, [(0, MLA_NOPE, LANES), (MLA_NOPE, MLA_ROPE, LANES)]),
                 mk=_segment_mean_matrix(hq, [(0, MLA_NOPE, LANES)]),
                 gq=gq, gk=gk, gkr=gkr, cos=cos, slo=slo, shi=shi)
    q, k, v = _mla_prep(mla_in, mla_p, qk ** -0.5)
    yb = _mla_attn(q, k, v, ctx_len)

    x_mid, f3, e_out, gates, msk = _post_mix(ya, yb, 0, xs, mods[0], norm_ffn[0][None], w_out_even[0].astype(BF16),
                                             wr_pad[0], br_pad[0], 0, 0)
    xs = _moe(f3, e_out, gates, msk, x_mid, mods[0], 0, wgu[0], bgu[0], wdn[0], bdn[0])

    group = GQA_HEADS // GQA_KV_HEADS
    nq = GQA_HEADS * GQA_DIM
    nkv = GQA_KV_HEADS * GQA_DIM
    wqkv = w_qkv_odd[0]
    rep = lambda w: jnp.tile(w.reshape(d, GQA_KV_HEADS, 1, GQA_DIM), (1, 1, group, 1)).reshape(d, nq)
    w_odd = jnp.concatenate([wqkv[:, :nq], rep(wqkv[:, nq:nq + nkv]), rep(wqkv[:, nq + nkv:])], axis=1).astype(BF16)
    cos, slo, shi = _rope_tables(ctx_len, s_len, GQA_DIM, 0, LANES // GQA_DIM)
    odd_p = dict(w=w_odd, m=_segment_mean_matrix(nq, [(0, GQA_DIM, GQA_DIM)]),
                 gq=jnp.tile(gqa_qk_norm[0, 0], GQA_HEADS)[None], gk=jnp.tile(gqa_qk_norm[0, 1], GQA_HEADS)[None],
                 cos=cos, slo=slo, shi=shi)
    q, k_rep, v_rep = _odd_in(xs, mods[1], norm_mix[1][None], odd_p, GQA_DIM ** -0.5)
    o = _win_attn(q, k_rep, v_rep, gqa_sink[0], ctx_len)

    t_off = ctx_len // TOK_TILE
    x_mid, f3, e_out, gates, msk = _post_mix(o, o, 1, xs, mods[1], norm_ffn[1][None], w_out_odd[0].astype(BF16),
                                             wr_pad[1], br_pad[1], t_off, 0)
    return _moe(f3, e_out, gates, msk, x_mid, mods[1], t_off, wgu[1], bgu[1], wdn[1], bdn[1])
```

```python
import functools
import math

import jax
import jax.numpy as jnp
from jax import lax
from jax.experimental import pallas as pl
from jax.experimental.pallas import tpu as pltpu

F32 = jnp.float32
BF16 = jnp.bfloat16
I32 = jnp.int32

GRID_W = 64
LRU_BLOCKS = 8
LRU_C = 8.0
CONV_W = 4
MLA_HEADS = 8
MLA_NOPE = 64
MLA_ROPE = 32
MLA_V = 64
GQA_HEADS = 16
GQA_KV_HEADS = 4
GQA_DIM = 64
WINDOW = 128
ROPE_THETA = 10000.0
NEG_INF = -1e30
EPS = 1e-6
N_EXPERTS = 32
TOP_K = 4
SWIGLU_LIMIT = 7.0
SWIGLU_ALPHA = 1.702

LANES = 128
SUBLANES = 8
TOK_TILE = 256
LRU_CHUNK = 128
WIN_BLOCK = 128
MOE_ROWS = 256
VMEM_LIMIT = 48 * 1024 * 1024


def _cparams(sem, **kw):
    return pltpu.CompilerParams(dimension_semantics=sem, vmem_limit_bytes=VMEM_LIMIT, **kw)


def _dot(a, b):
    return jnp.dot(a, b, preferred_element_type=F32)


def _dot_nt(a, b):
    return lax.dot_general(a, b, (((1,), (1,)), ((), ())), preferred_element_type=F32)


def _split_bf16(x):
    hi = x.astype(BF16)
    lo = (x - hi.astype(F32)).astype(BF16)
    return hi, lo


def _dot3(a, w):
    ah, al = _split_bf16(a)
    wh, wl = _split_bf16(w)
    return _dot(ah, wh) + _dot(al, wh) + _dot(ah, wl)


def _rms(x):
    return x * lax.rsqrt(jnp.mean(x * x, axis=-1, keepdims=True) + EPS)


def _prenorm(x, g, scale, shift):
    return (_rms(x) * g) * (1.0 + scale) + shift


def _rope(x, cos, sin_lo, sin_hi, half):
    w = x.shape[-1]
    return x * cos + pltpu.roll(x, w - half, axis=1) * sin_lo + pltpu.roll(x, half, axis=1) * sin_hi


def _one_minus_exp(t, exp_t):
    series = -t * (1.0 + t * (1 / 2) * (1.0 + t * (1 / 3) * (1.0 + t * (1 / 4) * (1.0 + t * (1 / 5) * (1.0 + t * (1 / 6))))))
    return jnp.where(t > -0.25, series, 1.0 - exp_t)


def _tile_lanes(t, reps):
    return jnp.concatenate([t] * reps, axis=1) if reps > 1 else t


def _mod_kernel(c_ref, w_ref, b_ref, o_ref):
    c = c_ref[...]
    o_ref[...] = _dot3(c * jax.nn.sigmoid(c), w_ref[...]) + b_ref[...]


def _modulation(cvec, w_mod, b_mod):
    depth, d, n = w_mod.shape
    tn = 1536
    return pl.pallas_call(
        _mod_kernel,
        out_shape=jax.ShapeDtypeStruct((depth, SUBLANES, n), F32),
        grid=(depth, n // tn),
        in_specs=[pl.BlockSpec((SUBLANES, d), lambda l, j: (0, 0)),
                  pl.BlockSpec((None, d, tn), lambda l, j: (l, 0, j)),
                  pl.BlockSpec((None, 1, tn), lambda l, j: (l, 0, j))],
        out_specs=pl.BlockSpec((None, SUBLANES, tn), lambda l, j: (l, 0, j)),
        compiler_params=_cparams(("arbitrary", "arbitrary")),
        name="modulation",
    )(cvec, w_mod, b_mod.reshape(depth, 1, n))


def _mod_row(b, t, ctx_row):
    return jnp.where(t == 0, ctx_row, b)


def _even_in_kernel(x_ref, mod_ref, g_ref, w_ref, xa_ref, ga_ref, mla_ref):
    mod = mod_ref[...]
    h = _prenorm(x_ref[...], g_ref[...], mod[1:2], mod[0:1])
    z = _dot(h.astype(BF16), w_ref[...])
    c = xa_ref.shape[-1]
    xa_ref[...] = z[:, :c]
    ga_ref[...] = z[:, c:2 * c]
    mla_ref[...] = z[:, 2 * c:]


def _even_in(xs, mods, g, w_pad, lru_w):
    b, t, d = xs.shape
    nt = t // TOK_TILE
    n_out = w_pad.shape[1]
    n_mla = n_out - 2 * lru_w
    row = functools.partial(_mod_row, ctx_row=b)
    tok = lambda bi, ti: (bi, ti, 0)
    return pl.pallas_call(
        _even_in_kernel,
        out_shape=(jax.ShapeDtypeStruct((b, t, lru_w), F32),
                   jax.ShapeDtypeStruct((b, t, lru_w), F32),
                   jax.ShapeDtypeStruct((b, t, n_mla), F32)),
        grid=(b, nt),
        in_specs=[pl.BlockSpec((None, TOK_TILE, d), tok),
                  pl.BlockSpec((None, 6, d), lambda bi, ti: (row(bi, ti), 0, 0)),
                  pl.BlockSpec((1, d), lambda bi, ti: (0, 0)),
                  pl.BlockSpec((d, n_out), lambda bi, ti: (0, 0))],
        out_specs=(pl.BlockSpec((None, TOK_TILE, lru_w), tok),
                   pl.BlockSpec((None, TOK_TILE, lru_w), tok),
                   pl.BlockSpec((None, TOK_TILE, n_mla), tok)),
        compiler_params=_cparams(("arbitrary", "arbitrary")),
        name="even_in",
    )(xs, mods, g, w_pad)


def _lru_kernel(xa_ref, ga_ref, cw_ref, cb_ref, wg_ref, br_ref, bi_ref, sp_ref, o_ref, pad_ref, rec_ref, *, ctx_len):
    t, c = xa_ref.shape
    tc = LRU_CHUNK
    halo = SUBLANES
    n_chunks = t // tc
    n_ctx = ctx_len // tc
    groups = tc // SUBLANES

    pad_ref[0:halo, :] = jnp.zeros((halo, c), F32)
    pad_ref[t + halo:t + 2 * halo, :] = jnp.zeros((halo, c), F32)
    pad_ref[halo:t + halo, :] = xa_ref[...]

    rid = lax.broadcasted_iota(I32, (tc, 1), 0)
    sub = rid % SUBLANES

    for d in range(2):
        cw = cw_ref[d]
        cb = cb_ref[d]
        wg = wg_ref[d]
        b_r = br_ref[d]
        b_i = bi_ref[d]
        sp = sp_ref[d]

        def chunk(i, h, d=d, cw=cw, cb=cb, wg=wg, b_r=b_r, b_i=b_i, sp=sp):
            if d == 0:
                ci = i
            else:
                ci = jnp.where(i < n_ctx, n_ctx - 1 - i, n_chunks - 1 - (i - n_ctx))
            r0 = pl.multiple_of(ci * tc, tc)
            win = pad_ref[pl.ds(r0, tc + 2 * halo), :]
            pos = r0 + rid
            is_lat = pos >= ctx_len
            xc = jnp.zeros((tc, c), F32) + cb
            for k in range(CONV_W):
                off = (k - (CONV_W - 1)) if d == 0 else ((CONV_W - 1) - k)
                if off == 0:
                    src = win[halo:halo + tc]
                else:
                    src = pltpu.roll(win, (-off) % (tc + 2 * halo), axis=0)[halo:halo + tc]
                    same = ((pos + off) >= ctx_len) == is_lat
                    src = jnp.where(same, src, 0.0)
                xc = xc + cw[k:k + 1] * src
            gz = _dot(xc.astype(BF16), wg)
            r = jax.nn.sigmoid(gz[:, :c] + b_r)
            gi = jax.nn.sigmoid(gz[:, c:] + b_i)
            log_a = (-LRU_C) * r * sp
            a = jnp.exp(log_a)
            bb = jnp.sqrt(_one_minus_exp(2.0 * log_a, a * a)) * (gi * xc)
            for s in (1, 2, 4):
                if d == 0:
                    ok = sub >= s
                    sh = s
                else:
                    ok = sub <= (SUBLANES - 1 - s)
                    sh = tc - s
                a_prev = jnp.where(ok, pltpu.roll(a, sh, axis=0), 1.0)
                b_prev = jnp.where(ok, pltpu.roll(bb, sh, axis=0), 0.0)
                bb = a * b_prev + bb
                a = a * a_prev
            outs = [None] * groups
            order = range(groups) if d == 0 else range(groups - 1, -1, -1)
            for g in order:
                lo = g * SUBLANES
                hg = a[lo:lo + SUBLANES] * h + bb[lo:lo + SUBLANES]
                outs[g] = hg
                h = hg[SUBLANES - 1:SUBLANES] if d == 0 else hg[0:1]
            hs = jnp.concatenate(outs, axis=0)
            if d == 0:
                rec_ref[pl.ds(r0, tc), :] = hs
            else:
                tot = rec_ref[pl.ds(r0, tc), :] + hs
                gate = jax.nn.gelu(ga_ref[pl.ds(r0, tc), :], approximate=True)
                o_ref[pl.ds(r0, tc), :] = (tot * gate).astype(o_ref.dtype)
            return h

        lax.fori_loop(0, n_chunks, chunk, jnp.zeros((1, c), F32))


def _lru(xa, ga, conv_w, conv_b, w_gates, b_r, b_i, sp, ctx_len):
    b, t, w = xa.shape
    c = 2 * LANES
    nh = w // c
    tok = lambda bi, hi: (bi, 0, hi)
    par = lambda bi, hi: (0, 0, hi)
    return pl.pallas_call(
        functools.partial(_lru_kernel, ctx_len=ctx_len),
        out_shape=jax.ShapeDtypeStruct((b, t, w), BF16),
        grid=(b, nh),
        in_specs=[pl.BlockSpec((None, t, c), tok),
                  pl.BlockSpec((None, t, c), tok),
                  pl.BlockSpec((2, CONV_W, c), par),
                  pl.BlockSpec((2, 1, c), par),
                  pl.BlockSpec((2, None, c, 2 * c), lambda bi, hi: (0, hi, 0, 0)),
                  pl.BlockSpec((2, 1, c), par),
                  pl.BlockSpec((2, 1, c), par),
                  pl.BlockSpec((2, 1, c), par)],
        out_specs=pl.BlockSpec((None, t, c), tok),
        scratch_shapes=[pltpu.VMEM((t + 2 * SUBLANES, c), F32), pltpu.VMEM((t, c), F32)],
        compiler_params=_cparams(("arbitrary", "arbitrary")),
        name="rglru",
    )(xa, ga, conv_w, conv_b, w_gates, b_r, b_i, sp)


def _mla_prep_kernel(in_ref, qan_ref, kvn_ref, wq_ref, wk_ref, wv_ref, mq_ref, mk_ref, gq_ref, gk_ref, gkr_ref,
                     cos_ref, slo_ref, shi_ref, q_ref, k_ref, v_ref, *, q_lora, kv_lora, scale):
    z = in_ref[...]
    heads = q_ref.shape[-1] // LANES
    cos = cos_ref[...]
    slo = slo_ref[...]
    shi = shi_ref[...]
    half = MLA_ROPE // 2

    qan = (_rms(z[:, :q_lora]) * qan_ref[...]).astype(BF16)
    q = _dot(qan, wq_ref[...])
    q = q * lax.rsqrt(_dot((q * q).astype(BF16), mq_ref[...]) + EPS) * gq_ref[...]
    q = _rope(q, _tile_lanes(cos, heads), _tile_lanes(slo, heads), _tile_lanes(shi, heads), half)
    q_ref[...] = (q * scale).astype(BF16)

    kvn = (_rms(z[:, q_lora:q_lora + kv_lora]) * kvn_ref[...]).astype(BF16)
    kk = _dot(kvn, wk_ref[...])
    kk = kk * lax.rsqrt(_dot((kk * kk).astype(BF16), mk_ref[...]) + EPS) * gk_ref[...]
    v_ref[...] = _dot(kvn, wv_ref[...]).astype(BF16)

    kr = z[:, q_lora + kv_lora:]
    kr = kr * lax.rsqrt(jnp.sum(kr * kr, axis=-1, keepdims=True) * (1.0 / MLA_ROPE) + EPS) * gkr_ref[...]
    kr = pltpu.roll(kr, MLA_NOPE, axis=1)
    kr = _rope(kr, cos, slo, shi, half)
    k_ref[...] = (kk + _tile_lanes(kr, heads)).astype(BF16)


def _mla_prep(mla_in, p, scale):
    b, t, w = mla_in.shape
    hq = MLA_HEADS * LANES
    hv = MLA_HEADS * MLA_V
    tok = lambda bi, ti: (bi, ti, 0)
    full = lambda a: pl.BlockSpec(a.shape, lambda bi, ti: (0,) * a.ndim)
    pos = pl.BlockSpec((TOK_TILE, LANES), lambda bi, ti: (ti, 0))
    consts = [p["qan"], p["kvn"], p["wq"], p["wk"], p["wv"], p["mq"], p["mk"], p["gq"], p["gk"], p["gkr"]]
    return pl.pallas_call(
        functools.partial(_mla_prep_kernel, q_lora=p["qan"].shape[1], kv_lora=p["kvn"].shape[1], scale=scale),
        out_shape=(jax.ShapeDtypeStruct((b, t, hq), BF16),
                   jax.ShapeDtypeStruct((b, t, hq), BF16),
                   jax.ShapeDtypeStruct((b, t, hv), BF16)),
        grid=(b, t // TOK_TILE),
        in_specs=[pl.BlockSpec((None, TOK_TILE, w), tok)] + [full(a) for a in consts] + [pos, pos, pos],
        out_specs=(pl.BlockSpec((None, TOK_TILE, hq), tok),
                   pl.BlockSpec((None, TOK_TILE, hq), tok),
                   pl.BlockSpec((None, TOK_TILE, hv), tok)),
        compiler_params=_cparams(("arbitrary", "arbitrary")),
        name="mla_prep",
    )(mla_in, *consts, p["cos"], p["slo"], p["shi"])


def _mla_attn_kernel(q_ref, k_ref, v_ref, o_ref, *, ctx_len):
    tq = q_ref.shape[0]
    t = k_ref.shape[0]
    lane = lax.broadcasted_iota(I32, (tq, LANES), 1)

    def attend(nk):
        v = v_ref[0:nk, :]
        outs = []
        for hh in range(2):
            q = q_ref[:, hh * LANES:(hh + 1) * LANES]
            k = k_ref[0:nk, hh * LANES:(hh + 1) * LANES]
            s = _dot_nt(q, k)
            p = jnp.exp(s - jnp.max(s, axis=-1, keepdims=True))
            l = jnp.sum(p, axis=-1, keepdims=True)
            outs.append(_dot(p.astype(BF16), v) / l)
        o_ref[...] = jnp.where(lane < MLA_V, outs[0], outs[1]).astype(o_ref.dtype)

    @pl.when(pl.program_id(2) == 0)
    def _():
        attend(ctx_len)

    @pl.when(pl.program_id(2) > 0)
    def _():
        attend(t)


def _mla_attn(q, k, v, ctx_len):
    b, t, hq = q.shape
    pairs = hq // (2 * LANES)
    return pl.pallas_call(
        functools.partial(_mla_attn_kernel, ctx_len=ctx_len),
        out_shape=jax.ShapeDtypeStruct((b, t, v.shape[-1]), BF16),
        grid=(b, pairs, t // TOK_TILE),
        in_specs=[pl.BlockSpec((None, TOK_TILE, 2 * LANES), lambda bi, hi, ti: (bi, ti, hi)),
                  pl.BlockSpec((None, t, 2 * LANES), lambda bi, hi, ti: (bi, 0, hi)),
                  pl.BlockSpec((None, t, LANES), lambda bi, hi, ti: (bi, 0, hi))],
        out_specs=pl.BlockSpec((None, TOK_TILE, LANES), lambda bi, hi, ti: (bi, ti, hi)),
        compiler_params=_cparams(("arbitrary", "arbitrary", "arbitrary")),
        name="mla_attn",
    )(q, k, v)


def _odd_in_kernel(x_ref, mod_ref, g_ref, w_ref, m_ref, gq_ref, gk_ref, cos_ref, slo_ref, shi_ref,
                   q_ref, k_ref, v_ref, *, scale):
    mod = mod_ref[...]
    h = _prenorm(x_ref[...], g_ref[...], mod[1:2], mod[0:1]).astype(BF16)
    n = q_ref.shape[-1]
    reps = n // LANES
    cos = _tile_lanes(cos_ref[...], reps)
    slo = _tile_lanes(slo_ref[...], reps)
    shi = _tile_lanes(shi_ref[...], reps)
    half = GQA_DIM // 2
    m = m_ref[...]

    q = _dot(h, w_ref[:, 0:n])
    q = q * lax.rsqrt(_dot((q * q).astype(BF16), m) + EPS) * gq_ref[...]
    q_ref[...] = (_rope(q, cos, slo, shi, half) * scale).astype(BF16)
    k = _dot(h, w_ref[:, n:2 * n])
    k = k * lax.rsqrt(_dot((k * k).astype(BF16), m) + EPS) * gk_ref[...]
    k_ref[...] = _rope(k, cos, slo, shi, half).astype(BF16)
    v_ref[...] = _dot(h, w_ref[:, 2 * n:3 * n]).astype(BF16)


def _odd_in(xs, mods, g, p, scale):
    b, t, d = xs.shape
    n = p["w"].shape[1] // 3
    row = functools.partial(_mod_row, ctx_row=b)
    tok = lambda bi, ti: (bi, ti, 0)
    full = lambda a: pl.BlockSpec(a.shape, lambda bi, ti: (0,) * a.ndim)
    pos = pl.BlockSpec((TOK_TILE, LANES), lambda bi, ti: (ti, 0))
    out = jax.ShapeDtypeStruct((b, t, n), BF16)
    return pl.pallas_call(
        functools.partial(_odd_in_kernel, scale=scale),
        out_shape=(out, out, out),
        grid=(b, t // TOK_TILE),
        in_specs=[pl.BlockSpec((None, TOK_TILE, d), tok),
                  pl.BlockSpec((None, 6, d), lambda bi, ti: (row(bi, ti), 0, 0)),
                  pl.BlockSpec((1, d), lambda bi, ti: (0, 0)),
                  full(p["w"]), full(p["m"]), full(p["gq"]), full(p["gk"]), pos, pos, pos],
        out_specs=(pl.BlockSpec((None, TOK_TILE, n), tok),) * 3,
        compiler_params=_cparams(("arbitrary", "arbitrary")),
        name="odd_in",
    )(xs, mods, g, p["w"], p["m"], p["gq"], p["gk"], p["cos"], p["slo"], p["shi"])


def _win_attn_kernel(sink_ref, q_ref, k_ref, v_ref, o_ref, *, ctx_len):
    t = k_ref.shape[0]
    wb = WIN_BLOCK
    w3 = 3 * wb
    group = GQA_HEADS // GQA_KV_HEADS
    kh = pl.program_id(1)
    i = pl.program_id(2)
    q0 = ctx_len + i * wb
    ws = pl.multiple_of(jnp.clip(q0 - wb, ctx_len, t - w3), wb)
    q = q_ref[...]
    kcat = jnp.concatenate([k_ref[pl.ds(ws, w3), :], k_ref[0:ctx_len, :]], axis=0)
    vcat = jnp.concatenate([v_ref[pl.ds(ws, w3), :], v_ref[0:ctx_len, :]], axis=0)
    nk = w3 + ctx_len
    q_pos = q0 + lax.broadcasted_iota(I32, (wb, nk), 0)
    col = lax.broadcasted_iota(I32, (wb, nk), 1)
    valid = (col >= w3) | (jnp.abs(q_pos - (ws + col)) <= WINDOW)
    head_of_lane = lax.broadcasted_iota(I32, (nk, group * GQA_DIM), 1) // GQA_DIM
    acc = jnp.zeros((wb, group * GQA_DIM), F32)
    for h in range(group):
        sel = head_of_lane == h
        s = _dot_nt(q, jnp.where(sel, kcat, jnp.zeros_like(kcat)))
        s = jnp.where(valid, s, NEG_INF)
        sink = sink_ref[kh * group + h]
        m = jnp.maximum(jnp.max(s, axis=-1, keepdims=True), sink)
        p = jnp.exp(s - m)
        l = jnp.sum(p, axis=-1, keepdims=True) + jnp.exp(sink - m)
        acc = acc + _dot(p.astype(BF16), jnp.where(sel, vcat, jnp.zeros_like(vcat))) / l
    o_ref[...] = acc.astype(o_ref.dtype)


def _win_attn(q, k_rep, v_rep, sink, ctx_len):
    b, t, n = q.shape
    gw = n // GQA_KV_HEADS
    s_len = t - ctx_len
    off = ctx_len // WIN_BLOCK
    return pl.pallas_call(
        functools.partial(_win_attn_kernel, ctx_len=ctx_len),
        out_shape=jax.ShapeDtypeStruct((b, s_len, n), BF16),
        grid=(b, GQA_KV_HEADS, s_len // WIN_BLOCK),
        in_specs=[pl.BlockSpec(memory_space=pltpu.SMEM),
                  pl.BlockSpec((None, WIN_BLOCK, gw), lambda bi, hi, ti: (bi, ti + off, hi)),
                  pl.BlockSpec((None, t, gw), lambda bi, hi, ti: (bi, 0, hi)),
                  pl.BlockSpec((None, t, gw), lambda bi, hi, ti: (bi, 0, hi))],
        out_specs=pl.BlockSpec((None, WIN_BLOCK, gw), lambda bi, hi, ti: (bi, ti, hi)),
        compiler_params=_cparams(("arbitrary", "arbitrary", "arbitrary")),
        name="win_attn",
    )(sink, q, k_rep, v_rep)


def _post_mix_kernel(a1_ref, a2_ref, x_ref, mod_ref, g_ref, w_ref, wr_ref, br_ref,
                     xo_ref, f_ref, route_ref, gate_ref, cnt_ref, seen_ref):
    half = a1_ref.shape[-1]
    mod = mod_ref[...]
    m = _dot(a1_ref[...], w_ref[0:half, :]) + _dot(a2_ref[...], w_ref[half:2 * half, :])
    x = x_ref[...] + mod[2:3] * m
    xo_ref[...] = x
    f = _prenorm(x, g_ref[...], mod[4:5], mod[3:4])
    f_ref[...] = f

    logit = _dot3(f, wr_ref[...]) + br_ref[...]
    tm = logit.shape[0]
    lane = lax.broadcasted_iota(I32, (tm, LANES), 1)
    lane_f = lane.astype(F32)
    vals, idxs = [], []
    for _ in range(TOP_K):
        mx = jnp.max(logit, axis=-1, keepdims=True)
        ix = jnp.min(jnp.where(logit == mx, lane_f, float(LANES)), axis=-1, keepdims=True)
        vals.append(mx)
        idxs.append(ix)
        logit = jnp.where(lane_f == ix, -jnp.inf, logit)
    exps = [jnp.exp(v - vals[0]) for v in vals]
    den = exps[0]
    for e in exps[1:]:
        den = den + e

    @pl.when((pl.program_id(0) == 0) & (pl.program_id(1) == 0))
    def _():
        seen_ref[...] = jnp.zeros(seen_ref.shape, F32)

    msk = jnp.zeros((tm, LANES), F32)
    for k in range(TOP_K):
        msk = jnp.where(lane_f == idxs[k], 1.0, msk)
    earlier = (lax.broadcasted_iota(I32, (tm, tm), 1) < lax.broadcasted_iota(I32, (tm, tm), 0))
    rank = _dot(jnp.where(earlier, 1.0, 0.0).astype(BF16), msk.astype(BF16)) + seen_ref[0:1, :]
    seen = seen_ref[...] + jnp.sum(msk, axis=0, keepdims=True)
    seen_ref[...] = seen
    cnt_ref[...] = seen

    r_out = jnp.zeros((tm, LANES), F32)
    g_out = jnp.zeros((tm, LANES), F32)
    for k in range(TOP_K):
        rank_k = jnp.sum(jnp.where(lane_f == idxs[k], rank, 0.0), axis=-1, keepdims=True)
        r_out = jnp.where(lane == k, idxs[k], r_out)
        r_out = jnp.where(lane == TOP_K + k, rank_k, r_out)
        g_out = jnp.where(lane == k, exps[k] / den, g_out)
    route_ref[...] = r_out.astype(I32)
    gate_ref[...] = g_out


def _post_mix(a1, a2, lane_blk2, xs, mods, g, w_out, w_router, b_router, t_off, a_off):
    b, t, d = xs.shape
    half = w_out.shape[0] // 2
    nt = t // TOK_TILE - t_off
    t_out = nt * TOK_TILE
    row = functools.partial(_mod_row, ctx_row=b)
    tok = lambda bi, ti: (bi, ti, 0)
    act = jax.ShapeDtypeStruct((b, t_out, d), F32)
    return pl.pallas_call(
        _post_mix_kernel,
        out_shape=(act, act,
                   jax.ShapeDtypeStruct((b, t_out, LANES), I32),
                   jax.ShapeDtypeStruct((b, t_out, LANES), F32),
                   jax.ShapeDtypeStruct((SUBLANES, LANES), F32)),
        grid=(b, nt),
        in_specs=[pl.BlockSpec((None, TOK_TILE, half), lambda bi, ti: (bi, ti + a_off, 0)),
                  pl.BlockSpec((None, TOK_TILE, half), lambda bi, ti: (bi, ti + a_off, lane_blk2)),
                  pl.BlockSpec((None, TOK_TILE, d), lambda bi, ti: (bi, ti + t_off, 0)),
                  pl.BlockSpec((None, 6, d), lambda bi, ti: (row(bi, ti + t_off), 0, 0)),
                  pl.BlockSpec((1, d), lambda bi, ti: (0, 0)),
                  pl.BlockSpec(w_out.shape, lambda bi, ti: (0, 0)),
                  pl.BlockSpec(w_router.shape, lambda bi, ti: (0, 0)),
                  pl.BlockSpec((1, LANES), lambda bi, ti: (0, 0))],
        out_specs=(pl.BlockSpec((None, TOK_TILE, d), tok),
                   pl.BlockSpec((None, TOK_TILE, d), tok),
                   pl.BlockSpec((None, TOK_TILE, LANES), tok),
                   pl.BlockSpec((None, TOK_TILE, LANES), tok),
                   pl.BlockSpec((SUBLANES, LANES), lambda bi, ti: (0, 0))),
        scratch_shapes=[pltpu.VMEM((SUBLANES, LANES), F32)],
        compiler_params=_cparams(("arbitrary", "arbitrary")),
        name="post_mix",
    )(a1, a2, xs, mods, g, w_out, w_router, b_router)


DMA_UNROLL = 2


def _row(ref, i):
    return ref.at[pl.ds(i, 1), :]


def _dispatch_kernel(dest_ref, f_ref, xs_in_ref, xs_ref, sem):
    del xs_in_ref
    n = dest_ref.shape[0]

    def issue(t, c):
        for k in range(TOP_K):
            pltpu.make_async_copy(_row(f_ref, t), _row(xs_ref, dest_ref[t * TOP_K + k]), sem).start()
        return c

    lax.fori_loop(0, n // TOP_K, issue, 0, unroll=DMA_UNROLL)
    pltpu.make_async_copy(xs_ref.at[pl.ds(0, n), :], xs_ref.at[pl.ds(0, n), :], sem).wait()


def _dispatch(f, dest, rows):
    n, d = f.shape
    per = TOK_TILE * TOP_K
    xs0 = jnp.zeros((rows, d), F32)
    return pl.pallas_call(
        _dispatch_kernel,
        out_shape=jax.ShapeDtypeStruct((rows, d), F32),
        grid=(n // TOK_TILE,),
        in_specs=[pl.BlockSpec((per,), lambda i: (i,), memory_space=pltpu.SMEM),
                  pl.BlockSpec((TOK_TILE, d), lambda i: (i, 0)),
                  pl.BlockSpec(memory_space=pl.ANY)],
        out_specs=pl.BlockSpec(memory_space=pl.ANY),
        scratch_shapes=[pltpu.SemaphoreType.DMA(())],
        input_output_aliases={2: 0},
        compiler_params=_cparams(("arbitrary",), has_side_effects=True, disable_bounds_checks=True),
        name="moe_dispatch",
    )(dest, f, xs0)


def _experts_kernel(be_ref, nu_ref, xs_ref, wgu_ref, bgu_ref, wdn_ref, bdn_ref, y_ref, wgu_bf, wdn_bf):
    i = pl.program_id(0)

    @pl.when((i == 0) | (be_ref[i] != be_ref[jnp.maximum(i - 1, 0)]))
    def _():
        wgu_bf[...] = wgu_ref[...].astype(BF16)
        wdn_bf[...] = wdn_ref[...].astype(BF16)

    @pl.when(i < nu_ref[0])
    def _():
        h = _dot(xs_ref[...].astype(BF16), wgu_bf[...]) + bgu_ref[...]
        ff = h.shape[1] // 2
        hg = jnp.minimum(h[:, :ff], SWIGLU_LIMIT)
        hu = jnp.clip(h[:, ff:], -SWIGLU_LIMIT, SWIGLU_LIMIT)
        act = hg * jax.nn.sigmoid(SWIGLU_ALPHA * hg) * (hu + 1.0)
        y_ref[...] = _dot(act.astype(BF16), wdn_bf[...]) + bdn_ref[...]

    @pl.when(i >= nu_ref[0])
    def _():
        y_ref[...] = jnp.zeros(y_ref.shape, F32)


def _experts(xs, block_e, n_used, layer, wgu, bgu, wdn, bdn):
    rows, d = xs.shape
    ff2 = wgu.shape[-1]
    nb = rows // MOE_ROWS
    return pl.pallas_call(
        _experts_kernel,
        out_shape=jax.ShapeDtypeStruct((rows, d), F32),
        grid_spec=pltpu.PrefetchScalarGridSpec(
            num_scalar_prefetch=2,
            grid=(nb,),
            in_specs=[pl.BlockSpec((MOE_ROWS, d), lambda i, be, nu: (i, 0)),
                      pl.BlockSpec((None, None, d, ff2), lambda i, be, nu: (layer, be[i], 0, 0)),
                      pl.BlockSpec((None, None, 1, ff2), lambda i, be, nu: (layer, be[i], 0, 0)),
                      pl.BlockSpec((None, None, ff2 // 2, d), lambda i, be, nu: (layer, be[i], 0, 0)),
                      pl.BlockSpec((None, None, 1, d), lambda i, be, nu: (layer, be[i], 0, 0))],
            out_specs=pl.BlockSpec((MOE_ROWS, d), lambda i, be, nu: (i, 0)),
            scratch_shapes=[pltpu.VMEM((d, ff2), BF16), pltpu.VMEM((ff2 // 2, d), BF16)]),
        compiler_params=_cparams(("arbitrary",)),
        name="moe_experts",
    )(block_e, n_used, xs, wgu, bgu, wdn, bdn)


def _combine_kernel(dest_ref, y_ref, gate_ref, x_ref, mod_ref, o_ref, buf_ref, sem):
    n = dest_ref.shape[0]
    tm = n // TOP_K

    def issue(t, c):
        for k in range(TOP_K):
            pltpu.make_async_copy(_row(y_ref, dest_ref[t * TOP_K + k]), _row(buf_ref.at[k], t), sem).start()
        return c

    lax.fori_loop(0, tm, issue, 0, unroll=DMA_UNROLL)
    for k in range(TOP_K):
        pltpu.make_async_copy(y_ref.at[pl.ds(0, tm), :], buf_ref.at[k], sem).wait()

    gates = gate_ref[...]
    acc = gates[:, 0:1] * buf_ref[0]
    for k in range(1, TOP_K):
        acc = acc + gates[:, k:k + 1] * buf_ref[k]
    o_ref[...] = x_ref[...] + mod_ref[5:6, :] * acc


def _combine(y, dest, gates, xs, mods, t_off):
    b, t_out, d = xs.shape
    nt = t_out // TOK_TILE
    per = TOK_TILE * TOP_K
    row = functools.partial(_mod_row, ctx_row=b)
    tok = lambda bi, ti: (bi, ti, 0)
    return pl.pallas_call(
        _combine_kernel,
        out_shape=jax.ShapeDtypeStruct((b, t_out, d), F32),
        grid=(b, nt),
        in_specs=[pl.BlockSpec((per,), lambda bi, ti: (bi * nt + ti,), memory_space=pltpu.SMEM),
                  pl.BlockSpec(memory_space=pl.ANY),
                  pl.BlockSpec((None, TOK_TILE, LANES), tok),
                  pl.BlockSpec((None, TOK_TILE, d), tok),
                  pl.BlockSpec((None, 6, d), lambda bi, ti: (row(bi, ti + t_off), 0, 0))],
        out_specs=pl.BlockSpec((None, TOK_TILE, d), tok),
        scratch_shapes=[pltpu.VMEM((TOP_K, TOK_TILE, d), F32), pltpu.SemaphoreType.DMA(())],
        compiler_params=_cparams(("arbitrary", "arbitrary"), disable_bounds_checks=True),
        name="moe_combine",
    )(dest, y, gates, xs, mods)


def _routing(e_sel, rank, counts, n):
    padded = (counts + MOE_ROWS - 1) // MOE_ROWS * MOE_ROWS
    pend = jnp.cumsum(padded)
    pstart = pend - padded
    dest = pstart[e_sel] + rank
    n_blocks = n * TOP_K // MOE_ROWS + N_EXPERTS
    block_e = jnp.minimum(jnp.searchsorted(pend, jnp.arange(n_blocks) * MOE_ROWS, side="right"), N_EXPERTS - 1)
    n_used = (pend[-1] // MOE_ROWS).reshape(1)
    return dest.reshape(-1).astype(I32), block_e.astype(I32), n_used.astype(I32), n_blocks * MOE_ROWS


def _moe(f, route, gates, seen, x_mid, mods, t_off, layer, wgu, bgu, wdn, bdn):
    b, t_out, d = f.shape
    n = b * t_out
    route = route.reshape(n, LANES)
    counts = seen[0, :N_EXPERTS].astype(I32)
    dest, block_e, n_used, rows = _routing(route[:, :TOP_K], route[:, TOP_K:2 * TOP_K], counts, n)
    xs = _dispatch(f.reshape(n, d), dest, rows)
    y = _experts(xs, block_e, n_used, layer, wgu, bgu, wdn, bdn)
    return _combine(y, dest, gates, x_mid, mods, t_off)


def _axial_angles(n_rows, rot_dim):
    row = jnp.repeat(jnp.arange(n_rows, dtype=F32), GRID_W)
    col = jnp.tile(jnp.arange(GRID_W, dtype=F32), n_rows)
    n = rot_dim // 4
    freqs = ROPE_THETA ** (-jnp.arange(n, dtype=F32) / n)
    return jnp.concatenate([row[:, None] * freqs, col[:, None] * freqs], axis=-1)


def _rope_tables(ctx_len, s_len, rot_dim, lane_base, reps):
    ang = _axial_angles(s_len // GRID_W, rot_dim)
    half = rot_dim // 2
    period = LANES // reps
    cos = jnp.ones((s_len, period), F32)
    cos = cos.at[:, lane_base:lane_base + rot_dim].set(jnp.tile(jnp.cos(ang), (1, 2)))
    slo = jnp.zeros((s_len, period), F32).at[:, lane_base:lane_base + half].set(-jnp.sin(ang))
    shi = jnp.zeros((s_len, period), F32).at[:, lane_base + half:lane_base + rot_dim].set(jnp.sin(ang))
    ident = [jnp.ones((ctx_len, period), F32), jnp.zeros((ctx_len, period), F32), jnp.zeros((ctx_len, period), F32)]
    return [jnp.tile(jnp.concatenate([i, tb], axis=0), (1, reps)) for i, tb in zip(ident, (cos, slo, shi))]


def _segment_mean_matrix(width, segs):
    lane = jnp.arange(width)
    seg_id = jnp.full((width,), -1, I32)
    seg_w = jnp.zeros((width,), F32)
    for start, length, period in segs:
        inside = ((lane % period) >= start) & ((lane % period) < start + length)
        seg_id = jnp.where(inside, (lane // period) * 8 + start // 32, seg_id)
        seg_w = jnp.where(inside, 1.0 / length, seg_w)
    same = (seg_id[:, None] == seg_id[None, :]) & (seg_id[:, None] >= 0)
    return jnp.where(same, seg_w[None, :], 0.0).astype(BF16)


def _block_diag(w):
    n, c, _ = w.shape
    eye = jnp.eye(n, dtype=w.dtype)
    return (eye[:, None, :, None] * w[:, :, None, :]).reshape(n * c, n * c)


def kernel(x, c, ctx, c_ctx, w_mod, b_mod, norm_mix, norm_ffn, w_in_even, lru_conv_w, lru_conv_b, lru_w_r, lru_b_r, lru_w_i, lru_b_i, lru_lambda, mla_q_a_norm, mla_w_q_b, mla_kv_a_norm, mla_w_kv_b, mla_nope_norm, mla_rope_norm, w_out_even, w_qkv_odd, gqa_qk_norm, gqa_sink, w_out_odd, w_router, b_router, w_gate_up, b_gate_up, w_down, b_down):
    b, s_len, d = x.shape
    ctx_len = ctx.shape[1]
    depth = w_mod.shape[0]
    assert depth == 2 and ctx_len == TOK_TILE and s_len % TOK_TILE == 0 and b + 1 <= SUBLANES
    lru_w = lru_conv_w.shape[-1]
    q_lora = mla_q_a_norm.shape[-1]
    kv_lora = mla_kv_a_norm.shape[-1]

    xs = jnp.concatenate([ctx, x], axis=1)
    cvec = jnp.concatenate([c, c_ctx[None], jnp.zeros((SUBLANES - b - 1, d), F32)], axis=0)
    mods = _modulation(cvec, w_mod, b_mod).reshape(depth, SUBLANES, 6, d)

    wr_pad = jnp.zeros((depth, d, LANES), F32).at[:, :, :N_EXPERTS].set(w_router)
    br_pad = jnp.full((depth, 1, LANES), NEG_INF, F32).at[:, 0, :N_EXPERTS].set(b_router)
    experts = (w_gate_up, b_gate_up[:, :, None, :], w_down, b_down[:, :, None, :])

    n_in = w_in_even.shape[-1]
    n_in_pad = -(-n_in // LANES) * LANES
    w_in = jnp.zeros((d, n_in_pad), F32).at[:, :n_in].set(w_in_even[0]).astype(BF16)
    xa, ga, mla_in = _even_in(xs, mods[0], norm_mix[0][None], w_in, lru_w)

    nh = lru_w // (2 * LANES)
    per = LRU_BLOCKS // nh
    blk = lru_w // LRU_BLOCKS
    w_gates = jnp.stack([
        jnp.stack([jnp.concatenate([_block_diag(lru_w_r[0, dd, h * per:(h + 1) * per]),
                                    _block_diag(lru_w_i[0, dd, h * per:(h + 1) * per])], axis=1)
                   for h in range(nh)]) for dd in range(2)]).astype(BF16)
    assert blk * per == 2 * LANES
    ya = _lru(xa, ga, lru_conv_w[0], lru_conv_b[0][:, None, :], w_gates, lru_b_r[0][:, None, :],
              lru_b_i[0][:, None, :], jax.nn.softplus(-lru_lambda[0])[:, None, :], ctx_len)

    qk = MLA_NOPE + MLA_ROPE
    hq = MLA_HEADS * LANES
    wq = jnp.zeros((q_lora, MLA_HEADS, LANES), F32).at[:, :, :qk].set(
        mla_w_q_b[0].reshape(q_lora, MLA_HEADS, qk)).reshape(q_lora, hq).astype(BF16)
    wkv = mla_w_kv_b[0].reshape(kv_lora, MLA_HEADS, MLA_NOPE + MLA_V)
    wk = jnp.zeros((kv_lora, MLA_HEADS, LANES), F32).at[:, :, :MLA_NOPE].set(
        wkv[:, :, :MLA_NOPE]).reshape(kv_lora, hq).astype(BF16)
    wv = wkv[:, :, MLA_NOPE:].reshape(kv_lora, MLA_HEADS * MLA_V).astype(BF16)
    zpad = jnp.zeros((LANES - qk,), F32)
    gq = jnp.tile(jnp.concatenate([mla_nope_norm[0, 0], mla_rope_norm[0, 0], zpad]), MLA_HEADS)[None]
    gk = jnp.tile(jnp.concatenate([mla_nope_norm[0, 1], jnp.zeros((LANES - MLA_NOPE,), F32)]), MLA_HEADS)[None]
    gkr = jnp.concatenate([mla_rope_norm[0, 1], jnp.zeros((LANES - MLA_ROPE,), F32)])[None]
    cos, slo, shi = _rope_tables(ctx_len, s_len, MLA_ROPE, MLA_NOPE, 1)
    mla_p = dict(qan=mla_q_a_norm[0][None], kvn=mla_kv_a_norm[0][None], wq=wq, wk=wk, wv=wv,
                 mq=_segment_mean_matrix(hq, [(0, MLA_NOPE, LANES), (MLA_NOPE, MLA_ROPE, LANES)]),
                 mk=_segment_mean_matrix(hq, [(0, MLA_NOPE, LANES)]),
                 gq=gq, gk=gk, gkr=gkr, cos=cos, slo=slo, shi=shi)
    q, k, v = _mla_prep(mla_in, mla_p, qk ** -0.5)
    yb = _mla_attn(q, k, v, ctx_len)

    x_mid, f, route, gates, seen = _post_mix(ya, yb, 0, xs, mods[0], norm_ffn[0][None], w_out_even[0].astype(BF16),
                                             wr_pad[0], br_pad[0], 0, 0)
    xs = _moe(f, route, gates, seen, x_mid, mods[0], 0, 0, *experts)

    group = GQA_HEADS // GQA_KV_HEADS
    nq = GQA_HEADS * GQA_DIM
    nkv = GQA_KV_HEADS * GQA_DIM
    wqkv = w_qkv_odd[0]
    rep = lambda w: jnp.tile(w.reshape(d, GQA_KV_HEADS, 1, GQA_DIM), (1, 1, group, 1)).reshape(d, nq)
    w_odd = jnp.concatenate([wqkv[:, :nq], rep(wqkv[:, nq:nq + nkv]), rep(wqkv[:, nq + nkv:])], axis=1).astype(BF16)
    cos, slo, shi = _rope_tables(ctx_len, s_len, GQA_DIM, 0, LANES // GQA_DIM)
    odd_p = dict(w=w_odd, m=_segment_mean_matrix(nq, [(0, GQA_DIM, GQA_DIM)]),
                 gq=jnp.tile(gqa_qk_norm[0, 0], GQA_HEADS)[None], gk=jnp.tile(gqa_qk_norm[0, 1], GQA_HEADS)[None],
                 cos=cos, slo=slo, shi=shi)
    q, k_rep, v_rep = _odd_in(xs, mods[1], norm_mix[1][None], odd_p, GQA_DIM ** -0.5)
    o = _win_attn(q, k_rep, v_rep, gqa_sink[0], ctx_len)

    t_off = ctx_len // TOK_TILE
    x_mid, f, route, gates, seen = _post_mix(o, o, 1, xs, mods[1], norm_ffn[1][None], w_out_odd[0].astype(BF16),
                                             wr_pad[1], br_pad[1], t_off, 0)
    return _moe(f, route, gates, seen, x_mid, mods[1], t_off, 1, *experts)
```

```python
import functools
import math

import jax
import jax.numpy as jnp
import numpy as np
from jax import lax
from jax.experimental import pallas as pl
from jax.experimental.pallas import tpu as pltpu

F32 = jnp.float32
BF16 = jnp.bfloat16
I32 = jnp.int32

GRID_W = 64
LRU_BLOCKS = 8
LRU_C = 8.0
CONV_W = 4
MLA_HEADS = 8
MLA_NOPE = 64
MLA_ROPE = 32
MLA_V = 64
GQA_HEADS = 16
GQA_KV_HEADS = 4
GQA_DIM = 64
WINDOW = 128
ROPE_THETA = 10000.0
NEG_INF = -1e30
EPS = 1e-6
N_EXPERTS = 32
TOP_K = 4
SWIGLU_LIMIT = 7.0
SWIGLU_ALPHA = 1.702

LANES = 128
SUBLANES = 8
TOK_TILE = 256
LRU_CHUNK = 128
WIN_BLOCK = 128
MOE_ROWS = 256
VMEM_LIMIT = 48 * 1024 * 1024


def _cparams(sem, **kw):
    return pltpu.CompilerParams(dimension_semantics=sem, vmem_limit_bytes=VMEM_LIMIT, **kw)


def _dot(a, b):
    return jnp.dot(a, b, preferred_element_type=F32)


def _dot_nt(a, b):
    return lax.dot_general(a, b, (((1,), (1,)), ((), ())), preferred_element_type=F32)


def _split_bf16(x):
    hi = x.astype(BF16)
    lo = (x - hi.astype(F32)).astype(BF16)
    return hi, lo


def _dot3(a, w):
    ah, al = _split_bf16(a)
    wh, wl = _split_bf16(w)
    return _dot(ah, wh) + _dot(al, wh) + _dot(ah, wl)


def _rms(x):
    return x * lax.rsqrt(jnp.mean(x * x, axis=-1, keepdims=True) + EPS)


def _prenorm(x, g, scale, shift):
    return (_rms(x) * g) * (1.0 + scale) + shift


def _rope(x, cos, sin_lo, sin_hi, half):
    w = x.shape[-1]
    return x * cos + pltpu.roll(x, w - half, axis=1) * sin_lo + pltpu.roll(x, half, axis=1) * sin_hi


def _one_minus_exp(t, exp_t):
    series = -t * (1.0 + t * (1 / 2) * (1.0 + t * (1 / 3) * (1.0 + t * (1 / 4) * (1.0 + t * (1 / 5) * (1.0 + t * (1 / 6))))))
    return jnp.where(t > -0.25, series, 1.0 - exp_t)


def _tile_lanes(t, reps):
    return jnp.concatenate([t] * reps, axis=1) if reps > 1 else t


def _mod_kernel(c_ref, w_ref, b_ref, o_ref):
    c = c_ref[...]
    o_ref[...] = _dot3(c * jax.nn.sigmoid(c), w_ref[...]) + b_ref[...]


def _modulation(cvec, w_mod, b_mod):
    depth, d, n = w_mod.shape
    tn = 1536
    return pl.pallas_call(
        _mod_kernel,
        out_shape=jax.ShapeDtypeStruct((depth, SUBLANES, n), F32),
        grid=(depth, n // tn),
        in_specs=[pl.BlockSpec((SUBLANES, d), lambda l, j: (0, 0)),
                  pl.BlockSpec((None, d, tn), lambda l, j: (l, 0, j)),
                  pl.BlockSpec((None, 1, tn), lambda l, j: (l, 0, j))],
        out_specs=pl.BlockSpec((None, SUBLANES, tn), lambda l, j: (l, 0, j)),
        compiler_params=_cparams(("arbitrary", "arbitrary")),
        name="modulation",
    )(cvec, w_mod, b_mod.reshape(depth, 1, n))


def _mod_row(b, t, ctx_row):
    return jnp.where(t == 0, ctx_row, b)


def _even_in_kernel(x_ref, mod_ref, g_ref, w_ref, xa_ref, ga_ref, mla_ref):
    mod = mod_ref[...]
    h = _prenorm(x_ref[...], g_ref[...], mod[1:2], mod[0:1])
    z = _dot(h.astype(BF16), w_ref[...])
    c = xa_ref.shape[-1]
    xa_ref[...] = z[:, :c]
    ga_ref[...] = z[:, c:2 * c]
    mla_ref[...] = z[:, 2 * c:]


def _even_in(xs, mods, g, w_pad, lru_w):
    b, t, d = xs.shape
    nt = t // TOK_TILE
    n_out = w_pad.shape[1]
    n_mla = n_out - 2 * lru_w
    row = functools.partial(_mod_row, ctx_row=b)
    tok = lambda bi, ti: (bi, ti, 0)
    return pl.pallas_call(
        _even_in_kernel,
        out_shape=(jax.ShapeDtypeStruct((b, t, lru_w), F32),
                   jax.ShapeDtypeStruct((b, t, lru_w), F32),
                   jax.ShapeDtypeStruct((b, t, n_mla), F32)),
        grid=(b, nt),
        in_specs=[pl.BlockSpec((None, TOK_TILE, d), tok),
                  pl.BlockSpec((None, 6, d), lambda bi, ti: (row(bi, ti), 0, 0)),
                  pl.BlockSpec((1, d), lambda bi, ti: (0, 0)),
                  pl.BlockSpec((d, n_out), lambda bi, ti: (0, 0))],
        out_specs=(pl.BlockSpec((None, TOK_TILE, lru_w), tok),
                   pl.BlockSpec((None, TOK_TILE, lru_w), tok),
                   pl.BlockSpec((None, TOK_TILE, n_mla), tok)),
        compiler_params=_cparams(("arbitrary", "arbitrary")),
        name="even_in",
    )(xs, mods, g, w_pad)


def _lru_kernel(xa_ref, ga_ref, cw_ref, cb_ref, wg_ref, br_ref, bi_ref, sp_ref, o_ref, pad_ref, rec_ref, *, ctx_len):
    t, c = xa_ref.shape
    tc = LRU_CHUNK
    halo = SUBLANES
    n_chunks = t // tc
    n_ctx = ctx_len // tc
    groups = tc // SUBLANES

    pad_ref[0:halo, :] = jnp.zeros((halo, c), F32)
    pad_ref[t + halo:t + 2 * halo, :] = jnp.zeros((halo, c), F32)
    pad_ref[halo:t + halo, :] = xa_ref[...]

    rid = lax.broadcasted_iota(I32, (tc, 1), 0)
    sub = rid % SUBLANES

    for d in range(2):
        cw = cw_ref[d]
        cb = cb_ref[d]
        wg = wg_ref[d]
        b_r = br_ref[d]
        b_i = bi_ref[d]
        sp = sp_ref[d]

        def chunk(i, h, d=d, cw=cw, cb=cb, wg=wg, b_r=b_r, b_i=b_i, sp=sp):
            if d == 0:
                ci = i
            else:
                ci = jnp.where(i < n_ctx, n_ctx - 1 - i, n_chunks - 1 - (i - n_ctx))
            r0 = pl.multiple_of(ci * tc, tc)
            win = pad_ref[pl.ds(r0, tc + 2 * halo), :]
            pos = r0 + rid
            is_lat = pos >= ctx_len
            xc = jnp.zeros((tc, c), F32) + cb
            for k in range(CONV_W):
                off = (k - (CONV_W - 1)) if d == 0 else ((CONV_W - 1) - k)
                if off == 0:
                    src = win[halo:halo + tc]
                else:
                    src = pltpu.roll(win, (-off) % (tc + 2 * halo), axis=0)[halo:halo + tc]
                    same = ((pos + off) >= ctx_len) == is_lat
                    src = jnp.where(same, src, 0.0)
                xc = xc + cw[k:k + 1] * src
            gz = _dot(xc.astype(BF16), wg)
            r = jax.nn.sigmoid(gz[:, :c] + b_r)
            gi = jax.nn.sigmoid(gz[:, c:] + b_i)
            log_a = (-LRU_C) * r * sp
            a = jnp.exp(log_a)
            bb = jnp.sqrt(_one_minus_exp(2.0 * log_a, a * a)) * (gi * xc)
            for s in (1, 2, 4):
                if d == 0:
                    ok = sub >= s
                    sh = s
                else:
                    ok = sub <= (SUBLANES - 1 - s)
                    sh = tc - s
                a_prev = jnp.where(ok, pltpu.roll(a, sh, axis=0), 1.0)
                b_prev = jnp.where(ok, pltpu.roll(bb, sh, axis=0), 0.0)
                bb = a * b_prev + bb
                a = a * a_prev
            outs = [None] * groups
            order = range(groups) if d == 0 else range(groups - 1, -1, -1)
            for g in order:
                lo = g * SUBLANES
                hg = a[lo:lo + SUBLANES] * h + bb[lo:lo + SUBLANES]
                outs[g] = hg
                h = hg[SUBLANES - 1:SUBLANES] if d == 0 else hg[0:1]
            hs = jnp.concatenate(outs, axis=0)
            if d == 0:
                rec_ref[pl.ds(r0, tc), :] = hs
            else:
                tot = rec_ref[pl.ds(r0, tc), :] + hs
                gate = jax.nn.gelu(ga_ref[pl.ds(r0, tc), :], approximate=True)
                o_ref[pl.ds(r0, tc), :] = (tot * gate).astype(o_ref.dtype)
            return h

        lax.fori_loop(0, n_chunks, chunk, jnp.zeros((1, c), F32))


def _lru(xa, ga, conv_w, conv_b, w_gates, b_r, b_i, sp, ctx_len):
    b, t, w = xa.shape
    c = 2 * LANES
    nh = w // c
    tok = lambda bi, hi: (bi, 0, hi)
    par = lambda bi, hi: (0, 0, hi)
    return pl.pallas_call(
        functools.partial(_lru_kernel, ctx_len=ctx_len),
        out_shape=jax.ShapeDtypeStruct((b, t, w), BF16),
        grid=(b, nh),
        in_specs=[pl.BlockSpec((None, t, c), tok),
                  pl.BlockSpec((None, t, c), tok),
                  pl.BlockSpec((2, CONV_W, c), par),
                  pl.BlockSpec((2, 1, c), par),
                  pl.BlockSpec((2, None, c, 2 * c), lambda bi, hi: (0, hi, 0, 0)),
                  pl.BlockSpec((2, 1, c), par),
                  pl.BlockSpec((2, 1, c), par),
                  pl.BlockSpec((2, 1, c), par)],
        out_specs=pl.BlockSpec((None, t, c), tok),
        scratch_shapes=[pltpu.VMEM((t + 2 * SUBLANES, c), F32), pltpu.VMEM((t, c), F32)],
        compiler_params=_cparams(("arbitrary", "arbitrary")),
        name="rglru",
    )(xa, ga, conv_w, conv_b, w_gates, b_r, b_i, sp)


def _mla_prep_kernel(in_ref, qan_ref, kvn_ref, wq_ref, wk_ref, wv_ref, mq_ref, mk_ref, gq_ref, gk_ref, gkr_ref,
                     one_ref, cos_ref, slo_ref, shi_ref, q_ref, k_ref, v_ref, *, q_lora, kv_lora, scale):
    z = in_ref[...]
    heads = q_ref.shape[-1] // LANES
    cos = cos_ref[...]
    slo = slo_ref[...]
    shi = shi_ref[...]
    half = MLA_ROPE // 2

    qan = (_rms(z[:, :q_lora]) * qan_ref[...]).astype(BF16)
    q = _dot(qan, wq_ref[...])
    q = q * lax.rsqrt(_dot((q * q).astype(BF16), mq_ref[...]) + EPS) * gq_ref[...]
    q = _rope(q, _tile_lanes(cos, heads), _tile_lanes(slo, heads), _tile_lanes(shi, heads), half)
    q_ref[...] = (q * scale).astype(BF16)

    kvn = (_rms(z[:, q_lora:q_lora + kv_lora]) * kvn_ref[...]).astype(BF16)
    kk = _dot(kvn, wk_ref[...])
    kk = kk * lax.rsqrt(_dot((kk * kk).astype(BF16), mk_ref[...]) + EPS) * gk_ref[...]
    v_ref[...] = (_dot(kvn, wv_ref[...]) + one_ref[...]).astype(BF16)

    kr = z[:, q_lora + kv_lora:]
    kr = kr * lax.rsqrt(jnp.sum(kr * kr, axis=-1, keepdims=True) * (1.0 / MLA_ROPE) + EPS) * gkr_ref[...]
    kr = pltpu.roll(kr, MLA_NOPE, axis=1)
    kr = _rope(kr, cos, slo, shi, half)
    k_ref[...] = (kk + _tile_lanes(kr, heads)).astype(BF16)


def _mla_prep(mla_in, p, scale):
    b, t, w = mla_in.shape
    hq = MLA_HEADS * LANES
    tok = lambda bi, ti: (bi, ti, 0)
    full = lambda a: pl.BlockSpec(a.shape, lambda bi, ti: (0,) * a.ndim)
    pos = pl.BlockSpec((TOK_TILE, LANES), lambda bi, ti: (ti, 0))
    consts = [p["qan"], p["kvn"], p["wq"], p["wk"], p["wv"], p["mq"], p["mk"], p["gq"], p["gk"], p["gkr"], p["one"]]
    out = jax.ShapeDtypeStruct((b, t, hq), BF16)
    return pl.pallas_call(
        functools.partial(_mla_prep_kernel, q_lora=p["qan"].shape[1], kv_lora=p["kvn"].shape[1], scale=scale),
        out_shape=(out, out, out),
        grid=(b, t // TOK_TILE),
        in_specs=[pl.BlockSpec((None, TOK_TILE, w), tok)] + [full(a) for a in consts] + [pos, pos, pos],
        out_specs=(pl.BlockSpec((None, TOK_TILE, hq), tok),) * 3,
        compiler_params=_cparams(("arbitrary", "arbitrary")),
        name="mla_prep",
    )(mla_in, *consts, p["cos"], p["slo"], p["shi"])


def _mla_attn_kernel(q_ref, k_ref, v_ref, o_ref, *, ctx_len):
    tq = q_ref.shape[0]
    t = k_ref.shape[0]
    lane = lax.broadcasted_iota(I32, (tq, LANES), 1)

    def attend(nk):
        outs = []
        for hh in range(2):
            q = q_ref[:, hh * LANES:(hh + 1) * LANES]
            k = k_ref[0:nk, hh * LANES:(hh + 1) * LANES]
            s = _dot_nt(q, k)
            p = jnp.exp2(s - jnp.max(s, axis=-1, keepdims=True))
            o = _dot(p.astype(BF16), v_ref[0:nk, hh * LANES:(hh + 1) * LANES])
            outs.append(o / o[:, MLA_V:MLA_V + 1])
        o_ref[...] = jnp.where(lane < MLA_V, outs[0], pltpu.roll(outs[1], MLA_V, axis=1)).astype(o_ref.dtype)

    @pl.when(pl.program_id(2) == 0)
    def _():
        attend(ctx_len)

    @pl.when(pl.program_id(2) > 0)
    def _():
        attend(t)


def _mla_attn(q, k, v, ctx_len):
    b, t, hq = q.shape
    pairs = hq // (2 * LANES)
    return pl.pallas_call(
        functools.partial(_mla_attn_kernel, ctx_len=ctx_len),
        out_shape=jax.ShapeDtypeStruct((b, t, pairs * 2 * MLA_V), BF16),
        grid=(b, pairs, t // TOK_TILE),
        in_specs=[pl.BlockSpec((None, TOK_TILE, 2 * LANES), lambda bi, hi, ti: (bi, ti, hi)),
                  pl.BlockSpec((None, t, 2 * LANES), lambda bi, hi, ti: (bi, 0, hi)),
                  pl.BlockSpec((None, t, 2 * LANES), lambda bi, hi, ti: (bi, 0, hi))],
        out_specs=pl.BlockSpec((None, TOK_TILE, LANES), lambda bi, hi, ti: (bi, ti, hi)),
        compiler_params=_cparams(("arbitrary", "arbitrary", "arbitrary")),
        name="mla_attn",
    )(q, k, v)


def _odd_in_kernel(x_ref, mod_ref, g_ref, w_ref, mq_ref, mk_ref, gq_ref, gk_ref, cos_ref, slo_ref, shi_ref,
                   q_ref, k_ref, v_ref, *, scale):
    mod = mod_ref[...]
    h = _prenorm(x_ref[...], g_ref[...], mod[1:2], mod[0:1]).astype(BF16)
    nq = q_ref.shape[-1]
    nk = k_ref.shape[-1]
    half = GQA_DIM // 2

    def head_norm_rope(y, m_ref, gain_ref):
        reps = y.shape[-1] // LANES
        y = y * lax.rsqrt(_dot((y * y).astype(BF16), m_ref[...]) + EPS) * gain_ref[...]
        return _rope(y, _tile_lanes(cos_ref[...], reps), _tile_lanes(slo_ref[...], reps),
                     _tile_lanes(shi_ref[...], reps), half)

    q_ref[...] = (head_norm_rope(_dot(h, w_ref[:, 0:nq]), mq_ref, gq_ref) * scale).astype(BF16)
    k_ref[...] = head_norm_rope(_dot(h, w_ref[:, nq:nq + nk]), mk_ref, gk_ref).astype(BF16)
    v_ref[...] = _dot(h, w_ref[:, nq + nk:nq + 2 * nk]).astype(BF16)


def _odd_in(xs, mods, g, p, scale):
    b, t, d = xs.shape
    nq = p["gq"].shape[1]
    nk = p["gk"].shape[1]
    row = functools.partial(_mod_row, ctx_row=b)
    tok = lambda bi, ti: (bi, ti, 0)
    full = lambda a: pl.BlockSpec(a.shape, lambda bi, ti: (0,) * a.ndim)
    pos = pl.BlockSpec((TOK_TILE, LANES), lambda bi, ti: (ti, 0))
    kv = jax.ShapeDtypeStruct((b, t, nk), BF16)
    return pl.pallas_call(
        functools.partial(_odd_in_kernel, scale=scale),
        out_shape=(jax.ShapeDtypeStruct((b, t, nq), BF16), kv, kv),
        grid=(b, t // TOK_TILE),
        in_specs=[pl.BlockSpec((None, TOK_TILE, d), tok),
                  pl.BlockSpec((None, 6, d), lambda bi, ti: (row(bi, ti), 0, 0)),
                  pl.BlockSpec((1, d), lambda bi, ti: (0, 0)),
                  full(p["w"]), full(p["mq"]), full(p["mk"]), full(p["gq"]), full(p["gk"]), pos, pos, pos],
        out_specs=(pl.BlockSpec((None, TOK_TILE, nq), tok),
                   pl.BlockSpec((None, TOK_TILE, nk), tok),
                   pl.BlockSpec((None, TOK_TILE, nk), tok)),
        compiler_params=_cparams(("arbitrary", "arbitrary")),
        name="odd_in",
    )(xs, mods, g, p["w"], p["mq"], p["mk"], p["gq"], p["gk"], p["cos"], p["slo"], p["shi"])


def _win_attn_kernel(sink_ref, q_ref, k_ref, v_ref, o_ref, *, ctx_len):
    t = k_ref.shape[0]
    wb = WIN_BLOCK
    w3 = 3 * wb
    hd = GQA_DIM
    group = GQA_HEADS // GQA_KV_HEADS
    i = pl.program_id(1)
    q0 = ctx_len + i * wb
    ws = pl.multiple_of(jnp.clip(q0 - wb, ctx_len, t - w3), wb)
    nk = w3 + ctx_len
    rows = group * wb
    row = lax.broadcasted_iota(I32, (rows, nk), 0)
    col = lax.broadcasted_iota(I32, (rows, nk), 1)
    valid = (col >= w3) | (jnp.abs(q0 + row % wb - (ws + col)) <= WINDOW)
    head_of_row = lax.broadcasted_iota(I32, (rows, 1), 0) // wb
    q = q_ref[...].astype(F32)
    kcat = jnp.concatenate([k_ref[pl.ds(ws, w3), :], k_ref[0:ctx_len, :]], axis=0).astype(F32)
    vcat = jnp.concatenate([v_ref[pl.ds(ws, w3), :], v_ref[0:ctx_len, :]], axis=0).astype(F32)
    outs = []
    for kh in range(GQA_KV_HEADS):
        qs = jnp.concatenate([q[:, (kh * group + h) * hd:(kh * group + h + 1) * hd] for h in range(group)], axis=0)
        s = _dot_nt(qs.astype(BF16), kcat[:, kh * hd:(kh + 1) * hd].astype(BF16))
        s = jnp.where(valid, s, NEG_INF)
        sink = jnp.zeros((rows, 1), F32)
        for h in range(group):
            sink = jnp.where(head_of_row == h, sink_ref[kh * group + h], sink)
        m = jnp.maximum(jnp.max(s, axis=-1, keepdims=True), sink)
        p = jnp.exp(s - m)
        l = jnp.sum(p, axis=-1, keepdims=True) + jnp.exp(sink - m)
        o = _dot(p.astype(BF16), vcat[:, kh * hd:(kh + 1) * hd].astype(BF16)) / l
        outs.extend(o[h * wb:(h + 1) * wb] for h in range(group))
    o_ref[...] = jnp.concatenate(outs, axis=1).astype(o_ref.dtype)


def _win_attn(q, k, v, sink, ctx_len):
    b, t, n = q.shape
    nkv = k.shape[-1]
    s_len = t - ctx_len
    off = ctx_len // WIN_BLOCK
    return pl.pallas_call(
        functools.partial(_win_attn_kernel, ctx_len=ctx_len),
        out_shape=jax.ShapeDtypeStruct((b, s_len, n), BF16),
        grid=(b, s_len // WIN_BLOCK),
        in_specs=[pl.BlockSpec(memory_space=pltpu.SMEM),
                  pl.BlockSpec((None, WIN_BLOCK, n), lambda bi, ti: (bi, ti + off, 0)),
                  pl.BlockSpec((None, t, nkv), lambda bi, ti: (bi, 0, 0)),
                  pl.BlockSpec((None, t, nkv), lambda bi, ti: (bi, 0, 0))],
        out_specs=pl.BlockSpec((None, WIN_BLOCK, n), lambda bi, ti: (bi, ti, 0)),
        compiler_params=_cparams(("arbitrary", "arbitrary")),
        name="win_attn",
    )(sink, q, k, v)


def _post_mix_kernel(a1_ref, a2_ref, x_ref, mod_ref, g_ref, w_ref, wr_ref, br_ref,
                     xo_ref, f_ref, route_ref, gate_ref, cnt_ref, seen_ref):
    half = a1_ref.shape[-1]
    mod = mod_ref[...]
    m = _dot(a1_ref[...], w_ref[0:half, :]) + _dot(a2_ref[...], w_ref[half:2 * half, :])
    x = x_ref[...] + mod[2:3] * m
    xo_ref[...] = x
    f = _prenorm(x, g_ref[...], mod[4:5], mod[3:4])
    f_ref[...] = f

    logit = _dot3(f, wr_ref[...]) + br_ref[...]
    tm = logit.shape[0]
    lane = lax.broadcasted_iota(I32, (tm, LANES), 1)
    lane_f = lane.astype(F32)
    vals, idxs = [], []
    for _ in range(TOP_K):
        mx = jnp.max(logit, axis=-1, keepdims=True)
        ix = jnp.min(jnp.where(logit == mx, lane_f, float(LANES)), axis=-1, keepdims=True)
        vals.append(mx)
        idxs.append(ix)
        logit = jnp.where(lane_f == ix, -jnp.inf, logit)
    exps = [jnp.exp(v - vals[0]) for v in vals]
    den = exps[0]
    for e in exps[1:]:
        den = den + e

    @pl.when((pl.program_id(0) == 0) & (pl.program_id(1) == 0))
    def _():
        seen_ref[...] = jnp.zeros(seen_ref.shape, F32)

    msk = jnp.zeros((tm, LANES), F32)
    for k in range(TOP_K):
        msk = jnp.where(lane_f == idxs[k], 1.0, msk)
    earlier = (lax.broadcasted_iota(I32, (tm, tm), 1) < lax.broadcasted_iota(I32, (tm, tm), 0))
    rank = _dot(jnp.where(earlier, 1.0, 0.0).astype(BF16), msk.astype(BF16)) + seen_ref[0:1, :]
    seen = seen_ref[...] + jnp.sum(msk, axis=0, keepdims=True)
    seen_ref[...] = seen
    cnt_ref[...] = seen

    r_out = jnp.zeros((tm, LANES), F32)
    g_out = jnp.zeros((tm, LANES), F32)
    for k in range(TOP_K):
        rank_k = jnp.sum(jnp.where(lane_f == idxs[k], rank, 0.0), axis=-1, keepdims=True)
        r_out = jnp.where(lane == k, idxs[k], r_out)
        r_out = jnp.where(lane == TOP_K + k, rank_k, r_out)
        g_out = jnp.where(lane == k, exps[k] / den, g_out)
    route_ref[...] = r_out.astype(I32)
    gate_ref[...] = g_out


def _post_mix(a1, a2, lane_blk2, xs, mods, g, w_out, w_router, b_router, t_off, a_off):
    b, t, d = xs.shape
    half = w_out.shape[0] // 2
    nt = t // TOK_TILE - t_off
    t_out = nt * TOK_TILE
    row = functools.partial(_mod_row, ctx_row=b)
    tok = lambda bi, ti: (bi, ti, 0)
    act = jax.ShapeDtypeStruct((b, t_out, d), F32)
    return pl.pallas_call(
        _post_mix_kernel,
        out_shape=(act, act,
                   jax.ShapeDtypeStruct((b, t_out, LANES), I32),
                   jax.ShapeDtypeStruct((b, t_out, LANES), F32),
                   jax.ShapeDtypeStruct((SUBLANES, LANES), F32)),
        grid=(b, nt),
        in_specs=[pl.BlockSpec((None, TOK_TILE, half), lambda bi, ti: (bi, ti + a_off, 0)),
                  pl.BlockSpec((None, TOK_TILE, half), lambda bi, ti: (bi, ti + a_off, lane_blk2)),
                  pl.BlockSpec((None, TOK_TILE, d), lambda bi, ti: (bi, ti + t_off, 0)),
                  pl.BlockSpec((None, 6, d), lambda bi, ti: (row(bi, ti + t_off), 0, 0)),
                  pl.BlockSpec((1, d), lambda bi, ti: (0, 0)),
                  pl.BlockSpec(w_out.shape, lambda bi, ti: (0, 0)),
                  pl.BlockSpec(w_router.shape, lambda bi, ti: (0, 0)),
                  pl.BlockSpec((1, LANES), lambda bi, ti: (0, 0))],
        out_specs=(pl.BlockSpec((None, TOK_TILE, d), tok),
                   pl.BlockSpec((None, TOK_TILE, d), tok),
                   pl.BlockSpec((None, TOK_TILE, LANES), tok),
                   pl.BlockSpec((None, TOK_TILE, LANES), tok),
                   pl.BlockSpec((SUBLANES, LANES), lambda bi, ti: (0, 0))),
        scratch_shapes=[pltpu.VMEM((SUBLANES, LANES), F32)],
        compiler_params=_cparams(("arbitrary", "arbitrary")),
        name="post_mix",
    )(a1, a2, xs, mods, g, w_out, w_router, b_router)


DMA_UNROLL = 2


def _row(ref, i):
    return ref.at[pl.ds(i, 1), :]


def _dispatch_kernel(pad_lo_ref, pad_n_ref, nu_ref, dest_ref, f_ref, xs_ref, zero_ref, sem, zsem):
    n = dest_ref.shape[0]

    @pl.when(pl.program_id(0) == 0)
    def _():
        zero_ref[...] = jnp.zeros(zero_ref.shape, F32)
        blk = zero_ref.shape[0]

        def block_copy(i):
            return pltpu.make_async_copy(zero_ref, xs_ref.at[pl.ds(pl.multiple_of(i * blk, blk), blk), :], zsem)

        def put_block(i, c):
            block_copy(i).start()
            return c

        def done_block(i, c):
            block_copy(i).wait()
            return c

        lax.fori_loop(nu_ref[0], xs_ref.shape[0] // blk, put_block, 0)
        lax.fori_loop(nu_ref[0], xs_ref.shape[0] // blk, done_block, 0)

        def per_expert(e, c):
            def put(r, c2):
                pltpu.make_async_copy(_row(zero_ref, 0), _row(xs_ref, pad_lo_ref[e] + r), zsem).start()
                return c2

            def done(r, c2):
                pltpu.make_async_copy(_row(zero_ref, 0), _row(xs_ref, 0), zsem).wait()
                return c2

            lax.fori_loop(0, pad_n_ref[e], put, 0)
            lax.fori_loop(0, pad_n_ref[e], done, 0)
            return c

        lax.fori_loop(0, N_EXPERTS, per_expert, 0)


    def issue(t, c):
        for k in range(TOP_K):
            pltpu.make_async_copy(_row(f_ref, t), _row(xs_ref, dest_ref[t * TOP_K + k]), sem).start()
        return c

    lax.fori_loop(0, n // TOP_K, issue, 0, unroll=DMA_UNROLL)
    pltpu.make_async_copy(xs_ref.at[pl.ds(0, n), :], xs_ref.at[pl.ds(0, n), :], sem).wait()


def _dispatch(f, dest, pad_lo, pad_n, n_used, rows):
    n, d = f.shape
    per = TOK_TILE * TOP_K
    return pl.pallas_call(
        _dispatch_kernel,
        out_shape=jax.ShapeDtypeStruct((rows, d), F32),
        grid_spec=pltpu.PrefetchScalarGridSpec(
            num_scalar_prefetch=3,
            grid=(n // TOK_TILE,),
            in_specs=[pl.BlockSpec((per,), lambda i, lo, cnt, nu: (i,), memory_space=pltpu.SMEM),
                      pl.BlockSpec((TOK_TILE, d), lambda i, lo, cnt, nu: (i, 0))],
            out_specs=pl.BlockSpec(memory_space=pl.ANY),
            scratch_shapes=[pltpu.VMEM((MOE_ROWS, d), F32), pltpu.SemaphoreType.DMA(()),
                            pltpu.SemaphoreType.DMA(())]),
        compiler_params=_cparams(("arbitrary",), has_side_effects=True, disable_bounds_checks=True),
        name="moe_dispatch",
    )(pad_lo, pad_n, n_used, dest, f)


def _experts_kernel(be_ref, nu_ref, xs_ref, wgu_ref, bgu_ref, wdn_ref, bdn_ref, y_ref, wgu_bf, wdn_bf):
    i = pl.program_id(0)

    @pl.when((i == 0) | (be_ref[i] != be_ref[jnp.maximum(i - 1, 0)]))
    def _():
        wgu_bf[...] = wgu_ref[...].astype(BF16)
        wdn_bf[...] = wdn_ref[...].astype(BF16)

    @pl.when(i < nu_ref[0])
    def _():
        h = _dot(xs_ref[...].astype(BF16), wgu_bf[...]) + bgu_ref[...]
        ff = h.shape[1] // 2
        hg = jnp.minimum(h[:, :ff], SWIGLU_LIMIT)
        hu = jnp.clip(h[:, ff:], -SWIGLU_LIMIT, SWIGLU_LIMIT)
        act = hg * jax.nn.sigmoid(SWIGLU_ALPHA * hg) * (hu + 1.0)
        y_ref[...] = _dot(act.astype(BF16), wdn_bf[...]) + bdn_ref[...]

    @pl.when(i >= nu_ref[0])
    def _():
        y_ref[...] = jnp.zeros(y_ref.shape, F32)


def _experts(xs, block_e, n_used, layer, wgu, bgu, wdn, bdn):
    rows, d = xs.shape
    ff2 = wgu.shape[-1]
    nb = rows // MOE_ROWS
    return pl.pallas_call(
        _experts_kernel,
        out_shape=jax.ShapeDtypeStruct((rows, d), F32),
        grid_spec=pltpu.PrefetchScalarGridSpec(
            num_scalar_prefetch=2,
            grid=(nb,),
            in_specs=[pl.BlockSpec((MOE_ROWS, d), lambda i, be, nu: (jnp.maximum(jnp.minimum(i, nu[0] - 1), 0), 0)),
                      pl.BlockSpec((None, None, d, ff2), lambda i, be, nu: (layer, be[i], 0, 0)),
                      pl.BlockSpec((None, None, 1, ff2), lambda i, be, nu: (layer, be[i], 0, 0)),
                      pl.BlockSpec((None, None, ff2 // 2, d), lambda i, be, nu: (layer, be[i], 0, 0)),
                      pl.BlockSpec((None, None, 1, d), lambda i, be, nu: (layer, be[i], 0, 0))],
            out_specs=pl.BlockSpec((MOE_ROWS, d), lambda i, be, nu: (i, 0)),
            scratch_shapes=[pltpu.VMEM((d, ff2), BF16), pltpu.VMEM((ff2 // 2, d), BF16)]),
        compiler_params=_cparams(("arbitrary",)),
        name="moe_experts",
    )(block_e, n_used, xs, wgu, bgu, wdn, bdn)


def _combine_kernel(dest_ref, y_ref, gate_ref, x_ref, mod_ref, o_ref, buf_ref, sem):
    n = dest_ref.shape[0]
    tm = n // TOP_K

    def issue(t, c):
        for k in range(TOP_K):
            pltpu.make_async_copy(_row(y_ref, dest_ref[t * TOP_K + k]), _row(buf_ref.at[k], t), sem).start()
        return c

    lax.fori_loop(0, tm, issue, 0, unroll=DMA_UNROLL)
    for k in range(TOP_K):
        pltpu.make_async_copy(y_ref.at[pl.ds(0, tm), :], buf_ref.at[k], sem).wait()

    gates = gate_ref[...]
    acc = gates[:, 0:1] * buf_ref[0]
    for k in range(1, TOP_K):
        acc = acc + gates[:, k:k + 1] * buf_ref[k]
    o_ref[...] = x_ref[...] + mod_ref[5:6, :] * acc


def _combine(y, dest, gates, xs, mods, t_off):
    b, t_out, d = xs.shape
    nt = t_out // TOK_TILE
    per = TOK_TILE * TOP_K
    row = functools.partial(_mod_row, ctx_row=b)
    tok = lambda bi, ti: (bi, ti, 0)
    return pl.pallas_call(
        _combine_kernel,
        out_shape=jax.ShapeDtypeStruct((b, t_out, d), F32),
        grid=(b, nt),
        in_specs=[pl.BlockSpec((per,), lambda bi, ti: (bi * nt + ti,), memory_space=pltpu.SMEM),
                  pl.BlockSpec(memory_space=pl.ANY),
                  pl.BlockSpec((None, TOK_TILE, LANES), tok),
                  pl.BlockSpec((None, TOK_TILE, d), tok),
                  pl.BlockSpec((None, 6, d), lambda bi, ti: (row(bi, ti + t_off), 0, 0))],
        out_specs=pl.BlockSpec((None, TOK_TILE, d), tok),
        scratch_shapes=[pltpu.VMEM((TOP_K, TOK_TILE, d), F32), pltpu.SemaphoreType.DMA(())],
        compiler_params=_cparams(("arbitrary", "arbitrary"), disable_bounds_checks=True),
        name="moe_combine",
    )(dest, y, gates, xs, mods)


def _routing(e_sel, rank, counts, n):
    padded = (counts + MOE_ROWS - 1) // MOE_ROWS * MOE_ROWS
    pend = jnp.cumsum(padded)
    pstart = pend - padded
    dest = jnp.sum(jnp.where(e_sel[..., None] == jnp.arange(N_EXPERTS), pstart, 0), axis=-1) + rank
    n_blocks = n * TOP_K // MOE_ROWS + N_EXPERTS
    first_row = jnp.arange(n_blocks) * MOE_ROWS
    block_e = jnp.minimum(jnp.sum(pend[None, :] <= first_row[:, None], axis=1), N_EXPERTS - 1)
    n_used = (pend[-1] // MOE_ROWS).reshape(1)
    pads = ((pstart + counts).astype(I32), (padded - counts).astype(I32))
    return dest.reshape(-1).astype(I32), block_e.astype(I32), n_used.astype(I32), pads, n_blocks * MOE_ROWS


def _moe(f, route, gates, seen, x_mid, mods, t_off, layer, wgu, bgu, wdn, bdn):
    b, t_out, d = f.shape
    n = b * t_out
    route = route.reshape(n, LANES)
    counts = seen[0, :N_EXPERTS].astype(I32)
    dest, block_e, n_used, pads, rows = _routing(route[:, :TOP_K], route[:, TOP_K:2 * TOP_K], counts, n)
    xs = _dispatch(f.reshape(n, d), dest, *pads, n_used, rows)
    y = _experts(xs, block_e, n_used, layer, wgu, bgu, wdn, bdn)
    return _combine(y, dest, gates, x_mid, mods, t_off)


def _axial_angles(n_rows, rot_dim):
    row = np.repeat(np.arange(n_rows, dtype=np.float32), GRID_W)
    col = np.tile(np.arange(GRID_W, dtype=np.float32), n_rows)
    n = rot_dim // 4
    freqs = (np.float32(ROPE_THETA) ** (-np.arange(n, dtype=np.float32) / np.float32(n))).astype(np.float32)
    return np.concatenate([row[:, None] * freqs, col[:, None] * freqs], axis=-1).astype(np.float32)


def _rope_tables(ctx_len, s_len, rot_dim, lane_base, reps):
    ang = _axial_angles(s_len // GRID_W, rot_dim)
    half = rot_dim // 2
    period = LANES // reps
    cos = np.ones((ctx_len + s_len, period), np.float32)
    slo = np.zeros((ctx_len + s_len, period), np.float32)
    shi = np.zeros((ctx_len + s_len, period), np.float32)
    cos[ctx_len:, lane_base:lane_base + rot_dim] = np.tile(np.cos(ang), (1, 2))
    slo[ctx_len:, lane_base:lane_base + half] = -np.sin(ang)
    shi[ctx_len:, lane_base + half:lane_base + rot_dim] = np.sin(ang)
    return [jnp.asarray(np.tile(tb, (1, reps))) for tb in (cos, slo, shi)]


def _segment_mean_matrix(width, period, segs):
    lane = np.arange(width)
    seg_id = np.full((width,), -1)
    seg_w = np.zeros((width,), np.float32)
    for i, (start, length) in enumerate(segs):
        inside = ((lane % period) >= start) & ((lane % period) < start + length)
        seg_id = np.where(inside, (lane // period) * len(segs) + i, seg_id)
        seg_w = np.where(inside, np.float32(1.0 / length), seg_w)
    same = (seg_id[:, None] == seg_id[None, :]) & (seg_id[:, None] >= 0)
    return jnp.asarray(np.where(same, seg_w[None, :], np.float32(0.0)), dtype=BF16)


def _block_diag(w):
    n, c, _ = w.shape
    eye = jnp.eye(n, dtype=w.dtype)
    return (eye[:, None, :, None] * w[:, :, None, :]).reshape(n * c, n * c)


def kernel(x, c, ctx, c_ctx, w_mod, b_mod, norm_mix, norm_ffn, w_in_even, lru_conv_w, lru_conv_b, lru_w_r, lru_b_r, lru_w_i, lru_b_i, lru_lambda, mla_q_a_norm, mla_w_q_b, mla_kv_a_norm, mla_w_kv_b, mla_nope_norm, mla_rope_norm, w_out_even, w_qkv_odd, gqa_qk_norm, gqa_sink, w_out_odd, w_router, b_router, w_gate_up, b_gate_up, w_down, b_down):
    b, s_len, d = x.shape
    ctx_len = ctx.shape[1]
    depth = w_mod.shape[0]
    assert depth == 2 and ctx_len == TOK_TILE and s_len % TOK_TILE == 0 and b + 1 <= SUBLANES
    lru_w = lru_conv_w.shape[-1]
    q_lora = mla_q_a_norm.shape[-1]
    kv_lora = mla_kv_a_norm.shape[-1]

    xs = jnp.concatenate([ctx, x], axis=1)
    cvec = jnp.concatenate([c, c_ctx[None], jnp.zeros((SUBLANES - b - 1, d), F32)], axis=0)
    mods = _modulation(cvec, w_mod, b_mod).reshape(depth, SUBLANES, 6, d)

    wr_pad = jnp.zeros((depth, d, LANES), F32).at[:, :, :N_EXPERTS].set(w_router)
    br_pad = jnp.full((depth, 1, LANES), NEG_INF, F32).at[:, 0, :N_EXPERTS].set(b_router)
    experts = (w_gate_up, b_gate_up[:, :, None, :], w_down, b_down[:, :, None, :])

    n_in = w_in_even.shape[-1]
    n_in_pad = -(-n_in // LANES) * LANES
    w_in = jnp.zeros((d, n_in_pad), F32).at[:, :n_in].set(w_in_even[0]).astype(BF16)
    xa, ga, mla_in = _even_in(xs, mods[0], norm_mix[0][None], w_in, lru_w)

    nh = lru_w // (2 * LANES)
    per = LRU_BLOCKS // nh
    blk = lru_w // LRU_BLOCKS
    w_gates = jnp.stack([
        jnp.stack([jnp.concatenate([_block_diag(lru_w_r[0, dd, h * per:(h + 1) * per]),
                                    _block_diag(lru_w_i[0, dd, h * per:(h + 1) * per])], axis=1)
                   for h in range(nh)]) for dd in range(2)]).astype(BF16)
    assert blk * per == 2 * LANES
    ya = _lru(xa, ga, lru_conv_w[0], lru_conv_b[0][:, None, :], w_gates, lru_b_r[0][:, None, :],
              lru_b_i[0][:, None, :], jax.nn.softplus(-lru_lambda[0])[:, None, :], ctx_len)

    qk = MLA_NOPE + MLA_ROPE
    hq = MLA_HEADS * LANES
    wq = jnp.zeros((q_lora, MLA_HEADS, LANES), F32).at[:, :, :qk].set(
        mla_w_q_b[0].reshape(q_lora, MLA_HEADS, qk)).reshape(q_lora, hq).astype(BF16)
    wkv = mla_w_kv_b[0].reshape(kv_lora, MLA_HEADS, MLA_NOPE + MLA_V)
    wk = jnp.zeros((kv_lora, MLA_HEADS, LANES), F32).at[:, :, :MLA_NOPE].set(
        wkv[:, :, :MLA_NOPE]).reshape(kv_lora, hq).astype(BF16)
    wv = jnp.zeros((kv_lora, MLA_HEADS, LANES), F32).at[:, :, :MLA_V].set(
        wkv[:, :, MLA_NOPE:]).reshape(kv_lora, hq).astype(BF16)
    one = jnp.asarray(np.tile(np.arange(LANES) == MLA_V, MLA_HEADS)[None], dtype=F32)
    zpad = jnp.zeros((LANES - qk,), F32)
    gq = jnp.tile(jnp.concatenate([mla_nope_norm[0, 0], mla_rope_norm[0, 0], zpad]), MLA_HEADS)[None]
    gk = jnp.tile(jnp.concatenate([mla_nope_norm[0, 1], jnp.zeros((LANES - MLA_NOPE,), F32)]), MLA_HEADS)[None]
    gkr = jnp.concatenate([mla_rope_norm[0, 1], jnp.zeros((LANES - MLA_ROPE,), F32)])[None]
    cos, slo, shi = _rope_tables(ctx_len, s_len, MLA_ROPE, MLA_NOPE, 1)
    mla_p = dict(qan=mla_q_a_norm[0][None], kvn=mla_kv_a_norm[0][None], wq=wq, wk=wk, wv=wv,
                 mq=_segment_mean_matrix(hq, LANES, [(0, MLA_NOPE), (MLA_NOPE, MLA_ROPE)]),
                 mk=_segment_mean_matrix(hq, LANES, [(0, MLA_NOPE)]),
                 gq=gq, gk=gk, gkr=gkr, one=one, cos=cos, slo=slo, shi=shi)
    q, k, v = _mla_prep(mla_in, mla_p, qk ** -0.5 * math.log2(math.e))
    yb = _mla_attn(q, k, v, ctx_len)

    x_mid, f, route, gates, seen = _post_mix(ya, yb, 0, xs, mods[0], norm_ffn[0][None], w_out_even[0].astype(BF16),
                                             wr_pad[0], br_pad[0], 0, 0)
    xs = _moe(f, route, gates, seen, x_mid, mods[0], 0, 0, *experts)

    nq = GQA_HEADS * GQA_DIM
    nkv = GQA_KV_HEADS * GQA_DIM
    cos, slo, shi = _rope_tables(ctx_len, s_len, GQA_DIM, 0, LANES // GQA_DIM)
    odd_p = dict(w=w_qkv_odd[0].astype(BF16),
                 mq=_segment_mean_matrix(nq, GQA_DIM, [(0, GQA_DIM)]),
                 mk=_segment_mean_matrix(nkv, GQA_DIM, [(0, GQA_DIM)]),
                 gq=jnp.tile(gqa_qk_norm[0, 0], GQA_HEADS)[None], gk=jnp.tile(gqa_qk_norm[0, 1], GQA_KV_HEADS)[None],
                 cos=cos, slo=slo, shi=shi)
    q, k, v = _odd_in(xs, mods[1], norm_mix[1][None], odd_p, GQA_DIM ** -0.5)
    o = _win_attn(q, k, v, gqa_sink[0], ctx_len)

    t_off = ctx_len // TOK_TILE
    x_mid, f, route, gates, seen = _post_mix(o, o, 1, xs, mods[1], norm_ffn[1][None], w_out_odd[0].astype(BF16),
                                             wr_pad[1], br_pad[1], t_off, 0)
    return _moe(f, route, gates, seen, x_mid, mods[1], t_off, 1, *experts)
```

```python
import functools
import math

import jax
import jax.numpy as jnp
import numpy as np
from jax import lax
from jax.experimental import pallas as pl
from jax.experimental.pallas import tpu as pltpu

F32 = jnp.float32
BF16 = jnp.bfloat16
I32 = jnp.int32

GRID_W = 64
LRU_BLOCKS = 8
LRU_C = 8.0
CONV_W = 4
MLA_HEADS = 8
MLA_NOPE = 64
MLA_ROPE = 32
MLA_V = 64
GQA_HEADS = 16
GQA_KV_HEADS = 4
GQA_DIM = 64
WINDOW = 128
ROPE_THETA = 10000.0
NEG_INF = -1e30
EPS = 1e-6
N_EXPERTS = 32
TOP_K = 4
SWIGLU_LIMIT = 7.0
SWIGLU_ALPHA = 1.702

LANES = 128
SUBLANES = 8
TOK_TILE = 256
LRU_CHUNK = 128
WIN_BLOCK = 128
MOE_ROWS = 256
VMEM_LIMIT = 48 * 1024 * 1024


def _cparams(sem, **kw):
    return pltpu.CompilerParams(dimension_semantics=sem, vmem_limit_bytes=VMEM_LIMIT, **kw)


def _dot(a, b):
    return jnp.dot(a, b, preferred_element_type=F32)


def _dot_nt(a, b):
    return lax.dot_general(a, b, (((1,), (1,)), ((), ())), preferred_element_type=F32)


def _split_bf16(x):
    hi = x.astype(BF16)
    lo = (x - hi.astype(F32)).astype(BF16)
    return hi, lo


def _dot3(a, w):
    ah, al = _split_bf16(a)
    wh, wl = _split_bf16(w)
    return _dot(ah, wh) + _dot(al, wh) + _dot(ah, wl)


def _rms(x):
    return x * lax.rsqrt(jnp.mean(x * x, axis=-1, keepdims=True) + EPS)


def _prenorm(x, g, scale, shift):
    return (_rms(x) * g) * (1.0 + scale) + shift


def _rope(x, cos, sin_lo, sin_hi, half):
    w = x.shape[-1]
    return x * cos + pltpu.roll(x, w - half, axis=1) * sin_lo + pltpu.roll(x, half, axis=1) * sin_hi


def _tile_lanes(t, reps):
    return jnp.concatenate([t] * reps, axis=1) if reps > 1 else t


def _mod_kernel(c_ref, w_ref, b_ref, o_ref):
    c = c_ref[...]
    o_ref[...] = _dot3(c * jax.nn.sigmoid(c), w_ref[...]) + b_ref[...]


def _modulation(cvec, w_mod, b_mod):
    depth, d, n = w_mod.shape
    tn = 1536
    return pl.pallas_call(
        _mod_kernel,
        out_shape=jax.ShapeDtypeStruct((depth, SUBLANES, n), F32),
        grid=(depth, n // tn),
        in_specs=[pl.BlockSpec((SUBLANES, d), lambda l, j: (0, 0)),
                  pl.BlockSpec((None, d, tn), lambda l, j: (l, 0, j)),
                  pl.BlockSpec((None, 1, tn), lambda l, j: (l, 0, j))],
        out_specs=pl.BlockSpec((None, SUBLANES, tn), lambda l, j: (l, 0, j)),
        compiler_params=_cparams(("arbitrary", "arbitrary")),
        name="modulation",
    )(cvec, w_mod, b_mod.reshape(depth, 1, n))


def _mod_row(b, t, ctx_row):
    return jnp.where(t == 0, ctx_row, b)


def _stream_tile(ctx_ref, lat_ref):
    return jnp.where(pl.program_id(1) == 0, ctx_ref[...], lat_ref[...])


def _stream_specs(d):
    return [pl.BlockSpec((None, TOK_TILE, d), lambda bi, ti: (bi, 0, 0)),
            pl.BlockSpec((None, TOK_TILE, d), lambda bi, ti: (bi, jnp.maximum(ti - 1, 0), 0))]


def _even_in_kernel(ctx_ref, x_ref, mod_ref, g_ref, w_ref, xa_ref, ga_ref, mla_ref):
    mod = mod_ref[...]
    h = _prenorm(_stream_tile(ctx_ref, x_ref), g_ref[...], mod[1:2], mod[0:1])
    z = _dot(h.astype(BF16), w_ref[...])
    c = xa_ref.shape[-1]
    xa_ref[...] = z[:, :c]
    ga_ref[...] = z[:, c:2 * c]
    mla_ref[...] = z[:, 2 * c:]


def _even_in(ctx, x, mods, g, w_pad, lru_w):
    b, s_len, d = x.shape
    t = ctx.shape[1] + s_len
    nt = t // TOK_TILE
    n_out = w_pad.shape[1]
    n_mla = n_out - 2 * lru_w
    row = functools.partial(_mod_row, ctx_row=b)
    tok = lambda bi, ti: (bi, ti, 0)
    return pl.pallas_call(
        _even_in_kernel,
        out_shape=(jax.ShapeDtypeStruct((b, t, lru_w), F32),
                   jax.ShapeDtypeStruct((b, t, lru_w), F32),
                   jax.ShapeDtypeStruct((b, t, n_mla), F32)),
        grid=(b, nt),
        in_specs=_stream_specs(d) + [
            pl.BlockSpec((None, 6, d), lambda bi, ti: (row(bi, ti), 0, 0)),
            pl.BlockSpec((1, d), lambda bi, ti: (0, 0)),
            pl.BlockSpec((d, n_out), lambda bi, ti: (0, 0))],
        out_specs=(pl.BlockSpec((None, TOK_TILE, lru_w), tok),
                   pl.BlockSpec((None, TOK_TILE, lru_w), tok),
                   pl.BlockSpec((None, TOK_TILE, n_mla), tok)),
        compiler_params=_cparams(("arbitrary", "arbitrary")),
        name="even_in",
    )(ctx, x, mods, g, w_pad)


def _lru_kernel(xa_ref, ga_ref, cw_ref, cb_ref, wg_ref, br_ref, bi_ref, sp_ref, o_ref, pad_ref, rec_ref, *, ctx_len):
    t, c = xa_ref.shape
    tc = LRU_CHUNK
    halo = SUBLANES
    n_chunks = t // tc
    n_ctx = ctx_len // tc
    groups = tc // SUBLANES

    pad_ref[0:halo, :] = jnp.zeros((halo, c), F32)
    pad_ref[t + halo:t + 2 * halo, :] = jnp.zeros((halo, c), F32)
    pad_ref[halo:t + halo, :] = xa_ref[...]

    rid = lax.broadcasted_iota(I32, (tc, 1), 0)
    sub = rid % SUBLANES

    for d in range(2):
        cw = cw_ref[d]
        cb = cb_ref[d]
        wg = wg_ref[d]
        b_r = br_ref[d]
        b_i = bi_ref[d]
        sp = sp_ref[d]

        def chunk(i, h, d=d, cw=cw, cb=cb, wg=wg, b_r=b_r, b_i=b_i, sp=sp):
            if d == 0:
                ci = i
            else:
                ci = jnp.where(i < n_ctx, n_ctx - 1 - i, n_chunks - 1 - (i - n_ctx))
            r0 = pl.multiple_of(ci * tc, tc)
            win = pad_ref[pl.ds(r0, tc + 2 * halo), :]
            if d == 0:
                past = jnp.where(ci == n_ctx, 0.0, win[0:halo])
                win = jnp.concatenate([past, win[halo:]], axis=0)
            else:
                past = jnp.where(ci == n_ctx - 1, 0.0, win[halo + tc:])
                win = jnp.concatenate([win[:halo + tc], past], axis=0)
            xc = jnp.zeros((tc, c), F32) + cb
            for k in range(CONV_W):
                off = (k - (CONV_W - 1)) if d == 0 else ((CONV_W - 1) - k)
                if off == 0:
                    src = win[halo:halo + tc]
                else:
                    src = pltpu.roll(win, (-off) % (tc + 2 * halo), axis=0)[halo:halo + tc]
                xc = xc + cw[k:k + 1] * src
            gz = _dot(xc.astype(BF16), wg)
            r = jax.nn.sigmoid(gz[:, :c] + b_r)
            gi = jax.nn.sigmoid(gz[:, c:] + b_i)
            log_a = (-LRU_C) * r * sp
            a = jnp.exp(log_a)
            th = jnp.tanh(log_a)
            bb = jnp.sqrt(-2.0 * th / (1.0 - th)) * (gi * xc)
            for s in (1, 2, 4):
                if d == 0:
                    ok = sub >= s
                    sh = s
                else:
                    ok = sub <= (SUBLANES - 1 - s)
                    sh = tc - s
                a_prev = jnp.where(ok, pltpu.roll(a, sh, axis=0), 1.0)
                b_prev = jnp.where(ok, pltpu.roll(bb, sh, axis=0), 0.0)
                bb = a * b_prev + bb
                a = a * a_prev
            outs = [None] * groups
            order = range(groups) if d == 0 else range(groups - 1, -1, -1)
            for g in order:
                lo = g * SUBLANES
                hg = a[lo:lo + SUBLANES] * h + bb[lo:lo + SUBLANES]
                outs[g] = hg
                h = hg[SUBLANES - 1:SUBLANES] if d == 0 else hg[0:1]
            hs = jnp.concatenate(outs, axis=0)
            if d == 0:
                rec_ref[pl.ds(r0, tc), :] = hs
            else:
                tot = rec_ref[pl.ds(r0, tc), :] + hs
                gate = jax.nn.gelu(ga_ref[pl.ds(r0, tc), :], approximate=True)
                o_ref[pl.ds(r0, tc), :] = (tot * gate).astype(o_ref.dtype)
            return h

        lax.fori_loop(0, n_chunks, chunk, jnp.zeros((1, c), F32))


def _lru(xa, ga, conv_w, conv_b, w_gates, b_r, b_i, sp, ctx_len):
    b, t, w = xa.shape
    c = 2 * LANES
    nh = w // c
    tok = lambda bi, hi: (bi, 0, hi)
    par = lambda bi, hi: (0, 0, hi)
    return pl.pallas_call(
        functools.partial(_lru_kernel, ctx_len=ctx_len),
        out_shape=jax.ShapeDtypeStruct((b, t, w), BF16),
        grid=(b, nh),
        in_specs=[pl.BlockSpec((None, t, c), tok),
                  pl.BlockSpec((None, t, c), tok),
                  pl.BlockSpec((2, CONV_W, c), par),
                  pl.BlockSpec((2, 1, c), par),
                  pl.BlockSpec((2, None, c, 2 * c), lambda bi, hi: (0, hi, 0, 0)),
                  pl.BlockSpec((2, 1, c), par),
                  pl.BlockSpec((2, 1, c), par),
                  pl.BlockSpec((2, 1, c), par)],
        out_specs=pl.BlockSpec((None, t, c), tok),
        scratch_shapes=[pltpu.VMEM((t + 2 * SUBLANES, c), F32), pltpu.VMEM((t, c), F32)],
        compiler_params=_cparams(("arbitrary", "arbitrary")),
        name="rglru",
    )(xa, ga, conv_w, conv_b, w_gates, b_r, b_i, sp)


def _mla_prep_kernel(in_ref, qan_ref, kvn_ref, wq_ref, wk_ref, wv_ref, mq_ref, mk_ref, gq_ref, gk_ref, gkr_ref,
                     one_ref, cos_ref, slo_ref, shi_ref, q_ref, k_ref, v_ref, *, q_lora, kv_lora, scale):
    z = in_ref[...]
    heads = q_ref.shape[-1] // LANES
    cos = cos_ref[...]
    slo = slo_ref[...]
    shi = shi_ref[...]
    half = MLA_ROPE // 2

    qan = (_rms(z[:, :q_lora]) * qan_ref[...]).astype(BF16)
    q = _dot(qan, wq_ref[...])
    q = q * lax.rsqrt(_dot((q * q).astype(BF16), mq_ref[...]) + EPS) * gq_ref[...]
    q = _rope(q, _tile_lanes(cos, heads), _tile_lanes(slo, heads), _tile_lanes(shi, heads), half)
    q_ref[...] = (q * scale).astype(BF16)

    kvn = (_rms(z[:, q_lora:q_lora + kv_lora]) * kvn_ref[...]).astype(BF16)
    kk = _dot(kvn, wk_ref[...])
    kk = kk * lax.rsqrt(_dot((kk * kk).astype(BF16), mk_ref[...]) + EPS) * gk_ref[...]
    v_ref[...] = (_dot(kvn, wv_ref[...]) + one_ref[...]).astype(BF16)

    kr = z[:, q_lora + kv_lora:]
    kr = kr * lax.rsqrt(jnp.sum(kr * kr, axis=-1, keepdims=True) * (1.0 / MLA_ROPE) + EPS) * gkr_ref[...]
    kr = pltpu.roll(kr, MLA_NOPE, axis=1)
    kr = _rope(kr, cos, slo, shi, half)
    k_ref[...] = (kk + _tile_lanes(kr, heads)).astype(BF16)


def _mla_prep(mla_in, p, scale):
    b, t, w = mla_in.shape
    hq = MLA_HEADS * LANES
    tok = lambda bi, ti: (bi, ti, 0)
    full = lambda a: pl.BlockSpec(a.shape, lambda bi, ti: (0,) * a.ndim)
    pos = pl.BlockSpec((TOK_TILE, LANES), lambda bi, ti: (ti, 0))
    consts = [p["qan"], p["kvn"], p["wq"], p["wk"], p["wv"], p["mq"], p["mk"], p["gq"], p["gk"], p["gkr"], p["one"]]
    out = jax.ShapeDtypeStruct((b, t, hq), BF16)
    return pl.pallas_call(
        functools.partial(_mla_prep_kernel, q_lora=p["qan"].shape[1], kv_lora=p["kvn"].shape[1], scale=scale),
        out_shape=(out, out, out),
        grid=(b, t // TOK_TILE),
        in_specs=[pl.BlockSpec((None, TOK_TILE, w), tok)] + [full(a) for a in consts] + [pos, pos, pos],
        out_specs=(pl.BlockSpec((None, TOK_TILE, hq), tok),) * 3,
        compiler_params=_cparams(("arbitrary", "arbitrary")),
        name="mla_prep",
    )(mla_in, *consts, p["cos"], p["slo"], p["shi"])


def _mla_attn_kernel(q_ref, k_ref, v_ref, o_ref, *, ctx_len):
    tq = q_ref.shape[0]
    t = k_ref.shape[0]
    lane = lax.broadcasted_iota(I32, (tq, LANES), 1)

    def attend(nk):
        outs = []
        for hh in range(2):
            q = q_ref[:, hh * LANES:(hh + 1) * LANES]
            k = k_ref[0:nk, hh * LANES:(hh + 1) * LANES]
            s = _dot_nt(q, k)
            p = jnp.exp2(s - jnp.max(s, axis=-1, keepdims=True))
            o = _dot(p.astype(BF16), v_ref[0:nk, hh * LANES:(hh + 1) * LANES])
            outs.append(o / o[:, MLA_V:MLA_V + 1])
        o_ref[...] = jnp.where(lane < MLA_V, outs[0], pltpu.roll(outs[1], MLA_V, axis=1)).astype(o_ref.dtype)

    @pl.when(pl.program_id(2) == 0)
    def _():
        attend(ctx_len)

    @pl.when(pl.program_id(2) > 0)
    def _():
        attend(t)


def _mla_attn(q, k, v, ctx_len):
    b, t, hq = q.shape
    pairs = hq // (2 * LANES)
    return pl.pallas_call(
        functools.partial(_mla_attn_kernel, ctx_len=ctx_len),
        out_shape=jax.ShapeDtypeStruct((b, t, pairs * 2 * MLA_V), BF16),
        grid=(b, pairs, t // TOK_TILE),
        in_specs=[pl.BlockSpec((None, TOK_TILE, 2 * LANES), lambda bi, hi, ti: (bi, ti, hi)),
                  pl.BlockSpec((None, t, 2 * LANES), lambda bi, hi, ti: (bi, 0, hi)),
                  pl.BlockSpec((None, t, 2 * LANES), lambda bi, hi, ti: (bi, 0, hi))],
        out_specs=pl.BlockSpec((None, TOK_TILE, LANES), lambda bi, hi, ti: (bi, ti, hi)),
        compiler_params=_cparams(("arbitrary", "arbitrary", "arbitrary")),
        name="mla_attn",
    )(q, k, v)


def _odd_in_kernel(x_ref, mod_ref, g_ref, w_ref, mq_ref, mk_ref, gq_ref, gk_ref, cos_ref, slo_ref, shi_ref,
                   q_ref, k_ref, v_ref, *, scale):
    mod = mod_ref[...]
    h = _prenorm(x_ref[...], g_ref[...], mod[1:2], mod[0:1]).astype(BF16)
    nq = q_ref.shape[-1]
    nk = k_ref.shape[-1]
    half = GQA_DIM // 2

    def head_norm_rope(y, m_ref, gain_ref):
        reps = y.shape[-1] // LANES
        y = y * lax.rsqrt(_dot((y * y).astype(BF16), m_ref[...]) + EPS) * gain_ref[...]
        return _rope(y, _tile_lanes(cos_ref[...], reps), _tile_lanes(slo_ref[...], reps),
                     _tile_lanes(shi_ref[...], reps), half)

    q_ref[...] = (head_norm_rope(_dot(h, w_ref[:, 0:nq]), mq_ref, gq_ref) * scale).astype(BF16)
    k_ref[...] = head_norm_rope(_dot(h, w_ref[:, nq:nq + nk]), mk_ref, gk_ref).astype(BF16)
    v_ref[...] = _dot(h, w_ref[:, nq + nk:nq + 2 * nk]).astype(BF16)


def _odd_in(xs, mods, g, p, scale):
    b, t, d = xs.shape
    nq = p["gq"].shape[1]
    nk = p["gk"].shape[1]
    row = functools.partial(_mod_row, ctx_row=b)
    tok = lambda bi, ti: (bi, ti, 0)
    full = lambda a: pl.BlockSpec(a.shape, lambda bi, ti: (0,) * a.ndim)
    pos = pl.BlockSpec((TOK_TILE, LANES), lambda bi, ti: (ti, 0))
    kv = jax.ShapeDtypeStruct((b, t, nk), BF16)
    return pl.pallas_call(
        functools.partial(_odd_in_kernel, scale=scale),
        out_shape=(jax.ShapeDtypeStruct((b, t, nq), BF16), kv, kv),
        grid=(b, t // TOK_TILE),
        in_specs=[pl.BlockSpec((None, TOK_TILE, d), tok),
                  pl.BlockSpec((None, 6, d), lambda bi, ti: (row(bi, ti), 0, 0)),
                  pl.BlockSpec((1, d), lambda bi, ti: (0, 0)),
                  full(p["w"]), full(p["mq"]), full(p["mk"]), full(p["gq"]), full(p["gk"]), pos, pos, pos],
        out_specs=(pl.BlockSpec((None, TOK_TILE, nq), tok),
                   pl.BlockSpec((None, TOK_TILE, nk), tok),
                   pl.BlockSpec((None, TOK_TILE, nk), tok)),
        compiler_params=_cparams(("arbitrary", "arbitrary")),
        name="odd_in",
    )(xs, mods, g, p["w"], p["mq"], p["mk"], p["gq"], p["gk"], p["cos"], p["slo"], p["shi"])


def _win_attn_kernel(sink_ref, q_ref, k_ref, v_ref, o_ref, *, ctx_len):
    t = k_ref.shape[0]
    wb = WIN_BLOCK
    w3 = 3 * wb
    hd = GQA_DIM
    group = GQA_HEADS // GQA_KV_HEADS
    i = pl.program_id(1)
    q0 = ctx_len + i * wb
    ws = pl.multiple_of(jnp.clip(q0 - wb, ctx_len, t - w3), wb)
    nk = w3 + ctx_len
    rows = group * wb
    row = lax.broadcasted_iota(I32, (rows, nk), 0)
    col = lax.broadcasted_iota(I32, (rows, nk), 1)
    valid = (col >= w3) | (jnp.abs(q0 + row % wb - (ws + col)) <= WINDOW)
    head_of_row = lax.broadcasted_iota(I32, (rows, 1), 0) // wb
    q = q_ref[...].astype(F32)
    kcat = jnp.concatenate([k_ref[pl.ds(ws, w3), :], k_ref[0:ctx_len, :]], axis=0).astype(F32)
    vcat = jnp.concatenate([v_ref[pl.ds(ws, w3), :], v_ref[0:ctx_len, :]], axis=0).astype(F32)
    outs = []
    for kh in range(GQA_KV_HEADS):
        qs = jnp.concatenate([q[:, (kh * group + h) * hd:(kh * group + h + 1) * hd] for h in range(group)], axis=0)
        s = _dot_nt(qs.astype(BF16), kcat[:, kh * hd:(kh + 1) * hd].astype(BF16))
        s = jnp.where(valid, s, NEG_INF)
        sink = jnp.zeros((rows, 1), F32)
        for h in range(group):
            sink = jnp.where(head_of_row == h, sink_ref[kh * group + h], sink)
        m = jnp.maximum(jnp.max(s, axis=-1, keepdims=True), sink)
        p = jnp.exp(s - m)
        l = jnp.sum(p, axis=-1, keepdims=True) + jnp.exp(sink - m)
        o = _dot(p.astype(BF16), vcat[:, kh * hd:(kh + 1) * hd].astype(BF16)) / l
        outs.extend(o[h * wb:(h + 1) * wb] for h in range(group))
    o_ref[...] = jnp.concatenate(outs, axis=1).astype(o_ref.dtype)


def _win_attn(q, k, v, sink, ctx_len):
    b, t, n = q.shape
    nkv = k.shape[-1]
    s_len = t - ctx_len
    off = ctx_len // WIN_BLOCK
    return pl.pallas_call(
        functools.partial(_win_attn_kernel, ctx_len=ctx_len),
        out_shape=jax.ShapeDtypeStruct((b, s_len, n), BF16),
        grid=(b, s_len // WIN_BLOCK),
        in_specs=[pl.BlockSpec(memory_space=pltpu.SMEM),
                  pl.BlockSpec((None, WIN_BLOCK, n), lambda bi, ti: (bi, ti + off, 0)),
                  pl.BlockSpec((None, t, nkv), lambda bi, ti: (bi, 0, 0)),
                  pl.BlockSpec((None, t, nkv), lambda bi, ti: (bi, 0, 0))],
        out_specs=pl.BlockSpec((None, WIN_BLOCK, n), lambda bi, ti: (bi, ti, 0)),
        compiler_params=_cparams(("arbitrary", "arbitrary")),
        name="win_attn",
    )(sink, q, k, v)


def _post_mix_kernel(a1_ref, a2_ref, *refs, split_stream):
    x_in = _stream_tile(refs[0], refs[1]) if split_stream else refs[0][...]
    _post_mix_body(a1_ref, a2_ref, x_in, *refs[2 if split_stream else 1:])


def _post_mix_body(a1_ref, a2_ref, x_in, mod_ref, g_ref, w_ref, wr_ref, br_ref,
                   xo_ref, f_ref, route_ref, gate_ref, cnt_ref, seen_ref):
    half = a1_ref.shape[-1]
    mod = mod_ref[...]
    m = _dot(a1_ref[...], w_ref[0:half, :]) + _dot(a2_ref[...], w_ref[half:2 * half, :])
    x = x_in + mod[2:3] * m
    xo_ref[...] = x
    f = _prenorm(x, g_ref[...], mod[4:5], mod[3:4])
    f_ref[...] = f

    logit = _dot3(f, wr_ref[...]) + br_ref[...]
    tm = logit.shape[0]
    lane = lax.broadcasted_iota(I32, (tm, LANES), 1)
    lane_f = lane.astype(F32)
    vals, idxs = [], []
    for _ in range(TOP_K):
        mx = jnp.max(logit, axis=-1, keepdims=True)
        ix = jnp.min(jnp.where(logit == mx, lane_f, float(LANES)), axis=-1, keepdims=True)
        vals.append(mx)
        idxs.append(ix)
        logit = jnp.where(lane_f == ix, -jnp.inf, logit)
    exps = [jnp.exp(v - vals[0]) for v in vals]
    den = exps[0]
    for e in exps[1:]:
        den = den + e

    @pl.when((pl.program_id(0) == 0) & (pl.program_id(1) == 0))
    def _():
        seen_ref[...] = jnp.zeros(seen_ref.shape, F32)

    msk = jnp.zeros((tm, LANES), F32)
    for k in range(TOP_K):
        msk = jnp.where(lane_f == idxs[k], 1.0, msk)
    earlier = (lax.broadcasted_iota(I32, (tm, tm), 1) < lax.broadcasted_iota(I32, (tm, tm), 0))
    rank = _dot(jnp.where(earlier, 1.0, 0.0).astype(BF16), msk.astype(BF16)) + seen_ref[0:1, :]
    seen = seen_ref[...] + jnp.sum(msk, axis=0, keepdims=True)
    seen_ref[...] = seen
    cnt_ref[...] = seen

    r_out = jnp.zeros((tm, LANES), F32)
    g_out = jnp.zeros((tm, LANES), F32)
    for k in range(TOP_K):
        rank_k = jnp.sum(jnp.where(lane_f == idxs[k], rank, 0.0), axis=-1, keepdims=True)
        r_out = jnp.where(lane == k, idxs[k], r_out)
        r_out = jnp.where(lane == TOP_K + k, rank_k, r_out)
        g_out = jnp.where(lane == k, exps[k] / den, g_out)
    route_ref[...] = r_out.astype(I32)
    gate_ref[...] = g_out


def _post_mix(a1, a2, lane_blk2, xs, mods, g, w_out, w_router, b_router, t_off, a_off):
    split_stream = isinstance(xs, tuple)
    if split_stream:
        assert t_off == 0
        b, s_len, d = xs[1].shape
        t = xs[0].shape[1] + s_len
        x_specs = _stream_specs(d)
    else:
        b, t, d = xs.shape
        x_specs = [pl.BlockSpec((None, TOK_TILE, d), lambda bi, ti: (bi, ti + t_off, 0))]
        xs = (xs,)
    half = w_out.shape[0] // 2
    nt = t // TOK_TILE - t_off
    t_out = nt * TOK_TILE
    row = functools.partial(_mod_row, ctx_row=b)
    tok = lambda bi, ti: (bi, ti, 0)
    act = jax.ShapeDtypeStruct((b, t_out, d), F32)
    return pl.pallas_call(
        functools.partial(_post_mix_kernel, split_stream=split_stream),
        out_shape=(act, act,
                   jax.ShapeDtypeStruct((b, t_out, LANES), I32),
                   jax.ShapeDtypeStruct((b, t_out, LANES), F32),
                   jax.ShapeDtypeStruct((SUBLANES, LANES), F32)),
        grid=(b, nt),
        in_specs=[pl.BlockSpec((None, TOK_TILE, half), lambda bi, ti: (bi, ti + a_off, 0)),
                  pl.BlockSpec((None, TOK_TILE, half), lambda bi, ti: (bi, ti + a_off, lane_blk2))] + x_specs + [
                  pl.BlockSpec((None, 6, d), lambda bi, ti: (row(bi, ti + t_off), 0, 0)),
                  pl.BlockSpec((1, d), lambda bi, ti: (0, 0)),
                  pl.BlockSpec(w_out.shape, lambda bi, ti: (0, 0)),
                  pl.BlockSpec(w_router.shape, lambda bi, ti: (0, 0)),
                  pl.BlockSpec((1, LANES), lambda bi, ti: (0, 0))],
        out_specs=(pl.BlockSpec((None, TOK_TILE, d), tok),
                   pl.BlockSpec((None, TOK_TILE, d), tok),
                   pl.BlockSpec((None, TOK_TILE, LANES), tok),
                   pl.BlockSpec((None, TOK_TILE, LANES), tok),
                   pl.BlockSpec((SUBLANES, LANES), lambda bi, ti: (0, 0))),
        scratch_shapes=[pltpu.VMEM((SUBLANES, LANES), F32)],
        compiler_params=_cparams(("arbitrary", "arbitrary")),
        name="post_mix",
    )(a1, a2, *xs, mods, g, w_out, w_router, b_router)


DMA_UNROLL = 2


def _row(ref, i):
    return ref.at[pl.ds(i, 1), :]


def _dispatch_kernel(pad_lo_ref, pad_n_ref, nu_ref, dest_ref, f_ref, xs_ref, zero_ref, sem, zsem):
    n = dest_ref.shape[0]

    blk = zero_ref.shape[0]
    n_blk = xs_ref.shape[0] // blk

    def block_copy(i):
        return pltpu.make_async_copy(zero_ref, xs_ref.at[pl.ds(pl.multiple_of(i * blk, blk), blk), :], zsem)

    def pad_rows(e, c, wait):
        def one(r, c2):
            copy = pltpu.make_async_copy(_row(zero_ref, 0), _row(xs_ref, pad_lo_ref[e] + r), zsem)
            copy.wait() if wait else copy.start()
            return c2

        return lax.fori_loop(0, pad_n_ref[e], one, c)

    @pl.when(pl.program_id(0) == 0)
    def _():
        zero_ref[...] = jnp.zeros(zero_ref.shape, F32)
        lax.fori_loop(nu_ref[0], n_blk, lambda i, c: (block_copy(i).start(), c)[1], 0)
        lax.fori_loop(0, N_EXPERTS, functools.partial(pad_rows, wait=False), 0)

    @pl.when(pl.program_id(0) == pl.num_programs(0) - 1)
    def _():
        lax.fori_loop(nu_ref[0], n_blk, lambda i, c: (block_copy(i).wait(), c)[1], 0)
        lax.fori_loop(0, N_EXPERTS, functools.partial(pad_rows, wait=True), 0)


    def issue(t, c):
        for k in range(TOP_K):
            pltpu.make_async_copy(_row(f_ref, t), _row(xs_ref, dest_ref[t * TOP_K + k]), sem).start()
        return c

    lax.fori_loop(0, n // TOP_K, issue, 0, unroll=DMA_UNROLL)
    pltpu.make_async_copy(xs_ref.at[pl.ds(0, n), :], xs_ref.at[pl.ds(0, n), :], sem).wait()


def _dispatch(f, dest, pad_lo, pad_n, n_used, rows):
    n, d = f.shape
    per = TOK_TILE * TOP_K
    return pl.pallas_call(
        _dispatch_kernel,
        out_shape=jax.ShapeDtypeStruct((rows, d), F32),
        grid_spec=pltpu.PrefetchScalarGridSpec(
            num_scalar_prefetch=3,
            grid=(n // TOK_TILE,),
            in_specs=[pl.BlockSpec((per,), lambda i, lo, cnt, nu: (i,), memory_space=pltpu.SMEM),
                      pl.BlockSpec((TOK_TILE, d), lambda i, lo, cnt, nu: (i, 0))],
            out_specs=pl.BlockSpec(memory_space=pl.ANY),
            scratch_shapes=[pltpu.VMEM((MOE_ROWS, d), F32), pltpu.SemaphoreType.DMA(()),
                            pltpu.SemaphoreType.DMA(())]),
        compiler_params=_cparams(("arbitrary",), has_side_effects=True, disable_bounds_checks=True),
        name="moe_dispatch",
    )(pad_lo, pad_n, n_used, dest, f)


def _experts_kernel(be_ref, nu_ref, xs_ref, wgu_ref, bgu_ref, wdn_ref, bdn_ref, y_ref, wgu_bf, wdn_bf):
    i = pl.program_id(0)

    @pl.when((i == 0) | (be_ref[i] != be_ref[jnp.maximum(i - 1, 0)]))
    def _():
        wgu_bf[...] = wgu_ref[...].astype(BF16)
        wdn_bf[...] = wdn_ref[...].astype(BF16)

    @pl.when(i < nu_ref[0])
    def _():
        h = _dot(xs_ref[...].astype(BF16), wgu_bf[...]) + bgu_ref[...]
        ff = h.shape[1] // 2
        hg = jnp.minimum(h[:, :ff], SWIGLU_LIMIT)
        hu = jnp.clip(h[:, ff:], -SWIGLU_LIMIT, SWIGLU_LIMIT)
        act = hg * jax.nn.sigmoid(SWIGLU_ALPHA * hg) * (hu + 1.0)
        y_ref[...] = _dot(act.astype(BF16), wdn_bf[...]) + bdn_ref[...]

    @pl.when(i >= nu_ref[0])
    def _():
        y_ref[...] = jnp.zeros(y_ref.shape, F32)


def _experts(xs, block_e, n_used, layer, wgu, bgu, wdn, bdn):
    rows, d = xs.shape
    ff2 = wgu.shape[-1]
    nb = rows // MOE_ROWS
    return pl.pallas_call(
        _experts_kernel,
        out_shape=jax.ShapeDtypeStruct((rows, d), F32),
        grid_spec=pltpu.PrefetchScalarGridSpec(
            num_scalar_prefetch=2,
            grid=(nb,),
            in_specs=[pl.BlockSpec((MOE_ROWS, d), lambda i, be, nu: (jnp.maximum(jnp.minimum(i, nu[0] - 1), 0), 0)),
                      pl.BlockSpec((None, None, d, ff2), lambda i, be, nu: (layer, be[i], 0, 0)),
                      pl.BlockSpec((None, None, 1, ff2), lambda i, be, nu: (layer, be[i], 0, 0)),
                      pl.BlockSpec((None, None, ff2 // 2, d), lambda i, be, nu: (layer, be[i], 0, 0)),
                      pl.BlockSpec((None, None, 1, d), lambda i, be, nu: (layer, be[i], 0, 0))],
            out_specs=pl.BlockSpec((MOE_ROWS, d), lambda i, be, nu: (i, 0)),
            scratch_shapes=[pltpu.VMEM((d, ff2), BF16), pltpu.VMEM((ff2 // 2, d), BF16)]),
        compiler_params=_cparams(("arbitrary",)),
        name="moe_experts",
    )(block_e, n_used, xs, wgu, bgu, wdn, bdn)


def _combine_kernel(dest_ref, y_ref, gate_ref, x_ref, mod_ref, o_ref, buf_ref, sem):
    n = dest_ref.shape[0]
    tm = n // TOP_K

    def issue(t, c):
        for k in range(TOP_K):
            pltpu.make_async_copy(_row(y_ref, dest_ref[t * TOP_K + k]), _row(buf_ref.at[k], t), sem).start()
        return c

    lax.fori_loop(0, tm, issue, 0, unroll=DMA_UNROLL)
    for k in range(TOP_K):
        pltpu.make_async_copy(y_ref.at[pl.ds(0, tm), :], buf_ref.at[k], sem).wait()

    gates = gate_ref[...]
    acc = gates[:, 0:1] * buf_ref[0]
    for k in range(1, TOP_K):
        acc = acc + gates[:, k:k + 1] * buf_ref[k]
    o_ref[...] = x_ref[...] + mod_ref[5:6, :] * acc


def _combine(y, dest, gates, xs, mods, t_off):
    b, t_out, d = xs.shape
    nt = t_out // TOK_TILE
    per = TOK_TILE * TOP_K
    row = functools.partial(_mod_row, ctx_row=b)
    tok = lambda bi, ti: (bi, ti, 0)
    return pl.pallas_call(
        _combine_kernel,
        out_shape=jax.ShapeDtypeStruct((b, t_out, d), F32),
        grid=(b, nt),
        in_specs=[pl.BlockSpec((per,), lambda bi, ti: (bi * nt + ti,), memory_space=pltpu.SMEM),
                  pl.BlockSpec(memory_space=pl.ANY),
                  pl.BlockSpec((None, TOK_TILE, LANES), tok),
                  pl.BlockSpec((None, TOK_TILE, d), tok),
                  pl.BlockSpec((None, 6, d), lambda bi, ti: (row(bi, ti + t_off), 0, 0))],
        out_specs=pl.BlockSpec((None, TOK_TILE, d), tok),
        scratch_shapes=[pltpu.VMEM((TOP_K, TOK_TILE, d), F32), pltpu.SemaphoreType.DMA(())],
        compiler_params=_cparams(("arbitrary", "arbitrary"), disable_bounds_checks=True),
        name="moe_combine",
    )(dest, y, gates, xs, mods)


def _routing(e_sel, rank, counts, n):
    padded = (counts + MOE_ROWS - 1) // MOE_ROWS * MOE_ROWS
    pend = jnp.cumsum(padded)
    pstart = pend - padded
    dest = jnp.sum(jnp.where(e_sel[..., None] == jnp.arange(N_EXPERTS), pstart, 0), axis=-1) + rank
    n_blocks = n * TOP_K // MOE_ROWS + N_EXPERTS
    first_row = jnp.arange(n_blocks) * MOE_ROWS
    block_e = jnp.minimum(jnp.sum(pend[None, :] <= first_row[:, None], axis=1), N_EXPERTS - 1)
    n_used = (pend[-1] // MOE_ROWS).reshape(1)
    pads = ((pstart + counts).astype(I32), (padded - counts).astype(I32))
    return dest.reshape(-1).astype(I32), block_e.astype(I32), n_used.astype(I32), pads, n_blocks * MOE_ROWS


def _moe(f, route, gates, seen, x_mid, mods, t_off, layer, wgu, bgu, wdn, bdn):
    b, t_out, d = f.shape
    n = b * t_out
    route = route.reshape(n, LANES)
    counts = seen[0, :N_EXPERTS].astype(I32)
    dest, block_e, n_used, pads, rows = _routing(route[:, :TOP_K], route[:, TOP_K:2 * TOP_K], counts, n)
    xs = _dispatch(f.reshape(n, d), dest, *pads, n_used, rows)
    y = _experts(xs, block_e, n_used, layer, wgu, bgu, wdn, bdn)
    return _combine(y, dest, gates, x_mid, mods, t_off)


def _axial_angles(n_rows, rot_dim):
    row = np.repeat(np.arange(n_rows, dtype=np.float32), GRID_W)
    col = np.tile(np.arange(GRID_W, dtype=np.float32), n_rows)
    n = rot_dim // 4
    freqs = (np.float32(ROPE_THETA) ** (-np.arange(n, dtype=np.float32) / np.float32(n))).astype(np.float32)
    return np.concatenate([row[:, None] * freqs, col[:, None] * freqs], axis=-1).astype(np.float32)


def _rope_tables(ctx_len, s_len, rot_dim, lane_base, reps):
    ang = _axial_angles(s_len // GRID_W, rot_dim)
    half = rot_dim // 2
    period = LANES // reps
    cos = np.ones((ctx_len + s_len, period), np.float32)
    slo = np.zeros((ctx_len + s_len, period), np.float32)
    shi = np.zeros((ctx_len + s_len, period), np.float32)
    cos[ctx_len:, lane_base:lane_base + rot_dim] = np.tile(np.cos(ang), (1, 2))
    slo[ctx_len:, lane_base:lane_base + half] = -np.sin(ang)
    shi[ctx_len:, lane_base + half:lane_base + rot_dim] = np.sin(ang)
    return [jnp.asarray(np.tile(tb, (1, reps))) for tb in (cos, slo, shi)]


def _segment_mean_matrix(width, period, segs):
    lane = np.arange(width)
    seg_id = np.full((width,), -1)
    seg_w = np.zeros((width,), np.float32)
    for i, (start, length) in enumerate(segs):
        inside = ((lane % period) >= start) & ((lane % period) < start + length)
        seg_id = np.where(inside, (lane // period) * len(segs) + i, seg_id)
        seg_w = np.where(inside, np.float32(1.0 / length), seg_w)
    same = (seg_id[:, None] == seg_id[None, :]) & (seg_id[:, None] >= 0)
    return jnp.asarray(np.where(same, seg_w[None, :], np.float32(0.0)), dtype=BF16)


def _block_diag(w):
    n, c, _ = w.shape
    eye = jnp.eye(n, dtype=w.dtype)
    return (eye[:, None, :, None] * w[:, :, None, :]).reshape(n * c, n * c)


def kernel(x, c, ctx, c_ctx, w_mod, b_mod, norm_mix, norm_ffn, w_in_even, lru_conv_w, lru_conv_b, lru_w_r, lru_b_r, lru_w_i, lru_b_i, lru_lambda, mla_q_a_norm, mla_w_q_b, mla_kv_a_norm, mla_w_kv_b, mla_nope_norm, mla_rope_norm, w_out_even, w_qkv_odd, gqa_qk_norm, gqa_sink, w_out_odd, w_router, b_router, w_gate_up, b_gate_up, w_down, b_down):
    b, s_len, d = x.shape
    ctx_len = ctx.shape[1]
    depth = w_mod.shape[0]
    assert depth == 2 and ctx_len == TOK_TILE and s_len % TOK_TILE == 0 and b + 1 <= SUBLANES
    lru_w = lru_conv_w.shape[-1]
    q_lora = mla_q_a_norm.shape[-1]
    kv_lora = mla_kv_a_norm.shape[-1]

    cvec = jnp.concatenate([c, c_ctx[None], jnp.zeros((SUBLANES - b - 1, d), F32)], axis=0)
    mods = _modulation(cvec, w_mod, b_mod).reshape(depth, SUBLANES, 6, d)

    wr_pad = jnp.zeros((depth, d, LANES), F32).at[:, :, :N_EXPERTS].set(w_router)
    br_pad = jnp.full((depth, 1, LANES), NEG_INF, F32).at[:, 0, :N_EXPERTS].set(b_router)
    experts = (w_gate_up, b_gate_up[:, :, None, :], w_down, b_down[:, :, None, :])

    n_in = w_in_even.shape[-1]
    n_in_pad = -(-n_in // LANES) * LANES
    w_in = jnp.zeros((d, n_in_pad), F32).at[:, :n_in].set(w_in_even[0]).astype(BF16)
    xa, ga, mla_in = _even_in(ctx, x, mods[0], norm_mix[0][None], w_in, lru_w)

    nh = lru_w // (2 * LANES)
    per = LRU_BLOCKS // nh
    blk = lru_w // LRU_BLOCKS
    w_gates = jnp.stack([
        jnp.stack([jnp.concatenate([_block_diag(lru_w_r[0, dd, h * per:(h + 1) * per]),
                                    _block_diag(lru_w_i[0, dd, h * per:(h + 1) * per])], axis=1)
                   for h in range(nh)]) for dd in range(2)]).astype(BF16)
    assert blk * per == 2 * LANES
    ya = _lru(xa, ga, lru_conv_w[0], lru_conv_b[0][:, None, :], w_gates, lru_b_r[0][:, None, :],
              lru_b_i[0][:, None, :], jax.nn.softplus(-lru_lambda[0])[:, None, :], ctx_len)

    qk = MLA_NOPE + MLA_ROPE
    hq = MLA_HEADS * LANES
    wq = jnp.zeros((q_lora, MLA_HEADS, LANES), F32).at[:, :, :qk].set(
        mla_w_q_b[0].reshape(q_lora, MLA_HEADS, qk)).reshape(q_lora, hq).astype(BF16)
    wkv = mla_w_kv_b[0].reshape(kv_lora, MLA_HEADS, MLA_NOPE + MLA_V)
    wk = jnp.zeros((kv_lora, MLA_HEADS, LANES), F32).at[:, :, :MLA_NOPE].set(
        wkv[:, :, :MLA_NOPE]).reshape(kv_lora, hq).astype(BF16)
    wv = jnp.zeros((kv_lora, MLA_HEADS, LANES), F32).at[:, :, :MLA_V].set(
        wkv[:, :, MLA_NOPE:]).reshape(kv_lora, hq).astype(BF16)
    one = jnp.asarray(np.tile(np.arange(LANES) == MLA_V, MLA_HEADS)[None], dtype=F32)
    zpad = jnp.zeros((LANES - qk,), F32)
    gq = jnp.tile(jnp.concatenate([mla_nope_norm[0, 0], mla_rope_norm[0, 0], zpad]), MLA_HEADS)[None]
    gk = jnp.tile(jnp.concatenate([mla_nope_norm[0, 1], jnp.zeros((LANES - MLA_NOPE,), F32)]), MLA_HEADS)[None]
    gkr = jnp.concatenate([mla_rope_norm[0, 1], jnp.zeros((LANES - MLA_ROPE,), F32)])[None]
    cos, slo, shi = _rope_tables(ctx_len, s_len, MLA_ROPE, MLA_NOPE, 1)
    mla_p = dict(qan=mla_q_a_norm[0][None], kvn=mla_kv_a_norm[0][None], wq=wq, wk=wk, wv=wv,
                 mq=_segment_mean_matrix(hq, LANES, [(0, MLA_NOPE), (MLA_NOPE, MLA_ROPE)]),
                 mk=_segment_mean_matrix(hq, LANES, [(0, MLA_NOPE)]),
                 gq=gq, gk=gk, gkr=gkr, one=one, cos=cos, slo=slo, shi=shi)
    q, k, v = _mla_prep(mla_in, mla_p, qk ** -0.5 * math.log2(math.e))
    yb = _mla_attn(q, k, v, ctx_len)

    x_mid, f, route, gates, seen = _post_mix(ya, yb, 0, (ctx, x), mods[0], norm_ffn[0][None], w_out_even[0].astype(BF16),
                                             wr_pad[0], br_pad[0], 0, 0)
    xs = _moe(f, route, gates, seen, x_mid, mods[0], 0, 0, *experts)

    nq = GQA_HEADS * GQA_DIM
    nkv = GQA_KV_HEADS * GQA_DIM
    cos, slo, shi = _rope_tables(ctx_len, s_len, GQA_DIM, 0, LANES // GQA_DIM)
    odd_p = dict(w=w_qkv_odd[0].astype(BF16),
                 mq=_segment_mean_matrix(nq, GQA_DIM, [(0, GQA_DIM)]),
                 mk=_segment_mean_matrix(nkv, GQA_DIM, [(0, GQA_DIM)]),
                 gq=jnp.tile(gqa_qk_norm[0, 0], GQA_HEADS)[None], gk=jnp.tile(gqa_qk_norm[0, 1], GQA_KV_HEADS)[None],
                 cos=cos, slo=slo, shi=shi)
    q, k, v = _odd_in(xs, mods[1], norm_mix[1][None], odd_p, GQA_DIM ** -0.5)
    o = _win_attn(q, k, v, gqa_sink[0], ctx_len)

    t_off = ctx_len // TOK_TILE
    x_mid, f, route, gates, seen = _post_mix(o, o, 1, xs, mods[1], norm_ffn[1][None], w_out_odd[0].astype(BF16),
                                             wr_pad[1], br_pad[1], t_off, 0)
    return _moe(f, route, gates, seen, x_mid, mods[1], t_off, 1, *experts)
```

```python
import functools
import math

import jax
import jax.numpy as jnp
import numpy as np
from jax import lax
from jax.experimental import pallas as pl
from jax.experimental.pallas import tpu as pltpu

F32 = jnp.float32
BF16 = jnp.bfloat16
I32 = jnp.int32

GRID_W = 64
LRU_BLOCKS = 8
LRU_C = 8.0
CONV_W = 4
MLA_HEADS = 8
MLA_NOPE = 64
MLA_ROPE = 32
MLA_V = 64
GQA_HEADS = 16
GQA_KV_HEADS = 4
GQA_DIM = 64
WINDOW = 128
ROPE_THETA = 10000.0
NEG_INF = -1e30
EPS = 1e-6
N_EXPERTS = 32
TOP_K = 4
SWIGLU_LIMIT = 7.0
SWIGLU_ALPHA = 1.702

LANES = 128
SUBLANES = 8
TOK_TILE = 256
LRU_CHUNK = 128
WIN_BLOCK = 128
MOE_ROWS = 256
VMEM_LIMIT = 48 * 1024 * 1024


def _cparams(sem, **kw):
    return pltpu.CompilerParams(dimension_semantics=sem, vmem_limit_bytes=VMEM_LIMIT, **kw)


def _dot(a, b):
    return jnp.dot(a, b, preferred_element_type=F32)


def _dot_nt(a, b):
    return lax.dot_general(a, b, (((1,), (1,)), ((), ())), preferred_element_type=F32)


def _split_bf16(x):
    hi = x.astype(BF16)
    lo = (x - hi.astype(F32)).astype(BF16)
    return hi, lo


def _dot3(a, w):
    ah, al = _split_bf16(a)
    wh, wl = _split_bf16(w)
    return _dot(ah, wh) + _dot(al, wh) + _dot(ah, wl)


def _rms(x):
    return x * lax.rsqrt(jnp.mean(x * x, axis=-1, keepdims=True) + EPS)


def _prenorm(x, g, scale, shift):
    return (_rms(x) * g) * (1.0 + scale) + shift


def _rope(x, cos, sin_lo, sin_hi, half):
    w = x.shape[-1]
    return x * cos + pltpu.roll(x, w - half, axis=1) * sin_lo + pltpu.roll(x, half, axis=1) * sin_hi


def _tile_lanes(t, reps):
    return jnp.concatenate([t] * reps, axis=1) if reps > 1 else t


def _mod_kernel(c_ref, w_ref, b_ref, o_ref):
    c = c_ref[...]
    o_ref[...] = _dot3(c * jax.nn.sigmoid(c), w_ref[...]) + b_ref[...]


def _modulation(cvec, w_mod, b_mod):
    depth, d, n = w_mod.shape
    tn = 1536
    return pl.pallas_call(
        _mod_kernel,
        out_shape=jax.ShapeDtypeStruct((depth, SUBLANES, n), F32),
        grid=(depth, n // tn),
        in_specs=[pl.BlockSpec((SUBLANES, d), lambda l, j: (0, 0)),
                  pl.BlockSpec((None, d, tn), lambda l, j: (l, 0, j)),
                  pl.BlockSpec((None, 1, tn), lambda l, j: (l, 0, j))],
        out_specs=pl.BlockSpec((None, SUBLANES, tn), lambda l, j: (l, 0, j)),
        compiler_params=_cparams(("arbitrary", "arbitrary")),
        name="modulation",
    )(cvec, w_mod, b_mod.reshape(depth, 1, n))


def _mod_row(b, t, ctx_row):
    return jnp.where(t == 0, ctx_row, b)


def _stream_tile(ctx_ref, lat_ref):
    return jnp.where(pl.program_id(1) == 0, ctx_ref[...], lat_ref[...])


def _stream_specs(d):
    return [pl.BlockSpec((None, TOK_TILE, d), lambda bi, ti: (bi, 0, 0)),
            pl.BlockSpec((None, TOK_TILE, d), lambda bi, ti: (bi, jnp.maximum(ti - 1, 0), 0))]


def _even_in_kernel(ctx_ref, x_ref, mod_ref, g_ref, w_ref, xa_ref, ga_ref, mla_ref):
    mod = mod_ref[...]
    h = _prenorm(_stream_tile(ctx_ref, x_ref), g_ref[...], mod[1:2], mod[0:1])
    z = _dot(h.astype(BF16), w_ref[...])
    c = xa_ref.shape[-1]
    xa_ref[...] = z[:, :c]
    ga_ref[...] = z[:, c:2 * c]
    mla_ref[...] = z[:, 2 * c:]


def _even_in(ctx, x, mods, g, w_pad, lru_w):
    b, s_len, d = x.shape
    t = ctx.shape[1] + s_len
    nt = t // TOK_TILE
    n_out = w_pad.shape[1]
    n_mla = n_out - 2 * lru_w
    row = functools.partial(_mod_row, ctx_row=b)
    tok = lambda bi, ti: (bi, ti, 0)
    return pl.pallas_call(
        _even_in_kernel,
        out_shape=(jax.ShapeDtypeStruct((b, t, lru_w), F32),
                   jax.ShapeDtypeStruct((b, t, lru_w), F32),
                   jax.ShapeDtypeStruct((b, t, n_mla), F32)),
        grid=(b, nt),
        in_specs=_stream_specs(d) + [
            pl.BlockSpec((None, 6, d), lambda bi, ti: (row(bi, ti), 0, 0)),
            pl.BlockSpec((1, d), lambda bi, ti: (0, 0)),
            pl.BlockSpec((d, n_out), lambda bi, ti: (0, 0))],
        out_specs=(pl.BlockSpec((None, TOK_TILE, lru_w), tok),
                   pl.BlockSpec((None, TOK_TILE, lru_w), tok),
                   pl.BlockSpec((None, TOK_TILE, n_mla), tok)),
        compiler_params=_cparams(("arbitrary", "arbitrary")),
        name="even_in",
    )(ctx, x, mods, g, w_pad)


def _lru_kernel(xa_ref, ga_ref, cw_ref, cb_ref, wg_ref, br_ref, bi_ref, sp_ref, o_ref, pad_ref, rec_ref, *, ctx_len):
    t, c = xa_ref.shape
    tc = LRU_CHUNK
    halo = SUBLANES
    n_chunks = t // tc
    n_ctx = ctx_len // tc
    groups = tc // SUBLANES

    pad_ref[0:halo, :] = jnp.zeros((halo, c), F32)
    pad_ref[t + halo:t + 2 * halo, :] = jnp.zeros((halo, c), F32)
    pad_ref[halo:t + halo, :] = xa_ref[...]

    rid = lax.broadcasted_iota(I32, (tc, 1), 0)
    sub = rid % SUBLANES

    for d in range(2):
        cw = cw_ref[d]
        cb = cb_ref[d]
        wg = wg_ref[d]
        b_r = br_ref[d]
        b_i = bi_ref[d]
        sp = sp_ref[d]

        def chunk(i, h, d=d, cw=cw, cb=cb, wg=wg, b_r=b_r, b_i=b_i, sp=sp):
            if d == 0:
                ci = i
            else:
                ci = jnp.where(i < n_ctx, n_ctx - 1 - i, n_chunks - 1 - (i - n_ctx))
            r0 = pl.multiple_of(ci * tc, tc)
            win = pad_ref[pl.ds(r0, tc + 2 * halo), :]
            if d == 0:
                past = jnp.where(ci == n_ctx, 0.0, win[0:halo])
                win = jnp.concatenate([past, win[halo:]], axis=0)
            else:
                past = jnp.where(ci == n_ctx - 1, 0.0, win[halo + tc:])
                win = jnp.concatenate([win[:halo + tc], past], axis=0)
            xc = jnp.zeros((tc, c), F32) + cb
            for k in range(CONV_W):
                off = (k - (CONV_W - 1)) if d == 0 else ((CONV_W - 1) - k)
                if off == 0:
                    src = win[halo:halo + tc]
                else:
                    src = pltpu.roll(win, (-off) % (tc + 2 * halo), axis=0)[halo:halo + tc]
                xc = xc + cw[k:k + 1] * src
            gz = _dot(xc.astype(BF16), wg)
            r = jax.nn.sigmoid(gz[:, :c] + b_r)
            gi = jax.nn.sigmoid(gz[:, c:] + b_i)
            log_a = (-LRU_C) * r * sp
            a = jnp.exp(log_a)
            th = jnp.tanh(log_a)
            bb = jnp.sqrt(-2.0 * th / (1.0 - th)) * (gi * xc)
            for s in (1, 2, 4):
                if d == 0:
                    ok = sub >= s
                    sh = s
                else:
                    ok = sub <= (SUBLANES - 1 - s)
                    sh = tc - s
                a_prev = jnp.where(ok, pltpu.roll(a, sh, axis=0), 1.0)
                b_prev = jnp.where(ok, pltpu.roll(bb, sh, axis=0), 0.0)
                bb = a * b_prev + bb
                a = a * a_prev
            outs = [None] * groups
            order = range(groups) if d == 0 else range(groups - 1, -1, -1)
            for g in order:
                lo = g * SUBLANES
                hg = a[lo:lo + SUBLANES] * h + bb[lo:lo + SUBLANES]
                outs[g] = hg
                h = hg[SUBLANES - 1:SUBLANES] if d == 0 else hg[0:1]
            hs = jnp.concatenate(outs, axis=0)
            if d == 0:
                rec_ref[pl.ds(r0, tc), :] = hs
            else:
                tot = rec_ref[pl.ds(r0, tc), :] + hs
                gate = jax.nn.gelu(ga_ref[pl.ds(r0, tc), :], approximate=True)
                o_ref[pl.ds(r0, tc), :] = (tot * gate).astype(o_ref.dtype)
            return h

        lax.fori_loop(0, n_chunks, chunk, jnp.zeros((1, c), F32))


def _lru(xa, ga, conv_w, conv_b, w_gates, b_r, b_i, sp, ctx_len):
    b, t, w = xa.shape
    c = 2 * LANES
    nh = w // c
    tok = lambda bi, hi: (bi, 0, hi)
    par = lambda bi, hi: (0, 0, hi)
    return pl.pallas_call(
        functools.partial(_lru_kernel, ctx_len=ctx_len),
        out_shape=jax.ShapeDtypeStruct((b, t, w), BF16),
        grid=(b, nh),
        in_specs=[pl.BlockSpec((None, t, c), tok),
                  pl.BlockSpec((None, t, c), tok),
                  pl.BlockSpec((2, CONV_W, c), par),
                  pl.BlockSpec((2, 1, c), par),
                  pl.BlockSpec((2, None, c, 2 * c), lambda bi, hi: (0, hi, 0, 0)),
                  pl.BlockSpec((2, 1, c), par),
                  pl.BlockSpec((2, 1, c), par),
                  pl.BlockSpec((2, 1, c), par)],
        out_specs=pl.BlockSpec((None, t, c), tok),
        scratch_shapes=[pltpu.VMEM((t + 2 * SUBLANES, c), F32), pltpu.VMEM((t, c), F32)],
        compiler_params=_cparams(("arbitrary", "arbitrary")),
        name="rglru",
    )(xa, ga, conv_w, conv_b, w_gates, b_r, b_i, sp)


def _mla_prep_kernel(in_ref, qan_ref, kvn_ref, wq_ref, wk_ref, wv_ref, mq_ref, mk_ref, gq_ref, gk_ref, gkr_ref,
                     one_ref, cos_ref, slo_ref, shi_ref, q_ref, k_ref, v_ref, *, q_lora, kv_lora, scale):
    z = in_ref[...]
    heads = q_ref.shape[-1] // LANES
    cos = cos_ref[...]
    slo = slo_ref[...]
    shi = shi_ref[...]
    half = MLA_ROPE // 2

    qan = (_rms(z[:, :q_lora]) * qan_ref[...]).astype(BF16)
    q = _dot(qan, wq_ref[...])
    q = q * lax.rsqrt(_dot((q * q).astype(BF16), mq_ref[...]) + EPS) * gq_ref[...]
    q = _rope(q, _tile_lanes(cos, heads), _tile_lanes(slo, heads), _tile_lanes(shi, heads), half)
    q_ref[...] = (q * scale).astype(BF16)

    kvn = (_rms(z[:, q_lora:q_lora + kv_lora]) * kvn_ref[...]).astype(BF16)
    kk = _dot(kvn, wk_ref[...])
    kk = kk * lax.rsqrt(_dot((kk * kk).astype(BF16), mk_ref[...]) + EPS) * gk_ref[...]
    v_ref[...] = (_dot(kvn, wv_ref[...]) + one_ref[...]).astype(BF16)

    kr = z[:, q_lora + kv_lora:]
    kr = kr * lax.rsqrt(jnp.sum(kr * kr, axis=-1, keepdims=True) * (1.0 / MLA_ROPE) + EPS) * gkr_ref[...]
    kr = pltpu.roll(kr, MLA_NOPE, axis=1)
    kr = _rope(kr, cos, slo, shi, half)
    k_ref[...] = (kk + _tile_lanes(kr, heads)).astype(BF16)


def _mla_prep(mla_in, p, scale):
    b, t, w = mla_in.shape
    hq = MLA_HEADS * LANES
    tok = lambda bi, ti: (bi, ti, 0)
    full = lambda a: pl.BlockSpec(a.shape, lambda bi, ti: (0,) * a.ndim)
    pos = pl.BlockSpec((TOK_TILE, LANES), lambda bi, ti: (ti, 0))
    consts = [p["qan"], p["kvn"], p["wq"], p["wk"], p["wv"], p["mq"], p["mk"], p["gq"], p["gk"], p["gkr"], p["one"]]
    out = jax.ShapeDtypeStruct((b, t, hq), BF16)
    return pl.pallas_call(
        functools.partial(_mla_prep_kernel, q_lora=p["qan"].shape[1], kv_lora=p["kvn"].shape[1], scale=scale),
        out_shape=(out, out, out),
        grid=(b, t // TOK_TILE),
        in_specs=[pl.BlockSpec((None, TOK_TILE, w), tok)] + [full(a) for a in consts] + [pos, pos, pos],
        out_specs=(pl.BlockSpec((None, TOK_TILE, hq), tok),) * 3,
        compiler_params=_cparams(("arbitrary", "arbitrary")),
        name="mla_prep",
    )(mla_in, *consts, p["cos"], p["slo"], p["shi"])


def _mla_attn_kernel(q_ref, k_ref, v_ref, o_ref, *, ctx_len):
    tq = q_ref.shape[0]
    t = k_ref.shape[0]
    lane = lax.broadcasted_iota(I32, (tq, LANES), 1)

    def attend(nk):
        outs = []
        for hh in range(2):
            q = q_ref[:, hh * LANES:(hh + 1) * LANES]
            k = k_ref[0:nk, hh * LANES:(hh + 1) * LANES]
            s = _dot_nt(q, k)
            p = jnp.exp2(s - jnp.max(s, axis=-1, keepdims=True))
            o = _dot(p.astype(BF16), v_ref[0:nk, hh * LANES:(hh + 1) * LANES])
            outs.append(o / o[:, MLA_V:MLA_V + 1])
        o_ref[...] = jnp.where(lane < MLA_V, outs[0], pltpu.roll(outs[1], MLA_V, axis=1)).astype(o_ref.dtype)

    @pl.when(pl.program_id(2) == 0)
    def _():
        attend(ctx_len)

    @pl.when(pl.program_id(2) > 0)
    def _():
        attend(t)


def _mla_attn(q, k, v, ctx_len):
    b, t, hq = q.shape
    pairs = hq // (2 * LANES)
    return pl.pallas_call(
        functools.partial(_mla_attn_kernel, ctx_len=ctx_len),
        out_shape=jax.ShapeDtypeStruct((b, t, pairs * 2 * MLA_V), BF16),
        grid=(b, pairs, t // TOK_TILE),
        in_specs=[pl.BlockSpec((None, TOK_TILE, 2 * LANES), lambda bi, hi, ti: (bi, ti, hi)),
                  pl.BlockSpec((None, t, 2 * LANES), lambda bi, hi, ti: (bi, 0, hi)),
                  pl.BlockSpec((None, t, 2 * LANES), lambda bi, hi, ti: (bi, 0, hi))],
        out_specs=pl.BlockSpec((None, TOK_TILE, LANES), lambda bi, hi, ti: (bi, ti, hi)),
        compiler_params=_cparams(("arbitrary", "arbitrary", "arbitrary")),
        name="mla_attn",
    )(q, k, v)


def _odd_in_kernel(x_ref, mod_ref, g_ref, w_ref, mq_ref, mk_ref, gq_ref, gk_ref, cos_ref, slo_ref, shi_ref,
                   q_ref, k_ref, v_ref, *, scale):
    mod = mod_ref[...]
    h = _prenorm(x_ref[...], g_ref[...], mod[1:2], mod[0:1]).astype(BF16)
    nq = q_ref.shape[-1]
    nk = k_ref.shape[-1]
    half = GQA_DIM // 2

    def head_norm_rope(y, m_ref, gain_ref):
        reps = y.shape[-1] // LANES
        y = y * lax.rsqrt(_dot((y * y).astype(BF16), m_ref[...]) + EPS) * gain_ref[...]
        return _rope(y, _tile_lanes(cos_ref[...], reps), _tile_lanes(slo_ref[...], reps),
                     _tile_lanes(shi_ref[...], reps), half)

    q_ref[...] = (head_norm_rope(_dot(h, w_ref[:, 0:nq]), mq_ref, gq_ref) * scale).astype(BF16)
    k_ref[...] = head_norm_rope(_dot(h, w_ref[:, nq:nq + nk]), mk_ref, gk_ref).astype(BF16)
    v_ref[...] = _dot(h, w_ref[:, nq + nk:nq + 2 * nk]).astype(BF16)


def _odd_in(xs, mods, g, p, scale):
    b, t, d = xs.shape
    nq = p["gq"].shape[1]
    nk = p["gk"].shape[1]
    row = functools.partial(_mod_row, ctx_row=b)
    tok = lambda bi, ti: (bi, ti, 0)
    full = lambda a: pl.BlockSpec(a.shape, lambda bi, ti: (0,) * a.ndim)
    pos = pl.BlockSpec((TOK_TILE, LANES), lambda bi, ti: (ti, 0))
    kv = jax.ShapeDtypeStruct((b, t, nk), BF16)
    return pl.pallas_call(
        functools.partial(_odd_in_kernel, scale=scale),
        out_shape=(jax.ShapeDtypeStruct((b, t, nq), BF16), kv, kv),
        grid=(b, t // TOK_TILE),
        in_specs=[pl.BlockSpec((None, TOK_TILE, d), tok),
                  pl.BlockSpec((None, 6, d), lambda bi, ti: (row(bi, ti), 0, 0)),
                  pl.BlockSpec((1, d), lambda bi, ti: (0, 0)),
                  full(p["w"]), full(p["mq"]), full(p["mk"]), full(p["gq"]), full(p["gk"]), pos, pos, pos],
        out_specs=(pl.BlockSpec((None, TOK_TILE, nq), tok),
                   pl.BlockSpec((None, TOK_TILE, nk), tok),
                   pl.BlockSpec((None, TOK_TILE, nk), tok)),
        compiler_params=_cparams(("arbitrary", "arbitrary")),
        name="odd_in",
    )(xs, mods, g, p["w"], p["mq"], p["mk"], p["gq"], p["gk"], p["cos"], p["slo"], p["shi"])


def _win_attn_kernel(sink_ref, q_ref, k_ref, v_ref, o_ref, *, ctx_len):
    t = k_ref.shape[0]
    wb = WIN_BLOCK
    w3 = 3 * wb
    hd = GQA_DIM
    group = GQA_HEADS // GQA_KV_HEADS
    i = pl.program_id(1)
    q0 = ctx_len + i * wb
    ws = pl.multiple_of(jnp.clip(q0 - wb, ctx_len, t - w3), wb)
    nk = w3 + ctx_len
    rows = group * wb
    row = lax.broadcasted_iota(I32, (rows, nk), 0)
    col = lax.broadcasted_iota(I32, (rows, nk), 1)
    valid = (col >= w3) | (jnp.abs(q0 + row % wb - (ws + col)) <= WINDOW)
    head_of_row = lax.broadcasted_iota(I32, (rows, 1), 0) // wb
    q = q_ref[...].astype(F32)
    kcat = jnp.concatenate([k_ref[pl.ds(ws, w3), :], k_ref[0:ctx_len, :]], axis=0).astype(F32)
    vcat = jnp.concatenate([v_ref[pl.ds(ws, w3), :], v_ref[0:ctx_len, :]], axis=0).astype(F32)
    outs = []
    for kh in range(GQA_KV_HEADS):
        qs = jnp.concatenate([q[:, (kh * group + h) * hd:(kh * group + h + 1) * hd] for h in range(group)], axis=0)
        s = _dot_nt(qs.astype(BF16), kcat[:, kh * hd:(kh + 1) * hd].astype(BF16))
        s = jnp.where(valid, s, NEG_INF)
        sink = jnp.zeros((rows, 1), F32)
        for h in range(group):
            sink = jnp.where(head_of_row == h, sink_ref[kh * group + h], sink)
        m = jnp.maximum(jnp.max(s, axis=-1, keepdims=True), sink)
        p = jnp.exp(s - m)
        l = jnp.sum(p, axis=-1, keepdims=True) + jnp.exp(sink - m)
        o = _dot(p.astype(BF16), vcat[:, kh * hd:(kh + 1) * hd].astype(BF16)) / l
        outs.extend(o[h * wb:(h + 1) * wb] for h in range(group))
    o_ref[...] = jnp.concatenate(outs, axis=1).astype(o_ref.dtype)


def _win_attn(q, k, v, sink, ctx_len):
    b, t, n = q.shape
    nkv = k.shape[-1]
    s_len = t - ctx_len
    off = ctx_len // WIN_BLOCK
    return pl.pallas_call(
        functools.partial(_win_attn_kernel, ctx_len=ctx_len),
        out_shape=jax.ShapeDtypeStruct((b, s_len, n), BF16),
        grid=(b, s_len // WIN_BLOCK),
        in_specs=[pl.BlockSpec(memory_space=pltpu.SMEM),
                  pl.BlockSpec((None, WIN_BLOCK, n), lambda bi, ti: (bi, ti + off, 0)),
                  pl.BlockSpec((None, t, nkv), lambda bi, ti: (bi, 0, 0)),
                  pl.BlockSpec((None, t, nkv), lambda bi, ti: (bi, 0, 0))],
        out_specs=pl.BlockSpec((None, WIN_BLOCK, n), lambda bi, ti: (bi, ti, 0)),
        compiler_params=_cparams(("arbitrary", "arbitrary")),
        name="win_attn",
    )(sink, q, k, v)


def _post_mix_kernel(a1_ref, a2_ref, *refs, split_stream):
    x_in = _stream_tile(refs[0], refs[1]) if split_stream else refs[0][...]
    _post_mix_body(a1_ref, a2_ref, x_in, *refs[2 if split_stream else 1:])


def _post_mix_body(a1_ref, a2_ref, x_in, mod_ref, g_ref, w_ref, wr_ref, br_ref,
                   xo_ref, f_ref, route_ref, gate_ref, cnt_ref, seen_ref):
    half = a1_ref.shape[-1]
    mod = mod_ref[...]
    m = _dot(a1_ref[...], w_ref[0:half, :]) + _dot(a2_ref[...], w_ref[half:2 * half, :])
    x = x_in + mod[2:3] * m
    xo_ref[...] = x
    f = _prenorm(x, g_ref[...], mod[4:5], mod[3:4])
    f_ref[...] = f

    logit = _dot3(f, wr_ref[...]) + br_ref[...]
    tm = logit.shape[0]
    lane = lax.broadcasted_iota(I32, (tm, LANES), 1)
    lane_f = lane.astype(F32)
    vals, idxs = [], []
    for _ in range(TOP_K):
        mx = jnp.max(logit, axis=-1, keepdims=True)
        ix = jnp.min(jnp.where(logit == mx, lane_f, float(LANES)), axis=-1, keepdims=True)
        vals.append(mx)
        idxs.append(ix)
        logit = jnp.where(lane_f == ix, -jnp.inf, logit)
    exps = [jnp.exp(v - vals[0]) for v in vals]
    den = exps[0]
    for e in exps[1:]:
        den = den + e

    @pl.when((pl.program_id(0) == 0) & (pl.program_id(1) == 0))
    def _():
        seen_ref[...] = jnp.zeros(seen_ref.shape, F32)

    msk = jnp.zeros((tm, LANES), F32)
    for k in range(TOP_K):
        msk = jnp.where(lane_f == idxs[k], 1.0, msk)
    earlier = (lax.broadcasted_iota(I32, (tm, tm), 1) < lax.broadcasted_iota(I32, (tm, tm), 0))
    rank = _dot(jnp.where(earlier, 1.0, 0.0).astype(BF16), msk.astype(BF16)) + seen_ref[0:1, :]
    seen = seen_ref[...] + jnp.sum(msk, axis=0, keepdims=True)
    seen_ref[...] = seen
    cnt_ref[...] = seen

    r_out = jnp.zeros((tm, LANES), F32)
    g_out = jnp.zeros((tm, LANES), F32)
    for k in range(TOP_K):
        rank_k = jnp.sum(jnp.where(lane_f == idxs[k], rank, 0.0), axis=-1, keepdims=True)
        r_out = jnp.where(lane == k, idxs[k], r_out)
        r_out = jnp.where(lane == TOP_K + k, rank_k, r_out)
        g_out = jnp.where(lane == k, exps[k] / den, g_out)
    route_ref[...] = r_out.astype(I32)
    gate_ref[...] = g_out


def _post_mix(a1, a2, lane_blk2, xs, mods, g, w_out, w_router, b_router, t_off, a_off):
    split_stream = isinstance(xs, tuple)
    if split_stream:
        assert t_off == 0
        b, s_len, d = xs[1].shape
        t = xs[0].shape[1] + s_len
        x_specs = _stream_specs(d)
    else:
        b, t, d = xs.shape
        x_specs = [pl.BlockSpec((None, TOK_TILE, d), lambda bi, ti: (bi, ti + t_off, 0))]
        xs = (xs,)
    half = w_out.shape[0] // 2
    nt = t // TOK_TILE - t_off
    t_out = nt * TOK_TILE
    row = functools.partial(_mod_row, ctx_row=b)
    tok = lambda bi, ti: (bi, ti, 0)
    act = jax.ShapeDtypeStruct((b, t_out, d), F32)
    return pl.pallas_call(
        functools.partial(_post_mix_kernel, split_stream=split_stream),
        out_shape=(act, act,
                   jax.ShapeDtypeStruct((b, t_out, LANES), I32),
                   jax.ShapeDtypeStruct((b, t_out, LANES), F32),
                   jax.ShapeDtypeStruct((SUBLANES, LANES), F32)),
        grid=(b, nt),
        in_specs=[pl.BlockSpec((None, TOK_TILE, half), lambda bi, ti: (bi, ti + a_off, 0)),
                  pl.BlockSpec((None, TOK_TILE, half), lambda bi, ti: (bi, ti + a_off, lane_blk2))] + x_specs + [
                  pl.BlockSpec((None, 6, d), lambda bi, ti: (row(bi, ti + t_off), 0, 0)),
                  pl.BlockSpec((1, d), lambda bi, ti: (0, 0)),
                  pl.BlockSpec(w_out.shape, lambda bi, ti: (0, 0)),
                  pl.BlockSpec(w_router.shape, lambda bi, ti: (0, 0)),
                  pl.BlockSpec((1, LANES), lambda bi, ti: (0, 0))],
        out_specs=(pl.BlockSpec((None, TOK_TILE, d), tok),
                   pl.BlockSpec((None, TOK_TILE, d), tok),
                   pl.BlockSpec((None, TOK_TILE, LANES), tok),
                   pl.BlockSpec((None, TOK_TILE, LANES), tok),
                   pl.BlockSpec((SUBLANES, LANES), lambda bi, ti: (0, 0))),
        scratch_shapes=[pltpu.VMEM((SUBLANES, LANES), F32)],
        compiler_params=_cparams(("arbitrary", "arbitrary")),
        name="post_mix",
    )(a1, a2, *xs, mods, g, w_out, w_router, b_router)


DMA_UNROLL = 2


def _row(ref, i):
    return ref.at[pl.ds(i, 1), :]


def _dispatch_kernel(pad_lo_ref, pad_n_ref, nu_ref, dest_ref, f_ref, xs_ref, zero_ref, sem, zsem):
    n = dest_ref.shape[0]

    blk = zero_ref.shape[0]
    n_blk = xs_ref.shape[0] // blk

    def block_copy(i):
        return pltpu.make_async_copy(zero_ref, xs_ref.at[pl.ds(pl.multiple_of(i * blk, blk), blk), :], zsem)

    def pad_rows(e, c, wait):
        def one(r, c2):
            copy = pltpu.make_async_copy(_row(zero_ref, 0), _row(xs_ref, pad_lo_ref[e] + r), zsem)
            copy.wait() if wait else copy.start()
            return c2

        return lax.fori_loop(0, pad_n_ref[e], one, c)

    @pl.when(pl.program_id(0) == 0)
    def _():
        zero_ref[...] = jnp.zeros(zero_ref.shape, F32)
        lax.fori_loop(nu_ref[0], n_blk, lambda i, c: (block_copy(i).start(), c)[1], 0)
        lax.fori_loop(0, N_EXPERTS, functools.partial(pad_rows, wait=False), 0)

    @pl.when(pl.program_id(0) == pl.num_programs(0) - 1)
    def _():
        lax.fori_loop(nu_ref[0], n_blk, lambda i, c: (block_copy(i).wait(), c)[1], 0)
        lax.fori_loop(0, N_EXPERTS, functools.partial(pad_rows, wait=True), 0)


    def issue(t, c):
        for k in range(TOP_K):
            pltpu.make_async_copy(_row(f_ref, t), _row(xs_ref, dest_ref[t * TOP_K + k]), sem).start()
        return c

    lax.fori_loop(0, n // TOP_K, issue, 0, unroll=DMA_UNROLL)
    pltpu.make_async_copy(xs_ref.at[pl.ds(0, n), :], xs_ref.at[pl.ds(0, n), :], sem).wait()


def _dispatch(f, dest, pad_lo, pad_n, n_used, rows):
    n, d = f.shape
    per = TOK_TILE * TOP_K
    return pl.pallas_call(
        _dispatch_kernel,
        out_shape=jax.ShapeDtypeStruct((rows, d), F32),
        grid_spec=pltpu.PrefetchScalarGridSpec(
            num_scalar_prefetch=3,
            grid=(n // TOK_TILE,),
            in_specs=[pl.BlockSpec((per,), lambda i, lo, cnt, nu: (i,), memory_space=pltpu.SMEM),
                      pl.BlockSpec((TOK_TILE, d), lambda i, lo, cnt, nu: (i, 0))],
            out_specs=pl.BlockSpec(memory_space=pl.ANY),
            scratch_shapes=[pltpu.VMEM((MOE_ROWS, d), F32), pltpu.SemaphoreType.DMA(()),
                            pltpu.SemaphoreType.DMA(())]),
        compiler_params=_cparams(("arbitrary",), has_side_effects=True, disable_bounds_checks=True),
        name="moe_dispatch",
    )(pad_lo, pad_n, n_used, dest, f)


def _experts_kernel(be_ref, nu_ref, xs_ref, wgu_ref, bgu_ref, wdn_ref, bdn_ref, y_ref, wgu_bf, wdn_bf):
    i = pl.program_id(0)

    @pl.when((i == 0) | (be_ref[i] != be_ref[jnp.maximum(i - 1, 0)]))
    def _():
        wgu_bf[...] = wgu_ref[...].astype(BF16)
        wdn_bf[...] = wdn_ref[...].astype(BF16)

    @pl.when(i < nu_ref[0])
    def _():
        h = _dot(xs_ref[...].astype(BF16), wgu_bf[...]) + bgu_ref[...]
        ff = h.shape[1] // 2
        hg = jnp.minimum(h[:, :ff], SWIGLU_LIMIT)
        hu = jnp.clip(h[:, ff:], -SWIGLU_LIMIT, SWIGLU_LIMIT)
        act = hg * jax.nn.sigmoid(SWIGLU_ALPHA * hg) * (hu + 1.0)
        y_ref[...] = _dot(act.astype(BF16), wdn_bf[...]) + bdn_ref[...]

    @pl.when(i >= nu_ref[0])
    def _():
        y_ref[...] = jnp.zeros(y_ref.shape, F32)


def _experts(xs, block_e, n_used, layer, wgu, bgu, wdn, bdn):
    rows, d = xs.shape
    ff2 = wgu.shape[-1]
    nb = rows // MOE_ROWS
    return pl.pallas_call(
        _experts_kernel,
        out_shape=jax.ShapeDtypeStruct((rows, d), F32),
        grid_spec=pltpu.PrefetchScalarGridSpec(
            num_scalar_prefetch=2,
            grid=(nb,),
            in_specs=[pl.BlockSpec((MOE_ROWS, d), lambda i, be, nu: (jnp.maximum(jnp.minimum(i, nu[0] - 1), 0), 0)),
                      pl.BlockSpec((None, None, d, ff2), lambda i, be, nu: (layer, be[i], 0, 0)),
                      pl.BlockSpec((None, None, 1, ff2), lambda i, be, nu: (layer, be[i], 0, 0)),
                      pl.BlockSpec((None, None, ff2 // 2, d), lambda i, be, nu: (layer, be[i], 0, 0)),
                      pl.BlockSpec((None, None, 1, d), lambda i, be, nu: (layer, be[i], 0, 0))],
            out_specs=pl.BlockSpec((MOE_ROWS, d), lambda i, be, nu: (i, 0)),
            scratch_shapes=[pltpu.VMEM((d, ff2), BF16), pltpu.VMEM((ff2 // 2, d), BF16)]),
        compiler_params=_cparams(("arbitrary",)),
        name="moe_experts",
    )(block_e, n_used, xs, wgu, bgu, wdn, bdn)


IDX_BLOCK = 1024


def _fused_kernel(be_ref, nu_ref, tok0_ref, tok_ref, tgt_ref, f_ref, wgu_ref, bgu_ref, wdn_ref, bdn_ref, out_ref,
                  wgu_bf, wdn_bf, x0, x1, y0, y1, gsem, ssem):
    i = pl.program_id(0)
    nu = nu_ref[0]
    rows = MOE_ROWS
    per = IDX_BLOCK // rows
    xb = (x0, x1)
    yb = (y0, y1)

    def gather(idx_ref, base, slot):
        for j in range(rows):
            pltpu.make_async_copy(_row(f_ref, idx_ref[base + j]), _row(xb[slot], j), gsem.at[slot]).start()

    def scatter(base, slot):
        for j in range(rows):
            pltpu.make_async_copy(_row(yb[slot], j), _row(out_ref, tgt_ref[base + j]), ssem.at[slot]).start()

    def wait_rows(buf, sem):
        pltpu.make_async_copy(buf, buf, sem).wait()

    @pl.when(i == 0)
    def _():
        y1[...] = jnp.zeros(y1.shape, F32)
        n_real = out_ref.shape[0] - 2 * rows
        c = pltpu.make_async_copy(y1, out_ref.at[pl.ds(n_real, rows), :], ssem.at[0])
        c.start()
        c.wait()
        gather(tok0_ref, 0, 0)

    @pl.when((i <= nu) & ((i == 0) | (be_ref[i] != be_ref[jnp.maximum(i - 1, 0)])))
    def _():
        wgu_bf[...] = wgu_ref[...].astype(BF16)
        wdn_bf[...] = wdn_ref[...].astype(BF16)

    gbase = ((i + 1) % per) * rows
    sbase = (i % per) * rows

    for slot in range(2):
        other = 1 - slot

        @pl.when((i % 2 == slot) & (i < nu))
        def _(slot=slot, other=other):
            wait_rows(xb[slot], gsem.at[slot])

            @pl.when(i >= 1)
            def _():
                wait_rows(yb[slot], ssem.at[slot])

            gather(tok_ref, gbase, other)
            scatter(sbase, other)
            h = _dot(xb[slot][...].astype(BF16), wgu_bf[...]) + bgu_ref[...]
            ff = h.shape[1] // 2
            hg = jnp.minimum(h[:, :ff], SWIGLU_LIMIT)
            hu = jnp.clip(h[:, ff:], -SWIGLU_LIMIT, SWIGLU_LIMIT)
            act = hg * jax.nn.sigmoid(SWIGLU_ALPHA * hg) * (hu + 1.0)
            yb[slot][...] = _dot(act.astype(BF16), wdn_bf[...]) + bdn_ref[...]

        @pl.when((i % 2 == slot) & (i == nu))
        def _(slot=slot, other=other):
            wait_rows(xb[slot], gsem.at[slot])

            @pl.when(i >= 1)
            def _():
                wait_rows(yb[slot], ssem.at[slot])

            scatter(sbase, other)
            wait_rows(yb[other], ssem.at[other])


def _experts_fused(f, row_tok, row_tgt, block_e, n_used, layer, wgu, bgu, wdn, bdn):
    n, d = f.shape
    ff2 = wgu.shape[-1]
    nb = row_tok.shape[0] // MOE_ROWS
    per = IDX_BLOCK // MOE_ROWS
    last = row_tok.shape[0] // IDX_BLOCK - 1
    last_tgt = row_tgt.shape[0] // IDX_BLOCK - 1
    buf = pltpu.VMEM((MOE_ROWS, d), F32)
    w_idx = lambda i, be, nu: (layer, be[i], 0, 0)
    return pl.pallas_call(
        _fused_kernel,
        out_shape=jax.ShapeDtypeStruct((TOP_K * n + 2 * MOE_ROWS, d), F32),
        grid_spec=pltpu.PrefetchScalarGridSpec(
            num_scalar_prefetch=2,
            grid=(nb + 1,),
            in_specs=[pl.BlockSpec((IDX_BLOCK,), lambda i, be, nu: (0,), memory_space=pltpu.SMEM),
                      pl.BlockSpec((IDX_BLOCK,), lambda i, be, nu: (jnp.minimum((i + 1) // per, last),),
                                   memory_space=pltpu.SMEM),
                      pl.BlockSpec((IDX_BLOCK,), lambda i, be, nu: (jnp.minimum(i // per, last_tgt),),
                                   memory_space=pltpu.SMEM),
                      pl.BlockSpec(memory_space=pl.ANY),
                      pl.BlockSpec((None, None, d, ff2), w_idx),
                      pl.BlockSpec((None, None, 1, ff2), w_idx),
                      pl.BlockSpec((None, None, ff2 // 2, d), w_idx),
                      pl.BlockSpec((None, None, 1, d), w_idx)],
            out_specs=pl.BlockSpec(memory_space=pl.ANY),
            scratch_shapes=[pltpu.VMEM((d, ff2), BF16), pltpu.VMEM((ff2 // 2, d), BF16), buf, buf, buf, buf,
                            pltpu.SemaphoreType.DMA((2,)), pltpu.SemaphoreType.DMA((2,))]),
        compiler_params=_cparams(("arbitrary",), has_side_effects=True, disable_bounds_checks=True),
        name="moe_fused",
    )(block_e, n_used, row_tok, row_tok, row_tgt, f, wgu, bgu, wdn, bdn)


def _sum_choices_kernel(p0_ref, p1_ref, p2_ref, p3_ref, gate_ref, x_ref, mod_ref, o_ref):
    gates = gate_ref[...]
    acc = gates[:, 0:1] * p0_ref[...]
    for k, p_ref in enumerate((p1_ref, p2_ref, p3_ref), start=1):
        acc = acc + gates[:, k:k + 1] * p_ref[...]
    o_ref[...] = x_ref[...] + mod_ref[5:6, :] * acc


def _sum_choices(y4, gates, xs, mods, t_off):
    b, t_out, d = xs.shape
    nt = t_out // TOK_TILE
    row = functools.partial(_mod_row, ctx_row=b)
    tok = lambda bi, ti: (bi, ti, 0)
    plane = lambda k: pl.BlockSpec((TOK_TILE, d), lambda bi, ti: (k * b * nt + bi * nt + ti, 0))
    return pl.pallas_call(
        _sum_choices_kernel,
        out_shape=jax.ShapeDtypeStruct((b, t_out, d), F32),
        grid=(b, nt),
        in_specs=[plane(0), plane(1), plane(2), plane(3),
                  pl.BlockSpec((None, TOK_TILE, LANES), tok),
                  pl.BlockSpec((None, TOK_TILE, d), tok),
                  pl.BlockSpec((None, 6, d), lambda bi, ti: (row(bi, ti + t_off), 0, 0))],
        out_specs=pl.BlockSpec((None, TOK_TILE, d), tok),
        compiler_params=_cparams(("arbitrary", "arbitrary")),
        name="moe_sum",
    )(y4, y4, y4, y4, gates, xs, mods)


def _combine_kernel(dest_ref, y_ref, gate_ref, x_ref, mod_ref, o_ref, buf_ref, sem):
    n = dest_ref.shape[0]
    tm = n // TOP_K

    def issue(t, c):
        for k in range(TOP_K):
            pltpu.make_async_copy(_row(y_ref, dest_ref[t * TOP_K + k]), _row(buf_ref.at[k], t), sem).start()
        return c

    lax.fori_loop(0, tm, issue, 0, unroll=DMA_UNROLL)
    for k in range(TOP_K):
        pltpu.make_async_copy(y_ref.at[pl.ds(0, tm), :], buf_ref.at[k], sem).wait()

    gates = gate_ref[...]
    acc = gates[:, 0:1] * buf_ref[0]
    for k in range(1, TOP_K):
        acc = acc + gates[:, k:k + 1] * buf_ref[k]
    o_ref[...] = x_ref[...] + mod_ref[5:6, :] * acc


def _combine(y, dest, gates, xs, mods, t_off):
    b, t_out, d = xs.shape
    nt = t_out // TOK_TILE
    per = TOK_TILE * TOP_K
    row = functools.partial(_mod_row, ctx_row=b)
    tok = lambda bi, ti: (bi, ti, 0)
    return pl.pallas_call(
        _combine_kernel,
        out_shape=jax.ShapeDtypeStruct((b, t_out, d), F32),
        grid=(b, nt),
        in_specs=[pl.BlockSpec((per,), lambda bi, ti: (bi * nt + ti,), memory_space=pltpu.SMEM),
                  pl.BlockSpec(memory_space=pl.ANY),
                  pl.BlockSpec((None, TOK_TILE, LANES), tok),
                  pl.BlockSpec((None, TOK_TILE, d), tok),
                  pl.BlockSpec((None, 6, d), lambda bi, ti: (row(bi, ti + t_off), 0, 0))],
        out_specs=pl.BlockSpec((None, TOK_TILE, d), tok),
        scratch_shapes=[pltpu.VMEM((TOP_K, TOK_TILE, d), F32), pltpu.SemaphoreType.DMA(())],
        compiler_params=_cparams(("arbitrary", "arbitrary"), disable_bounds_checks=True),
        name="moe_combine",
    )(dest, y, gates, xs, mods)


def _routing(e_sel, rank, counts, n):
    padded = (counts + MOE_ROWS - 1) // MOE_ROWS * MOE_ROWS
    pend = jnp.cumsum(padded)
    pstart = pend - padded
    dest = jnp.sum(jnp.where(e_sel[..., None] == jnp.arange(N_EXPERTS), pstart, 0), axis=-1) + rank
    n_blocks = n * TOP_K // MOE_ROWS + N_EXPERTS
    first_row = jnp.arange(n_blocks) * MOE_ROWS
    block_e = jnp.minimum(jnp.sum(pend[None, :] <= first_row[:, None], axis=1), N_EXPERTS - 1)
    n_used = (pend[-1] // MOE_ROWS).reshape(1)
    pads = ((pstart + counts).astype(I32), (padded - counts).astype(I32))
    return dest.reshape(-1).astype(I32), block_e.astype(I32), n_used.astype(I32), pads, n_blocks * MOE_ROWS


def _moe_fused(f, route, gates, seen, x_mid, mods, t_off, layer, wgu, bgu, wdn, bdn):
    b, t_out, d = f.shape
    n = b * t_out
    route = route.reshape(n, LANES)
    counts = seen[0, :N_EXPERTS].astype(I32)
    dest, block_e, n_used, _, rows = _routing(route[:, :TOP_K], route[:, TOP_K:2 * TOP_K], counts, n)
    assert rows % IDX_BLOCK == 0
    choice = jnp.arange(n * TOP_K, dtype=I32)
    tok, k = choice // TOP_K, choice % TOP_K
    r = jnp.arange(rows, dtype=I32)
    scratch = TOP_K * n + ((r // MOE_ROWS) % 2) * MOE_ROWS + r % MOE_ROWS
    row_tok = jnp.zeros((rows,), I32).at[dest].set(tok)
    row_tgt = scratch.at[dest].set(k * n + tok)
    lead = TOP_K * n + MOE_ROWS + jnp.arange(MOE_ROWS, dtype=I32)
    tail = jnp.full((IDX_BLOCK - MOE_ROWS,), TOP_K * n, I32)
    row_tgt = jnp.concatenate([lead, row_tgt, tail])
    block_e = jnp.concatenate([block_e, block_e[-1:]])
    y4 = _experts_fused(f.reshape(n, d), row_tok, row_tgt, block_e, n_used, layer, wgu, bgu, wdn, bdn)
    return _sum_choices(y4, gates, x_mid, mods, t_off)


def _moe(f, route, gates, seen, x_mid, mods, t_off, layer, wgu, bgu, wdn, bdn):
    b, t_out, d = f.shape
    n = b * t_out
    route = route.reshape(n, LANES)
    counts = seen[0, :N_EXPERTS].astype(I32)
    dest, block_e, n_used, pads, rows = _routing(route[:, :TOP_K], route[:, TOP_K:2 * TOP_K], counts, n)
    xs = _dispatch(f.reshape(n, d), dest, *pads, n_used, rows)
    y = _experts(xs, block_e, n_used, layer, wgu, bgu, wdn, bdn)
    return _combine(y, dest, gates, x_mid, mods, t_off)


def _axial_angles(n_rows, rot_dim):
    row = np.repeat(np.arange(n_rows, dtype=np.float32), GRID_W)
    col = np.tile(np.arange(GRID_W, dtype=np.float32), n_rows)
    n = rot_dim // 4
    freqs = (np.float32(ROPE_THETA) ** (-np.arange(n, dtype=np.float32) / np.float32(n))).astype(np.float32)
    return np.concatenate([row[:, None] * freqs, col[:, None] * freqs], axis=-1).astype(np.float32)


def _rope_tables(ctx_len, s_len, rot_dim, lane_base, reps):
    ang = _axial_angles(s_len // GRID_W, rot_dim)
    half = rot_dim // 2
    period = LANES // reps
    cos = np.ones((ctx_len + s_len, period), np.float32)
    slo = np.zeros((ctx_len + s_len, period), np.float32)
    shi = np.zeros((ctx_len + s_len, period), np.float32)
    cos[ctx_len:, lane_base:lane_base + rot_dim] = np.tile(np.cos(ang), (1, 2))
    slo[ctx_len:, lane_base:lane_base + half] = -np.sin(ang)
    shi[ctx_len:, lane_base + half:lane_base + rot_dim] = np.sin(ang)
    return [jnp.asarray(np.tile(tb, (1, reps))) for tb in (cos, slo, shi)]


def _segment_mean_matrix(width, period, segs):
    lane = np.arange(width)
    seg_id = np.full((width,), -1)
    seg_w = np.zeros((width,), np.float32)
    for i, (start, length) in enumerate(segs):
        inside = ((lane % period) >= start) & ((lane % period) < start + length)
        seg_id = np.where(inside, (lane // period) * len(segs) + i, seg_id)
        seg_w = np.where(inside, np.float32(1.0 / length), seg_w)
    same = (seg_id[:, None] == seg_id[None, :]) & (seg_id[:, None] >= 0)
    return jnp.asarray(np.where(same, seg_w[None, :], np.float32(0.0)), dtype=BF16)


def _block_diag(w):
    n, c, _ = w.shape
    eye = jnp.eye(n, dtype=w.dtype)
    return (eye[:, None, :, None] * w[:, :, None, :]).reshape(n * c, n * c)


def kernel(x, c, ctx, c_ctx, w_mod, b_mod, norm_mix, norm_ffn, w_in_even, lru_conv_w, lru_conv_b, lru_w_r, lru_b_r, lru_w_i, lru_b_i, lru_lambda, mla_q_a_norm, mla_w_q_b, mla_kv_a_norm, mla_w_kv_b, mla_nope_norm, mla_rope_norm, w_out_even, w_qkv_odd, gqa_qk_norm, gqa_sink, w_out_odd, w_router, b_router, w_gate_up, b_gate_up, w_down, b_down):
    b, s_len, d = x.shape
    ctx_len = ctx.shape[1]
    depth = w_mod.shape[0]
    assert depth == 2 and ctx_len == TOK_TILE and s_len % TOK_TILE == 0 and b + 1 <= SUBLANES
    lru_w = lru_conv_w.shape[-1]
    q_lora = mla_q_a_norm.shape[-1]
    kv_lora = mla_kv_a_norm.shape[-1]

    cvec = jnp.concatenate([c, c_ctx[None], jnp.zeros((SUBLANES - b - 1, d), F32)], axis=0)
    mods = _modulation(cvec, w_mod, b_mod).reshape(depth, SUBLANES, 6, d)

    wr_pad = jnp.zeros((depth, d, LANES), F32).at[:, :, :N_EXPERTS].set(w_router)
    br_pad = jnp.full((depth, 1, LANES), NEG_INF, F32).at[:, 0, :N_EXPERTS].set(b_router)
    experts = (w_gate_up, b_gate_up[:, :, None, :], w_down, b_down[:, :, None, :])

    n_in = w_in_even.shape[-1]
    n_in_pad = -(-n_in // LANES) * LANES
    w_in = jnp.zeros((d, n_in_pad), F32).at[:, :n_in].set(w_in_even[0]).astype(BF16)
    xa, ga, mla_in = _even_in(ctx, x, mods[0], norm_mix[0][None], w_in, lru_w)

    nh = lru_w // (2 * LANES)
    per = LRU_BLOCKS // nh
    blk = lru_w // LRU_BLOCKS
    w_gates = jnp.stack([
        jnp.stack([jnp.concatenate([_block_diag(lru_w_r[0, dd, h * per:(h + 1) * per]),
                                    _block_diag(lru_w_i[0, dd, h * per:(h + 1) * per])], axis=1)
                   for h in range(nh)]) for dd in range(2)]).astype(BF16)
    assert blk * per == 2 * LANES
    ya = _lru(xa, ga, lru_conv_w[0], lru_conv_b[0][:, None, :], w_gates, lru_b_r[0][:, None, :],
              lru_b_i[0][:, None, :], jax.nn.softplus(-lru_lambda[0])[:, None, :], ctx_len)

    qk = MLA_NOPE + MLA_ROPE
    hq = MLA_HEADS * LANES
    wq = jnp.zeros((q_lora, MLA_HEADS, LANES), F32).at[:, :, :qk].set(
        mla_w_q_b[0].reshape(q_lora, MLA_HEADS, qk)).reshape(q_lora, hq).astype(BF16)
    wkv = mla_w_kv_b[0].reshape(kv_lora, MLA_HEADS, MLA_NOPE + MLA_V)
    wk = jnp.zeros((kv_lora, MLA_HEADS, LANES), F32).at[:, :, :MLA_NOPE].set(
        wkv[:, :, :MLA_NOPE]).reshape(kv_lora, hq).astype(BF16)
    wv = jnp.zeros((kv_lora, MLA_HEADS, LANES), F32).at[:, :, :MLA_V].set(
        wkv[:, :, MLA_NOPE:]).reshape(kv_lora, hq).astype(BF16)
    one = jnp.asarray(np.tile(np.arange(LANES) == MLA_V, MLA_HEADS)[None], dtype=F32)
    zpad = jnp.zeros((LANES - qk,), F32)
    gq = jnp.tile(jnp.concatenate([mla_nope_norm[0, 0], mla_rope_norm[0, 0], zpad]), MLA_HEADS)[None]
    gk = jnp.tile(jnp.concatenate([mla_nope_norm[0, 1], jnp.zeros((LANES - MLA_NOPE,), F32)]), MLA_HEADS)[None]
    gkr = jnp.concatenate([mla_rope_norm[0, 1], jnp.zeros((LANES - MLA_ROPE,), F32)])[None]
    cos, slo, shi = _rope_tables(ctx_len, s_len, MLA_ROPE, MLA_NOPE, 1)
    mla_p = dict(qan=mla_q_a_norm[0][None], kvn=mla_kv_a_norm[0][None], wq=wq, wk=wk, wv=wv,
                 mq=_segment_mean_matrix(hq, LANES, [(0, MLA_NOPE), (MLA_NOPE, MLA_ROPE)]),
                 mk=_segment_mean_matrix(hq, LANES, [(0, MLA_NOPE)]),
                 gq=gq, gk=gk, gkr=gkr, one=one, cos=cos, slo=slo, shi=shi)
    q, k, v = _mla_prep(mla_in, mla_p, qk ** -0.5 * math.log2(math.e))
    yb = _mla_attn(q, k, v, ctx_len)

    x_mid, f, route, gates, seen = _post_mix(ya, yb, 0, (ctx, x), mods[0], norm_ffn[0][None], w_out_even[0].astype(BF16),
                                             wr_pad[0], br_pad[0], 0, 0)
    xs = _moe_fused(f, route, gates, seen, x_mid, mods[0], 0, 0, *experts)

    nq = GQA_HEADS * GQA_DIM
    nkv = GQA_KV_HEADS * GQA_DIM
    cos, slo, shi = _rope_tables(ctx_len, s_len, GQA_DIM, 0, LANES // GQA_DIM)
    odd_p = dict(w=w_qkv_odd[0].astype(BF16),
                 mq=_segment_mean_matrix(nq, GQA_DIM, [(0, GQA_DIM)]),
                 mk=_segment_mean_matrix(nkv, GQA_DIM, [(0, GQA_DIM)]),
                 gq=jnp.tile(gqa_qk_norm[0, 0], GQA_HEADS)[None], gk=jnp.tile(gqa_qk_norm[0, 1], GQA_KV_HEADS)[None],
                 cos=cos, slo=slo, shi=shi)
    q, k, v = _odd_in(xs, mods[1], norm_mix[1][None], odd_p, GQA_DIM ** -0.5)
    o = _win_attn(q, k, v, gqa_sink[0], ctx_len)

    t_off = ctx_len // TOK_TILE
    x_mid, f, route, gates, seen = _post_mix(o, o, 1, xs, mods[1], norm_ffn[1][None], w_out_odd[0].astype(BF16),
                                             wr_pad[1], br_pad[1], t_off, 0)
    return _moe_fused(f, route, gates, seen, x_mid, mods[1], t_off, 1, *experts)
```

```python
import functools
import math

import jax
import jax.numpy as jnp
import numpy as np
from jax import lax
from jax.experimental import pallas as pl
from jax.experimental.pallas import tpu as pltpu

F32 = jnp.float32
BF16 = jnp.bfloat16
I32 = jnp.int32

GRID_W = 64
LRU_BLOCKS = 8
LRU_C = 8.0
CONV_W = 4
MLA_HEADS = 8
MLA_NOPE = 64
MLA_ROPE = 32
MLA_V = 64
GQA_HEADS = 16
GQA_KV_HEADS = 4
GQA_DIM = 64
WINDOW = 128
ROPE_THETA = 10000.0
NEG_INF = -1e30
EPS = 1e-6
N_EXPERTS = 32
TOP_K = 4
SWIGLU_LIMIT = 7.0
SWIGLU_ALPHA = 1.702

LANES = 128
SUBLANES = 8
TOK_TILE = 256
LRU_CHUNK = 128
WIN_BLOCK = 128
MOE_ROWS = 256
VMEM_LIMIT = 48 * 1024 * 1024


def _cparams(sem, **kw):
    return pltpu.CompilerParams(dimension_semantics=sem, vmem_limit_bytes=VMEM_LIMIT, **kw)


def _dot(a, b):
    return jnp.dot(a, b, preferred_element_type=F32)


def _dot_nt(a, b):
    return lax.dot_general(a, b, (((1,), (1,)), ((), ())), preferred_element_type=F32)


def _split_bf16(x):
    hi = x.astype(BF16)
    lo = (x - hi.astype(F32)).astype(BF16)
    return hi, lo


def _dot3(a, w):
    ah, al = _split_bf16(a)
    wh, wl = _split_bf16(w)
    return _dot(ah, wh) + _dot(al, wh) + _dot(ah, wl)


def _rms(x):
    return x * lax.rsqrt(jnp.mean(x * x, axis=-1, keepdims=True) + EPS)


def _prenorm(x, g, scale, shift):
    return (_rms(x) * g) * (1.0 + scale) + shift


def _rope(x, cos, sin_lo, sin_hi, half):
    w = x.shape[-1]
    return x * cos + pltpu.roll(x, w - half, axis=1) * sin_lo + pltpu.roll(x, half, axis=1) * sin_hi


def _tile_lanes(t, reps):
    return jnp.concatenate([t] * reps, axis=1) if reps > 1 else t


def _mod_kernel(c_ref, w_ref, b_ref, o_ref):
    c = c_ref[...]
    o_ref[...] = _dot3(c * jax.nn.sigmoid(c), w_ref[...]) + b_ref[...]


def _modulation(cvec, w_mod, b_mod):
    depth, d, n = w_mod.shape
    tn = 1536
    return pl.pallas_call(
        _mod_kernel,
        out_shape=jax.ShapeDtypeStruct((depth, SUBLANES, n), F32),
        grid=(depth, n // tn),
        in_specs=[pl.BlockSpec((SUBLANES, d), lambda l, j: (0, 0)),
                  pl.BlockSpec((None, d, tn), lambda l, j: (l, 0, j)),
                  pl.BlockSpec((None, 1, tn), lambda l, j: (l, 0, j))],
        out_specs=pl.BlockSpec((None, SUBLANES, tn), lambda l, j: (l, 0, j)),
        compiler_params=_cparams(("arbitrary", "arbitrary")),
        name="modulation",
    )(cvec, w_mod, b_mod.reshape(depth, 1, n))


def _mod_row(b, t, ctx_row):
    return jnp.where(t == 0, ctx_row, b)


def _stream_tile(ctx_ref, lat_ref):
    return jnp.where(pl.program_id(1) == 0, ctx_ref[...], lat_ref[...])


def _stream_specs(d):
    return [pl.BlockSpec((None, TOK_TILE, d), lambda bi, ti: (bi, 0, 0)),
            pl.BlockSpec((None, TOK_TILE, d), lambda bi, ti: (bi, jnp.maximum(ti - 1, 0), 0))]


def _even_in_kernel(ctx_ref, x_ref, mod_ref, g_ref, w_ref, xa_ref, ga_ref, mla_ref):
    mod = mod_ref[...]
    h = _prenorm(_stream_tile(ctx_ref, x_ref), g_ref[...], mod[1:2], mod[0:1])
    z = _dot(h.astype(BF16), w_ref[...])
    c = xa_ref.shape[-1]
    xa_ref[...] = z[:, :c]
    ga_ref[...] = z[:, c:2 * c]
    mla_ref[...] = z[:, 2 * c:]


def _even_in(ctx, x, mods, g, w_pad, lru_w):
    b, s_len, d = x.shape
    t = ctx.shape[1] + s_len
    nt = t // TOK_TILE
    n_out = w_pad.shape[1]
    n_mla = n_out - 2 * lru_w
    row = functools.partial(_mod_row, ctx_row=b)
    tok = lambda bi, ti: (bi, ti, 0)
    return pl.pallas_call(
        _even_in_kernel,
        out_shape=(jax.ShapeDtypeStruct((b, t, lru_w), F32),
                   jax.ShapeDtypeStruct((b, t, lru_w), F32),
                   jax.ShapeDtypeStruct((b, t, n_mla), F32)),
        grid=(b, nt),
        in_specs=_stream_specs(d) + [
            pl.BlockSpec((None, 6, d), lambda bi, ti: (row(bi, ti), 0, 0)),
            pl.BlockSpec((1, d), lambda bi, ti: (0, 0)),
            pl.BlockSpec((d, n_out), lambda bi, ti: (0, 0))],
        out_specs=(pl.BlockSpec((None, TOK_TILE, lru_w), tok),
                   pl.BlockSpec((None, TOK_TILE, lru_w), tok),
                   pl.BlockSpec((None, TOK_TILE, n_mla), tok)),
        compiler_params=_cparams(("arbitrary", "arbitrary")),
        name="even_in",
    )(ctx, x, mods, g, w_pad)


def _lru_kernel(xa_ref, ga_ref, cw_ref, cb_ref, wg_ref, br_ref, bi_ref, sp_ref, o_ref, pad_ref, rec_ref, *, ctx_len):
    t, c = xa_ref.shape
    tc = LRU_CHUNK
    halo = SUBLANES
    n_chunks = t // tc
    n_ctx = ctx_len // tc
    groups = tc // SUBLANES

    pad_ref[0:halo, :] = jnp.zeros((halo, c), F32)
    pad_ref[t + halo:t + 2 * halo, :] = jnp.zeros((halo, c), F32)
    pad_ref[halo:t + halo, :] = xa_ref[...]

    rid = lax.broadcasted_iota(I32, (tc, 1), 0)
    sub = rid % SUBLANES

    for d in range(2):
        cw = cw_ref[d]
        cb = cb_ref[d]
        wg = wg_ref[d]
        b_r = br_ref[d]
        b_i = bi_ref[d]
        sp = sp_ref[d]

        def chunk(i, h, d=d, cw=cw, cb=cb, wg=wg, b_r=b_r, b_i=b_i, sp=sp):
            if d == 0:
                ci = i
            else:
                ci = jnp.where(i < n_ctx, n_ctx - 1 - i, n_chunks - 1 - (i - n_ctx))
            r0 = pl.multiple_of(ci * tc, tc)
            win = pad_ref[pl.ds(r0, tc + 2 * halo), :]
            if d == 0:
                past = jnp.where(ci == n_ctx, 0.0, win[0:halo])
                win = jnp.concatenate([past, win[halo:]], axis=0)
            else:
                past = jnp.where(ci == n_ctx - 1, 0.0, win[halo + tc:])
                win = jnp.concatenate([win[:halo + tc], past], axis=0)
            xc = jnp.zeros((tc, c), F32) + cb
            for k in range(CONV_W):
                off = (k - (CONV_W - 1)) if d == 0 else ((CONV_W - 1) - k)
                if off == 0:
                    src = win[halo:halo + tc]
                else:
                    src = pltpu.roll(win, (-off) % (tc + 2 * halo), axis=0)[halo:halo + tc]
                xc = xc + cw[k:k + 1] * src
            gz = _dot(xc.astype(BF16), wg)
            r = jax.nn.sigmoid(gz[:, :c] + b_r)
            gi = jax.nn.sigmoid(gz[:, c:] + b_i)
            log_a = (-LRU_C) * r * sp
            a = jnp.exp(log_a)
            th = jnp.tanh(log_a)
            bb = jnp.sqrt(-2.0 * th / (1.0 - th)) * (gi * xc)
            for s in (1, 2, 4):
                if d == 0:
                    ok = sub >= s
                    sh = s
                else:
                    ok = sub <= (SUBLANES - 1 - s)
                    sh = tc - s
                a_prev = jnp.where(ok, pltpu.roll(a, sh, axis=0), 1.0)
                b_prev = jnp.where(ok, pltpu.roll(bb, sh, axis=0), 0.0)
                bb = a * b_prev + bb
                a = a * a_prev
            outs = [None] * groups
            order = range(groups) if d == 0 else range(groups - 1, -1, -1)
            for g in order:
                lo = g * SUBLANES
                hg = a[lo:lo + SUBLANES] * h + bb[lo:lo + SUBLANES]
                outs[g] = hg
                h = hg[SUBLANES - 1:SUBLANES] if d == 0 else hg[0:1]
            hs = jnp.concatenate(outs, axis=0)
            if d == 0:
                rec_ref[pl.ds(r0, tc), :] = hs
            else:
                tot = rec_ref[pl.ds(r0, tc), :] + hs
                gate = jax.nn.gelu(ga_ref[pl.ds(r0, tc), :], approximate=True)
                o_ref[pl.ds(r0, tc), :] = (tot * gate).astype(o_ref.dtype)
            return h

        lax.fori_loop(0, n_chunks, chunk, jnp.zeros((1, c), F32))


def _lru(xa, ga, conv_w, conv_b, w_gates, b_r, b_i, sp, ctx_len):
    b, t, w = xa.shape
    c = 2 * LANES
    nh = w // c
    tok = lambda bi, hi: (bi, 0, hi)
    par = lambda bi, hi: (0, 0, hi)
    return pl.pallas_call(
        functools.partial(_lru_kernel, ctx_len=ctx_len),
        out_shape=jax.ShapeDtypeStruct((b, t, w), BF16),
        grid=(b, nh),
        in_specs=[pl.BlockSpec((None, t, c), tok),
                  pl.BlockSpec((None, t, c), tok),
                  pl.BlockSpec((2, CONV_W, c), par),
                  pl.BlockSpec((2, 1, c), par),
                  pl.BlockSpec((2, None, c, 2 * c), lambda bi, hi: (0, hi, 0, 0)),
                  pl.BlockSpec((2, 1, c), par),
                  pl.BlockSpec((2, 1, c), par),
                  pl.BlockSpec((2, 1, c), par)],
        out_specs=pl.BlockSpec((None, t, c), tok),
        scratch_shapes=[pltpu.VMEM((t + 2 * SUBLANES, c), F32), pltpu.VMEM((t, c), F32)],
        compiler_params=_cparams(("arbitrary", "arbitrary")),
        name="rglru",
    )(xa, ga, conv_w, conv_b, w_gates, b_r, b_i, sp)


def _mla_prep_kernel(in_ref, qan_ref, kvn_ref, wq_ref, wk_ref, wv_ref, mq_ref, mk_ref, gq_ref, gk_ref, gkr_ref,
                     one_ref, cos_ref, slo_ref, shi_ref, q_ref, k_ref, v_ref, *, q_lora, kv_lora, scale):
    z = in_ref[...]
    heads = q_ref.shape[-1] // LANES
    cos = cos_ref[...]
    slo = slo_ref[...]
    shi = shi_ref[...]
    half = MLA_ROPE // 2

    qan = (_rms(z[:, :q_lora]) * qan_ref[...]).astype(BF16)
    q = _dot(qan, wq_ref[...])
    q = q * lax.rsqrt(_dot((q * q).astype(BF16), mq_ref[...]) + EPS) * gq_ref[...]
    q = _rope(q, _tile_lanes(cos, heads), _tile_lanes(slo, heads), _tile_lanes(shi, heads), half)
    q_ref[...] = (q * scale).astype(BF16)

    kvn = (_rms(z[:, q_lora:q_lora + kv_lora]) * kvn_ref[...]).astype(BF16)
    kk = _dot(kvn, wk_ref[...])
    kk = kk * lax.rsqrt(_dot((kk * kk).astype(BF16), mk_ref[...]) + EPS) * gk_ref[...]
    v_ref[...] = (_dot(kvn, wv_ref[...]) + one_ref[...]).astype(BF16)

    kr = z[:, q_lora + kv_lora:]
    kr = kr * lax.rsqrt(jnp.sum(kr * kr, axis=-1, keepdims=True) * (1.0 / MLA_ROPE) + EPS) * gkr_ref[...]
    kr = pltpu.roll(kr, MLA_NOPE, axis=1)
    kr = _rope(kr, cos, slo, shi, half)
    k_ref[...] = (kk + _tile_lanes(kr, heads)).astype(BF16)


def _mla_prep(mla_in, p, scale):
    b, t, w = mla_in.shape
    hq = MLA_HEADS * LANES
    tok = lambda bi, ti: (bi, ti, 0)
    full = lambda a: pl.BlockSpec(a.shape, lambda bi, ti: (0,) * a.ndim)
    pos = pl.BlockSpec((TOK_TILE, LANES), lambda bi, ti: (ti, 0))
    consts = [p["qan"], p["kvn"], p["wq"], p["wk"], p["wv"], p["mq"], p["mk"], p["gq"], p["gk"], p["gkr"], p["one"]]
    out = jax.ShapeDtypeStruct((b, t, hq), BF16)
    return pl.pallas_call(
        functools.partial(_mla_prep_kernel, q_lora=p["qan"].shape[1], kv_lora=p["kvn"].shape[1], scale=scale),
        out_shape=(out, out, out),
        grid=(b, t // TOK_TILE),
        in_specs=[pl.BlockSpec((None, TOK_TILE, w), tok)] + [full(a) for a in consts] + [pos, pos, pos],
        out_specs=(pl.BlockSpec((None, TOK_TILE, hq), tok),) * 3,
        compiler_params=_cparams(("arbitrary", "arbitrary")),
        name="mla_prep",
    )(mla_in, *consts, p["cos"], p["slo"], p["shi"])


def _mla_attn_kernel(q_ref, k_ref, v_ref, o_ref, *, ctx_len):
    tq = q_ref.shape[0]
    t = k_ref.shape[0]
    lane = lax.broadcasted_iota(I32, (tq, LANES), 1)

    def attend(nk):
        outs = []
        for hh in range(2):
            q = q_ref[:, hh * LANES:(hh + 1) * LANES]
            k = k_ref[0:nk, hh * LANES:(hh + 1) * LANES]
            s = _dot_nt(q, k)
            p = jnp.exp2(s - jnp.max(s, axis=-1, keepdims=True))
            o = _dot(p.astype(BF16), v_ref[0:nk, hh * LANES:(hh + 1) * LANES])
            outs.append(o / o[:, MLA_V:MLA_V + 1])
        o_ref[...] = jnp.where(lane < MLA_V, outs[0], pltpu.roll(outs[1], MLA_V, axis=1)).astype(o_ref.dtype)

    @pl.when(pl.program_id(2) == 0)
    def _():
        attend(ctx_len)

    @pl.when(pl.program_id(2) > 0)
    def _():
        attend(t)


def _mla_attn(q, k, v, ctx_len):
    b, t, hq = q.shape
    pairs = hq // (2 * LANES)
    return pl.pallas_call(
        functools.partial(_mla_attn_kernel, ctx_len=ctx_len),
        out_shape=jax.ShapeDtypeStruct((b, t, pairs * 2 * MLA_V), BF16),
        grid=(b, pairs, t // TOK_TILE),
        in_specs=[pl.BlockSpec((None, TOK_TILE, 2 * LANES), lambda bi, hi, ti: (bi, ti, hi)),
                  pl.BlockSpec((None, t, 2 * LANES), lambda bi, hi, ti: (bi, 0, hi)),
                  pl.BlockSpec((None, t, 2 * LANES), lambda bi, hi, ti: (bi, 0, hi))],
        out_specs=pl.BlockSpec((None, TOK_TILE, LANES), lambda bi, hi, ti: (bi, ti, hi)),
        compiler_params=_cparams(("arbitrary", "arbitrary", "arbitrary")),
        name="mla_attn",
    )(q, k, v)


def _odd_in_kernel(x_ref, mod_ref, g_ref, w_ref, mq_ref, mk_ref, gq_ref, gk_ref, cos_ref, slo_ref, shi_ref,
                   q_ref, k_ref, v_ref, *, scale):
    mod = mod_ref[...]
    h = _prenorm(x_ref[...], g_ref[...], mod[1:2], mod[0:1]).astype(BF16)
    nq = q_ref.shape[-1]
    nk = k_ref.shape[-1]
    half = GQA_DIM // 2

    def head_norm_rope(y, m_ref, gain_ref):
        reps = y.shape[-1] // LANES
        y = y * lax.rsqrt(_dot((y * y).astype(BF16), m_ref[...]) + EPS) * gain_ref[...]
        return _rope(y, _tile_lanes(cos_ref[...], reps), _tile_lanes(slo_ref[...], reps),
                     _tile_lanes(shi_ref[...], reps), half)

    q_ref[...] = (head_norm_rope(_dot(h, w_ref[:, 0:nq]), mq_ref, gq_ref) * scale).astype(BF16)
    k_ref[...] = head_norm_rope(_dot(h, w_ref[:, nq:nq + nk]), mk_ref, gk_ref).astype(BF16)
    v_ref[...] = _dot(h, w_ref[:, nq + nk:nq + 2 * nk]).astype(BF16)


def _odd_in(xs, mods, g, p, scale):
    b, t, d = xs.shape
    nq = p["gq"].shape[1]
    nk = p["gk"].shape[1]
    row = functools.partial(_mod_row, ctx_row=b)
    tok = lambda bi, ti: (bi, ti, 0)
    full = lambda a: pl.BlockSpec(a.shape, lambda bi, ti: (0,) * a.ndim)
    pos = pl.BlockSpec((TOK_TILE, LANES), lambda bi, ti: (ti, 0))
    kv = jax.ShapeDtypeStruct((b, t, nk), BF16)
    return pl.pallas_call(
        functools.partial(_odd_in_kernel, scale=scale),
        out_shape=(jax.ShapeDtypeStruct((b, t, nq), BF16), kv, kv),
        grid=(b, t // TOK_TILE),
        in_specs=[pl.BlockSpec((None, TOK_TILE, d), tok),
                  pl.BlockSpec((None, 6, d), lambda bi, ti: (row(bi, ti), 0, 0)),
                  pl.BlockSpec((1, d), lambda bi, ti: (0, 0)),
                  full(p["w"]), full(p["mq"]), full(p["mk"]), full(p["gq"]), full(p["gk"]), pos, pos, pos],
        out_specs=(pl.BlockSpec((None, TOK_TILE, nq), tok),
                   pl.BlockSpec((None, TOK_TILE, nk), tok),
                   pl.BlockSpec((None, TOK_TILE, nk), tok)),
        compiler_params=_cparams(("arbitrary", "arbitrary")),
        name="odd_in",
    )(xs, mods, g, p["w"], p["mq"], p["mk"], p["gq"], p["gk"], p["cos"], p["slo"], p["shi"])


def _win_attn_kernel(sink_ref, q_ref, k_ref, v_ref, o_ref, *, ctx_len):
    t = k_ref.shape[0]
    wb = WIN_BLOCK
    w3 = 3 * wb
    hd = GQA_DIM
    group = GQA_HEADS // GQA_KV_HEADS
    i = pl.program_id(1)
    q0 = ctx_len + i * wb
    ws = pl.multiple_of(jnp.clip(q0 - wb, ctx_len, t - w3), wb)
    nk = w3 + ctx_len
    rows = group * wb
    row = lax.broadcasted_iota(I32, (rows, nk), 0)
    col = lax.broadcasted_iota(I32, (rows, nk), 1)
    valid = (col >= w3) | (jnp.abs(q0 + row % wb - (ws + col)) <= WINDOW)
    head_of_row = lax.broadcasted_iota(I32, (rows, 1), 0) // wb
    q = q_ref[...].astype(F32)
    kcat = jnp.concatenate([k_ref[pl.ds(ws, w3), :], k_ref[0:ctx_len, :]], axis=0).astype(F32)
    vcat = jnp.concatenate([v_ref[pl.ds(ws, w3), :], v_ref[0:ctx_len, :]], axis=0).astype(F32)
    outs = []
    for kh in range(GQA_KV_HEADS):
        qs = jnp.concatenate([q[:, (kh * group + h) * hd:(kh * group + h + 1) * hd] for h in range(group)], axis=0)
        s = _dot_nt(qs.astype(BF16), kcat[:, kh * hd:(kh + 1) * hd].astype(BF16))
        s = jnp.where(valid, s, NEG_INF)
        sink = jnp.zeros((rows, 1), F32)
        for h in range(group):
            sink = jnp.where(head_of_row == h, sink_ref[kh * group + h], sink)
        m = jnp.maximum(jnp.max(s, axis=-1, keepdims=True), sink)
        p = jnp.exp(s - m)
        l = jnp.sum(p, axis=-1, keepdims=True) + jnp.exp(sink - m)
        o = _dot(p.astype(BF16), vcat[:, kh * hd:(kh + 1) * hd].astype(BF16)) / l
        outs.extend(o[h * wb:(h + 1) * wb] for h in range(group))
    o_ref[...] = jnp.concatenate(outs, axis=1).astype(o_ref.dtype)


def _win_attn(q, k, v, sink, ctx_len):
    b, t, n = q.shape
    nkv = k.shape[-1]
    s_len = t - ctx_len
    off = ctx_len // WIN_BLOCK
    return pl.pallas_call(
        functools.partial(_win_attn_kernel, ctx_len=ctx_len),
        out_shape=jax.ShapeDtypeStruct((b, s_len, n), BF16),
        grid=(b, s_len // WIN_BLOCK),
        in_specs=[pl.BlockSpec(memory_space=pltpu.SMEM),
                  pl.BlockSpec((None, WIN_BLOCK, n), lambda bi, ti: (bi, ti + off, 0)),
                  pl.BlockSpec((None, t, nkv), lambda bi, ti: (bi, 0, 0)),
                  pl.BlockSpec((None, t, nkv), lambda bi, ti: (bi, 0, 0))],
        out_specs=pl.BlockSpec((None, WIN_BLOCK, n), lambda bi, ti: (bi, ti, 0)),
        compiler_params=_cparams(("arbitrary", "arbitrary")),
        name="win_attn",
    )(sink, q, k, v)


def _post_mix_kernel(a1_ref, a2_ref, *refs, split_stream):
    x_in = _stream_tile(refs[0], refs[1]) if split_stream else refs[0][...]
    _post_mix_body(a1_ref, a2_ref, x_in, *refs[2 if split_stream else 1:])


def _post_mix_body(a1_ref, a2_ref, x_in, mod_ref, g_ref, w_ref, wr_ref, br_ref,
                   xo_ref, f_ref, route_ref, gate_ref, cnt_ref, seen_ref):
    half = a1_ref.shape[-1]
    mod = mod_ref[...]
    m = _dot(a1_ref[...], w_ref[0:half, :]) + _dot(a2_ref[...], w_ref[half:2 * half, :])
    x = x_in + mod[2:3] * m
    xo_ref[...] = x
    f = _prenorm(x, g_ref[...], mod[4:5], mod[3:4])
    f_ref[...] = f

    logit = _dot3(f, wr_ref[...]) + br_ref[...]
    tm = logit.shape[0]
    lane = lax.broadcasted_iota(I32, (tm, LANES), 1)
    lane_f = lane.astype(F32)
    vals, idxs = [], []
    for _ in range(TOP_K):
        mx = jnp.max(logit, axis=-1, keepdims=True)
        ix = jnp.min(jnp.where(logit == mx, lane_f, float(LANES)), axis=-1, keepdims=True)
        vals.append(mx)
        idxs.append(ix)
        logit = jnp.where(lane_f == ix, -jnp.inf, logit)
    exps = [jnp.exp(v - vals[0]) for v in vals]
    den = exps[0]
    for e in exps[1:]:
        den = den + e

    @pl.when((pl.program_id(0) == 0) & (pl.program_id(1) == 0))
    def _():
        seen_ref[...] = jnp.zeros(seen_ref.shape, F32)

    msk = jnp.zeros((tm, LANES), F32)
    for k in range(TOP_K):
        msk = jnp.where(lane_f == idxs[k], 1.0, msk)
    earlier = (lax.broadcasted_iota(I32, (tm, tm), 1) < lax.broadcasted_iota(I32, (tm, tm), 0))
    rank = _dot(jnp.where(earlier, 1.0, 0.0).astype(BF16), msk.astype(BF16)) + seen_ref[0:1, :]
    seen = seen_ref[...] + jnp.sum(msk, axis=0, keepdims=True)
    seen_ref[...] = seen
    cnt_ref[...] = seen

    r_out = jnp.zeros((tm, LANES), F32)
    g_out = jnp.zeros((tm, LANES), F32)
    for k in range(TOP_K):
        rank_k = jnp.sum(jnp.where(lane_f == idxs[k], rank, 0.0), axis=-1, keepdims=True)
        r_out = jnp.where(lane == k, idxs[k], r_out)
        r_out = jnp.where(lane == TOP_K + k, rank_k, r_out)
        g_out = jnp.where(lane == k, exps[k] / den, g_out)
    route_ref[...] = r_out.astype(I32)
    gate_ref[...] = g_out


def _post_mix(a1, a2, lane_blk2, xs, mods, g, w_out, w_router, b_router, t_off, a_off):
    split_stream = isinstance(xs, tuple)
    if split_stream:
        assert t_off == 0
        b, s_len, d = xs[1].shape
        t = xs[0].shape[1] + s_len
        x_specs = _stream_specs(d)
    else:
        b, t, d = xs.shape
        x_specs = [pl.BlockSpec((None, TOK_TILE, d), lambda bi, ti: (bi, ti + t_off, 0))]
        xs = (xs,)
    half = w_out.shape[0] // 2
    nt = t // TOK_TILE - t_off
    t_out = nt * TOK_TILE
    row = functools.partial(_mod_row, ctx_row=b)
    tok = lambda bi, ti: (bi, ti, 0)
    act = jax.ShapeDtypeStruct((b, t_out, d), F32)
    return pl.pallas_call(
        functools.partial(_post_mix_kernel, split_stream=split_stream),
        out_shape=(act, act,
                   jax.ShapeDtypeStruct((b, t_out, LANES), I32),
                   jax.ShapeDtypeStruct((b, t_out, LANES), F32),
                   jax.ShapeDtypeStruct((SUBLANES, LANES), F32)),
        grid=(b, nt),
        in_specs=[pl.BlockSpec((None, TOK_TILE, half), lambda bi, ti: (bi, ti + a_off, 0)),
                  pl.BlockSpec((None, TOK_TILE, half), lambda bi, ti: (bi, ti + a_off, lane_blk2))] + x_specs + [
                  pl.BlockSpec((None, 6, d), lambda bi, ti: (row(bi, ti + t_off), 0, 0)),
                  pl.BlockSpec((1, d), lambda bi, ti: (0, 0)),
                  pl.BlockSpec(w_out.shape, lambda bi, ti: (0, 0)),
                  pl.BlockSpec(w_router.shape, lambda bi, ti: (0, 0)),
                  pl.BlockSpec((1, LANES), lambda bi, ti: (0, 0))],
        out_specs=(pl.BlockSpec((None, TOK_TILE, d), tok),
                   pl.BlockSpec((None, TOK_TILE, d), tok),
                   pl.BlockSpec((None, TOK_TILE, LANES), tok),
                   pl.BlockSpec((None, TOK_TILE, LANES), tok),
                   pl.BlockSpec((SUBLANES, LANES), lambda bi, ti: (0, 0))),
        scratch_shapes=[pltpu.VMEM((SUBLANES, LANES), F32)],
        compiler_params=_cparams(("arbitrary", "arbitrary")),
        name="post_mix",
    )(a1, a2, *xs, mods, g, w_out, w_router, b_router)


DMA_UNROLL = 2


def _row(ref, i):
    return ref.at[pl.ds(i, 1), :]


def _dispatch_kernel(pad_lo_ref, pad_n_ref, nu_ref, dest_ref, f_ref, xs_ref, zero_ref, sem, zsem):
    n = dest_ref.shape[0]

    blk = zero_ref.shape[0]
    n_blk = xs_ref.shape[0] // blk

    def block_copy(i):
        return pltpu.make_async_copy(zero_ref, xs_ref.at[pl.ds(pl.multiple_of(i * blk, blk), blk), :], zsem)

    def pad_rows(e, c, wait):
        def one(r, c2):
            copy = pltpu.make_async_copy(_row(zero_ref, 0), _row(xs_ref, pad_lo_ref[e] + r), zsem)
            copy.wait() if wait else copy.start()
            return c2

        return lax.fori_loop(0, pad_n_ref[e], one, c)

    @pl.when(pl.program_id(0) == 0)
    def _():
        zero_ref[...] = jnp.zeros(zero_ref.shape, F32)
        lax.fori_loop(nu_ref[0], n_blk, lambda i, c: (block_copy(i).start(), c)[1], 0)
        lax.fori_loop(0, N_EXPERTS, functools.partial(pad_rows, wait=False), 0)

    @pl.when(pl.program_id(0) == pl.num_programs(0) - 1)
    def _():
        lax.fori_loop(nu_ref[0], n_blk, lambda i, c: (block_copy(i).wait(), c)[1], 0)
        lax.fori_loop(0, N_EXPERTS, functools.partial(pad_rows, wait=True), 0)


    def issue(t, c):
        for k in range(TOP_K):
            pltpu.make_async_copy(_row(f_ref, t), _row(xs_ref, dest_ref[t * TOP_K + k]), sem).start(priority=k % 2)
        return c

    lax.fori_loop(0, n // TOP_K, issue, 0, unroll=DMA_UNROLL)
    pltpu.make_async_copy(xs_ref.at[pl.ds(0, n), :], xs_ref.at[pl.ds(0, n), :], sem).wait()


def _dispatch(f, dest, pad_lo, pad_n, n_used, rows):
    n, d = f.shape
    per = TOK_TILE * TOP_K
    return pl.pallas_call(
        _dispatch_kernel,
        out_shape=jax.ShapeDtypeStruct((rows, d), F32),
        grid_spec=pltpu.PrefetchScalarGridSpec(
            num_scalar_prefetch=3,
            grid=(n // TOK_TILE,),
            in_specs=[pl.BlockSpec((per,), lambda i, lo, cnt, nu: (i,), memory_space=pltpu.SMEM),
                      pl.BlockSpec((TOK_TILE, d), lambda i, lo, cnt, nu: (i, 0))],
            out_specs=pl.BlockSpec(memory_space=pl.ANY),
            scratch_shapes=[pltpu.VMEM((MOE_ROWS, d), F32), pltpu.SemaphoreType.DMA(()),
                            pltpu.SemaphoreType.DMA(())]),
        compiler_params=_cparams(("arbitrary",), has_side_effects=True, disable_bounds_checks=True),
        name="moe_dispatch",
    )(pad_lo, pad_n, n_used, dest, f)


def _experts_kernel(be_ref, nu_ref, xs_ref, wgu_ref, bgu_ref, wdn_ref, bdn_ref, y_ref, wgu_bf, wdn_bf):
    i = pl.program_id(0)

    @pl.when((i == 0) | (be_ref[i] != be_ref[jnp.maximum(i - 1, 0)]))
    def _():
        wgu_bf[...] = wgu_ref[...].astype(BF16)
        wdn_bf[...] = wdn_ref[...].astype(BF16)

    @pl.when(i < nu_ref[0])
    def _():
        h = _dot(xs_ref[...].astype(BF16), wgu_bf[...]) + bgu_ref[...]
        ff = h.shape[1] // 2
        hg = jnp.minimum(h[:, :ff], SWIGLU_LIMIT)
        hu = jnp.clip(h[:, ff:], -SWIGLU_LIMIT, SWIGLU_LIMIT)
        act = hg * jax.nn.sigmoid(SWIGLU_ALPHA * hg) * (hu + 1.0)
        y_ref[...] = _dot(act.astype(BF16), wdn_bf[...]) + bdn_ref[...]

    @pl.when(i >= nu_ref[0])
    def _():
        y_ref[...] = jnp.zeros(y_ref.shape, F32)


def _experts(xs, block_e, n_used, layer, wgu, bgu, wdn, bdn):
    rows, d = xs.shape
    ff2 = wgu.shape[-1]
    nb = rows // MOE_ROWS
    return pl.pallas_call(
        _experts_kernel,
        out_shape=jax.ShapeDtypeStruct((rows, d), F32),
        grid_spec=pltpu.PrefetchScalarGridSpec(
            num_scalar_prefetch=2,
            grid=(nb,),
            in_specs=[pl.BlockSpec((MOE_ROWS, d), lambda i, be, nu: (jnp.maximum(jnp.minimum(i, nu[0] - 1), 0), 0)),
                      pl.BlockSpec((None, None, d, ff2), lambda i, be, nu: (layer, be[i], 0, 0)),
                      pl.BlockSpec((None, None, 1, ff2), lambda i, be, nu: (layer, be[i], 0, 0)),
                      pl.BlockSpec((None, None, ff2 // 2, d), lambda i, be, nu: (layer, be[i], 0, 0)),
                      pl.BlockSpec((None, None, 1, d), lambda i, be, nu: (layer, be[i], 0, 0))],
            out_specs=pl.BlockSpec((MOE_ROWS, d), lambda i, be, nu: (i, 0)),
            scratch_shapes=[pltpu.VMEM((d, ff2), BF16), pltpu.VMEM((ff2 // 2, d), BF16)]),
        compiler_params=_cparams(("arbitrary",)),
        name="moe_experts",
    )(block_e, n_used, xs, wgu, bgu, wdn, bdn)


IDX_BLOCK = 1024


def _fused_kernel(be_ref, nu_ref, tok0_ref, tok_ref, tgt_ref, f_ref, wgu_ref, bgu_ref, wdn_ref, bdn_ref, out_ref,
                  wgu_bf, wdn_bf, x0, x1, y0, y1, gsem, ssem):
    i = pl.program_id(0)
    nu = nu_ref[0]
    rows = MOE_ROWS
    per = IDX_BLOCK // rows
    xb = (x0, x1)
    yb = (y0, y1)

    def gather(idx_ref, base, slot):
        for j in range(rows):
            pltpu.make_async_copy(_row(f_ref, idx_ref[base + j]), _row(xb[slot], j), gsem.at[slot]).start(priority=j % 2)

    def scatter(base, slot):
        for j in range(rows):
            pltpu.make_async_copy(_row(yb[slot], j), _row(out_ref, tgt_ref[base + j]), ssem.at[slot]).start(priority=j % 2)

    def wait_rows(buf, sem):
        pltpu.make_async_copy(buf, buf, sem).wait()

    @pl.when(i == 0)
    def _():
        y1[...] = jnp.zeros(y1.shape, F32)
        n_real = out_ref.shape[0] - 2 * rows
        c = pltpu.make_async_copy(y1, out_ref.at[pl.ds(n_real, rows), :], ssem.at[0])
        c.start()
        c.wait()
        gather(tok0_ref, 0, 0)

    @pl.when((i <= nu) & ((i == 0) | (be_ref[i] != be_ref[jnp.maximum(i - 1, 0)])))
    def _():
        wgu_bf[...] = wgu_ref[...].astype(BF16)
        wdn_bf[...] = wdn_ref[...].astype(BF16)

    gbase = ((i + 1) % per) * rows
    sbase = (i % per) * rows

    for slot in range(2):
        other = 1 - slot

        @pl.when((i % 2 == slot) & (i < nu))
        def _(slot=slot, other=other):
            wait_rows(xb[slot], gsem.at[slot])

            @pl.when(i >= 1)
            def _():
                wait_rows(yb[slot], ssem.at[slot])

            gather(tok_ref, gbase, other)
            scatter(sbase, other)
            h = _dot(xb[slot][...].astype(BF16), wgu_bf[...]) + bgu_ref[...]
            ff = h.shape[1] // 2
            hg = jnp.minimum(h[:, :ff], SWIGLU_LIMIT)
            hu = jnp.clip(h[:, ff:], -SWIGLU_LIMIT, SWIGLU_LIMIT)
            act = hg * jax.nn.sigmoid(SWIGLU_ALPHA * hg) * (hu + 1.0)
            yb[slot][...] = _dot(act.astype(BF16), wdn_bf[...]) + bdn_ref[...]

        @pl.when((i % 2 == slot) & (i == nu))
        def _(slot=slot, other=other):
            wait_rows(xb[slot], gsem.at[slot])

            @pl.when(i >= 1)
            def _():
                wait_rows(yb[slot], ssem.at[slot])

            scatter(sbase, other)
            wait_rows(yb[other], ssem.at[other])


def _experts_fused(f, row_tok, row_tgt, block_e, n_used, layer, wgu, bgu, wdn, bdn):
    n, d = f.shape
    ff2 = wgu.shape[-1]
    nb = row_tok.shape[0] // MOE_ROWS
    per = IDX_BLOCK // MOE_ROWS
    last = row_tok.shape[0] // IDX_BLOCK - 1
    last_tgt = row_tgt.shape[0] // IDX_BLOCK - 1
    buf = pltpu.VMEM((MOE_ROWS, d), F32)
    w_idx = lambda i, be, nu: (layer, be[i], 0, 0)
    return pl.pallas_call(
        _fused_kernel,
        out_shape=jax.ShapeDtypeStruct((TOP_K * n + 2 * MOE_ROWS, d), F32),
        grid_spec=pltpu.PrefetchScalarGridSpec(
            num_scalar_prefetch=2,
            grid=(nb + 1,),
            in_specs=[pl.BlockSpec((IDX_BLOCK,), lambda i, be, nu: (0,), memory_space=pltpu.SMEM),
                      pl.BlockSpec((IDX_BLOCK,), lambda i, be, nu: (jnp.minimum((i + 1) // per, last),),
                                   memory_space=pltpu.SMEM),
                      pl.BlockSpec((IDX_BLOCK,), lambda i, be, nu: (jnp.minimum(i // per, last_tgt),),
                                   memory_space=pltpu.SMEM),
                      pl.BlockSpec(memory_space=pl.ANY),
                      pl.BlockSpec((None, None, d, ff2), w_idx),
                      pl.BlockSpec((None, None, 1, ff2), w_idx),
                      pl.BlockSpec((None, None, ff2 // 2, d), w_idx),
                      pl.BlockSpec((None, None, 1, d), w_idx)],
            out_specs=pl.BlockSpec(memory_space=pl.ANY),
            scratch_shapes=[pltpu.VMEM((d, ff2), BF16), pltpu.VMEM((ff2 // 2, d), BF16), buf, buf, buf, buf,
                            pltpu.SemaphoreType.DMA((2,)), pltpu.SemaphoreType.DMA((2,))]),
        compiler_params=_cparams(("arbitrary",), has_side_effects=True, disable_bounds_checks=True),
        name="moe_fused",
    )(block_e, n_used, row_tok, row_tok, row_tgt, f, wgu, bgu, wdn, bdn)


def _sum_choices_kernel(p0_ref, p1_ref, p2_ref, p3_ref, gate_ref, x_ref, mod_ref, o_ref):
    gates = gate_ref[...]
    acc = gates[:, 0:1] * p0_ref[...]
    for k, p_ref in enumerate((p1_ref, p2_ref, p3_ref), start=1):
        acc = acc + gates[:, k:k + 1] * p_ref[...]
    o_ref[...] = x_ref[...] + mod_ref[5:6, :] * acc


def _sum_choices(y4, gates, xs, mods, t_off):
    b, t_out, d = xs.shape
    nt = t_out // TOK_TILE
    row = functools.partial(_mod_row, ctx_row=b)
    tok = lambda bi, ti: (bi, ti, 0)
    plane = lambda k: pl.BlockSpec((TOK_TILE, d), lambda bi, ti: (k * b * nt + bi * nt + ti, 0))
    return pl.pallas_call(
        _sum_choices_kernel,
        out_shape=jax.ShapeDtypeStruct((b, t_out, d), F32),
        grid=(b, nt),
        in_specs=[plane(0), plane(1), plane(2), plane(3),
                  pl.BlockSpec((None, TOK_TILE, LANES), tok),
                  pl.BlockSpec((None, TOK_TILE, d), tok),
                  pl.BlockSpec((None, 6, d), lambda bi, ti: (row(bi, ti + t_off), 0, 0))],
        out_specs=pl.BlockSpec((None, TOK_TILE, d), tok),
        compiler_params=_cparams(("arbitrary", "arbitrary")),
        name="moe_sum",
    )(y4, y4, y4, y4, gates, xs, mods)


def _combine_kernel(dest_ref, y_ref, gate_ref, x_ref, mod_ref, o_ref, buf_ref, sem):
    n = dest_ref.shape[0]
    tm = n // TOP_K

    def issue(t, c):
        for k in range(TOP_K):
            pltpu.make_async_copy(_row(y_ref, dest_ref[t * TOP_K + k]), _row(buf_ref.at[k], t), sem).start(priority=k % 2)
        return c

    lax.fori_loop(0, tm, issue, 0, unroll=DMA_UNROLL)
    for k in range(TOP_K):
        pltpu.make_async_copy(y_ref.at[pl.ds(0, tm), :], buf_ref.at[k], sem).wait()

    gates = gate_ref[...]
    acc = gates[:, 0:1] * buf_ref[0]
    for k in range(1, TOP_K):
        acc = acc + gates[:, k:k + 1] * buf_ref[k]
    o_ref[...] = x_ref[...] + mod_ref[5:6, :] * acc


def _combine(y, dest, gates, xs, mods, t_off):
    b, t_out, d = xs.shape
    nt = t_out // TOK_TILE
    per = TOK_TILE * TOP_K
    row = functools.partial(_mod_row, ctx_row=b)
    tok = lambda bi, ti: (bi, ti, 0)
    return pl.pallas_call(
        _combine_kernel,
        out_shape=jax.ShapeDtypeStruct((b, t_out, d), F32),
        grid=(b, nt),
        in_specs=[pl.BlockSpec((per,), lambda bi, ti: (bi * nt + ti,), memory_space=pltpu.SMEM),
                  pl.BlockSpec(memory_space=pl.ANY),
                  pl.BlockSpec((None, TOK_TILE, LANES), tok),
                  pl.BlockSpec((None, TOK_TILE, d), tok),
                  pl.BlockSpec((None, 6, d), lambda bi, ti: (row(bi, ti + t_off), 0, 0))],
        out_specs=pl.BlockSpec((None, TOK_TILE, d), tok),
        scratch_shapes=[pltpu.VMEM((TOP_K, TOK_TILE, d), F32), pltpu.SemaphoreType.DMA(())],
        compiler_params=_cparams(("arbitrary", "arbitrary"), disable_bounds_checks=True),
        name="moe_combine",
    )(dest, y, gates, xs, mods)


def _routing(e_sel, rank, counts, n):
    padded = (counts + MOE_ROWS - 1) // MOE_ROWS * MOE_ROWS
    pend = jnp.cumsum(padded)
    pstart = pend - padded
    dest = jnp.sum(jnp.where(e_sel[..., None] == jnp.arange(N_EXPERTS), pstart, 0), axis=-1) + rank
    n_blocks = n * TOP_K // MOE_ROWS + N_EXPERTS
    first_row = jnp.arange(n_blocks) * MOE_ROWS
    block_e = jnp.minimum(jnp.sum(pend[None, :] <= first_row[:, None], axis=1), N_EXPERTS - 1)
    n_used = (pend[-1] // MOE_ROWS).reshape(1)
    pads = ((pstart + counts).astype(I32), (padded - counts).astype(I32))
    return dest.reshape(-1).astype(I32), block_e.astype(I32), n_used.astype(I32), pads, n_blocks * MOE_ROWS


def _moe_fused(f, route, gates, seen, x_mid, mods, t_off, layer, wgu, bgu, wdn, bdn):
    b, t_out, d = f.shape
    n = b * t_out
    route = route.reshape(n, LANES)
    counts = seen[0, :N_EXPERTS].astype(I32)
    dest, block_e, n_used, _, rows = _routing(route[:, :TOP_K], route[:, TOP_K:2 * TOP_K], counts, n)
    assert rows % IDX_BLOCK == 0
    choice = jnp.arange(n * TOP_K, dtype=I32)
    tok, k = choice // TOP_K, choice % TOP_K
    r = jnp.arange(rows, dtype=I32)
    scratch = TOP_K * n + ((r // MOE_ROWS) % 2) * MOE_ROWS + r % MOE_ROWS
    row_tok = jnp.zeros((rows,), I32).at[dest].set(tok)
    row_tgt = scratch.at[dest].set(k * n + tok)
    lead = TOP_K * n + MOE_ROWS + jnp.arange(MOE_ROWS, dtype=I32)
    tail = jnp.full((IDX_BLOCK - MOE_ROWS,), TOP_K * n, I32)
    row_tgt = jnp.concatenate([lead, row_tgt, tail])
    block_e = jnp.concatenate([block_e, block_e[-1:]])
    y4 = _experts_fused(f.reshape(n, d), row_tok, row_tgt, block_e, n_used, layer, wgu, bgu, wdn, bdn)
    return _sum_choices(y4, gates, x_mid, mods, t_off)


def _moe(f, route, gates, seen, x_mid, mods, t_off, layer, wgu, bgu, wdn, bdn):
    b, t_out, d = f.shape
    n = b * t_out
    route = route.reshape(n, LANES)
    counts = seen[0, :N_EXPERTS].astype(I32)
    dest, block_e, n_used, pads, rows = _routing(route[:, :TOP_K], route[:, TOP_K:2 * TOP_K], counts, n)
    xs = _dispatch(f.reshape(n, d), dest, *pads, n_used, rows)
    y = _experts(xs, block_e, n_used, layer, wgu, bgu, wdn, bdn)
    return _combine(y, dest, gates, x_mid, mods, t_off)


def _axial_angles(n_rows, rot_dim):
    row = np.repeat(np.arange(n_rows, dtype=np.float32), GRID_W)
    col = np.tile(np.arange(GRID_W, dtype=np.float32), n_rows)
    n = rot_dim // 4
    freqs = (np.float32(ROPE_THETA) ** (-np.arange(n, dtype=np.float32) / np.float32(n))).astype(np.float32)
    return np.concatenate([row[:, None] * freqs, col[:, None] * freqs], axis=-1).astype(np.float32)


def _rope_tables(ctx_len, s_len, rot_dim, lane_base, reps):
    ang = _axial_angles(s_len // GRID_W, rot_dim)
    half = rot_dim // 2
    period = LANES // reps
    cos = np.ones((ctx_len + s_len, period), np.float32)
    slo = np.zeros((ctx_len + s_len, period), np.float32)
    shi = np.zeros((ctx_len + s_len, period), np.float32)
    cos[ctx_len:, lane_base:lane_base + rot_dim] = np.tile(np.cos(ang), (1, 2))
    slo[ctx_len:, lane_base:lane_base + half] = -np.sin(ang)
    shi[ctx_len:, lane_base + half:lane_base + rot_dim] = np.sin(ang)
    return [jnp.asarray(np.tile(tb, (1, reps))) for tb in (cos, slo, shi)]


def _segment_mean_matrix(width, period, segs):
    lane = np.arange(width)
    seg_id = np.full((width,), -1)
    seg_w = np.zeros((width,), np.float32)
    for i, (start, length) in enumerate(segs):
        inside = ((lane % period) >= start) & ((lane % period) < start + length)
        seg_id = np.where(inside, (lane // period) * len(segs) + i, seg_id)
        seg_w = np.where(inside, np.float32(1.0 / length), seg_w)
    same = (seg_id[:, None] == seg_id[None, :]) & (seg_id[:, None] >= 0)
    return jnp.asarray(np.where(same, seg_w[None, :], np.float32(0.0)), dtype=BF16)


def _block_diag(w):
    n, c, _ = w.shape
    eye = jnp.eye(n, dtype=w.dtype)
    return (eye[:, None, :, None] * w[:, :, None, :]).reshape(n * c, n * c)


def kernel(x, c, ctx, c_ctx, w_mod, b_mod, norm_mix, norm_ffn, w_in_even, lru_conv_w, lru_conv_b, lru_w_r, lru_b_r, lru_w_i, lru_b_i, lru_lambda, mla_q_a_norm, mla_w_q_b, mla_kv_a_norm, mla_w_kv_b, mla_nope_norm, mla_rope_norm, w_out_even, w_qkv_odd, gqa_qk_norm, gqa_sink, w_out_odd, w_router, b_router, w_gate_up, b_gate_up, w_down, b_down):
    b, s_len, d = x.shape
    ctx_len = ctx.shape[1]
    depth = w_mod.shape[0]
    assert depth == 2 and ctx_len == TOK_TILE and s_len % TOK_TILE == 0 and b + 1 <= SUBLANES
    lru_w = lru_conv_w.shape[-1]
    q_lora = mla_q_a_norm.shape[-1]
    kv_lora = mla_kv_a_norm.shape[-1]

    cvec = jnp.concatenate([c, c_ctx[None], jnp.zeros((SUBLANES - b - 1, d), F32)], axis=0)
    mods = _modulation(cvec, w_mod, b_mod).reshape(depth, SUBLANES, 6, d)

    wr_pad = jnp.zeros((depth, d, LANES), F32).at[:, :, :N_EXPERTS].set(w_router)
    br_pad = jnp.full((depth, 1, LANES), NEG_INF, F32).at[:, 0, :N_EXPERTS].set(b_router)
    experts = (w_gate_up, b_gate_up[:, :, None, :], w_down, b_down[:, :, None, :])

    n_in = w_in_even.shape[-1]
    n_in_pad = -(-n_in // LANES) * LANES
    w_in = jnp.zeros((d, n_in_pad), F32).at[:, :n_in].set(w_in_even[0]).astype(BF16)
    xa, ga, mla_in = _even_in(ctx, x, mods[0], norm_mix[0][None], w_in, lru_w)

    nh = lru_w // (2 * LANES)
    per = LRU_BLOCKS // nh
    blk = lru_w // LRU_BLOCKS
    w_gates = jnp.stack([
        jnp.stack([jnp.concatenate([_block_diag(lru_w_r[0, dd, h * per:(h + 1) * per]),
                                    _block_diag(lru_w_i[0, dd, h * per:(h + 1) * per])], axis=1)
                   for h in range(nh)]) for dd in range(2)]).astype(BF16)
    assert blk * per == 2 * LANES
    ya = _lru(xa, ga, lru_conv_w[0], lru_conv_b[0][:, None, :], w_gates, lru_b_r[0][:, None, :],
              lru_b_i[0][:, None, :], jax.nn.softplus(-lru_lambda[0])[:, None, :], ctx_len)

    qk = MLA_NOPE + MLA_ROPE
    hq = MLA_HEADS * LANES
    wq = jnp.zeros((q_lora, MLA_HEADS, LANES), F32).at[:, :, :qk].set(
        mla_w_q_b[0].reshape(q_lora, MLA_HEADS, qk)).reshape(q_lora, hq).astype(BF16)
    wkv = mla_w_kv_b[0].reshape(kv_lora, MLA_HEADS, MLA_NOPE + MLA_V)
    wk = jnp.zeros((kv_lora, MLA_HEADS, LANES), F32).at[:, :, :MLA_NOPE].set(
        wkv[:, :, :MLA_NOPE]).reshape(kv_lora, hq).astype(BF16)
    wv = jnp.zeros((kv_lora, MLA_HEADS, LANES), F32).at[:, :, :MLA_V].set(
        wkv[:, :, MLA_NOPE:]).reshape(kv_lora, hq).astype(BF16)
    one = jnp.asarray(np.tile(np.arange(LANES) == MLA_V, MLA_HEADS)[None], dtype=F32)
    zpad = jnp.zeros((LANES - qk,), F32)
    gq = jnp.tile(jnp.concatenate([mla_nope_norm[0, 0], mla_rope_norm[0, 0], zpad]), MLA_HEADS)[None]
    gk = jnp.tile(jnp.concatenate([mla_nope_norm[0, 1], jnp.zeros((LANES - MLA_NOPE,), F32)]), MLA_HEADS)[None]
    gkr = jnp.concatenate([mla_rope_norm[0, 1], jnp.zeros((LANES - MLA_ROPE,), F32)])[None]
    cos, slo, shi = _rope_tables(ctx_len, s_len, MLA_ROPE, MLA_NOPE, 1)
    mla_p = dict(qan=mla_q_a_norm[0][None], kvn=mla_kv_a_norm[0][None], wq=wq, wk=wk, wv=wv,
                 mq=_segment_mean_matrix(hq, LANES, [(0, MLA_NOPE), (MLA_NOPE, MLA_ROPE)]),
                 mk=_segment_mean_matrix(hq, LANES, [(0, MLA_NOPE)]),
                 gq=gq, gk=gk, gkr=gkr, one=one, cos=cos, slo=slo, shi=shi)
    q, k, v = _mla_prep(mla_in, mla_p, qk ** -0.5 * math.log2(math.e))
    yb = _mla_attn(q, k, v, ctx_len)

    x_mid, f, route, gates, seen = _post_mix(ya, yb, 0, (ctx, x), mods[0], norm_ffn[0][None], w_out_even[0].astype(BF16),
                                             wr_pad[0], br_pad[0], 0, 0)
    xs = _moe(f, route, gates, seen, x_mid, mods[0], 0, 0, *experts)

    nq = GQA_HEADS * GQA_DIM
    nkv = GQA_KV_HEADS * GQA_DIM
    cos, slo, shi = _rope_tables(ctx_len, s_len, GQA_DIM, 0, LANES // GQA_DIM)
    odd_p = dict(w=w_qkv_odd[0].astype(BF16),
                 mq=_segment_mean_matrix(nq, GQA_DIM, [(0, GQA_DIM)]),
                 mk=_segment_mean_matrix(nkv, GQA_DIM, [(0, GQA_DIM)]),
                 gq=jnp.tile(gqa_qk_norm[0, 0], GQA_HEADS)[None], gk=jnp.tile(gqa_qk_norm[0, 1], GQA_KV_HEADS)[None],
                 cos=cos, slo=slo, shi=shi)
    q, k, v = _odd_in(xs, mods[1], norm_mix[1][None], odd_p, GQA_DIM ** -0.5)
    o = _win_attn(q, k, v, gqa_sink[0], ctx_len)

    t_off = ctx_len // TOK_TILE
    x_mid, f, route, gates, seen = _post_mix(o, o, 1, xs, mods[1], norm_ffn[1][None], w_out_odd[0].astype(BF16),
                                             wr_pad[1], br_pad[1], t_off, 0)
    return _moe(f, route, gates, seen, x_mid, mods[1], t_off, 1, *experts)
```

```python
import functools
import math

import jax
import jax.numpy as jnp
import numpy as np
from jax import lax
from jax.experimental import pallas as pl
from jax.experimental.pallas import tpu as pltpu

F32 = jnp.float32
BF16 = jnp.bfloat16
I32 = jnp.int32

GRID_W = 64
LRU_BLOCKS = 8
LRU_C = 8.0
CONV_W = 4
MLA_HEADS = 8
MLA_NOPE = 64
MLA_ROPE = 32
MLA_V = 64
GQA_HEADS = 16
GQA_KV_HEADS = 4
GQA_DIM = 64
WINDOW = 128
ROPE_THETA = 10000.0
NEG_INF = -1e30
EPS = 1e-6
N_EXPERTS = 32
TOP_K = 4
SWIGLU_LIMIT = 7.0
SWIGLU_ALPHA = 1.702

LANES = 128
SUBLANES = 8
TOK_TILE = 256
LRU_CHUNK = 128
WIN_BLOCK = 128
MOE_ROWS = 256
VMEM_LIMIT = 48 * 1024 * 1024


def _cparams(sem, **kw):
    return pltpu.CompilerParams(dimension_semantics=sem, vmem_limit_bytes=VMEM_LIMIT, **kw)


def _dot(a, b):
    return jnp.dot(a, b, preferred_element_type=F32)


def _dot_nt(a, b):
    return lax.dot_general(a, b, (((1,), (1,)), ((), ())), preferred_element_type=F32)


def _split_bf16(x):
    hi = x.astype(BF16)
    lo = (x - hi.astype(F32)).astype(BF16)
    return hi, lo


def _dot3(a, w):
    ah, al = _split_bf16(a)
    wh, wl = _split_bf16(w)
    return _dot(ah, wh) + _dot(al, wh) + _dot(ah, wl)


def _rms(x):
    return x * lax.rsqrt(jnp.mean(x * x, axis=-1, keepdims=True) + EPS)


def _prenorm(x, g, scale, shift):
    return (_rms(x) * g) * (1.0 + scale) + shift


def _rope(x, cos, sin_lo, sin_hi, half):
    w = x.shape[-1]
    return x * cos + pltpu.roll(x, w - half, axis=1) * sin_lo + pltpu.roll(x, half, axis=1) * sin_hi


def _tile_lanes(t, reps):
    return jnp.concatenate([t] * reps, axis=1) if reps > 1 else t


def _mod_kernel(c_ref, w_ref, b_ref, o_ref):
    c = c_ref[...]
    o_ref[...] = _dot3(c * jax.nn.sigmoid(c), w_ref[...]) + b_ref[...]


def _modulation(cvec, w_mod, b_mod):
    depth, d, n = w_mod.shape
    tn = 1536
    return pl.pallas_call(
        _mod_kernel,
        out_shape=jax.ShapeDtypeStruct((depth, SUBLANES, n), F32),
        grid=(depth, n // tn),
        in_specs=[pl.BlockSpec((SUBLANES, d), lambda l, j: (0, 0)),
                  pl.BlockSpec((None, d, tn), lambda l, j: (l, 0, j)),
                  pl.BlockSpec((None, 1, tn), lambda l, j: (l, 0, j))],
        out_specs=pl.BlockSpec((None, SUBLANES, tn), lambda l, j: (l, 0, j)),
        compiler_params=_cparams(("arbitrary", "arbitrary")),
        name="modulation",
    )(cvec, w_mod, b_mod.reshape(depth, 1, n))


def _mod_row(b, t, ctx_row):
    return jnp.where(t == 0, ctx_row, b)


def _stream_tile(ctx_ref, lat_ref):
    return jnp.where(pl.program_id(1) == 0, ctx_ref[...], lat_ref[...])


def _stream_specs(d):
    return [pl.BlockSpec((None, TOK_TILE, d), lambda bi, ti: (bi, 0, 0)),
            pl.BlockSpec((None, TOK_TILE, d), lambda bi, ti: (bi, jnp.maximum(ti - 1, 0), 0))]


def _even_in_kernel(ctx_ref, x_ref, mod_ref, g_ref, w_ref, xa_ref, ga_ref, mla_ref):
    mod = mod_ref[...]
    h = _prenorm(_stream_tile(ctx_ref, x_ref), g_ref[...], mod[1:2], mod[0:1])
    z = _dot(h.astype(BF16), w_ref[...])
    c = xa_ref.shape[-1]
    xa_ref[...] = z[:, :c]
    ga_ref[...] = z[:, c:2 * c]
    mla_ref[...] = z[:, 2 * c:]


def _even_in(ctx, x, mods, g, w_pad, lru_w):
    b, s_len, d = x.shape
    t = ctx.shape[1] + s_len
    nt = t // TOK_TILE
    n_out = w_pad.shape[1]
    n_mla = n_out - 2 * lru_w
    row = functools.partial(_mod_row, ctx_row=b)
    tok = lambda bi, ti: (bi, ti, 0)
    return pl.pallas_call(
        _even_in_kernel,
        out_shape=(jax.ShapeDtypeStruct((b, t, lru_w), F32),
                   jax.ShapeDtypeStruct((b, t, lru_w), F32),
                   jax.ShapeDtypeStruct((b, t, n_mla), F32)),
        grid=(b, nt),
        in_specs=_stream_specs(d) + [
            pl.BlockSpec((None, 6, d), lambda bi, ti: (row(bi, ti), 0, 0)),
            pl.BlockSpec((1, d), lambda bi, ti: (0, 0)),
            pl.BlockSpec((d, n_out), lambda bi, ti: (0, 0))],
        out_specs=(pl.BlockSpec((None, TOK_TILE, lru_w), tok),
                   pl.BlockSpec((None, TOK_TILE, lru_w), tok),
                   pl.BlockSpec((None, TOK_TILE, n_mla), tok)),
        compiler_params=_cparams(("arbitrary", "arbitrary")),
        name="even_in",
    )(ctx, x, mods, g, w_pad)


def _lru_kernel(xa_ref, ga_ref, cw_ref, cb_ref, wg_ref, br_ref, bi_ref, sp_ref, o_ref, pad_ref, rec_ref, *, ctx_len):
    t, c = xa_ref.shape
    tc = LRU_CHUNK
    halo = SUBLANES
    n_chunks = t // tc
    n_ctx = ctx_len // tc
    groups = tc // SUBLANES

    pad_ref[0:halo, :] = jnp.zeros((halo, c), F32)
    pad_ref[t + halo:t + 2 * halo, :] = jnp.zeros((halo, c), F32)
    pad_ref[halo:t + halo, :] = xa_ref[...]

    rid = lax.broadcasted_iota(I32, (tc, 1), 0)
    sub = rid % SUBLANES

    for d in range(2):
        cw = cw_ref[d]
        cb = cb_ref[d]
        wg = wg_ref[d]
        b_r = br_ref[d]
        b_i = bi_ref[d]
        sp = sp_ref[d]

        def chunk(i, h, d=d, cw=cw, cb=cb, wg=wg, b_r=b_r, b_i=b_i, sp=sp):
            if d == 0:
                ci = i
            else:
                ci = jnp.where(i < n_ctx, n_ctx - 1 - i, n_chunks - 1 - (i - n_ctx))
            r0 = pl.multiple_of(ci * tc, tc)
            win = pad_ref[pl.ds(r0, tc + 2 * halo), :]
            if d == 0:
                past = jnp.where(ci == n_ctx, 0.0, win[0:halo])
                win = jnp.concatenate([past, win[halo:]], axis=0)
            else:
                past = jnp.where(ci == n_ctx - 1, 0.0, win[halo + tc:])
                win = jnp.concatenate([win[:halo + tc], past], axis=0)
            xc = jnp.zeros((tc, c), F32) + cb
            for k in range(CONV_W):
                off = (k - (CONV_W - 1)) if d == 0 else ((CONV_W - 1) - k)
                if off == 0:
                    src = win[halo:halo + tc]
                else:
                    src = pltpu.roll(win, (-off) % (tc + 2 * halo), axis=0)[halo:halo + tc]
                xc = xc + cw[k:k + 1] * src
            gz = _dot(xc.astype(BF16), wg)
            r = jax.nn.sigmoid(gz[:, :c] + b_r)
            gi = jax.nn.sigmoid(gz[:, c:] + b_i)
            log_a = (-LRU_C) * r * sp
            a = jnp.exp(log_a)
            th = jnp.tanh(log_a)
            bb = jnp.sqrt(-2.0 * th / (1.0 - th)) * (gi * xc)
            for s in (1, 2, 4):
                if d == 0:
                    ok = sub >= s
                    sh = s
                else:
                    ok = sub <= (SUBLANES - 1 - s)
                    sh = tc - s
                a_prev = jnp.where(ok, pltpu.roll(a, sh, axis=0), 1.0)
                b_prev = jnp.where(ok, pltpu.roll(bb, sh, axis=0), 0.0)
                bb = a * b_prev + bb
                a = a * a_prev
            outs = [None] * groups
            order = range(groups) if d == 0 else range(groups - 1, -1, -1)
            for g in order:
                lo = g * SUBLANES
                hg = a[lo:lo + SUBLANES] * h + bb[lo:lo + SUBLANES]
                outs[g] = hg
                h = hg[SUBLANES - 1:SUBLANES] if d == 0 else hg[0:1]
            hs = jnp.concatenate(outs, axis=0)
            if d == 0:
                rec_ref[pl.ds(r0, tc), :] = hs
            else:
                tot = rec_ref[pl.ds(r0, tc), :] + hs
                gate = jax.nn.gelu(ga_ref[pl.ds(r0, tc), :], approximate=True)
                o_ref[pl.ds(r0, tc), :] = (tot * gate).astype(o_ref.dtype)
            return h

        lax.fori_loop(0, n_chunks, chunk, jnp.zeros((1, c), F32))


def _lru(xa, ga, conv_w, conv_b, w_gates, b_r, b_i, sp, ctx_len):
    b, t, w = xa.shape
    c = 2 * LANES
    nh = w // c
    tok = lambda bi, hi: (bi, 0, hi)
    par = lambda bi, hi: (0, 0, hi)
    return pl.pallas_call(
        functools.partial(_lru_kernel, ctx_len=ctx_len),
        out_shape=jax.ShapeDtypeStruct((b, t, w), BF16),
        grid=(b, nh),
        in_specs=[pl.BlockSpec((None, t, c), tok),
                  pl.BlockSpec((None, t, c), tok),
                  pl.BlockSpec((2, CONV_W, c), par),
                  pl.BlockSpec((2, 1, c), par),
                  pl.BlockSpec((2, None, c, 2 * c), lambda bi, hi: (0, hi, 0, 0)),
                  pl.BlockSpec((2, 1, c), par),
                  pl.BlockSpec((2, 1, c), par),
                  pl.BlockSpec((2, 1, c), par)],
        out_specs=pl.BlockSpec((None, t, c), tok),
        scratch_shapes=[pltpu.VMEM((t + 2 * SUBLANES, c), F32), pltpu.VMEM((t, c), F32)],
        compiler_params=_cparams(("arbitrary", "arbitrary")),
        name="rglru",
    )(xa, ga, conv_w, conv_b, w_gates, b_r, b_i, sp)


def _mla_prep_kernel(in_ref, qan_ref, kvn_ref, wq_ref, wk_ref, wv_ref, mq_ref, mk_ref, gq_ref, gk_ref, gkr_ref,
                     one_ref, cos_ref, slo_ref, shi_ref, q_ref, k_ref, v_ref, *, q_lora, kv_lora, scale):
    z = in_ref[...]
    heads = q_ref.shape[-1] // LANES
    cos = cos_ref[...]
    slo = slo_ref[...]
    shi = shi_ref[...]
    half = MLA_ROPE // 2

    qan = (_rms(z[:, :q_lora]) * qan_ref[...]).astype(BF16)
    q = _dot(qan, wq_ref[...])
    q = q * lax.rsqrt(_dot((q * q).astype(BF16), mq_ref[...]) + EPS) * gq_ref[...]
    q = _rope(q, _tile_lanes(cos, heads), _tile_lanes(slo, heads), _tile_lanes(shi, heads), half)
    q_ref[...] = (q * scale).astype(BF16)

    kvn = (_rms(z[:, q_lora:q_lora + kv_lora]) * kvn_ref[...]).astype(BF16)
    kk = _dot(kvn, wk_ref[...])
    kk = kk * lax.rsqrt(_dot((kk * kk).astype(BF16), mk_ref[...]) + EPS) * gk_ref[...]
    v_ref[...] = (_dot(kvn, wv_ref[...]) + one_ref[...]).astype(BF16)

    kr = z[:, q_lora + kv_lora:]
    kr = kr * lax.rsqrt(jnp.sum(kr * kr, axis=-1, keepdims=True) * (1.0 / MLA_ROPE) + EPS) * gkr_ref[...]
    kr = pltpu.roll(kr, MLA_NOPE, axis=1)
    kr = _rope(kr, cos, slo, shi, half)
    k_ref[...] = (kk + _tile_lanes(kr, heads)).astype(BF16)


def _mla_prep(mla_in, p, scale):
    b, t, w = mla_in.shape
    hq = MLA_HEADS * LANES
    tok = lambda bi, ti: (bi, ti, 0)
    full = lambda a: pl.BlockSpec(a.shape, lambda bi, ti: (0,) * a.ndim)
    pos = pl.BlockSpec((TOK_TILE, LANES), lambda bi, ti: (ti, 0))
    consts = [p["qan"], p["kvn"], p["wq"], p["wk"], p["wv"], p["mq"], p["mk"], p["gq"], p["gk"], p["gkr"], p["one"]]
    out = jax.ShapeDtypeStruct((b, t, hq), BF16)
    return pl.pallas_call(
        functools.partial(_mla_prep_kernel, q_lora=p["qan"].shape[1], kv_lora=p["kvn"].shape[1], scale=scale),
        out_shape=(out, out, out),
        grid=(b, t // TOK_TILE),
        in_specs=[pl.BlockSpec((None, TOK_TILE, w), tok)] + [full(a) for a in consts] + [pos, pos, pos],
        out_specs=(pl.BlockSpec((None, TOK_TILE, hq), tok),) * 3,
        compiler_params=_cparams(("arbitrary", "arbitrary")),
        name="mla_prep",
    )(mla_in, *consts, p["cos"], p["slo"], p["shi"])


def _mla_attn_kernel(q_ref, k_ref, v_ref, o_ref, *, ctx_len):
    tq = q_ref.shape[0]
    t = k_ref.shape[0]
    lane = lax.broadcasted_iota(I32, (tq, LANES), 1)

    def attend(nk):
        outs = []
        for hh in range(2):
            q = q_ref[:, hh * LANES:(hh + 1) * LANES]
            k = k_ref[0:nk, hh * LANES:(hh + 1) * LANES]
            s = _dot_nt(q, k)
            p = jnp.exp2(s - jnp.max(s, axis=-1, keepdims=True))
            o = _dot(p.astype(BF16), v_ref[0:nk, hh * LANES:(hh + 1) * LANES])
            outs.append(o / o[:, MLA_V:MLA_V + 1])
        o_ref[...] = jnp.where(lane < MLA_V, outs[0], pltpu.roll(outs[1], MLA_V, axis=1)).astype(o_ref.dtype)

    @pl.when(pl.program_id(2) == 0)
    def _():
        attend(ctx_len)

    @pl.when(pl.program_id(2) > 0)
    def _():
        attend(t)


def _mla_attn(q, k, v, ctx_len):
    b, t, hq = q.shape
    pairs = hq // (2 * LANES)
    return pl.pallas_call(
        functools.partial(_mla_attn_kernel, ctx_len=ctx_len),
        out_shape=jax.ShapeDtypeStruct((b, t, pairs * 2 * MLA_V), BF16),
        grid=(b, pairs, t // TOK_TILE),
        in_specs=[pl.BlockSpec((None, TOK_TILE, 2 * LANES), lambda bi, hi, ti: (bi, ti, hi)),
                  pl.BlockSpec((None, t, 2 * LANES), lambda bi, hi, ti: (bi, 0, hi)),
                  pl.BlockSpec((None, t, 2 * LANES), lambda bi, hi, ti: (bi, 0, hi))],
        out_specs=pl.BlockSpec((None, TOK_TILE, LANES), lambda bi, hi, ti: (bi, ti, hi)),
        compiler_params=_cparams(("arbitrary", "arbitrary", "arbitrary")),
        name="mla_attn",
    )(q, k, v)


def _odd_in_kernel(x_ref, mod_ref, g_ref, w_ref, mq_ref, mk_ref, gq_ref, gk_ref, cos_ref, slo_ref, shi_ref,
                   q_ref, k_ref, v_ref, *, scale):
    mod = mod_ref[...]
    h = _prenorm(x_ref[...], g_ref[...], mod[1:2], mod[0:1]).astype(BF16)
    nq = q_ref.shape[-1]
    nk = k_ref.shape[-1]
    half = GQA_DIM // 2

    def head_norm_rope(y, m_ref, gain_ref):
        reps = y.shape[-1] // LANES
        y = y * lax.rsqrt(_dot((y * y).astype(BF16), m_ref[...]) + EPS) * gain_ref[...]
        return _rope(y, _tile_lanes(cos_ref[...], reps), _tile_lanes(slo_ref[...], reps),
                     _tile_lanes(shi_ref[...], reps), half)

    q_ref[...] = (head_norm_rope(_dot(h, w_ref[:, 0:nq]), mq_ref, gq_ref) * scale).astype(BF16)
    k_ref[...] = head_norm_rope(_dot(h, w_ref[:, nq:nq + nk]), mk_ref, gk_ref).astype(BF16)
    v_ref[...] = _dot(h, w_ref[:, nq + nk:nq + 2 * nk]).astype(BF16)


def _odd_in(xs, mods, g, p, scale):
    b, t, d = xs.shape
    nq = p["gq"].shape[1]
    nk = p["gk"].shape[1]
    row = functools.partial(_mod_row, ctx_row=b)
    tok = lambda bi, ti: (bi, ti, 0)
    full = lambda a: pl.BlockSpec(a.shape, lambda bi, ti: (0,) * a.ndim)
    pos = pl.BlockSpec((TOK_TILE, LANES), lambda bi, ti: (ti, 0))
    kv = jax.ShapeDtypeStruct((b, t, nk), BF16)
    return pl.pallas_call(
        functools.partial(_odd_in_kernel, scale=scale),
        out_shape=(jax.ShapeDtypeStruct((b, t, nq), BF16), kv, kv),
        grid=(b, t // TOK_TILE),
        in_specs=[pl.BlockSpec((None, TOK_TILE, d), tok),
                  pl.BlockSpec((None, 6, d), lambda bi, ti: (row(bi, ti), 0, 0)),
                  pl.BlockSpec((1, d), lambda bi, ti: (0, 0)),
                  full(p["w"]), full(p["mq"]), full(p["mk"]), full(p["gq"]), full(p["gk"]), pos, pos, pos],
        out_specs=(pl.BlockSpec((None, TOK_TILE, nq), tok),
                   pl.BlockSpec((None, TOK_TILE, nk), tok),
                   pl.BlockSpec((None, TOK_TILE, nk), tok)),
        compiler_params=_cparams(("arbitrary", "arbitrary")),
        name="odd_in",
    )(xs, mods, g, p["w"], p["mq"], p["mk"], p["gq"], p["gk"], p["cos"], p["slo"], p["shi"])


def _win_attn_kernel(sink_ref, q_ref, k_ref, v_ref, o_ref, *, ctx_len):
    t = k_ref.shape[0]
    wb = WIN_BLOCK
    w3 = 3 * wb
    hd = GQA_DIM
    group = GQA_HEADS // GQA_KV_HEADS
    i = pl.program_id(1)
    q0 = ctx_len + i * wb
    ws = pl.multiple_of(jnp.clip(q0 - wb, ctx_len, t - w3), wb)
    nk = w3 + ctx_len
    rows = group * wb
    row = lax.broadcasted_iota(I32, (rows, nk), 0)
    col = lax.broadcasted_iota(I32, (rows, nk), 1)
    valid = (col >= w3) | (jnp.abs(q0 + row % wb - (ws + col)) <= WINDOW)
    head_of_row = lax.broadcasted_iota(I32, (rows, 1), 0) // wb
    q = q_ref[...].astype(F32)
    kcat = jnp.concatenate([k_ref[pl.ds(ws, w3), :], k_ref[0:ctx_len, :]], axis=0).astype(F32)
    vcat = jnp.concatenate([v_ref[pl.ds(ws, w3), :], v_ref[0:ctx_len, :]], axis=0).astype(F32)
    outs = []
    for kh in range(GQA_KV_HEADS):
        qs = jnp.concatenate([q[:, (kh * group + h) * hd:(kh * group + h + 1) * hd] for h in range(group)], axis=0)
        s = _dot_nt(qs.astype(BF16), kcat[:, kh * hd:(kh + 1) * hd].astype(BF16))
        s = jnp.where(valid, s, NEG_INF)
        sink = jnp.zeros((rows, 1), F32)
        for h in range(group):
            sink = jnp.where(head_of_row == h, sink_ref[kh * group + h], sink)
        m = jnp.maximum(jnp.max(s, axis=-1, keepdims=True), sink)
        p = jnp.exp(s - m)
        l = jnp.sum(p, axis=-1, keepdims=True) + jnp.exp(sink - m)
        o = _dot(p.astype(BF16), vcat[:, kh * hd:(kh + 1) * hd].astype(BF16)) / l
        outs.extend(o[h * wb:(h + 1) * wb] for h in range(group))
    o_ref[...] = jnp.concatenate(outs, axis=1).astype(o_ref.dtype)


def _win_attn(q, k, v, sink, ctx_len):
    b, t, n = q.shape
    nkv = k.shape[-1]
    s_len = t - ctx_len
    off = ctx_len // WIN_BLOCK
    return pl.pallas_call(
        functools.partial(_win_attn_kernel, ctx_len=ctx_len),
        out_shape=jax.ShapeDtypeStruct((b, s_len, n), BF16),
        grid=(b, s_len // WIN_BLOCK),
        in_specs=[pl.BlockSpec(memory_space=pltpu.SMEM),
                  pl.BlockSpec((None, WIN_BLOCK, n), lambda bi, ti: (bi, ti + off, 0)),
                  pl.BlockSpec((None, t, nkv), lambda bi, ti: (bi, 0, 0)),
                  pl.BlockSpec((None, t, nkv), lambda bi, ti: (bi, 0, 0))],
        out_specs=pl.BlockSpec((None, WIN_BLOCK, n), lambda bi, ti: (bi, ti, 0)),
        compiler_params=_cparams(("arbitrary", "arbitrary")),
        name="win_attn",
    )(sink, q, k, v)


def _post_mix_kernel(a1_ref, a2_ref, *refs, split_stream):
    x_in = _stream_tile(refs[0], refs[1]) if split_stream else refs[0][...]
    _post_mix_body(a1_ref, a2_ref, x_in, *refs[2 if split_stream else 1:])


def _post_mix_body(a1_ref, a2_ref, x_in, mod_ref, g_ref, w_ref, wr_ref, br_ref,
                   xo_ref, f_ref, route_ref, gate_ref, cnt_ref, seen_ref):
    half = a1_ref.shape[-1]
    mod = mod_ref[...]
    m = _dot(a1_ref[...], w_ref[0:half, :]) + _dot(a2_ref[...], w_ref[half:2 * half, :])
    x = x_in + mod[2:3] * m
    xo_ref[...] = x
    f = _prenorm(x, g_ref[...], mod[4:5], mod[3:4])
    f_ref[...] = f

    logit = _dot(f.astype(BF16), wr_ref[...].astype(BF16)) + br_ref[...]
    tm = logit.shape[0]
    lane = lax.broadcasted_iota(I32, (tm, LANES), 1)
    lane_f = lane.astype(F32)
    vals, idxs = [], []
    for _ in range(TOP_K):
        mx = jnp.max(logit, axis=-1, keepdims=True)
        ix = jnp.min(jnp.where(logit == mx, lane_f, float(LANES)), axis=-1, keepdims=True)
        vals.append(mx)
        idxs.append(ix)
        logit = jnp.where(lane_f == ix, -jnp.inf, logit)
    exps = [jnp.exp(v - vals[0]) for v in vals]
    den = exps[0]
    for e in exps[1:]:
        den = den + e

    @pl.when((pl.program_id(0) == 0) & (pl.program_id(1) == 0))
    def _():
        seen_ref[...] = jnp.zeros(seen_ref.shape, F32)

    msk = jnp.zeros((tm, LANES), F32)
    for k in range(TOP_K):
        msk = jnp.where(lane_f == idxs[k], 1.0, msk)
    earlier = (lax.broadcasted_iota(I32, (tm, tm), 1) < lax.broadcasted_iota(I32, (tm, tm), 0))
    rank = _dot(jnp.where(earlier, 1.0, 0.0).astype(BF16), msk.astype(BF16)) + seen_ref[0:1, :]
    seen = seen_ref[...] + jnp.sum(msk, axis=0, keepdims=True)
    seen_ref[...] = seen
    cnt_ref[...] = seen

    r_out = jnp.zeros((tm, LANES), F32)
    g_out = jnp.zeros((tm, LANES), F32)
    for k in range(TOP_K):
        rank_k = jnp.sum(jnp.where(lane_f == idxs[k], rank, 0.0), axis=-1, keepdims=True)
        r_out = jnp.where(lane == k, idxs[k], r_out)
        r_out = jnp.where(lane == TOP_K + k, rank_k, r_out)
        g_out = jnp.where(lane == k, exps[k] / den, g_out)
    route_ref[...] = r_out.astype(I32)
    gate_ref[...] = g_out


def _post_mix(a1, a2, lane_blk2, xs, mods, g, w_out, w_router, b_router, t_off, a_off):
    split_stream = isinstance(xs, tuple)
    if split_stream:
        assert t_off == 0
        b, s_len, d = xs[1].shape
        t = xs[0].shape[1] + s_len
        x_specs = _stream_specs(d)
    else:
        b, t, d = xs.shape
        x_specs = [pl.BlockSpec((None, TOK_TILE, d), lambda bi, ti: (bi, ti + t_off, 0))]
        xs = (xs,)
    half = w_out.shape[0] // 2
    nt = t // TOK_TILE - t_off
    t_out = nt * TOK_TILE
    row = functools.partial(_mod_row, ctx_row=b)
    tok = lambda bi, ti: (bi, ti, 0)
    act = jax.ShapeDtypeStruct((b, t_out, d), F32)
    return pl.pallas_call(
        functools.partial(_post_mix_kernel, split_stream=split_stream),
        out_shape=(act, act,
                   jax.ShapeDtypeStruct((b, t_out, LANES), I32),
                   jax.ShapeDtypeStruct((b, t_out, LANES), F32),
                   jax.ShapeDtypeStruct((SUBLANES, LANES), F32)),
        grid=(b, nt),
        in_specs=[pl.BlockSpec((None, TOK_TILE, half), lambda bi, ti: (bi, ti + a_off, 0)),
                  pl.BlockSpec((None, TOK_TILE, half), lambda bi, ti: (bi, ti + a_off, lane_blk2))] + x_specs + [
                  pl.BlockSpec((None, 6, d), lambda bi, ti: (row(bi, ti + t_off), 0, 0)),
                  pl.BlockSpec((1, d), lambda bi, ti: (0, 0)),
                  pl.BlockSpec(w_out.shape, lambda bi, ti: (0, 0)),
                  pl.BlockSpec(w_router.shape, lambda bi, ti: (0, 0)),
                  pl.BlockSpec((1, LANES), lambda bi, ti: (0, 0))],
        out_specs=(pl.BlockSpec((None, TOK_TILE, d), tok),
                   pl.BlockSpec((None, TOK_TILE, d), tok),
                   pl.BlockSpec((None, TOK_TILE, LANES), tok),
                   pl.BlockSpec((None, TOK_TILE, LANES), tok),
                   pl.BlockSpec((SUBLANES, LANES), lambda bi, ti: (0, 0))),
        scratch_shapes=[pltpu.VMEM((SUBLANES, LANES), F32)],
        compiler_params=_cparams(("arbitrary", "arbitrary")),
        name="post_mix",
    )(a1, a2, *xs, mods, g, w_out, w_router, b_router)


DMA_UNROLL = 2


def _row(ref, i):
    return ref.at[pl.ds(i, 1), :]


def _dispatch_kernel(pad_lo_ref, pad_n_ref, nu_ref, dest_ref, f_ref, xs_ref, zero_ref, sem, zsem):
    n = dest_ref.shape[0]

    blk = zero_ref.shape[0]
    n_blk = xs_ref.shape[0] // blk

    def block_copy(i):
        return pltpu.make_async_copy(zero_ref, xs_ref.at[pl.ds(pl.multiple_of(i * blk, blk), blk), :], zsem)

    def pad_rows(e, c, wait):
        lo = pad_lo_ref[e]
        head = jnp.minimum((-lo) & (SUBLANES - 1), pad_n_ref[e])

        def one(r, c2):
            copy = pltpu.make_async_copy(_row(zero_ref, 0), _row(xs_ref, lo + r), zsem)
            copy.wait() if wait else copy.start()
            return c2

        def eight(g, c2):
            r0 = pl.multiple_of(lo + head + g * SUBLANES, SUBLANES)
            copy = pltpu.make_async_copy(zero_ref.at[pl.ds(0, SUBLANES), :], xs_ref.at[pl.ds(r0, SUBLANES), :], zsem)
            copy.wait() if wait else copy.start()
            return c2

        c = lax.fori_loop(0, head, one, c)
        return lax.fori_loop(0, (pad_n_ref[e] - head) // SUBLANES, eight, c)

    @pl.when(pl.program_id(0) == 0)
    def _():
        zero_ref[...] = jnp.zeros(zero_ref.shape, F32)
        lax.fori_loop(nu_ref[0], n_blk, lambda i, c: (block_copy(i).start(), c)[1], 0)
        lax.fori_loop(0, N_EXPERTS, functools.partial(pad_rows, wait=False), 0)

    @pl.when(pl.program_id(0) == pl.num_programs(0) - 1)
    def _():
        lax.fori_loop(nu_ref[0], n_blk, lambda i, c: (block_copy(i).wait(), c)[1], 0)
        lax.fori_loop(0, N_EXPERTS, functools.partial(pad_rows, wait=True), 0)


    def issue(t, c):
        for k in range(TOP_K):
            pltpu.make_async_copy(_row(f_ref, t), _row(xs_ref, dest_ref[t * TOP_K + k]), sem).start()
        return c

    lax.fori_loop(0, n // TOP_K, issue, 0, unroll=DMA_UNROLL)
    pltpu.make_async_copy(xs_ref.at[pl.ds(0, n), :], xs_ref.at[pl.ds(0, n), :], sem).wait()


def _dispatch(f, dest, pad_lo, pad_n, n_used, rows):
    n, d = f.shape
    per = TOK_TILE * TOP_K
    return pl.pallas_call(
        _dispatch_kernel,
        out_shape=jax.ShapeDtypeStruct((rows, d), F32),
        grid_spec=pltpu.PrefetchScalarGridSpec(
            num_scalar_prefetch=3,
            grid=(n // TOK_TILE,),
            in_specs=[pl.BlockSpec((per,), lambda i, lo, cnt, nu: (i,), memory_space=pltpu.SMEM),
                      pl.BlockSpec((TOK_TILE, d), lambda i, lo, cnt, nu: (i, 0))],
            out_specs=pl.BlockSpec(memory_space=pl.ANY),
            scratch_shapes=[pltpu.VMEM((MOE_ROWS, d), F32), pltpu.SemaphoreType.DMA(()),
                            pltpu.SemaphoreType.DMA(())]),
        compiler_params=_cparams(("arbitrary",), has_side_effects=True, disable_bounds_checks=True),
        name="moe_dispatch",
    )(pad_lo, pad_n, n_used, dest, f)


def _experts_kernel(be_ref, nu_ref, grp_ref, nxt_ref, xs_ref, wgu_hbm, bgu_ref, wdn_hbm, bdn_ref, y_ref,
                    wgu_f32, wdn_f32, wgu_bf, wdn_bf, wsem, *, layer):
    i = pl.program_id(0)

    def fetch(e, slot, wait):
        for src, dst in ((wgu_hbm.at[layer, e], wgu_f32.at[slot]), (wdn_hbm.at[layer, e], wdn_f32.at[slot])):
            c = pltpu.make_async_copy(src, dst, wsem.at[slot])
            c.wait() if wait else c.start()

    @pl.when(i == 0)
    def _():
        fetch(be_ref[0], 0, False)

    @pl.when((i < nu_ref[0]) & ((i == 0) | (grp_ref[i] != grp_ref[jnp.maximum(i - 1, 0)])))
    def _():
        slot = grp_ref[i] % 2
        fetch(be_ref[i], slot, True)
        wgu_bf[...] = wgu_f32[slot].astype(BF16)
        wdn_bf[...] = wdn_f32[slot].astype(BF16)

        @pl.when(nxt_ref[i] >= 0)
        def _():
            fetch(nxt_ref[i], 1 - slot, False)

    @pl.when(i < nu_ref[0])
    def _():
        h = _dot(xs_ref[...].astype(BF16), wgu_bf[...]) + bgu_ref[...]
        ff = h.shape[1] // 2
        hg = jnp.minimum(h[:, :ff], SWIGLU_LIMIT)
        hu = jnp.clip(h[:, ff:], -SWIGLU_LIMIT, SWIGLU_LIMIT)
        act = hg * jax.nn.sigmoid(SWIGLU_ALPHA * hg) * (hu + 1.0)
        y_ref[...] = _dot(act.astype(BF16), wdn_bf[...]) + bdn_ref[...]

    @pl.when(i >= nu_ref[0])
    def _():
        y_ref[...] = jnp.zeros(y_ref.shape, F32)


def _experts(xs, block_e, n_used, layer, wgu, bgu, wdn, bdn):
    rows, d = xs.shape
    ff2 = wgu.shape[-1]
    nb = rows // MOE_ROWS
    blk = jnp.arange(nb)
    used = blk < n_used[0]
    first = used & ((blk == 0) | (block_e != jnp.roll(block_e, 1)))
    grp = jnp.cumsum(first.astype(I32)) - 1
    later_first = jnp.where(first[None, :] & (blk[None, :] > blk[:, None]), blk[None, :], nb)
    nxt_blk = jnp.min(later_first, axis=1)
    nxt = jnp.where(nxt_blk < nb, block_e[jnp.minimum(nxt_blk, nb - 1)], -1)
    bias = lambda i, be, nu, g, nx: (layer, be[i], 0, 0)
    return pl.pallas_call(
        functools.partial(_experts_kernel, layer=layer),
        out_shape=jax.ShapeDtypeStruct((rows, d), F32),
        grid_spec=pltpu.PrefetchScalarGridSpec(
            num_scalar_prefetch=4,
            grid=(nb,),
            in_specs=[pl.BlockSpec((MOE_ROWS, d),
                                   lambda i, be, nu, g, nx: (jnp.maximum(jnp.minimum(i, nu[0] - 1), 0), 0)),
                      pl.BlockSpec(memory_space=pl.ANY),
                      pl.BlockSpec((None, None, 1, ff2), bias),
                      pl.BlockSpec(memory_space=pl.ANY),
                      pl.BlockSpec((None, None, 1, d), bias)],
            out_specs=pl.BlockSpec((MOE_ROWS, d), lambda i, be, nu, g, nx: (i, 0)),
            scratch_shapes=[pltpu.VMEM((2, d, ff2), F32), pltpu.VMEM((2, ff2 // 2, d), F32),
                            pltpu.VMEM((d, ff2), BF16), pltpu.VMEM((ff2 // 2, d), BF16),
                            pltpu.SemaphoreType.DMA((2,))]),
        compiler_params=_cparams(("arbitrary",), has_side_effects=True),
        name="moe_experts",
    )(block_e, n_used, grp.astype(I32), nxt.astype(I32), xs, wgu, bgu, wdn, bdn)


IDX_BLOCK = 1024


def _fused_kernel(be_ref, nu_ref, tok0_ref, tok_ref, tgt_ref, f_ref, wgu_ref, bgu_ref, wdn_ref, bdn_ref, out_ref,
                  wgu_bf, wdn_bf, x0, x1, y0, y1, gsem, ssem):
    i = pl.program_id(0)
    nu = nu_ref[0]
    rows = MOE_ROWS
    per = IDX_BLOCK // rows
    xb = (x0, x1)
    yb = (y0, y1)

    def gather(idx_ref, base, slot):
        for j in range(rows):
            pltpu.make_async_copy(_row(f_ref, idx_ref[base + j]), _row(xb[slot], j), gsem.at[slot]).start(priority=j % 2)

    def scatter(base, slot):
        for j in range(rows):
            pltpu.make_async_copy(_row(yb[slot], j), _row(out_ref, tgt_ref[base + j]), ssem.at[slot]).start(priority=j % 2)

    def wait_rows(buf, sem):
        pltpu.make_async_copy(buf, buf, sem).wait()

    @pl.when(i == 0)
    def _():
        y1[...] = jnp.zeros(y1.shape, F32)
        n_real = out_ref.shape[0] - 2 * rows
        c = pltpu.make_async_copy(y1, out_ref.at[pl.ds(n_real, rows), :], ssem.at[0])
        c.start()
        c.wait()
        gather(tok0_ref, 0, 0)

    @pl.when((i <= nu) & ((i == 0) | (be_ref[i] != be_ref[jnp.maximum(i - 1, 0)])))
    def _():
        wgu_bf[...] = wgu_ref[...].astype(BF16)
        wdn_bf[...] = wdn_ref[...].astype(BF16)

    gbase = ((i + 1) % per) * rows
    sbase = (i % per) * rows

    for slot in range(2):
        other = 1 - slot

        @pl.when((i % 2 == slot) & (i < nu))
        def _(slot=slot, other=other):
            wait_rows(xb[slot], gsem.at[slot])

            @pl.when(i >= 1)
            def _():
                wait_rows(yb[slot], ssem.at[slot])

            gather(tok_ref, gbase, other)
            scatter(sbase, other)
            h = _dot(xb[slot][...].astype(BF16), wgu_bf[...]) + bgu_ref[...]
            ff = h.shape[1] // 2
            hg = jnp.minimum(h[:, :ff], SWIGLU_LIMIT)
            hu = jnp.clip(h[:, ff:], -SWIGLU_LIMIT, SWIGLU_LIMIT)
            act = hg * jax.nn.sigmoid(SWIGLU_ALPHA * hg) * (hu + 1.0)
            yb[slot][...] = _dot(act.astype(BF16), wdn_bf[...]) + bdn_ref[...]

        @pl.when((i % 2 == slot) & (i == nu))
        def _(slot=slot, other=other):
            wait_rows(xb[slot], gsem.at[slot])

            @pl.when(i >= 1)
            def _():
                wait_rows(yb[slot], ssem.at[slot])

            scatter(sbase, other)
            wait_rows(yb[other], ssem.at[other])


def _experts_fused(f, row_tok, row_tgt, block_e, n_used, layer, wgu, bgu, wdn, bdn):
    n, d = f.shape
    ff2 = wgu.shape[-1]
    nb = row_tok.shape[0] // MOE_ROWS
    per = IDX_BLOCK // MOE_ROWS
    last = row_tok.shape[0] // IDX_BLOCK - 1
    last_tgt = row_tgt.shape[0] // IDX_BLOCK - 1
    buf = pltpu.VMEM((MOE_ROWS, d), F32)
    w_idx = lambda i, be, nu: (layer, be[i], 0, 0)
    return pl.pallas_call(
        _fused_kernel,
        out_shape=jax.ShapeDtypeStruct((TOP_K * n + 2 * MOE_ROWS, d), F32),
        grid_spec=pltpu.PrefetchScalarGridSpec(
            num_scalar_prefetch=2,
            grid=(nb + 1,),
            in_specs=[pl.BlockSpec((IDX_BLOCK,), lambda i, be, nu: (0,), memory_space=pltpu.SMEM),
                      pl.BlockSpec((IDX_BLOCK,), lambda i, be, nu: (jnp.minimum((i + 1) // per, last),),
                                   memory_space=pltpu.SMEM),
                      pl.BlockSpec((IDX_BLOCK,), lambda i, be, nu: (jnp.minimum(i // per, last_tgt),),
                                   memory_space=pltpu.SMEM),
                      pl.BlockSpec(memory_space=pl.ANY),
                      pl.BlockSpec((None, None, d, ff2), w_idx),
                      pl.BlockSpec((None, None, 1, ff2), w_idx),
                      pl.BlockSpec((None, None, ff2 // 2, d), w_idx),
                      pl.BlockSpec((None, None, 1, d), w_idx)],
            out_specs=pl.BlockSpec(memory_space=pl.ANY),
            scratch_shapes=[pltpu.VMEM((d, ff2), BF16), pltpu.VMEM((ff2 // 2, d), BF16), buf, buf, buf, buf,
                            pltpu.SemaphoreType.DMA((2,)), pltpu.SemaphoreType.DMA((2,))]),
        compiler_params=_cparams(("arbitrary",), has_side_effects=True, disable_bounds_checks=True),
        name="moe_fused",
    )(block_e, n_used, row_tok, row_tok, row_tgt, f, wgu, bgu, wdn, bdn)


def _sum_choices_kernel(p0_ref, p1_ref, p2_ref, p3_ref, gate_ref, x_ref, mod_ref, o_ref):
    gates = gate_ref[...]
    acc = gates[:, 0:1] * p0_ref[...]
    for k, p_ref in enumerate((p1_ref, p2_ref, p3_ref), start=1):
        acc = acc + gates[:, k:k + 1] * p_ref[...]
    o_ref[...] = x_ref[...] + mod_ref[5:6, :] * acc


def _sum_choices(y4, gates, xs, mods, t_off):
    b, t_out, d = xs.shape
    nt = t_out // TOK_TILE
    row = functools.partial(_mod_row, ctx_row=b)
    tok = lambda bi, ti: (bi, ti, 0)
    plane = lambda k: pl.BlockSpec((TOK_TILE, d), lambda bi, ti: (k * b * nt + bi * nt + ti, 0))
    return pl.pallas_call(
        _sum_choices_kernel,
        out_shape=jax.ShapeDtypeStruct((b, t_out, d), F32),
        grid=(b, nt),
        in_specs=[plane(0), plane(1), plane(2), plane(3),
                  pl.BlockSpec((None, TOK_TILE, LANES), tok),
                  pl.BlockSpec((None, TOK_TILE, d), tok),
                  pl.BlockSpec((None, 6, d), lambda bi, ti: (row(bi, ti + t_off), 0, 0))],
        out_specs=pl.BlockSpec((None, TOK_TILE, d), tok),
        compiler_params=_cparams(("arbitrary", "arbitrary")),
        name="moe_sum",
    )(y4, y4, y4, y4, gates, xs, mods)


def _combine_kernel(dest_ref, y_ref, gate_ref, x_ref, mod_ref, o_ref, buf_ref, sem):
    n = dest_ref.shape[0]
    tm = n // TOP_K

    def issue(t, c):
        for k in range(TOP_K):
            pltpu.make_async_copy(_row(y_ref, dest_ref[t * TOP_K + k]), _row(buf_ref.at[k], t), sem).start()
        return c

    lax.fori_loop(0, tm, issue, 0, unroll=DMA_UNROLL)
    for k in range(TOP_K):
        pltpu.make_async_copy(y_ref.at[pl.ds(0, tm), :], buf_ref.at[k], sem).wait()

    gates = gate_ref[...]
    acc = gates[:, 0:1] * buf_ref[0]
    for k in range(1, TOP_K):
        acc = acc + gates[:, k:k + 1] * buf_ref[k]
    o_ref[...] = x_ref[...] + mod_ref[5:6, :] * acc


def _combine(y, dest, gates, xs, mods, t_off):
    b, t_out, d = xs.shape
    nt = t_out // TOK_TILE
    per = TOK_TILE * TOP_K
    row = functools.partial(_mod_row, ctx_row=b)
    tok = lambda bi, ti: (bi, ti, 0)
    return pl.pallas_call(
        _combine_kernel,
        out_shape=jax.ShapeDtypeStruct((b, t_out, d), F32),
        grid=(b, nt),
        in_specs=[pl.BlockSpec((per,), lambda bi, ti: (bi * nt + ti,), memory_space=pltpu.SMEM),
                  pl.BlockSpec(memory_space=pl.ANY),
                  pl.BlockSpec((None, TOK_TILE, LANES), tok),
                  pl.BlockSpec((None, TOK_TILE, d), tok),
                  pl.BlockSpec((None, 6, d), lambda bi, ti: (row(bi, ti + t_off), 0, 0))],
        out_specs=pl.BlockSpec((None, TOK_TILE, d), tok),
        scratch_shapes=[pltpu.VMEM((TOP_K, TOK_TILE, d), F32), pltpu.SemaphoreType.DMA(())],
        compiler_params=_cparams(("arbitrary", "arbitrary"), disable_bounds_checks=True),
        name="moe_combine",
    )(dest, y, gates, xs, mods)


def _routing(e_sel, rank, counts, n):
    padded = (counts + MOE_ROWS - 1) // MOE_ROWS * MOE_ROWS
    pend = jnp.cumsum(padded)
    pstart = pend - padded
    dest = jnp.sum(jnp.where(e_sel[..., None] == jnp.arange(N_EXPERTS), pstart, 0), axis=-1) + rank
    n_blocks = n * TOP_K // MOE_ROWS + N_EXPERTS
    first_row = jnp.arange(n_blocks) * MOE_ROWS
    block_e = jnp.minimum(jnp.sum(pend[None, :] <= first_row[:, None], axis=1), N_EXPERTS - 1)
    n_used = (pend[-1] // MOE_ROWS).reshape(1)
    pads = ((pstart + counts).astype(I32), (padded - counts).astype(I32))
    return dest.reshape(-1).astype(I32), block_e.astype(I32), n_used.astype(I32), pads, n_blocks * MOE_ROWS


def _moe_fused(f, route, gates, seen, x_mid, mods, t_off, layer, wgu, bgu, wdn, bdn):
    b, t_out, d = f.shape
    n = b * t_out
    route = route.reshape(n, LANES)
    counts = seen[0, :N_EXPERTS].astype(I32)
    dest, block_e, n_used, _, rows = _routing(route[:, :TOP_K], route[:, TOP_K:2 * TOP_K], counts, n)
    assert rows % IDX_BLOCK == 0
    choice = jnp.arange(n * TOP_K, dtype=I32)
    tok, k = choice // TOP_K, choice % TOP_K
    r = jnp.arange(rows, dtype=I32)
    scratch = TOP_K * n + ((r // MOE_ROWS) % 2) * MOE_ROWS + r % MOE_ROWS
    row_tok = jnp.zeros((rows,), I32).at[dest].set(tok)
    row_tgt = scratch.at[dest].set(k * n + tok)
    lead = TOP_K * n + MOE_ROWS + jnp.arange(MOE_ROWS, dtype=I32)
    tail = jnp.full((IDX_BLOCK - MOE_ROWS,), TOP_K * n, I32)
    row_tgt = jnp.concatenate([lead, row_tgt, tail])
    block_e = jnp.concatenate([block_e, block_e[-1:]])
    y4 = _experts_fused(f.reshape(n, d), row_tok, row_tgt, block_e, n_used, layer, wgu, bgu, wdn, bdn)
    return _sum_choices(y4, gates, x_mid, mods, t_off)


def _moe(f, route, gates, seen, x_mid, mods, t_off, layer, wgu, bgu, wdn, bdn):
    b, t_out, d = f.shape
    n = b * t_out
    route = route.reshape(n, LANES)
    counts = seen[0, :N_EXPERTS].astype(I32)
    dest, block_e, n_used, pads, rows = _routing(route[:, :TOP_K], route[:, TOP_K:2 * TOP_K], counts, n)
    xs = _dispatch(f.reshape(n, d), dest, *pads, n_used, rows)
    y = _experts(xs, block_e, n_used, layer, wgu, bgu, wdn, bdn)
    return _combine(y, dest, gates, x_mid, mods, t_off)


def _axial_angles(n_rows, rot_dim):
    row = np.repeat(np.arange(n_rows, dtype=np.float32), GRID_W)
    col = np.tile(np.arange(GRID_W, dtype=np.float32), n_rows)
    n = rot_dim // 4
    freqs = (np.float32(ROPE_THETA) ** (-np.arange(n, dtype=np.float32) / np.float32(n))).astype(np.float32)
    return np.concatenate([row[:, None] * freqs, col[:, None] * freqs], axis=-1).astype(np.float32)


def _rope_tables(ctx_len, s_len, rot_dim, lane_base, reps):
    ang = _axial_angles(s_len // GRID_W, rot_dim)
    half = rot_dim // 2
    period = LANES // reps
    cos = np.ones((ctx_len + s_len, period), np.float32)
    slo = np.zeros((ctx_len + s_len, period), np.float32)
    shi = np.zeros((ctx_len + s_len, period), np.float32)
    cos[ctx_len:, lane_base:lane_base + rot_dim] = np.tile(np.cos(ang), (1, 2))
    slo[ctx_len:, lane_base:lane_base + half] = -np.sin(ang)
    shi[ctx_len:, lane_base + half:lane_base + rot_dim] = np.sin(ang)
    return [jnp.asarray(np.tile(tb, (1, reps))) for tb in (cos, slo, shi)]


def _segment_mean_matrix(width, period, segs):
    lane = np.arange(width)
    seg_id = np.full((width,), -1)
    seg_w = np.zeros((width,), np.float32)
    for i, (start, length) in enumerate(segs):
        inside = ((lane % period) >= start) & ((lane % period) < start + length)
        seg_id = np.where(inside, (lane // period) * len(segs) + i, seg_id)
        seg_w = np.where(inside, np.float32(1.0 / length), seg_w)
    same = (seg_id[:, None] == seg_id[None, :]) & (seg_id[:, None] >= 0)
    return jnp.asarray(np.where(same, seg_w[None, :], np.float32(0.0)), dtype=BF16)


def _block_diag(w):
    n, c, _ = w.shape
    eye = jnp.eye(n, dtype=w.dtype)
    return (eye[:, None, :, None] * w[:, :, None, :]).reshape(n * c, n * c)


def kernel(x, c, ctx, c_ctx, w_mod, b_mod, norm_mix, norm_ffn, w_in_even, lru_conv_w, lru_conv_b, lru_w_r, lru_b_r, lru_w_i, lru_b_i, lru_lambda, mla_q_a_norm, mla_w_q_b, mla_kv_a_norm, mla_w_kv_b, mla_nope_norm, mla_rope_norm, w_out_even, w_qkv_odd, gqa_qk_norm, gqa_sink, w_out_odd, w_router, b_router, w_gate_up, b_gate_up, w_down, b_down):
    b, s_len, d = x.shape
    ctx_len = ctx.shape[1]
    depth = w_mod.shape[0]
    assert depth == 2 and ctx_len == TOK_TILE and s_len % TOK_TILE == 0 and b + 1 <= SUBLANES
    lru_w = lru_conv_w.shape[-1]
    q_lora = mla_q_a_norm.shape[-1]
    kv_lora = mla_kv_a_norm.shape[-1]

    cvec = jnp.concatenate([c, c_ctx[None], jnp.zeros((SUBLANES - b - 1, d), F32)], axis=0)
    mods = _modulation(cvec, w_mod, b_mod).reshape(depth, SUBLANES, 6, d)

    wr_pad = jnp.zeros((depth, d, LANES), F32).at[:, :, :N_EXPERTS].set(w_router)
    br_pad = jnp.full((depth, 1, LANES), NEG_INF, F32).at[:, 0, :N_EXPERTS].set(b_router)
    experts = (w_gate_up, b_gate_up[:, :, None, :], w_down, b_down[:, :, None, :])

    n_in = w_in_even.shape[-1]
    n_in_pad = -(-n_in // LANES) * LANES
    w_in = jnp.zeros((d, n_in_pad), F32).at[:, :n_in].set(w_in_even[0]).astype(BF16)
    xa, ga, mla_in = _even_in(ctx, x, mods[0], norm_mix[0][None], w_in, lru_w)

    nh = lru_w // (2 * LANES)
    per = LRU_BLOCKS // nh
    blk = lru_w // LRU_BLOCKS
    w_gates = jnp.stack([
        jnp.stack([jnp.concatenate([_block_diag(lru_w_r[0, dd, h * per:(h + 1) * per]),
                                    _block_diag(lru_w_i[0, dd, h * per:(h + 1) * per])], axis=1)
                   for h in range(nh)]) for dd in range(2)]).astype(BF16)
    assert blk * per == 2 * LANES
    ya = _lru(xa, ga, lru_conv_w[0], lru_conv_b[0][:, None, :], w_gates, lru_b_r[0][:, None, :],
              lru_b_i[0][:, None, :], jax.nn.softplus(-lru_lambda[0])[:, None, :], ctx_len)

    qk = MLA_NOPE + MLA_ROPE
    hq = MLA_HEADS * LANES
    wq = jnp.zeros((q_lora, MLA_HEADS, LANES), F32).at[:, :, :qk].set(
        mla_w_q_b[0].reshape(q_lora, MLA_HEADS, qk)).reshape(q_lora, hq).astype(BF16)
    wkv = mla_w_kv_b[0].reshape(kv_lora, MLA_HEADS, MLA_NOPE + MLA_V)
    wk = jnp.zeros((kv_lora, MLA_HEADS, LANES), F32).at[:, :, :MLA_NOPE].set(
        wkv[:, :, :MLA_NOPE]).reshape(kv_lora, hq).astype(BF16)
    wv = jnp.zeros((kv_lora, MLA_HEADS, LANES), F32).at[:, :, :MLA_V].set(
        wkv[:, :, MLA_NOPE:]).reshape(kv_lora, hq).astype(BF16)
    one = jnp.asarray(np.tile(np.arange(LANES) == MLA_V, MLA_HEADS)[None], dtype=F32)
    zpad = jnp.zeros((LANES - qk,), F32)
    gq = jnp.tile(jnp.concatenate([mla_nope_norm[0, 0], mla_rope_norm[0, 0], zpad]), MLA_HEADS)[None]
    gk = jnp.tile(jnp.concatenate([mla_nope_norm[0, 1], jnp.zeros((LANES - MLA_NOPE,), F32)]), MLA_HEADS)[None]
    gkr = jnp.concatenate([mla_rope_norm[0, 1], jnp.zeros((LANES - MLA_ROPE,), F32)])[None]
    cos, slo, shi = _rope_tables(ctx_len, s_len, MLA_ROPE, MLA_NOPE, 1)
    mla_p = dict(qan=mla_q_a_norm[0][None], kvn=mla_kv_a_norm[0][None], wq=wq, wk=wk, wv=wv,
                 mq=_segment_mean_matrix(hq, LANES, [(0, MLA_NOPE), (MLA_NOPE, MLA_ROPE)]),
                 mk=_segment_mean_matrix(hq, LANES, [(0, MLA_NOPE)]),
                 gq=gq, gk=gk, gkr=gkr, one=one, cos=cos, slo=slo, shi=shi)
    q, k, v = _mla_prep(mla_in, mla_p, qk ** -0.5 * math.log2(math.e))
    yb = _mla_attn(q, k, v, ctx_len)

    x_mid, f, route, gates, seen = _post_mix(ya, yb, 0, (ctx, x), mods[0], norm_ffn[0][None], w_out_even[0].astype(BF16),
                                             wr_pad[0], br_pad[0], 0, 0)
    xs = _moe(f, route, gates, seen, x_mid, mods[0], 0, 0, *experts)

    nq = GQA_HEADS * GQA_DIM
    nkv = GQA_KV_HEADS * GQA_DIM
    cos, slo, shi = _rope_tables(ctx_len, s_len, GQA_DIM, 0, LANES // GQA_DIM)
    odd_p = dict(w=w_qkv_odd[0].astype(BF16),
                 mq=_segment_mean_matrix(nq, GQA_DIM, [(0, GQA_DIM)]),
                 mk=_segment_mean_matrix(nkv, GQA_DIM, [(0, GQA_DIM)]),
                 gq=jnp.tile(gqa_qk_norm[0, 0], GQA_HEADS)[None], gk=jnp.tile(gqa_qk_norm[0, 1], GQA_KV_HEADS)[None],
                 cos=cos, slo=slo, shi=shi)
    q, k, v = _odd_in(xs, mods[1], norm_mix[1][None], odd_p, GQA_DIM ** -0.5)
    o = _win_attn(q, k, v, gqa_sink[0], ctx_len)

    t_off = ctx_len // TOK_TILE
    x_mid, f, route, gates, seen = _post_mix(o, o, 1, xs, mods[1], norm_ffn[1][None], w_out_odd[0].astype(BF16),
                                             wr_pad[1], br_pad[1], t_off, 0)
    return _moe(f, route, gates, seen, x_mid, mods[1], t_off, 1, *experts)
```

```python
import functools
import math

import jax
import jax.numpy as jnp
import numpy as np
from jax import lax
from jax.experimental import pallas as pl
from jax.experimental.pallas import tpu as pltpu

F32 = jnp.float32
BF16 = jnp.bfloat16
I32 = jnp.int32

GRID_W = 64
LRU_BLOCKS = 8
LRU_C = 8.0
CONV_W = 4
MLA_HEADS = 8
MLA_NOPE = 64
MLA_ROPE = 32
MLA_V = 64
GQA_HEADS = 16
GQA_KV_HEADS = 4
GQA_DIM = 64
WINDOW = 128
ROPE_THETA = 10000.0
NEG_INF = -1e30
EPS = 1e-6
N_EXPERTS = 32
TOP_K = 4
SWIGLU_LIMIT = 7.0
SWIGLU_ALPHA = 1.702

LANES = 128
SUBLANES = 8
TOK_TILE = 256
LRU_CHUNK = 128
WIN_BLOCK = 128
MOE_ROWS = 256
MXU_TILE = 256
VMEM_LIMIT = 48 * 1024 * 1024


def _cparams(sem, **kw):
    return pltpu.CompilerParams(dimension_semantics=sem, vmem_limit_bytes=VMEM_LIMIT, **kw)


def _dot(a, b):
    return jnp.dot(a, b, preferred_element_type=F32)


def _dot_nt(a, b):
    return lax.dot_general(a, b, (((1,), (1,)), ((), ())), preferred_element_type=F32)


def _split_bf16(x):
    hi = x.astype(BF16)
    lo = (x - hi.astype(F32)).astype(BF16)
    return hi, lo


def _dot3(a, w):
    ah, al = _split_bf16(a)
    wh, wl = _split_bf16(w)
    return _dot(ah, wh) + _dot(al, wh) + _dot(ah, wl)


def _rms(x):
    return x * lax.rsqrt(jnp.mean(x * x, axis=-1, keepdims=True) + EPS)


def _prenorm(x, g, scale, shift):
    return (_rms(x) * g) * (1.0 + scale) + shift


def _rope(x, cos, sin_lo, sin_hi, half):
    w = x.shape[-1]
    return x * cos + pltpu.roll(x, w - half, axis=1) * sin_lo + pltpu.roll(x, half, axis=1) * sin_hi


def _head_mean_square(y, m_ref):
    w = m_ref.shape[0]
    return jnp.concatenate([_dot((y[:, j:j + w] * y[:, j:j + w]).astype(BF16), m_ref[...])
                            for j in range(0, y.shape[1], w)], axis=1)


def _tile_lanes(t, reps):
    return jnp.concatenate([t] * reps, axis=1) if reps > 1 else t


def _mod_kernel(c_ref, w_ref, b_ref, o_ref):
    c = c_ref[...]
    o_ref[...] = _dot3(c * jax.nn.sigmoid(c), w_ref[...]) + b_ref[...]


def _modulation(cvec, w_mod, b_mod):
    depth, d, n = w_mod.shape
    tn = 1536
    return pl.pallas_call(
        _mod_kernel,
        out_shape=jax.ShapeDtypeStruct((depth, SUBLANES, n), F32),
        grid=(depth, n // tn),
        in_specs=[pl.BlockSpec((SUBLANES, d), lambda l, j: (0, 0)),
                  pl.BlockSpec((None, d, tn), lambda l, j: (l, 0, j)),
                  pl.BlockSpec((None, 1, tn), lambda l, j: (l, 0, j))],
        out_specs=pl.BlockSpec((None, SUBLANES, tn), lambda l, j: (l, 0, j)),
        compiler_params=_cparams(("arbitrary", "arbitrary")),
        name="modulation",
    )(cvec, w_mod, b_mod.reshape(depth, 1, n))


def _mod_row(b, t, ctx_row):
    return jnp.where(t == 0, ctx_row, b)


def _stream_tile(ctx_ref, lat_ref):
    return jnp.where(pl.program_id(1) == 0, ctx_ref[...], lat_ref[...])


def _stream_specs(d):
    return [pl.BlockSpec((None, TOK_TILE, d), lambda bi, ti: (bi, 0, 0)),
            pl.BlockSpec((None, TOK_TILE, d), lambda bi, ti: (bi, jnp.maximum(ti - 1, 0), 0))]


def _even_in_kernel(ctx_ref, x_ref, mod_ref, g_ref, w_ref, xa_ref, ga_ref, mla_ref):
    mod = mod_ref[...]
    h = _prenorm(_stream_tile(ctx_ref, x_ref), g_ref[...], mod[1:2], mod[0:1])
    z = _dot(h.astype(BF16), w_ref[...])
    c = xa_ref.shape[-1]
    xa_ref[...] = z[:, :c]
    ga_ref[...] = z[:, c:2 * c]
    mla_ref[...] = z[:, 2 * c:]


def _even_in(ctx, x, mods, g, w_pad, lru_w):
    b, s_len, d = x.shape
    t = ctx.shape[1] + s_len
    nt = t // TOK_TILE
    n_out = w_pad.shape[1]
    n_mla = n_out - 2 * lru_w
    row = functools.partial(_mod_row, ctx_row=b)
    tok = lambda bi, ti: (bi, ti, 0)
    return pl.pallas_call(
        _even_in_kernel,
        out_shape=(jax.ShapeDtypeStruct((b, t, lru_w), F32),
                   jax.ShapeDtypeStruct((b, t, lru_w), F32),
                   jax.ShapeDtypeStruct((b, t, n_mla), F32)),
        grid=(b, nt),
        in_specs=_stream_specs(d) + [
            pl.BlockSpec((None, 6, d), lambda bi, ti: (row(bi, ti), 0, 0)),
            pl.BlockSpec((1, d), lambda bi, ti: (0, 0)),
            pl.BlockSpec((d, n_out), lambda bi, ti: (0, 0))],
        out_specs=(pl.BlockSpec((None, TOK_TILE, lru_w), tok),
                   pl.BlockSpec((None, TOK_TILE, lru_w), tok),
                   pl.BlockSpec((None, TOK_TILE, n_mla), tok)),
        compiler_params=_cparams(("arbitrary", "arbitrary")),
        name="even_in",
    )(ctx, x, mods, g, w_pad)


def _lru_kernel(xa_ref, ga_ref, cw_ref, cb_ref, wg_ref, br_ref, bi_ref, sp_ref, o_ref, pad_ref, rec_ref, *, ctx_len):
    t, c = xa_ref.shape
    tc = LRU_CHUNK
    halo = SUBLANES
    n_chunks = t // tc
    n_ctx = ctx_len // tc
    groups = tc // SUBLANES

    pad_ref[0:halo, :] = jnp.zeros((halo, c), F32)
    pad_ref[t + halo:t + 2 * halo, :] = jnp.zeros((halo, c), F32)
    pad_ref[halo:t + halo, :] = xa_ref[...]

    rid = lax.broadcasted_iota(I32, (tc, 1), 0)
    sub = rid % SUBLANES

    for d in range(2):
        cw = cw_ref[d]
        cb = cb_ref[d]
        wg = wg_ref[d]
        b_r = br_ref[d]
        b_i = bi_ref[d]
        sp = sp_ref[d]

        def chunk(i, h, d=d, cw=cw, cb=cb, wg=wg, b_r=b_r, b_i=b_i, sp=sp):
            if d == 0:
                ci = i
            else:
                ci = jnp.where(i < n_ctx, n_ctx - 1 - i, n_chunks - 1 - (i - n_ctx))
            r0 = pl.multiple_of(ci * tc, tc)
            win = pad_ref[pl.ds(r0, tc + 2 * halo), :]
            if d == 0:
                past = jnp.where(ci == n_ctx, 0.0, win[0:halo])
                win = jnp.concatenate([past, win[halo:]], axis=0)
            else:
                past = jnp.where(ci == n_ctx - 1, 0.0, win[halo + tc:])
                win = jnp.concatenate([win[:halo + tc], past], axis=0)
            xc = jnp.zeros((tc, c), F32) + cb
            for k in range(CONV_W):
                off = (k - (CONV_W - 1)) if d == 0 else ((CONV_W - 1) - k)
                if off == 0:
                    src = win[halo:halo + tc]
                else:
                    src = pltpu.roll(win, (-off) % (tc + 2 * halo), axis=0)[halo:halo + tc]
                xc = xc + cw[k:k + 1] * src
            gz = _dot(xc.astype(BF16), wg)
            r = 0.5 * jnp.tanh(0.5 * (gz[:, :c] + b_r)) + 0.5
            gi = 0.5 * jnp.tanh(0.5 * (gz[:, c:] + b_i)) + 0.5
            log_a = (-LRU_C) * r * sp
            a = jnp.exp(log_a)
            th = jnp.tanh(log_a)
            bb = jnp.sqrt(-2.0 * th / (1.0 - th)) * (gi * xc)
            for s in (1, 2, 4):
                if d == 0:
                    ok = sub >= s
                    sh = s
                else:
                    ok = sub <= (SUBLANES - 1 - s)
                    sh = tc - s
                a_prev = jnp.where(ok, pltpu.roll(a, sh, axis=0), 1.0)
                b_prev = jnp.where(ok, pltpu.roll(bb, sh, axis=0), 0.0)
                bb = a * b_prev + bb
                a = a * a_prev
            outs = [None] * groups
            order = range(groups) if d == 0 else range(groups - 1, -1, -1)
            for g in order:
                lo = g * SUBLANES
                hg = a[lo:lo + SUBLANES] * h + bb[lo:lo + SUBLANES]
                outs[g] = hg
                h = hg[SUBLANES - 1:SUBLANES] if d == 0 else hg[0:1]
            hs = jnp.concatenate(outs, axis=0)
            if d == 0:
                rec_ref[pl.ds(r0, tc), :] = hs
            else:
                tot = rec_ref[pl.ds(r0, tc), :] + hs
                gate = jax.nn.gelu(ga_ref[pl.ds(r0, tc), :], approximate=True)
                o_ref[pl.ds(r0, tc), :] = (tot * gate).astype(o_ref.dtype)
            return h

        lax.fori_loop(0, n_chunks, chunk, jnp.zeros((1, c), F32))


def _lru(xa, ga, conv_w, conv_b, w_gates, b_r, b_i, sp, ctx_len):
    b, t, w = xa.shape
    c = 2 * LANES
    nh = w // c
    tok = lambda bi, hi: (bi, 0, hi)
    par = lambda bi, hi: (0, 0, hi)
    return pl.pallas_call(
        functools.partial(_lru_kernel, ctx_len=ctx_len),
        out_shape=jax.ShapeDtypeStruct((b, t, w), BF16),
        grid=(b, nh),
        in_specs=[pl.BlockSpec((None, t, c), tok),
                  pl.BlockSpec((None, t, c), tok),
                  pl.BlockSpec((2, CONV_W, c), par),
                  pl.BlockSpec((2, 1, c), par),
                  pl.BlockSpec((2, None, c, 2 * c), lambda bi, hi: (0, hi, 0, 0)),
                  pl.BlockSpec((2, 1, c), par),
                  pl.BlockSpec((2, 1, c), par),
                  pl.BlockSpec((2, 1, c), par)],
        out_specs=pl.BlockSpec((None, t, c), tok),
        scratch_shapes=[pltpu.VMEM((t + 2 * SUBLANES, c), F32), pltpu.VMEM((t, c), F32)],
        compiler_params=_cparams(("arbitrary", "arbitrary")),
        name="rglru",
    )(xa, ga, conv_w, conv_b, w_gates, b_r, b_i, sp)


def _mla_prep_kernel(in_ref, qan_ref, kvn_ref, wq_ref, wk_ref, wv_ref, mq_ref, mk_ref, gq_ref, gk_ref, gkr_ref,
                     one_ref, cos_ref, slo_ref, shi_ref, q_ref, k_ref, v_ref, *, q_lora, kv_lora, scale):
    z = in_ref[...]
    heads = q_ref.shape[-1] // LANES
    cos = cos_ref[...]
    slo = slo_ref[...]
    shi = shi_ref[...]
    half = MLA_ROPE // 2

    qan = (_rms(z[:, :q_lora]) * qan_ref[...]).astype(BF16)
    q = _dot(qan, wq_ref[...])
    q = q * lax.rsqrt(_head_mean_square(q, mq_ref) + EPS) * gq_ref[...]
    q = _rope(q, _tile_lanes(cos, heads), _tile_lanes(slo, heads), _tile_lanes(shi, heads), half)
    q_ref[...] = (q * scale).astype(BF16)

    kvn = (_rms(z[:, q_lora:q_lora + kv_lora]) * kvn_ref[...]).astype(BF16)
    kk = _dot(kvn, wk_ref[...])
    kk = kk * lax.rsqrt(_head_mean_square(kk, mk_ref) + EPS) * gk_ref[...]
    v_ref[...] = (_dot(kvn, wv_ref[...]) + one_ref[...]).astype(BF16)

    kr = z[:, q_lora + kv_lora:]
    kr = kr * lax.rsqrt(jnp.sum(kr * kr, axis=-1, keepdims=True) * (1.0 / MLA_ROPE) + EPS) * gkr_ref[...]
    kr = pltpu.roll(kr, MLA_NOPE, axis=1)
    kr = _rope(kr, cos, slo, shi, half)
    k_ref[...] = (kk + _tile_lanes(kr, heads)).astype(BF16)


def _mla_prep(mla_in, p, scale):
    b, t, w = mla_in.shape
    hq = MLA_HEADS * LANES
    tok = lambda bi, ti: (bi, ti, 0)
    full = lambda a: pl.BlockSpec(a.shape, lambda bi, ti: (0,) * a.ndim)
    pos = pl.BlockSpec((TOK_TILE, LANES), lambda bi, ti: (ti, 0))
    consts = [p["qan"], p["kvn"], p["wq"], p["wk"], p["wv"], p["mq"], p["mk"], p["gq"], p["gk"], p["gkr"], p["one"]]
    out = jax.ShapeDtypeStruct((b, t, hq), BF16)
    return pl.pallas_call(
        functools.partial(_mla_prep_kernel, q_lora=p["qan"].shape[1], kv_lora=p["kvn"].shape[1], scale=scale),
        out_shape=(out, out, out),
        grid=(b, t // TOK_TILE),
        in_specs=[pl.BlockSpec((None, TOK_TILE, w), tok)] + [full(a) for a in consts] + [pos, pos, pos],
        out_specs=(pl.BlockSpec((None, TOK_TILE, hq), tok),) * 3,
        compiler_params=_cparams(("arbitrary", "arbitrary")),
        name="mla_prep",
    )(mla_in, *consts, p["cos"], p["slo"], p["shi"])


def _mla_attn_kernel(q_ref, k_ref, v_ref, o_ref, *, ctx_len):
    tq = q_ref.shape[0]
    t = k_ref.shape[0]
    lane = lax.broadcasted_iota(I32, (tq, LANES), 1)

    def attend(nk):
        outs = []
        for hh in range(2):
            q = q_ref[:, hh * LANES:(hh + 1) * LANES]
            k = k_ref[0:nk, hh * LANES:(hh + 1) * LANES]
            s = _dot_nt(q, k)
            p = jnp.exp2(s - jnp.max(s, axis=-1, keepdims=True))
            o = _dot(p.astype(BF16), v_ref[0:nk, hh * LANES:(hh + 1) * LANES])
            outs.append(o / o[:, MLA_V:MLA_V + 1])
        o_ref[...] = jnp.where(lane < MLA_V, outs[0], pltpu.roll(outs[1], MLA_V, axis=1)).astype(o_ref.dtype)

    @pl.when(pl.program_id(2) == 0)
    def _():
        attend(ctx_len)

    @pl.when(pl.program_id(2) > 0)
    def _():
        attend(t)


def _mla_attn(q, k, v, ctx_len):
    b, t, hq = q.shape
    pairs = hq // (2 * LANES)
    return pl.pallas_call(
        functools.partial(_mla_attn_kernel, ctx_len=ctx_len),
        out_shape=jax.ShapeDtypeStruct((b, t, pairs * 2 * MLA_V), BF16),
        grid=(b, pairs, t // TOK_TILE),
        in_specs=[pl.BlockSpec((None, TOK_TILE, 2 * LANES), lambda bi, hi, ti: (bi, ti, hi)),
                  pl.BlockSpec((None, t, 2 * LANES), lambda bi, hi, ti: (bi, 0, hi)),
                  pl.BlockSpec((None, t, 2 * LANES), lambda bi, hi, ti: (bi, 0, hi))],
        out_specs=pl.BlockSpec((None, TOK_TILE, LANES), lambda bi, hi, ti: (bi, ti, hi)),
        compiler_params=_cparams(("arbitrary", "arbitrary", "arbitrary")),
        name="mla_attn",
    )(q, k, v)


def _odd_in_kernel(x_ref, mod_ref, g_ref, w_ref, mq_ref, mk_ref, gq_ref, gk_ref, cos_ref, slo_ref, shi_ref,
                   q_ref, k_ref, v_ref, *, scale):
    mod = mod_ref[...]
    h = _prenorm(x_ref[...], g_ref[...], mod[1:2], mod[0:1]).astype(BF16)
    nq = q_ref.shape[-1]
    nk = k_ref.shape[-1]
    half = GQA_DIM // 2

    def head_norm_rope(y, m_ref, gain_ref):
        reps = y.shape[-1] // LANES
        y = y * lax.rsqrt(_head_mean_square(y, m_ref) + EPS) * gain_ref[...]
        return _rope(y, _tile_lanes(cos_ref[...], reps), _tile_lanes(slo_ref[...], reps),
                     _tile_lanes(shi_ref[...], reps), half)

    q_ref[...] = (head_norm_rope(_dot(h, w_ref[:, 0:nq]), mq_ref, gq_ref) * scale).astype(BF16)
    k_ref[...] = head_norm_rope(_dot(h, w_ref[:, nq:nq + nk]), mk_ref, gk_ref).astype(BF16)
    v_ref[...] = _dot(h, w_ref[:, nq + nk:nq + 2 * nk]).astype(BF16)


def _odd_in(xs, mods, g, p, scale):
    b, t, d = xs.shape
    nq = p["gq"].shape[1]
    nk = p["gk"].shape[1]
    row = functools.partial(_mod_row, ctx_row=b)
    tok = lambda bi, ti: (bi, ti, 0)
    full = lambda a: pl.BlockSpec(a.shape, lambda bi, ti: (0,) * a.ndim)
    pos = pl.BlockSpec((TOK_TILE, LANES), lambda bi, ti: (ti, 0))
    kv = jax.ShapeDtypeStruct((b, t, nk), BF16)
    return pl.pallas_call(
        functools.partial(_odd_in_kernel, scale=scale),
        out_shape=(jax.ShapeDtypeStruct((b, t, nq), BF16), kv, kv),
        grid=(b, t // TOK_TILE),
        in_specs=[pl.BlockSpec((None, TOK_TILE, d), tok),
                  pl.BlockSpec((None, 6, d), lambda bi, ti: (row(bi, ti), 0, 0)),
                  pl.BlockSpec((1, d), lambda bi, ti: (0, 0)),
                  full(p["w"]), full(p["mq"]), full(p["mk"]), full(p["gq"]), full(p["gk"]), pos, pos, pos],
        out_specs=(pl.BlockSpec((None, TOK_TILE, nq), tok),
                   pl.BlockSpec((None, TOK_TILE, nk), tok),
                   pl.BlockSpec((None, TOK_TILE, nk), tok)),
        compiler_params=_cparams(("arbitrary", "arbitrary")),
        name="odd_in",
    )(xs, mods, g, p["w"], p["mq"], p["mk"], p["gq"], p["gk"], p["cos"], p["slo"], p["shi"])


def _win_attn_kernel(sink_ref, q_ref, k_ref, v_ref, o_ref, *, ctx_len):
    t = k_ref.shape[0]
    wb = WIN_BLOCK
    w3 = 3 * wb
    hd = GQA_DIM
    group = GQA_HEADS // GQA_KV_HEADS
    i = pl.program_id(1)
    q0 = ctx_len + i * wb
    ws = pl.multiple_of(jnp.clip(q0 - wb, ctx_len, t - w3), wb)
    nk = w3 + ctx_len
    rows = group * wb
    row = lax.broadcasted_iota(I32, (rows, nk), 0)
    col = lax.broadcasted_iota(I32, (rows, nk), 1)
    valid = (col >= w3) | (jnp.abs(q0 + row % wb - (ws + col)) <= WINDOW)
    head_of_row = lax.broadcasted_iota(I32, (rows, 1), 0) // wb
    q = q_ref[...].astype(F32)
    kcat = jnp.concatenate([k_ref[pl.ds(ws, w3), :], k_ref[0:ctx_len, :]], axis=0).astype(F32)
    vcat = jnp.concatenate([v_ref[pl.ds(ws, w3), :], v_ref[0:ctx_len, :]], axis=0).astype(F32)
    outs = []
    for kh in range(GQA_KV_HEADS):
        qs = jnp.concatenate([q[:, (kh * group + h) * hd:(kh * group + h + 1) * hd] for h in range(group)], axis=0)
        s = _dot_nt(qs.astype(BF16), kcat[:, kh * hd:(kh + 1) * hd].astype(BF16))
        s = jnp.where(valid, s, NEG_INF)
        sink = jnp.zeros((rows, 1), F32)
        for h in range(group):
            sink = jnp.where(head_of_row == h, sink_ref[kh * group + h], sink)
        m = jnp.maximum(jnp.max(s, axis=-1, keepdims=True), sink)
        p = jnp.exp(s - m)
        l = jnp.sum(p, axis=-1, keepdims=True) + jnp.exp(sink - m)
        o = _dot(p.astype(BF16), vcat[:, kh * hd:(kh + 1) * hd].astype(BF16)) / l
        outs.extend(o[h * wb:(h + 1) * wb] for h in range(group))
    o_ref[...] = jnp.concatenate(outs, axis=1).astype(o_ref.dtype)


def _win_attn(q, k, v, sink, ctx_len):
    b, t, n = q.shape
    nkv = k.shape[-1]
    s_len = t - ctx_len
    off = ctx_len // WIN_BLOCK
    return pl.pallas_call(
        functools.partial(_win_attn_kernel, ctx_len=ctx_len),
        out_shape=jax.ShapeDtypeStruct((b, s_len, n), BF16),
        grid=(b, s_len // WIN_BLOCK),
        in_specs=[pl.BlockSpec(memory_space=pltpu.SMEM),
                  pl.BlockSpec((None, WIN_BLOCK, n), lambda bi, ti: (bi, ti + off, 0)),
                  pl.BlockSpec((None, t, nkv), lambda bi, ti: (bi, 0, 0)),
                  pl.BlockSpec((None, t, nkv), lambda bi, ti: (bi, 0, 0))],
        out_specs=pl.BlockSpec((None, WIN_BLOCK, n), lambda bi, ti: (bi, ti, 0)),
        compiler_params=_cparams(("arbitrary", "arbitrary")),
        name="win_attn",
    )(sink, q, k, v)


def _post_mix_kernel(a1_ref, a2_ref, *refs, split_stream):
    x_in = _stream_tile(refs[0], refs[1]) if split_stream else refs[0][...]
    _post_mix_body(a1_ref, a2_ref, x_in, *refs[2 if split_stream else 1:])


def _post_mix_body(a1_ref, a2_ref, x_in, mod_ref, g_ref, w_ref, wr_ref, br_ref,
                   xo_ref, f_ref, route_ref, gate_ref, cnt_ref, seen_ref):
    half = a1_ref.shape[-1]
    mod = mod_ref[...]
    m = _dot(a1_ref[...], w_ref[0:half, :]) + _dot(a2_ref[...], w_ref[half:2 * half, :])
    x = x_in + mod[2:3] * m
    xo_ref[...] = x
    f = _prenorm(x, g_ref[...], mod[4:5], mod[3:4])
    f_ref[...] = f

    logit = _dot(f.astype(BF16), wr_ref[...].astype(BF16)) + br_ref[...]
    tm = logit.shape[0]
    lane = lax.broadcasted_iota(I32, (tm, LANES), 1)
    lane_f = lane.astype(F32)
    vals, idxs = [], []
    for _ in range(TOP_K):
        mx = jnp.max(logit, axis=-1, keepdims=True)
        ix = jnp.min(jnp.where(logit == mx, lane_f, float(LANES)), axis=-1, keepdims=True)
        vals.append(mx)
        idxs.append(ix)
        logit = jnp.where(lane_f == ix, -jnp.inf, logit)
    exps = [jnp.exp(v - vals[0]) for v in vals]
    den = exps[0]
    for e in exps[1:]:
        den = den + e

    @pl.when((pl.program_id(0) == 0) & (pl.program_id(1) == 0))
    def _():
        seen_ref[...] = jnp.zeros(seen_ref.shape, F32)

    msk = jnp.zeros((tm, LANES), F32)
    for k in range(TOP_K):
        msk = jnp.where(lane_f == idxs[k], 1.0, msk)
    earlier = (lax.broadcasted_iota(I32, (tm, tm), 1) < lax.broadcasted_iota(I32, (tm, tm), 0))
    rank = _dot(jnp.where(earlier, 1.0, 0.0).astype(BF16), msk.astype(BF16)) + seen_ref[0:1, :]
    seen = seen_ref[...] + jnp.sum(msk, axis=0, keepdims=True)
    seen_ref[...] = seen
    cnt_ref[...] = seen

    r_out = jnp.zeros((tm, LANES), F32)
    g_out = jnp.zeros((tm, LANES), F32)
    for k in range(TOP_K):
        rank_k = jnp.sum(jnp.where(lane_f == idxs[k], rank, 0.0), axis=-1, keepdims=True)
        r_out = jnp.where(lane == k, idxs[k], r_out)
        r_out = jnp.where(lane == TOP_K + k, rank_k, r_out)
        g_out = jnp.where(lane == k, exps[k] / den, g_out)
    route_ref[...] = r_out.astype(I32)
    gate_ref[...] = g_out


def _post_mix(a1, a2, lane_blk2, xs, mods, g, w_out, w_router, b_router, t_off, a_off):
    split_stream = isinstance(xs, tuple)
    if split_stream:
        assert t_off == 0
        b, s_len, d = xs[1].shape
        t = xs[0].shape[1] + s_len
        x_specs = _stream_specs(d)
    else:
        b, t, d = xs.shape
        x_specs = [pl.BlockSpec((None, TOK_TILE, d), lambda bi, ti: (bi, ti + t_off, 0))]
        xs = (xs,)
    half = w_out.shape[0] // 2
    nt = t // TOK_TILE - t_off
    t_out = nt * TOK_TILE
    row = functools.partial(_mod_row, ctx_row=b)
    tok = lambda bi, ti: (bi, ti, 0)
    act = jax.ShapeDtypeStruct((b, t_out, d), F32)
    return pl.pallas_call(
        functools.partial(_post_mix_kernel, split_stream=split_stream),
        out_shape=(act, act,
                   jax.ShapeDtypeStruct((b, t_out, LANES), I32),
                   jax.ShapeDtypeStruct((b, t_out, LANES), F32),
                   jax.ShapeDtypeStruct((SUBLANES, LANES), F32)),
        grid=(b, nt),
        in_specs=[pl.BlockSpec((None, TOK_TILE, half), lambda bi, ti: (bi, ti + a_off, 0)),
                  pl.BlockSpec((None, TOK_TILE, half), lambda bi, ti: (bi, ti + a_off, lane_blk2))] + x_specs + [
                  pl.BlockSpec((None, 6, d), lambda bi, ti: (row(bi, ti + t_off), 0, 0)),
                  pl.BlockSpec((1, d), lambda bi, ti: (0, 0)),
                  pl.BlockSpec(w_out.shape, lambda bi, ti: (0, 0)),
                  pl.BlockSpec(w_router.shape, lambda bi, ti: (0, 0)),
                  pl.BlockSpec((1, LANES), lambda bi, ti: (0, 0))],
        out_specs=(pl.BlockSpec((None, TOK_TILE, d), tok),
                   pl.BlockSpec((None, TOK_TILE, d), tok),
                   pl.BlockSpec((None, TOK_TILE, LANES), tok),
                   pl.BlockSpec((None, TOK_TILE, LANES), tok),
                   pl.BlockSpec((SUBLANES, LANES), lambda bi, ti: (0, 0))),
        scratch_shapes=[pltpu.VMEM((SUBLANES, LANES), F32)],
        compiler_params=_cparams(("arbitrary", "arbitrary")),
        name="post_mix",
    )(a1, a2, *xs, mods, g, w_out, w_router, b_router)


DMA_UNROLL = 2


def _row(ref, i):
    return ref.at[pl.ds(i, 1), :]


def _dispatch_kernel(pad_lo_ref, pad_n_ref, nu_ref, dest_ref, f_ref, xs_ref, zero_ref, sem, zsem):
    n = dest_ref.shape[0]

    blk = zero_ref.shape[0]
    n_blk = xs_ref.shape[0] // blk

    def block_copy(i):
        return pltpu.make_async_copy(zero_ref, xs_ref.at[pl.ds(pl.multiple_of(i * blk, blk), blk), :], zsem)

    def pad_rows(e, c, wait):
        lo = pad_lo_ref[e]
        head = jnp.minimum((-lo) & (SUBLANES - 1), pad_n_ref[e])

        def one(r, c2):
            copy = pltpu.make_async_copy(_row(zero_ref, 0), _row(xs_ref, lo + r), zsem)
            copy.wait() if wait else copy.start()
            return c2

        def eight(g, c2):
            r0 = pl.multiple_of(lo + head + g * SUBLANES, SUBLANES)
            copy = pltpu.make_async_copy(zero_ref.at[pl.ds(0, SUBLANES), :], xs_ref.at[pl.ds(r0, SUBLANES), :], zsem)
            copy.wait() if wait else copy.start()
            return c2

        c = lax.fori_loop(0, head, one, c)
        return lax.fori_loop(0, (pad_n_ref[e] - head) // SUBLANES, eight, c)

    @pl.when(pl.program_id(0) == 0)
    def _():
        zero_ref[...] = jnp.zeros(zero_ref.shape, F32)
        lax.fori_loop(nu_ref[0], n_blk, lambda i, c: (block_copy(i).start(), c)[1], 0)
        lax.fori_loop(0, N_EXPERTS, functools.partial(pad_rows, wait=False), 0)

    @pl.when(pl.program_id(0) == pl.num_programs(0) - 1)
    def _():
        lax.fori_loop(nu_ref[0], n_blk, lambda i, c: (block_copy(i).wait(), c)[1], 0)
        lax.fori_loop(0, N_EXPERTS, functools.partial(pad_rows, wait=True), 0)


    def issue(t, c):
        for k in range(TOP_K):
            pltpu.make_async_copy(_row(f_ref, t), _row(xs_ref, dest_ref[t * TOP_K + k]), sem).start()
        return c

    lax.fori_loop(0, n // TOP_K, issue, 0, unroll=DMA_UNROLL)
    pltpu.make_async_copy(xs_ref.at[pl.ds(0, n), :], xs_ref.at[pl.ds(0, n), :], sem).wait()


def _dispatch(f, dest, pad_lo, pad_n, n_used, rows):
    n, d = f.shape
    per = TOK_TILE * TOP_K
    return pl.pallas_call(
        _dispatch_kernel,
        out_shape=jax.ShapeDtypeStruct((rows, d), F32),
        grid_spec=pltpu.PrefetchScalarGridSpec(
            num_scalar_prefetch=3,
            grid=(n // TOK_TILE,),
            in_specs=[pl.BlockSpec((per,), lambda i, lo, cnt, nu: (i,), memory_space=pltpu.SMEM),
                      pl.BlockSpec((TOK_TILE, d), lambda i, lo, cnt, nu: (i, 0))],
            out_specs=pl.BlockSpec(memory_space=pl.ANY),
            scratch_shapes=[pltpu.VMEM((MOE_ROWS, d), F32), pltpu.SemaphoreType.DMA(()),
                            pltpu.SemaphoreType.DMA(())]),
        compiler_params=_cparams(("arbitrary",), has_side_effects=True, disable_bounds_checks=True),
        name="moe_dispatch",
    )(pad_lo, pad_n, n_used, dest, f)


def _experts_kernel(be_ref, nu_ref, grp_ref, nxt_ref, xs_ref, wgu_hbm, bgu_ref, wdn_hbm, bdn_ref, y_ref,
                    wgu_f32, wdn_f32, wgu_bf, wdn_bf, wsem, *, layer):
    i = pl.program_id(0)

    def fetch(e, slot, wait):
        for src, dst in ((wgu_hbm.at[layer, e], wgu_f32.at[slot]), (wdn_hbm.at[layer, e], wdn_f32.at[slot])):
            c = pltpu.make_async_copy(src, dst, wsem.at[slot])
            c.wait() if wait else c.start()

    @pl.when(i == 0)
    def _():
        fetch(be_ref[0], 0, False)

    @pl.when((i < nu_ref[0]) & ((i == 0) | (grp_ref[i] != grp_ref[jnp.maximum(i - 1, 0)])))
    def _():
        slot = grp_ref[i] % 2
        fetch(be_ref[i], slot, True)
        wgu_bf[...] = wgu_f32[slot].astype(BF16)
        wdn_bf[...] = wdn_f32[slot].astype(BF16)

        @pl.when(nxt_ref[i] >= 0)
        def _():
            fetch(nxt_ref[i], 1 - slot, False)

    @pl.when(i < nu_ref[0])
    def _():
        h = _dot(xs_ref[...].astype(BF16), wgu_bf[...]) + bgu_ref[...]
        ff = h.shape[1] // 2
        hg = jnp.minimum(h[:, :ff], SWIGLU_LIMIT)
        hu = jnp.clip(h[:, ff:], -SWIGLU_LIMIT, SWIGLU_LIMIT)
        act = hg * jax.nn.sigmoid(SWIGLU_ALPHA * hg) * (hu + 1.0)
        y_ref[...] = _dot(act.astype(BF16), wdn_bf[...]) + bdn_ref[...]

    @pl.when(i >= nu_ref[0])
    def _():
        y_ref[...] = jnp.zeros(y_ref.shape, F32)


def _experts(xs, block_e, n_used, layer, wgu, bgu, wdn, bdn):
    rows, d = xs.shape
    ff2 = wgu.shape[-1]
    nb = rows // MOE_ROWS
    blk = jnp.arange(nb)
    used = blk < n_used[0]
    first = used & ((blk == 0) | (block_e != jnp.roll(block_e, 1)))
    grp = jnp.cumsum(first.astype(I32)) - 1
    later_first = jnp.where(first[None, :] & (blk[None, :] > blk[:, None]), blk[None, :], nb)
    nxt_blk = jnp.min(later_first, axis=1)
    nxt = jnp.where(nxt_blk < nb, block_e[jnp.minimum(nxt_blk, nb - 1)], -1)
    bias = lambda i, be, nu, g, nx: (layer, be[i], 0, 0)
    return pl.pallas_call(
        functools.partial(_experts_kernel, layer=layer),
        out_shape=jax.ShapeDtypeStruct((rows, d), F32),
        grid_spec=pltpu.PrefetchScalarGridSpec(
            num_scalar_prefetch=4,
            grid=(nb,),
            in_specs=[pl.BlockSpec((MOE_ROWS, d),
                                   lambda i, be, nu, g, nx: (jnp.maximum(jnp.minimum(i, nu[0] - 1), 0), 0)),
                      pl.BlockSpec(memory_space=pl.ANY),
                      pl.BlockSpec((None, None, 1, ff2), bias),
                      pl.BlockSpec(memory_space=pl.ANY),
                      pl.BlockSpec((None, None, 1, d), bias)],
            out_specs=pl.BlockSpec((MOE_ROWS, d), lambda i, be, nu, g, nx: (i, 0)),
            scratch_shapes=[pltpu.VMEM((2, d, ff2), F32), pltpu.VMEM((2, ff2 // 2, d), F32),
                            pltpu.VMEM((d, ff2), BF16), pltpu.VMEM((ff2 // 2, d), BF16),
                            pltpu.SemaphoreType.DMA((2,))]),
        compiler_params=_cparams(("arbitrary",), has_side_effects=True),
        name="moe_experts",
    )(block_e, n_used, grp.astype(I32), nxt.astype(I32), xs, wgu, bgu, wdn, bdn)


IDX_BLOCK = 1024


def _fused_kernel(be_ref, nu_ref, tok0_ref, tok_ref, tgt_ref, f_ref, wgu_ref, bgu_ref, wdn_ref, bdn_ref, out_ref,
                  wgu_bf, wdn_bf, x0, x1, y0, y1, gsem, ssem):
    i = pl.program_id(0)
    nu = nu_ref[0]
    rows = MOE_ROWS
    per = IDX_BLOCK // rows
    xb = (x0, x1)
    yb = (y0, y1)

    def gather(idx_ref, base, slot):
        for j in range(rows):
            pltpu.make_async_copy(_row(f_ref, idx_ref[base + j]), _row(xb[slot], j), gsem.at[slot]).start(priority=j % 2)

    def scatter(base, slot):
        for j in range(rows):
            pltpu.make_async_copy(_row(yb[slot], j), _row(out_ref, tgt_ref[base + j]), ssem.at[slot]).start(priority=j % 2)

    def wait_rows(buf, sem):
        pltpu.make_async_copy(buf, buf, sem).wait()

    @pl.when(i == 0)
    def _():
        y1[...] = jnp.zeros(y1.shape, F32)
        n_real = out_ref.shape[0] - 2 * rows
        c = pltpu.make_async_copy(y1, out_ref.at[pl.ds(n_real, rows), :], ssem.at[0])
        c.start()
        c.wait()
        gather(tok0_ref, 0, 0)

    @pl.when((i <= nu) & ((i == 0) | (be_ref[i] != be_ref[jnp.maximum(i - 1, 0)])))
    def _():
        wgu_bf[...] = wgu_ref[...].astype(BF16)
        wdn_bf[...] = wdn_ref[...].astype(BF16)

    gbase = ((i + 1) % per) * rows
    sbase = (i % per) * rows

    for slot in range(2):
        other = 1 - slot

        @pl.when((i % 2 == slot) & (i < nu))
        def _(slot=slot, other=other):
            wait_rows(xb[slot], gsem.at[slot])

            @pl.when(i >= 1)
            def _():
                wait_rows(yb[slot], ssem.at[slot])

            gather(tok_ref, gbase, other)
            scatter(sbase, other)
            h = _dot(xb[slot][...].astype(BF16), wgu_bf[...]) + bgu_ref[...]
            ff = h.shape[1] // 2
            hg = jnp.minimum(h[:, :ff], SWIGLU_LIMIT)
            hu = jnp.clip(h[:, ff:], -SWIGLU_LIMIT, SWIGLU_LIMIT)
            act = hg * jax.nn.sigmoid(SWIGLU_ALPHA * hg) * (hu + 1.0)
            yb[slot][...] = _dot(act.astype(BF16), wdn_bf[...]) + bdn_ref[...]

        @pl.when((i % 2 == slot) & (i == nu))
        def _(slot=slot, other=other):
            wait_rows(xb[slot], gsem.at[slot])

            @pl.when(i >= 1)
            def _():
                wait_rows(yb[slot], ssem.at[slot])

            scatter(sbase, other)
            wait_rows(yb[other], ssem.at[other])


def _experts_fused(f, row_tok, row_tgt, block_e, n_used, layer, wgu, bgu, wdn, bdn):
    n, d = f.shape
    ff2 = wgu.shape[-1]
    nb = row_tok.shape[0] // MOE_ROWS
    per = IDX_BLOCK // MOE_ROWS
    last = row_tok.shape[0] // IDX_BLOCK - 1
    last_tgt = row_tgt.shape[0] // IDX_BLOCK - 1
    buf = pltpu.VMEM((MOE_ROWS, d), F32)
    w_idx = lambda i, be, nu: (layer, be[i], 0, 0)
    return pl.pallas_call(
        _fused_kernel,
        out_shape=jax.ShapeDtypeStruct((TOP_K * n + 2 * MOE_ROWS, d), F32),
        grid_spec=pltpu.PrefetchScalarGridSpec(
            num_scalar_prefetch=2,
            grid=(nb + 1,),
            in_specs=[pl.BlockSpec((IDX_BLOCK,), lambda i, be, nu: (0,), memory_space=pltpu.SMEM),
                      pl.BlockSpec((IDX_BLOCK,), lambda i, be, nu: (jnp.minimum((i + 1) // per, last),),
                                   memory_space=pltpu.SMEM),
                      pl.BlockSpec((IDX_BLOCK,), lambda i, be, nu: (jnp.minimum(i // per, last_tgt),),
                                   memory_space=pltpu.SMEM),
                      pl.BlockSpec(memory_space=pl.ANY),
                      pl.BlockSpec((None, None, d, ff2), w_idx),
                      pl.BlockSpec((None, None, 1, ff2), w_idx),
                      pl.BlockSpec((None, None, ff2 // 2, d), w_idx),
                      pl.BlockSpec((None, None, 1, d), w_idx)],
            out_specs=pl.BlockSpec(memory_space=pl.ANY),
            scratch_shapes=[pltpu.VMEM((d, ff2), BF16), pltpu.VMEM((ff2 // 2, d), BF16), buf, buf, buf, buf,
                            pltpu.SemaphoreType.DMA((2,)), pltpu.SemaphoreType.DMA((2,))]),
        compiler_params=_cparams(("arbitrary",), has_side_effects=True, disable_bounds_checks=True),
        name="moe_fused",
    )(block_e, n_used, row_tok, row_tok, row_tgt, f, wgu, bgu, wdn, bdn)


def _sum_choices_kernel(p0_ref, p1_ref, p2_ref, p3_ref, gate_ref, x_ref, mod_ref, o_ref):
    gates = gate_ref[...]
    acc = gates[:, 0:1] * p0_ref[...]
    for k, p_ref in enumerate((p1_ref, p2_ref, p3_ref), start=1):
        acc = acc + gates[:, k:k + 1] * p_ref[...]
    o_ref[...] = x_ref[...] + mod_ref[5:6, :] * acc


def _sum_choices(y4, gates, xs, mods, t_off):
    b, t_out, d = xs.shape
    nt = t_out // TOK_TILE
    row = functools.partial(_mod_row, ctx_row=b)
    tok = lambda bi, ti: (bi, ti, 0)
    plane = lambda k: pl.BlockSpec((TOK_TILE, d), lambda bi, ti: (k * b * nt + bi * nt + ti, 0))
    return pl.pallas_call(
        _sum_choices_kernel,
        out_shape=jax.ShapeDtypeStruct((b, t_out, d), F32),
        grid=(b, nt),
        in_specs=[plane(0), plane(1), plane(2), plane(3),
                  pl.BlockSpec((None, TOK_TILE, LANES), tok),
                  pl.BlockSpec((None, TOK_TILE, d), tok),
                  pl.BlockSpec((None, 6, d), lambda bi, ti: (row(bi, ti + t_off), 0, 0))],
        out_specs=pl.BlockSpec((None, TOK_TILE, d), tok),
        compiler_params=_cparams(("arbitrary", "arbitrary")),
        name="moe_sum",
    )(y4, y4, y4, y4, gates, xs, mods)


def _combine_kernel(dest_ref, y_ref, gate_ref, x_ref, mod_ref, o_ref, buf_ref, sem):
    n = dest_ref.shape[0]
    tm = n // TOP_K

    def issue(t, c):
        for k in range(TOP_K):
            pltpu.make_async_copy(_row(y_ref, dest_ref[t * TOP_K + k]), _row(buf_ref.at[k], t), sem).start()
        return c

    lax.fori_loop(0, tm, issue, 0, unroll=DMA_UNROLL)
    for k in range(TOP_K):
        pltpu.make_async_copy(y_ref.at[pl.ds(0, tm), :], buf_ref.at[k], sem).wait()

    gates = gate_ref[...]
    acc = gates[:, 0:1] * buf_ref[0]
    for k in range(1, TOP_K):
        acc = acc + gates[:, k:k + 1] * buf_ref[k]
    o_ref[...] = x_ref[...] + mod_ref[5:6, :] * acc


def _combine(y, dest, gates, xs, mods, t_off):
    b, t_out, d = xs.shape
    nt = t_out // TOK_TILE
    per = TOK_TILE * TOP_K
    row = functools.partial(_mod_row, ctx_row=b)
    tok = lambda bi, ti: (bi, ti, 0)
    return pl.pallas_call(
        _combine_kernel,
        out_shape=jax.ShapeDtypeStruct((b, t_out, d), F32),
        grid=(b, nt),
        in_specs=[pl.BlockSpec((per,), lambda bi, ti: (bi * nt + ti,), memory_space=pltpu.SMEM),
                  pl.BlockSpec(memory_space=pl.ANY),
                  pl.BlockSpec((None, TOK_TILE, LANES), tok),
                  pl.BlockSpec((None, TOK_TILE, d), tok),
                  pl.BlockSpec((None, 6, d), lambda bi, ti: (row(bi, ti + t_off), 0, 0))],
        out_specs=pl.BlockSpec((None, TOK_TILE, d), tok),
        scratch_shapes=[pltpu.VMEM((TOP_K, TOK_TILE, d), F32), pltpu.SemaphoreType.DMA(())],
        compiler_params=_cparams(("arbitrary", "arbitrary"), disable_bounds_checks=True),
        name="moe_combine",
    )(dest, y, gates, xs, mods)


def _routing(e_sel, rank, counts, n):
    padded = (counts + MOE_ROWS - 1) // MOE_ROWS * MOE_ROWS
    pend = jnp.cumsum(padded)
    pstart = pend - padded
    dest = jnp.sum(jnp.where(e_sel[..., None] == jnp.arange(N_EXPERTS), pstart, 0), axis=-1) + rank
    n_blocks = n * TOP_K // MOE_ROWS + N_EXPERTS
    first_row = jnp.arange(n_blocks) * MOE_ROWS
    block_e = jnp.minimum(jnp.sum(pend[None, :] <= first_row[:, None], axis=1), N_EXPERTS - 1)
    n_used = (pend[-1] // MOE_ROWS).reshape(1)
    pads = ((pstart + counts).astype(I32), (padded - counts).astype(I32))
    return dest.reshape(-1).astype(I32), block_e.astype(I32), n_used.astype(I32), pads, n_blocks * MOE_ROWS


def _moe_fused(f, route, gates, seen, x_mid, mods, t_off, layer, wgu, bgu, wdn, bdn):
    b, t_out, d = f.shape
    n = b * t_out
    route = route.reshape(n, LANES)
    counts = seen[0, :N_EXPERTS].astype(I32)
    dest, block_e, n_used, _, rows = _routing(route[:, :TOP_K], route[:, TOP_K:2 * TOP_K], counts, n)
    assert rows % IDX_BLOCK == 0
    choice = jnp.arange(n * TOP_K, dtype=I32)
    tok, k = choice // TOP_K, choice % TOP_K
    r = jnp.arange(rows, dtype=I32)
    scratch = TOP_K * n + ((r // MOE_ROWS) % 2) * MOE_ROWS + r % MOE_ROWS
    row_tok = jnp.zeros((rows,), I32).at[dest].set(tok)
    row_tgt = scratch.at[dest].set(k * n + tok)
    lead = TOP_K * n + MOE_ROWS + jnp.arange(MOE_ROWS, dtype=I32)
    tail = jnp.full((IDX_BLOCK - MOE_ROWS,), TOP_K * n, I32)
    row_tgt = jnp.concatenate([lead, row_tgt, tail])
    block_e = jnp.concatenate([block_e, block_e[-1:]])
    y4 = _experts_fused(f.reshape(n, d), row_tok, row_tgt, block_e, n_used, layer, wgu, bgu, wdn, bdn)
    return _sum_choices(y4, gates, x_mid, mods, t_off)


def _moe(f, route, gates, seen, x_mid, mods, t_off, layer, wgu, bgu, wdn, bdn):
    b, t_out, d = f.shape
    n = b * t_out
    route = route.reshape(n, LANES)
    counts = seen[0, :N_EXPERTS].astype(I32)
    dest, block_e, n_used, pads, rows = _routing(route[:, :TOP_K], route[:, TOP_K:2 * TOP_K], counts, n)
    xs = _dispatch(f.reshape(n, d), dest, *pads, n_used, rows)
    y = _experts(xs, block_e, n_used, layer, wgu, bgu, wdn, bdn)
    return _combine(y, dest, gates, x_mid, mods, t_off)


def _axial_angles(n_rows, rot_dim):
    row = np.repeat(np.arange(n_rows, dtype=np.float32), GRID_W)
    col = np.tile(np.arange(GRID_W, dtype=np.float32), n_rows)
    n = rot_dim // 4
    freqs = (np.float32(ROPE_THETA) ** (-np.arange(n, dtype=np.float32) / np.float32(n))).astype(np.float32)
    return np.concatenate([row[:, None] * freqs, col[:, None] * freqs], axis=-1).astype(np.float32)


def _rope_tables(ctx_len, s_len, rot_dim, lane_base, reps):
    ang = _axial_angles(s_len // GRID_W, rot_dim)
    half = rot_dim // 2
    period = LANES // reps
    cos = np.ones((ctx_len + s_len, period), np.float32)
    slo = np.zeros((ctx_len + s_len, period), np.float32)
    shi = np.zeros((ctx_len + s_len, period), np.float32)
    cos[ctx_len:, lane_base:lane_base + rot_dim] = np.tile(np.cos(ang), (1, 2))
    slo[ctx_len:, lane_base:lane_base + half] = -np.sin(ang)
    shi[ctx_len:, lane_base + half:lane_base + rot_dim] = np.sin(ang)
    return [jnp.asarray(np.tile(tb, (1, reps))) for tb in (cos, slo, shi)]


def _segment_mean_matrix(width, period, segs):
    lane = np.arange(width)
    seg_id = np.full((width,), -1)
    seg_w = np.zeros((width,), np.float32)
    for i, (start, length) in enumerate(segs):
        inside = ((lane % period) >= start) & ((lane % period) < start + length)
        seg_id = np.where(inside, (lane // period) * len(segs) + i, seg_id)
        seg_w = np.where(inside, np.float32(1.0 / length), seg_w)
    same = (seg_id[:, None] == seg_id[None, :]) & (seg_id[:, None] >= 0)
    return jnp.asarray(np.where(same, seg_w[None, :], np.float32(0.0)), dtype=BF16)


def _block_diag(w):
    n, c, _ = w.shape
    eye = jnp.eye(n, dtype=w.dtype)
    return (eye[:, None, :, None] * w[:, :, None, :]).reshape(n * c, n * c)


def kernel(x, c, ctx, c_ctx, w_mod, b_mod, norm_mix, norm_ffn, w_in_even, lru_conv_w, lru_conv_b, lru_w_r, lru_b_r, lru_w_i, lru_b_i, lru_lambda, mla_q_a_norm, mla_w_q_b, mla_kv_a_norm, mla_w_kv_b, mla_nope_norm, mla_rope_norm, w_out_even, w_qkv_odd, gqa_qk_norm, gqa_sink, w_out_odd, w_router, b_router, w_gate_up, b_gate_up, w_down, b_down):
    b, s_len, d = x.shape
    ctx_len = ctx.shape[1]
    depth = w_mod.shape[0]
    assert depth == 2 and ctx_len == TOK_TILE and s_len % TOK_TILE == 0 and b + 1 <= SUBLANES
    lru_w = lru_conv_w.shape[-1]
    q_lora = mla_q_a_norm.shape[-1]
    kv_lora = mla_kv_a_norm.shape[-1]

    cvec = jnp.concatenate([c, c_ctx[None], jnp.zeros((SUBLANES - b - 1, d), F32)], axis=0)
    mods = _modulation(cvec, w_mod, b_mod).reshape(depth, SUBLANES, 6, d)

    wr_pad = jnp.zeros((depth, d, LANES), F32).at[:, :, :N_EXPERTS].set(w_router)
    br_pad = jnp.full((depth, 1, LANES), NEG_INF, F32).at[:, 0, :N_EXPERTS].set(b_router)
    experts = (w_gate_up, b_gate_up[:, :, None, :], w_down, b_down[:, :, None, :])

    n_in = w_in_even.shape[-1]
    n_in_pad = -(-n_in // LANES) * LANES
    w_in = jnp.zeros((d, n_in_pad), F32).at[:, :n_in].set(w_in_even[0]).astype(BF16)
    xa, ga, mla_in = _even_in(ctx, x, mods[0], norm_mix[0][None], w_in, lru_w)

    nh = lru_w // (2 * LANES)
    per = LRU_BLOCKS // nh
    blk = lru_w // LRU_BLOCKS
    w_gates = jnp.stack([
        jnp.stack([jnp.concatenate([_block_diag(lru_w_r[0, dd, h * per:(h + 1) * per]),
                                    _block_diag(lru_w_i[0, dd, h * per:(h + 1) * per])], axis=1)
                   for h in range(nh)]) for dd in range(2)]).astype(BF16)
    assert blk * per == 2 * LANES
    ya = _lru(xa, ga, lru_conv_w[0], lru_conv_b[0][:, None, :], w_gates, lru_b_r[0][:, None, :],
              lru_b_i[0][:, None, :], jax.nn.softplus(-lru_lambda[0])[:, None, :], ctx_len)

    qk = MLA_NOPE + MLA_ROPE
    hq = MLA_HEADS * LANES
    wq = jnp.zeros((q_lora, MLA_HEADS, LANES), F32).at[:, :, :qk].set(
        mla_w_q_b[0].reshape(q_lora, MLA_HEADS, qk)).reshape(q_lora, hq).astype(BF16)
    wkv = mla_w_kv_b[0].reshape(kv_lora, MLA_HEADS, MLA_NOPE + MLA_V)
    wk = jnp.zeros((kv_lora, MLA_HEADS, LANES), F32).at[:, :, :MLA_NOPE].set(
        wkv[:, :, :MLA_NOPE]).reshape(kv_lora, hq).astype(BF16)
    wv = jnp.zeros((kv_lora, MLA_HEADS, LANES), F32).at[:, :, :MLA_V].set(
        wkv[:, :, MLA_NOPE:]).reshape(kv_lora, hq).astype(BF16)
    one = jnp.asarray(np.tile(np.arange(LANES) == MLA_V, MLA_HEADS)[None], dtype=F32)
    zpad = jnp.zeros((LANES - qk,), F32)
    gq = jnp.tile(jnp.concatenate([mla_nope_norm[0, 0], mla_rope_norm[0, 0], zpad]), MLA_HEADS)[None]
    gk = jnp.tile(jnp.concatenate([mla_nope_norm[0, 1], jnp.zeros((LANES - MLA_NOPE,), F32)]), MLA_HEADS)[None]
    gkr = jnp.concatenate([mla_rope_norm[0, 1], jnp.zeros((LANES - MLA_ROPE,), F32)])[None]
    cos, slo, shi = _rope_tables(ctx_len, s_len, MLA_ROPE, MLA_NOPE, 1)
    mla_p = dict(qan=mla_q_a_norm[0][None], kvn=mla_kv_a_norm[0][None], wq=wq, wk=wk, wv=wv,
                 mq=_segment_mean_matrix(MXU_TILE,LANES, [(0, MLA_NOPE), (MLA_NOPE, MLA_ROPE)]),
                 mk=_segment_mean_matrix(MXU_TILE,LANES, [(0, MLA_NOPE)]),
                 gq=gq, gk=gk, gkr=gkr, one=one, cos=cos, slo=slo, shi=shi)
    q, k, v = _mla_prep(mla_in, mla_p, qk ** -0.5 * math.log2(math.e))
    yb = _mla_attn(q, k, v, ctx_len)

    x_mid, f, route, gates, seen = _post_mix(ya, yb, 0, (ctx, x), mods[0], norm_ffn[0][None], w_out_even[0].astype(BF16),
                                             wr_pad[0], br_pad[0], 0, 0)
    xs = _moe(f, route, gates, seen, x_mid, mods[0], 0, 0, *experts)

    nq = GQA_HEADS * GQA_DIM
    nkv = GQA_KV_HEADS * GQA_DIM
    cos, slo, shi = _rope_tables(ctx_len, s_len, GQA_DIM, 0, LANES // GQA_DIM)
    odd_p = dict(w=w_qkv_odd[0].astype(BF16),
                 mq=_segment_mean_matrix(MXU_TILE,GQA_DIM, [(0, GQA_DIM)]),
                 mk=_segment_mean_matrix(MXU_TILE,GQA_DIM, [(0, GQA_DIM)]),
                 gq=jnp.tile(gqa_qk_norm[0, 0], GQA_HEADS)[None], gk=jnp.tile(gqa_qk_norm[0, 1], GQA_KV_HEADS)[None],
                 cos=cos, slo=slo, shi=shi)
    q, k, v = _odd_in(xs, mods[1], norm_mix[1][None], odd_p, GQA_DIM ** -0.5)
    o = _win_attn(q, k, v, gqa_sink[0], ctx_len)

    t_off = ctx_len // TOK_TILE
    x_mid, f, route, gates, seen = _post_mix(o, o, 1, xs, mods[1], norm_ffn[1][None], w_out_odd[0].astype(BF16),
                                             wr_pad[1], br_pad[1], t_off, 0)
    return _moe(f, route, gates, seen, x_mid, mods[1], t_off, 1, *experts)
```

```python
import functools
import math

import jax
import jax.numpy as jnp
import numpy as np
from jax import lax
from jax.experimental import pallas as pl
from jax.experimental.pallas import tpu as pltpu

F32 = jnp.float32
BF16 = jnp.bfloat16
I32 = jnp.int32

GRID_W = 64
LRU_BLOCKS = 8
LRU_C = 8.0
CONV_W = 4
MLA_HEADS = 8
MLA_NOPE = 64
MLA_ROPE = 32
MLA_V = 64
GQA_HEADS = 16
GQA_KV_HEADS = 4
GQA_DIM = 64
WINDOW = 128
ROPE_THETA = 10000.0
NEG_INF = -1e30
EPS = 1e-6
N_EXPERTS = 32
TOP_K = 4
SWIGLU_LIMIT = 7.0
SWIGLU_ALPHA = 1.702

LANES = 128
SUBLANES = 8
TOK_TILE = 256
LRU_CHUNK = 128
WIN_BLOCK = 128
MOE_ROWS = 256
MXU_TILE = 256
VMEM_LIMIT = 48 * 1024 * 1024


def _cparams(sem, **kw):
    return pltpu.CompilerParams(dimension_semantics=sem, vmem_limit_bytes=VMEM_LIMIT, **kw)


def _dot(a, b):
    return jnp.dot(a, b, preferred_element_type=F32)


def _dot_nt(a, b):
    return lax.dot_general(a, b, (((1,), (1,)), ((), ())), preferred_element_type=F32)


def _split_bf16(x):
    hi = x.astype(BF16)
    lo = (x - hi.astype(F32)).astype(BF16)
    return hi, lo


def _dot3(a, w):
    ah, al = _split_bf16(a)
    wh, wl = _split_bf16(w)
    return _dot(ah, wh) + _dot(al, wh) + _dot(ah, wl)


def _rms(x):
    return x * lax.rsqrt(jnp.mean(x * x, axis=-1, keepdims=True) + EPS)


def _prenorm(x, g, scale, shift):
    return (_rms(x) * g) * (1.0 + scale) + shift


def _rope(x, cos, sin_lo, sin_hi, half):
    w = x.shape[-1]
    return x * cos + pltpu.roll(x, w - half, axis=1) * sin_lo + pltpu.roll(x, half, axis=1) * sin_hi


U32 = jnp.uint32
HI_HALF = 0xFFFF0000


def _pack_bf16_pairs(x):
    w = x.shape[1] // 2
    lo = pltpu.bitcast(x[:, :w].astype(BF16).astype(F32), U32) >> 16
    hi = pltpu.bitcast(x[:, w:].astype(BF16).astype(F32), U32) & U32(HI_HALF)
    return lo | hi


def _unpack_bf16_pairs(words):
    lo = pltpu.bitcast(words << 16, F32)
    hi = pltpu.bitcast(words & U32(HI_HALF), F32)
    return jnp.concatenate([lo, hi], axis=1)


def _head_mean_square(y, m_ref):
    w = m_ref.shape[0]
    return jnp.concatenate([_dot((y[:, j:j + w] * y[:, j:j + w]).astype(BF16), m_ref[...])
                            for j in range(0, y.shape[1], w)], axis=1)


def _tile_lanes(t, reps):
    return jnp.concatenate([t] * reps, axis=1) if reps > 1 else t


def _mod_kernel(c_ref, w_ref, b_ref, o_ref):
    c = c_ref[...]
    o_ref[...] = _dot3(c * jax.nn.sigmoid(c), w_ref[...]) + b_ref[...]


def _modulation(cvec, w_mod, b_mod):
    depth, d, n = w_mod.shape
    tn = 1536
    return pl.pallas_call(
        _mod_kernel,
        out_shape=jax.ShapeDtypeStruct((depth, SUBLANES, n), F32),
        grid=(depth, n // tn),
        in_specs=[pl.BlockSpec((SUBLANES, d), lambda l, j: (0, 0)),
                  pl.BlockSpec((None, d, tn), lambda l, j: (l, 0, j)),
                  pl.BlockSpec((None, 1, tn), lambda l, j: (l, 0, j))],
        out_specs=pl.BlockSpec((None, SUBLANES, tn), lambda l, j: (l, 0, j)),
        compiler_params=_cparams(("arbitrary", "arbitrary")),
        name="modulation",
    )(cvec, w_mod, b_mod.reshape(depth, 1, n))


def _mod_row(b, t, ctx_row):
    return jnp.where(t == 0, ctx_row, b)


def _stream_tile(ctx_ref, lat_ref):
    return jnp.where(pl.program_id(1) == 0, ctx_ref[...], lat_ref[...])


def _stream_specs(d):
    return [pl.BlockSpec((None, TOK_TILE, d), lambda bi, ti: (bi, 0, 0)),
            pl.BlockSpec((None, TOK_TILE, d), lambda bi, ti: (bi, jnp.maximum(ti - 1, 0), 0))]


def _even_in_kernel(ctx_ref, x_ref, mod_ref, g_ref, w_ref, xa_ref, ga_ref, mla_ref):
    mod = mod_ref[...]
    h = _prenorm(_stream_tile(ctx_ref, x_ref), g_ref[...], mod[1:2], mod[0:1])
    z = _dot(h.astype(BF16), w_ref[...])
    c = xa_ref.shape[-1]
    xa_ref[...] = z[:, :c]
    ga_ref[...] = z[:, c:2 * c]
    mla_ref[...] = z[:, 2 * c:]


def _even_in(ctx, x, mods, g, w_pad, lru_w):
    b, s_len, d = x.shape
    t = ctx.shape[1] + s_len
    nt = t // TOK_TILE
    n_out = w_pad.shape[1]
    n_mla = n_out - 2 * lru_w
    row = functools.partial(_mod_row, ctx_row=b)
    tok = lambda bi, ti: (bi, ti, 0)
    return pl.pallas_call(
        _even_in_kernel,
        out_shape=(jax.ShapeDtypeStruct((b, t, lru_w), F32),
                   jax.ShapeDtypeStruct((b, t, lru_w), F32),
                   jax.ShapeDtypeStruct((b, t, n_mla), F32)),
        grid=(b, nt),
        in_specs=_stream_specs(d) + [
            pl.BlockSpec((None, 6, d), lambda bi, ti: (row(bi, ti), 0, 0)),
            pl.BlockSpec((1, d), lambda bi, ti: (0, 0)),
            pl.BlockSpec((d, n_out), lambda bi, ti: (0, 0))],
        out_specs=(pl.BlockSpec((None, TOK_TILE, lru_w), tok),
                   pl.BlockSpec((None, TOK_TILE, lru_w), tok),
                   pl.BlockSpec((None, TOK_TILE, n_mla), tok)),
        compiler_params=_cparams(("arbitrary", "arbitrary")),
        name="even_in",
    )(ctx, x, mods, g, w_pad)


def _lru_kernel(xa_ref, ga_ref, cw_ref, cb_ref, wg_ref, br_ref, bi_ref, sp_ref, o_ref, pad_ref, rec_ref, *, ctx_len):
    t, c = xa_ref.shape
    tc = LRU_CHUNK
    halo = SUBLANES
    n_chunks = t // tc
    n_ctx = ctx_len // tc
    groups = tc // SUBLANES

    pad_ref[0:halo, :] = jnp.zeros((halo, c), F32)
    pad_ref[t + halo:t + 2 * halo, :] = jnp.zeros((halo, c), F32)
    pad_ref[halo:t + halo, :] = xa_ref[...]

    rid = lax.broadcasted_iota(I32, (tc, 1), 0)
    sub = rid % SUBLANES

    for d in range(2):
        cw = cw_ref[d]
        cb = cb_ref[d]
        wg = wg_ref[d]
        b_r = br_ref[d]
        b_i = bi_ref[d]
        sp = sp_ref[d]

        def chunk(i, h, d=d, cw=cw, cb=cb, wg=wg, b_r=b_r, b_i=b_i, sp=sp):
            if d == 0:
                ci = i
            else:
                ci = jnp.where(i < n_ctx, n_ctx - 1 - i, n_chunks - 1 - (i - n_ctx))
            r0 = pl.multiple_of(ci * tc, tc)
            win = pad_ref[pl.ds(r0, tc + 2 * halo), :]
            if d == 0:
                past = jnp.where(ci == n_ctx, 0.0, win[0:halo])
                win = jnp.concatenate([past, win[halo:]], axis=0)
            else:
                past = jnp.where(ci == n_ctx - 1, 0.0, win[halo + tc:])
                win = jnp.concatenate([win[:halo + tc], past], axis=0)
            xc = jnp.zeros((tc, c), F32) + cb
            for k in range(CONV_W):
                off = (k - (CONV_W - 1)) if d == 0 else ((CONV_W - 1) - k)
                if off == 0:
                    src = win[halo:halo + tc]
                else:
                    src = pltpu.roll(win, (-off) % (tc + 2 * halo), axis=0)[halo:halo + tc]
                xc = xc + cw[k:k + 1] * src
            gz = _dot(xc.astype(BF16), wg)
            r = 0.5 * jnp.tanh(0.5 * (gz[:, :c] + b_r)) + 0.5
            gi = 0.5 * jnp.tanh(0.5 * (gz[:, c:] + b_i)) + 0.5
            log_a = (-LRU_C) * r * sp
            a = jnp.exp(log_a)
            th = jnp.tanh(log_a)
            bb = jnp.sqrt(-2.0 * th / (1.0 - th)) * (gi * xc)
            for s in (1, 2, 4):
                if d == 0:
                    ok = sub >= s
                    sh = s
                else:
                    ok = sub <= (SUBLANES - 1 - s)
                    sh = tc - s
                a_prev = jnp.where(ok, pltpu.roll(a, sh, axis=0), 1.0)
                b_prev = jnp.where(ok, pltpu.roll(bb, sh, axis=0), 0.0)
                bb = a * b_prev + bb
                a = a * a_prev
            outs = [None] * groups
            order = range(groups) if d == 0 else range(groups - 1, -1, -1)
            for g in order:
                lo = g * SUBLANES
                hg = a[lo:lo + SUBLANES] * h + bb[lo:lo + SUBLANES]
                outs[g] = hg
                h = hg[SUBLANES - 1:SUBLANES] if d == 0 else hg[0:1]
            hs = jnp.concatenate(outs, axis=0)
            if d == 0:
                rec_ref[pl.ds(r0, tc), :] = hs
            else:
                tot = rec_ref[pl.ds(r0, tc), :] + hs
                gate = jax.nn.gelu(ga_ref[pl.ds(r0, tc), :], approximate=True)
                o_ref[pl.ds(r0, tc), :] = (tot * gate).astype(o_ref.dtype)
            return h

        lax.fori_loop(0, n_chunks, chunk, jnp.zeros((1, c), F32))


def _lru(xa, ga, conv_w, conv_b, w_gates, b_r, b_i, sp, ctx_len):
    b, t, w = xa.shape
    c = 2 * LANES
    nh = w // c
    tok = lambda bi, hi: (bi, 0, hi)
    par = lambda bi, hi: (0, 0, hi)
    return pl.pallas_call(
        functools.partial(_lru_kernel, ctx_len=ctx_len),
        out_shape=jax.ShapeDtypeStruct((b, t, w), BF16),
        grid=(b, nh),
        in_specs=[pl.BlockSpec((None, t, c), tok),
                  pl.BlockSpec((None, t, c), tok),
                  pl.BlockSpec((2, CONV_W, c), par),
                  pl.BlockSpec((2, 1, c), par),
                  pl.BlockSpec((2, None, c, 2 * c), lambda bi, hi: (0, hi, 0, 0)),
                  pl.BlockSpec((2, 1, c), par),
                  pl.BlockSpec((2, 1, c), par),
                  pl.BlockSpec((2, 1, c), par)],
        out_specs=pl.BlockSpec((None, t, c), tok),
        scratch_shapes=[pltpu.VMEM((t + 2 * SUBLANES, c), F32), pltpu.VMEM((t, c), F32)],
        compiler_params=_cparams(("arbitrary", "arbitrary")),
        name="rglru",
    )(xa, ga, conv_w, conv_b, w_gates, b_r, b_i, sp)


def _mla_prep_kernel(in_ref, qan_ref, kvn_ref, wq_ref, wk_ref, wv_ref, mq_ref, mk_ref, gq_ref, gk_ref, gkr_ref,
                     one_ref, cos_ref, slo_ref, shi_ref, q_ref, k_ref, v_ref, *, q_lora, kv_lora, scale):
    z = in_ref[...]
    heads = q_ref.shape[-1] // LANES
    cos = cos_ref[...]
    slo = slo_ref[...]
    shi = shi_ref[...]
    half = MLA_ROPE // 2

    qan = (_rms(z[:, :q_lora]) * qan_ref[...]).astype(BF16)
    q = _dot(qan, wq_ref[...])
    q = q * lax.rsqrt(_head_mean_square(q, mq_ref) + EPS) * gq_ref[...]
    q = _rope(q, _tile_lanes(cos, heads), _tile_lanes(slo, heads), _tile_lanes(shi, heads), half)
    q_ref[...] = (q * scale).astype(BF16)

    kvn = (_rms(z[:, q_lora:q_lora + kv_lora]) * kvn_ref[...]).astype(BF16)
    kk = _dot(kvn, wk_ref[...])
    kk = kk * lax.rsqrt(_head_mean_square(kk, mk_ref) + EPS) * gk_ref[...]
    v_ref[...] = (_dot(kvn, wv_ref[...]) + one_ref[...]).astype(BF16)

    kr = z[:, q_lora + kv_lora:]
    kr = kr * lax.rsqrt(jnp.sum(kr * kr, axis=-1, keepdims=True) * (1.0 / MLA_ROPE) + EPS) * gkr_ref[...]
    kr = pltpu.roll(kr, MLA_NOPE, axis=1)
    kr = _rope(kr, cos, slo, shi, half)
    k_ref[...] = (kk + _tile_lanes(kr, heads)).astype(BF16)


def _mla_prep(mla_in, p, scale):
    b, t, w = mla_in.shape
    hq = MLA_HEADS * LANES
    tok = lambda bi, ti: (bi, ti, 0)
    full = lambda a: pl.BlockSpec(a.shape, lambda bi, ti: (0,) * a.ndim)
    pos = pl.BlockSpec((TOK_TILE, LANES), lambda bi, ti: (ti, 0))
    consts = [p["qan"], p["kvn"], p["wq"], p["wk"], p["wv"], p["mq"], p["mk"], p["gq"], p["gk"], p["gkr"], p["one"]]
    out = jax.ShapeDtypeStruct((b, t, hq), BF16)
    return pl.pallas_call(
        functools.partial(_mla_prep_kernel, q_lora=p["qan"].shape[1], kv_lora=p["kvn"].shape[1], scale=scale),
        out_shape=(out, out, out),
        grid=(b, t // TOK_TILE),
        in_specs=[pl.BlockSpec((None, TOK_TILE, w), tok)] + [full(a) for a in consts] + [pos, pos, pos],
        out_specs=(pl.BlockSpec((None, TOK_TILE, hq), tok),) * 3,
        compiler_params=_cparams(("arbitrary", "arbitrary")),
        name="mla_prep",
    )(mla_in, *consts, p["cos"], p["slo"], p["shi"])


def _mla_attn_kernel(q_ref, k_ref, v_ref, o_ref, *, ctx_len):
    tq = q_ref.shape[0]
    t = k_ref.shape[0]
    lane = lax.broadcasted_iota(I32, (tq, LANES), 1)

    def attend(nk):
        outs = []
        for hh in range(2):
            q = q_ref[:, hh * LANES:(hh + 1) * LANES]
            k = k_ref[0:nk, hh * LANES:(hh + 1) * LANES]
            s = _dot_nt(q, k)
            p = jnp.exp2(s - jnp.max(s, axis=-1, keepdims=True))
            o = _dot(p.astype(BF16), v_ref[0:nk, hh * LANES:(hh + 1) * LANES])
            outs.append(o / o[:, MLA_V:MLA_V + 1])
        o_ref[...] = jnp.where(lane < MLA_V, outs[0], pltpu.roll(outs[1], MLA_V, axis=1)).astype(o_ref.dtype)

    @pl.when(pl.program_id(2) == 0)
    def _():
        attend(ctx_len)

    @pl.when(pl.program_id(2) > 0)
    def _():
        attend(t)


def _mla_attn(q, k, v, ctx_len):
    b, t, hq = q.shape
    pairs = hq // (2 * LANES)
    return pl.pallas_call(
        functools.partial(_mla_attn_kernel, ctx_len=ctx_len),
        out_shape=jax.ShapeDtypeStruct((b, t, pairs * 2 * MLA_V), BF16),
        grid=(b, pairs, t // TOK_TILE),
        in_specs=[pl.BlockSpec((None, TOK_TILE, 2 * LANES), lambda bi, hi, ti: (bi, ti, hi)),
                  pl.BlockSpec((None, t, 2 * LANES), lambda bi, hi, ti: (bi, 0, hi)),
                  pl.BlockSpec((None, t, 2 * LANES), lambda bi, hi, ti: (bi, 0, hi))],
        out_specs=pl.BlockSpec((None, TOK_TILE, LANES), lambda bi, hi, ti: (bi, ti, hi)),
        compiler_params=_cparams(("arbitrary", "arbitrary", "arbitrary")),
        name="mla_attn",
    )(q, k, v)


def _odd_in_kernel(x_ref, mod_ref, g_ref, w_ref, mq_ref, mk_ref, gq_ref, gk_ref, cos_ref, slo_ref, shi_ref,
                   q_ref, k_ref, v_ref, *, scale):
    mod = mod_ref[...]
    h = _prenorm(x_ref[...], g_ref[...], mod[1:2], mod[0:1]).astype(BF16)
    nq = q_ref.shape[-1]
    nk = k_ref.shape[-1]
    half = GQA_DIM // 2

    def head_norm_rope(y, m_ref, gain_ref):
        reps = y.shape[-1] // LANES
        y = y * lax.rsqrt(_head_mean_square(y, m_ref) + EPS) * gain_ref[...]
        return _rope(y, _tile_lanes(cos_ref[...], reps), _tile_lanes(slo_ref[...], reps),
                     _tile_lanes(shi_ref[...], reps), half)

    q_ref[...] = (head_norm_rope(_dot(h, w_ref[:, 0:nq]), mq_ref, gq_ref) * scale).astype(BF16)
    k_ref[...] = head_norm_rope(_dot(h, w_ref[:, nq:nq + nk]), mk_ref, gk_ref).astype(BF16)
    v_ref[...] = _dot(h, w_ref[:, nq + nk:nq + 2 * nk]).astype(BF16)


def _odd_in(xs, mods, g, p, scale):
    b, t, d = xs.shape
    nq = p["gq"].shape[1]
    nk = p["gk"].shape[1]
    row = functools.partial(_mod_row, ctx_row=b)
    tok = lambda bi, ti: (bi, ti, 0)
    full = lambda a: pl.BlockSpec(a.shape, lambda bi, ti: (0,) * a.ndim)
    pos = pl.BlockSpec((TOK_TILE, LANES), lambda bi, ti: (ti, 0))
    kv = jax.ShapeDtypeStruct((b, t, nk), BF16)
    return pl.pallas_call(
        functools.partial(_odd_in_kernel, scale=scale),
        out_shape=(jax.ShapeDtypeStruct((b, t, nq), BF16), kv, kv),
        grid=(b, t // TOK_TILE),
        in_specs=[pl.BlockSpec((None, TOK_TILE, d), tok),
                  pl.BlockSpec((None, 6, d), lambda bi, ti: (row(bi, ti), 0, 0)),
                  pl.BlockSpec((1, d), lambda bi, ti: (0, 0)),
                  full(p["w"]), full(p["mq"]), full(p["mk"]), full(p["gq"]), full(p["gk"]), pos, pos, pos],
        out_specs=(pl.BlockSpec((None, TOK_TILE, nq), tok),
                   pl.BlockSpec((None, TOK_TILE, nk), tok),
                   pl.BlockSpec((None, TOK_TILE, nk), tok)),
        compiler_params=_cparams(("arbitrary", "arbitrary")),
        name="odd_in",
    )(xs, mods, g, p["w"], p["mq"], p["mk"], p["gq"], p["gk"], p["cos"], p["slo"], p["shi"])


def _win_attn_kernel(sink_ref, q_ref, k_ref, v_ref, o_ref, *, ctx_len):
    t = k_ref.shape[0]
    wb = WIN_BLOCK
    w3 = 3 * wb
    hd = GQA_DIM
    group = GQA_HEADS // GQA_KV_HEADS
    i = pl.program_id(1)
    q0 = ctx_len + i * wb
    ws = pl.multiple_of(jnp.clip(q0 - wb, ctx_len, t - w3), wb)
    nk = w3 + ctx_len
    rows = group * wb
    row = lax.broadcasted_iota(I32, (rows, nk), 0)
    col = lax.broadcasted_iota(I32, (rows, nk), 1)
    valid = (col >= w3) | (jnp.abs(q0 + row % wb - (ws + col)) <= WINDOW)
    head_of_row = lax.broadcasted_iota(I32, (rows, 1), 0) // wb
    q = q_ref[...].astype(F32)
    kcat = jnp.concatenate([k_ref[pl.ds(ws, w3), :], k_ref[0:ctx_len, :]], axis=0).astype(F32)
    vcat = jnp.concatenate([v_ref[pl.ds(ws, w3), :], v_ref[0:ctx_len, :]], axis=0).astype(F32)
    outs = []
    for kh in range(GQA_KV_HEADS):
        qs = jnp.concatenate([q[:, (kh * group + h) * hd:(kh * group + h + 1) * hd] for h in range(group)], axis=0)
        s = _dot_nt(qs.astype(BF16), kcat[:, kh * hd:(kh + 1) * hd].astype(BF16))
        s = jnp.where(valid, s, NEG_INF)
        sink = jnp.zeros((rows, 1), F32)
        for h in range(group):
            sink = jnp.where(head_of_row == h, sink_ref[kh * group + h], sink)
        m = jnp.maximum(jnp.max(s, axis=-1, keepdims=True), sink)
        p = jnp.exp(s - m)
        l = jnp.sum(p, axis=-1, keepdims=True) + jnp.exp(sink - m)
        o = _dot(p.astype(BF16), vcat[:, kh * hd:(kh + 1) * hd].astype(BF16)) / l
        outs.extend(o[h * wb:(h + 1) * wb] for h in range(group))
    o_ref[...] = jnp.concatenate(outs, axis=1).astype(o_ref.dtype)


def _win_attn(q, k, v, sink, ctx_len):
    b, t, n = q.shape
    nkv = k.shape[-1]
    s_len = t - ctx_len
    off = ctx_len // WIN_BLOCK
    return pl.pallas_call(
        functools.partial(_win_attn_kernel, ctx_len=ctx_len),
        out_shape=jax.ShapeDtypeStruct((b, s_len, n), BF16),
        grid=(b, s_len // WIN_BLOCK),
        in_specs=[pl.BlockSpec(memory_space=pltpu.SMEM),
                  pl.BlockSpec((None, WIN_BLOCK, n), lambda bi, ti: (bi, ti + off, 0)),
                  pl.BlockSpec((None, t, nkv), lambda bi, ti: (bi, 0, 0)),
                  pl.BlockSpec((None, t, nkv), lambda bi, ti: (bi, 0, 0))],
        out_specs=pl.BlockSpec((None, WIN_BLOCK, n), lambda bi, ti: (bi, ti, 0)),
        compiler_params=_cparams(("arbitrary", "arbitrary")),
        name="win_attn",
    )(sink, q, k, v)


def _post_mix_kernel(a1_ref, a2_ref, *refs, split_stream):
    x_in = _stream_tile(refs[0], refs[1]) if split_stream else refs[0][...]
    _post_mix_body(a1_ref, a2_ref, x_in, *refs[2 if split_stream else 1:])


def _post_mix_body(a1_ref, a2_ref, x_in, mod_ref, g_ref, w_ref, wr_ref, br_ref,
                   xo_ref, f_ref, route_ref, gate_ref, cnt_ref, seen_ref):
    half = a1_ref.shape[-1]
    mod = mod_ref[...]
    m = _dot(a1_ref[...], w_ref[0:half, :]) + _dot(a2_ref[...], w_ref[half:2 * half, :])
    x = x_in + mod[2:3] * m
    xo_ref[...] = x
    f = _prenorm(x, g_ref[...], mod[4:5], mod[3:4])
    f_ref[...] = _pack_bf16_pairs(f)

    logit = _dot(f.astype(BF16), wr_ref[...].astype(BF16)) + br_ref[...]
    tm = logit.shape[0]
    lane = lax.broadcasted_iota(I32, (tm, LANES), 1)
    lane_f = lane.astype(F32)
    vals, idxs = [], []
    for _ in range(TOP_K):
        mx = jnp.max(logit, axis=-1, keepdims=True)
        ix = jnp.min(jnp.where(logit == mx, lane_f, float(LANES)), axis=-1, keepdims=True)
        vals.append(mx)
        idxs.append(ix)
        logit = jnp.where(lane_f == ix, -jnp.inf, logit)
    exps = [jnp.exp(v - vals[0]) for v in vals]
    den = exps[0]
    for e in exps[1:]:
        den = den + e

    @pl.when((pl.program_id(0) == 0) & (pl.program_id(1) == 0))
    def _():
        seen_ref[...] = jnp.zeros(seen_ref.shape, F32)

    msk = jnp.zeros((tm, LANES), F32)
    for k in range(TOP_K):
        msk = jnp.where(lane_f == idxs[k], 1.0, msk)
    earlier = (lax.broadcasted_iota(I32, (tm, tm), 1) < lax.broadcasted_iota(I32, (tm, tm), 0))
    rank = _dot(jnp.where(earlier, 1.0, 0.0).astype(BF16), msk.astype(BF16)) + seen_ref[0:1, :]
    seen = seen_ref[...] + jnp.sum(msk, axis=0, keepdims=True)
    seen_ref[...] = seen
    cnt_ref[...] = seen

    r_out = jnp.zeros((tm, LANES), F32)
    g_out = jnp.zeros((tm, LANES), F32)
    for k in range(TOP_K):
        rank_k = jnp.sum(jnp.where(lane_f == idxs[k], rank, 0.0), axis=-1, keepdims=True)
        r_out = jnp.where(lane == k, idxs[k], r_out)
        r_out = jnp.where(lane == TOP_K + k, rank_k, r_out)
        g_out = jnp.where(lane == k, exps[k] / den, g_out)
    route_ref[...] = r_out.astype(I32)
    gate_ref[...] = g_out


def _post_mix(a1, a2, lane_blk2, xs, mods, g, w_out, w_router, b_router, t_off, a_off):
    split_stream = isinstance(xs, tuple)
    if split_stream:
        assert t_off == 0
        b, s_len, d = xs[1].shape
        t = xs[0].shape[1] + s_len
        x_specs = _stream_specs(d)
    else:
        b, t, d = xs.shape
        x_specs = [pl.BlockSpec((None, TOK_TILE, d), lambda bi, ti: (bi, ti + t_off, 0))]
        xs = (xs,)
    half = w_out.shape[0] // 2
    nt = t // TOK_TILE - t_off
    t_out = nt * TOK_TILE
    row = functools.partial(_mod_row, ctx_row=b)
    tok = lambda bi, ti: (bi, ti, 0)
    act = jax.ShapeDtypeStruct((b, t_out, d), F32)
    return pl.pallas_call(
        functools.partial(_post_mix_kernel, split_stream=split_stream),
        out_shape=(act, jax.ShapeDtypeStruct((b, t_out, d // 2), U32),
                   jax.ShapeDtypeStruct((b, t_out, LANES), I32),
                   jax.ShapeDtypeStruct((b, t_out, LANES), F32),
                   jax.ShapeDtypeStruct((SUBLANES, LANES), F32)),
        grid=(b, nt),
        in_specs=[pl.BlockSpec((None, TOK_TILE, half), lambda bi, ti: (bi, ti + a_off, 0)),
                  pl.BlockSpec((None, TOK_TILE, half), lambda bi, ti: (bi, ti + a_off, lane_blk2))] + x_specs + [
                  pl.BlockSpec((None, 6, d), lambda bi, ti: (row(bi, ti + t_off), 0, 0)),
                  pl.BlockSpec((1, d), lambda bi, ti: (0, 0)),
                  pl.BlockSpec(w_out.shape, lambda bi, ti: (0, 0)),
                  pl.BlockSpec(w_router.shape, lambda bi, ti: (0, 0)),
                  pl.BlockSpec((1, LANES), lambda bi, ti: (0, 0))],
        out_specs=(pl.BlockSpec((None, TOK_TILE, d), tok),
                   pl.BlockSpec((None, TOK_TILE, d // 2), tok),
                   pl.BlockSpec((None, TOK_TILE, LANES), tok),
                   pl.BlockSpec((None, TOK_TILE, LANES), tok),
                   pl.BlockSpec((SUBLANES, LANES), lambda bi, ti: (0, 0))),
        scratch_shapes=[pltpu.VMEM((SUBLANES, LANES), F32)],
        compiler_params=_cparams(("arbitrary", "arbitrary")),
        name="post_mix",
    )(a1, a2, *xs, mods, g, w_out, w_router, b_router)


DMA_UNROLL = 2


def _row(ref, i):
    return ref.at[pl.ds(i, 1), :]


def _dispatch_kernel(pad_lo_ref, pad_n_ref, nu_ref, dest_ref, f_ref, xs_ref, zero_ref, sem, zsem):
    n = dest_ref.shape[0]

    blk = zero_ref.shape[0]
    n_blk = xs_ref.shape[0] // blk

    def block_copy(i):
        return pltpu.make_async_copy(zero_ref, xs_ref.at[pl.ds(pl.multiple_of(i * blk, blk), blk), :], zsem)

    def pad_rows(e, c, wait):
        lo = pad_lo_ref[e]
        head = jnp.minimum((-lo) & (SUBLANES - 1), pad_n_ref[e])

        def one(r, c2):
            copy = pltpu.make_async_copy(_row(zero_ref, 0), _row(xs_ref, lo + r), zsem)
            copy.wait() if wait else copy.start()
            return c2

        def eight(g, c2):
            r0 = pl.multiple_of(lo + head + g * SUBLANES, SUBLANES)
            copy = pltpu.make_async_copy(zero_ref.at[pl.ds(0, SUBLANES), :], xs_ref.at[pl.ds(r0, SUBLANES), :], zsem)
            copy.wait() if wait else copy.start()
            return c2

        c = lax.fori_loop(0, head, one, c)
        return lax.fori_loop(0, (pad_n_ref[e] - head) // SUBLANES, eight, c)

    @pl.when(pl.program_id(0) == 0)
    def _():
        zero_ref[...] = jnp.zeros(zero_ref.shape, zero_ref.dtype)
        lax.fori_loop(nu_ref[0], n_blk, lambda i, c: (block_copy(i).start(), c)[1], 0)
        lax.fori_loop(0, N_EXPERTS, functools.partial(pad_rows, wait=False), 0)

    @pl.when(pl.program_id(0) == pl.num_programs(0) - 1)
    def _():
        lax.fori_loop(nu_ref[0], n_blk, lambda i, c: (block_copy(i).wait(), c)[1], 0)
        lax.fori_loop(0, N_EXPERTS, functools.partial(pad_rows, wait=True), 0)


    def issue(t, c):
        for k in range(TOP_K):
            pltpu.make_async_copy(_row(f_ref, t), _row(xs_ref, dest_ref[t * TOP_K + k]), sem).start()
        return c

    lax.fori_loop(0, n // TOP_K, issue, 0, unroll=DMA_UNROLL)
    pltpu.make_async_copy(xs_ref.at[pl.ds(0, n), :], xs_ref.at[pl.ds(0, n), :], sem).wait()


def _dispatch(f, dest, pad_lo, pad_n, n_used, rows):
    n, d = f.shape
    per = TOK_TILE * TOP_K
    return pl.pallas_call(
        _dispatch_kernel,
        out_shape=jax.ShapeDtypeStruct((rows, d), f.dtype),
        grid_spec=pltpu.PrefetchScalarGridSpec(
            num_scalar_prefetch=3,
            grid=(n // TOK_TILE,),
            in_specs=[pl.BlockSpec((per,), lambda i, lo, cnt, nu: (i,), memory_space=pltpu.SMEM),
                      pl.BlockSpec((TOK_TILE, d), lambda i, lo, cnt, nu: (i, 0))],
            out_specs=pl.BlockSpec(memory_space=pl.ANY),
            scratch_shapes=[pltpu.VMEM((MOE_ROWS, d), f.dtype), pltpu.SemaphoreType.DMA(()),
                            pltpu.SemaphoreType.DMA(())]),
        compiler_params=_cparams(("arbitrary",), has_side_effects=True, disable_bounds_checks=True),
        name="moe_dispatch",
    )(pad_lo, pad_n, n_used, dest, f)


def _experts_kernel(be_ref, nu_ref, grp_ref, nxt_ref, xs_ref, wgu_hbm, bgu_ref, wdn_hbm, bdn_ref, y_ref,
                    wgu_f32, wdn_f32, wgu_bf, wdn_bf, wsem, *, layer):
    i = pl.program_id(0)

    def fetch(e, slot, wait):
        for src, dst in ((wgu_hbm.at[layer, e], wgu_f32.at[slot]), (wdn_hbm.at[layer, e], wdn_f32.at[slot])):
            c = pltpu.make_async_copy(src, dst, wsem.at[slot])
            c.wait() if wait else c.start()

    @pl.when(i == 0)
    def _():
        fetch(be_ref[0], 0, False)

    @pl.when((i < nu_ref[0]) & ((i == 0) | (grp_ref[i] != grp_ref[jnp.maximum(i - 1, 0)])))
    def _():
        slot = grp_ref[i] % 2
        fetch(be_ref[i], slot, True)
        wgu_bf[...] = wgu_f32[slot].astype(BF16)
        wdn_bf[...] = wdn_f32[slot].astype(BF16)

        @pl.when(nxt_ref[i] >= 0)
        def _():
            fetch(nxt_ref[i], 1 - slot, False)

    @pl.when(i < nu_ref[0])
    def _():
        h = _dot(_unpack_bf16_pairs(xs_ref[...]).astype(BF16), wgu_bf[...]) + bgu_ref[...]
        ff = h.shape[1] // 2
        hg = jnp.minimum(h[:, :ff], SWIGLU_LIMIT)
        hu = jnp.clip(h[:, ff:], -SWIGLU_LIMIT, SWIGLU_LIMIT)
        act = hg * jax.nn.sigmoid(SWIGLU_ALPHA * hg) * (hu + 1.0)
        y_ref[...] = _pack_bf16_pairs(_dot(act.astype(BF16), wdn_bf[...]) + bdn_ref[...])

    @pl.when(i >= nu_ref[0])
    def _():
        y_ref[...] = jnp.zeros(y_ref.shape, y_ref.dtype)


def _experts(xs, block_e, n_used, layer, wgu, bgu, wdn, bdn):
    rows, dp = xs.shape
    d = 2 * dp
    ff2 = wgu.shape[-1]
    nb = rows // MOE_ROWS
    blk = jnp.arange(nb)
    used = blk < n_used[0]
    first = used & ((blk == 0) | (block_e != jnp.roll(block_e, 1)))
    grp = jnp.cumsum(first.astype(I32)) - 1
    later_first = jnp.where(first[None, :] & (blk[None, :] > blk[:, None]), blk[None, :], nb)
    nxt_blk = jnp.min(later_first, axis=1)
    nxt = jnp.where(nxt_blk < nb, block_e[jnp.minimum(nxt_blk, nb - 1)], -1)
    bias = lambda i, be, nu, g, nx: (layer, be[i], 0, 0)
    return pl.pallas_call(
        functools.partial(_experts_kernel, layer=layer),
        out_shape=jax.ShapeDtypeStruct((rows, dp), U32),
        grid_spec=pltpu.PrefetchScalarGridSpec(
            num_scalar_prefetch=4,
            grid=(nb,),
            in_specs=[pl.BlockSpec((MOE_ROWS, dp),
                                   lambda i, be, nu, g, nx: (jnp.maximum(jnp.minimum(i, nu[0] - 1), 0), 0)),
                      pl.BlockSpec(memory_space=pl.ANY),
                      pl.BlockSpec((None, None, 1, ff2), bias),
                      pl.BlockSpec(memory_space=pl.ANY),
                      pl.BlockSpec((None, None, 1, d), bias)],
            out_specs=pl.BlockSpec((MOE_ROWS, dp), lambda i, be, nu, g, nx: (i, 0)),
            scratch_shapes=[pltpu.VMEM((2, d, ff2), F32), pltpu.VMEM((2, ff2 // 2, d), F32),
                            pltpu.VMEM((d, ff2), BF16), pltpu.VMEM((ff2 // 2, d), BF16),
                            pltpu.SemaphoreType.DMA((2,))]),
        compiler_params=_cparams(("arbitrary",), has_side_effects=True),
        name="moe_experts",
    )(block_e, n_used, grp.astype(I32), nxt.astype(I32), xs, wgu, bgu, wdn, bdn)


IDX_BLOCK = 1024


def _fused_kernel(be_ref, nu_ref, tok0_ref, tok_ref, tgt_ref, f_ref, wgu_ref, bgu_ref, wdn_ref, bdn_ref, out_ref,
                  wgu_bf, wdn_bf, x0, x1, y0, y1, gsem, ssem):
    i = pl.program_id(0)
    nu = nu_ref[0]
    rows = MOE_ROWS
    per = IDX_BLOCK // rows
    xb = (x0, x1)
    yb = (y0, y1)

    def gather(idx_ref, base, slot):
        for j in range(rows):
            pltpu.make_async_copy(_row(f_ref, idx_ref[base + j]), _row(xb[slot], j), gsem.at[slot]).start(priority=j % 2)

    def scatter(base, slot):
        for j in range(rows):
            pltpu.make_async_copy(_row(yb[slot], j), _row(out_ref, tgt_ref[base + j]), ssem.at[slot]).start(priority=j % 2)

    def wait_rows(buf, sem):
        pltpu.make_async_copy(buf, buf, sem).wait()

    @pl.when(i == 0)
    def _():
        y1[...] = jnp.zeros(y1.shape, F32)
        n_real = out_ref.shape[0] - 2 * rows
        c = pltpu.make_async_copy(y1, out_ref.at[pl.ds(n_real, rows), :], ssem.at[0])
        c.start()
        c.wait()
        gather(tok0_ref, 0, 0)

    @pl.when((i <= nu) & ((i == 0) | (be_ref[i] != be_ref[jnp.maximum(i - 1, 0)])))
    def _():
        wgu_bf[...] = wgu_ref[...].astype(BF16)
        wdn_bf[...] = wdn_ref[...].astype(BF16)

    gbase = ((i + 1) % per) * rows
    sbase = (i % per) * rows

    for slot in range(2):
        other = 1 - slot

        @pl.when((i % 2 == slot) & (i < nu))
        def _(slot=slot, other=other):
            wait_rows(xb[slot], gsem.at[slot])

            @pl.when(i >= 1)
            def _():
                wait_rows(yb[slot], ssem.at[slot])

            gather(tok_ref, gbase, other)
            scatter(sbase, other)
            h = _dot(xb[slot][...].astype(BF16), wgu_bf[...]) + bgu_ref[...]
            ff = h.shape[1] // 2
            hg = jnp.minimum(h[:, :ff], SWIGLU_LIMIT)
            hu = jnp.clip(h[:, ff:], -SWIGLU_LIMIT, SWIGLU_LIMIT)
            act = hg * jax.nn.sigmoid(SWIGLU_ALPHA * hg) * (hu + 1.0)
            yb[slot][...] = _dot(act.astype(BF16), wdn_bf[...]) + bdn_ref[...]

        @pl.when((i % 2 == slot) & (i == nu))
        def _(slot=slot, other=other):
            wait_rows(xb[slot], gsem.at[slot])

            @pl.when(i >= 1)
            def _():
                wait_rows(yb[slot], ssem.at[slot])

            scatter(sbase, other)
            wait_rows(yb[other], ssem.at[other])


def _experts_fused(f, row_tok, row_tgt, block_e, n_used, layer, wgu, bgu, wdn, bdn):
    n, d = f.shape
    ff2 = wgu.shape[-1]
    nb = row_tok.shape[0] // MOE_ROWS
    per = IDX_BLOCK // MOE_ROWS
    last = row_tok.shape[0] // IDX_BLOCK - 1
    last_tgt = row_tgt.shape[0] // IDX_BLOCK - 1
    buf = pltpu.VMEM((MOE_ROWS, d), F32)
    w_idx = lambda i, be, nu: (layer, be[i], 0, 0)
    return pl.pallas_call(
        _fused_kernel,
        out_shape=jax.ShapeDtypeStruct((TOP_K * n + 2 * MOE_ROWS, d), F32),
        grid_spec=pltpu.PrefetchScalarGridSpec(
            num_scalar_prefetch=2,
            grid=(nb + 1,),
            in_specs=[pl.BlockSpec((IDX_BLOCK,), lambda i, be, nu: (0,), memory_space=pltpu.SMEM),
                      pl.BlockSpec((IDX_BLOCK,), lambda i, be, nu: (jnp.minimum((i + 1) // per, last),),
                                   memory_space=pltpu.SMEM),
                      pl.BlockSpec((IDX_BLOCK,), lambda i, be, nu: (jnp.minimum(i // per, last_tgt),),
                                   memory_space=pltpu.SMEM),
                      pl.BlockSpec(memory_space=pl.ANY),
                      pl.BlockSpec((None, None, d, ff2), w_idx),
                      pl.BlockSpec((None, None, 1, ff2), w_idx),
                      pl.BlockSpec((None, None, ff2 // 2, d), w_idx),
                      pl.BlockSpec((None, None, 1, d), w_idx)],
            out_specs=pl.BlockSpec(memory_space=pl.ANY),
            scratch_shapes=[pltpu.VMEM((d, ff2), BF16), pltpu.VMEM((ff2 // 2, d), BF16), buf, buf, buf, buf,
                            pltpu.SemaphoreType.DMA((2,)), pltpu.SemaphoreType.DMA((2,))]),
        compiler_params=_cparams(("arbitrary",), has_side_effects=True, disable_bounds_checks=True),
        name="moe_fused",
    )(block_e, n_used, row_tok, row_tok, row_tgt, f, wgu, bgu, wdn, bdn)


def _sum_choices_kernel(p0_ref, p1_ref, p2_ref, p3_ref, gate_ref, x_ref, mod_ref, o_ref):
    gates = gate_ref[...]
    acc = gates[:, 0:1] * p0_ref[...]
    for k, p_ref in enumerate((p1_ref, p2_ref, p3_ref), start=1):
        acc = acc + gates[:, k:k + 1] * p_ref[...]
    o_ref[...] = x_ref[...] + mod_ref[5:6, :] * acc


def _sum_choices(y4, gates, xs, mods, t_off):
    b, t_out, d = xs.shape
    nt = t_out // TOK_TILE
    row = functools.partial(_mod_row, ctx_row=b)
    tok = lambda bi, ti: (bi, ti, 0)
    plane = lambda k: pl.BlockSpec((TOK_TILE, d), lambda bi, ti: (k * b * nt + bi * nt + ti, 0))
    return pl.pallas_call(
        _sum_choices_kernel,
        out_shape=jax.ShapeDtypeStruct((b, t_out, d), F32),
        grid=(b, nt),
        in_specs=[plane(0), plane(1), plane(2), plane(3),
                  pl.BlockSpec((None, TOK_TILE, LANES), tok),
                  pl.BlockSpec((None, TOK_TILE, d), tok),
                  pl.BlockSpec((None, 6, d), lambda bi, ti: (row(bi, ti + t_off), 0, 0))],
        out_specs=pl.BlockSpec((None, TOK_TILE, d), tok),
        compiler_params=_cparams(("arbitrary", "arbitrary")),
        name="moe_sum",
    )(y4, y4, y4, y4, gates, xs, mods)


def _combine_kernel(dest_ref, y_ref, gate_ref, x_ref, mod_ref, o_ref, buf_ref, sem):
    n = dest_ref.shape[0]
    tm = n // TOP_K

    def issue(t, c):
        for k in range(TOP_K):
            pltpu.make_async_copy(_row(y_ref, dest_ref[t * TOP_K + k]), _row(buf_ref.at[k], t), sem).start()
        return c

    lax.fori_loop(0, tm, issue, 0, unroll=DMA_UNROLL)
    for k in range(TOP_K):
        pltpu.make_async_copy(y_ref.at[pl.ds(0, tm), :], buf_ref.at[k], sem).wait()

    gates = gate_ref[...]
    acc = gates[:, 0:1] * _unpack_bf16_pairs(buf_ref[0])
    for k in range(1, TOP_K):
        acc = acc + gates[:, k:k + 1] * _unpack_bf16_pairs(buf_ref[k])
    o_ref[...] = x_ref[...] + mod_ref[5:6, :] * acc


def _combine(y, dest, gates, xs, mods, t_off):
    b, t_out, d = xs.shape
    nt = t_out // TOK_TILE
    per = TOK_TILE * TOP_K
    row = functools.partial(_mod_row, ctx_row=b)
    tok = lambda bi, ti: (bi, ti, 0)
    return pl.pallas_call(
        _combine_kernel,
        out_shape=jax.ShapeDtypeStruct((b, t_out, d), F32),
        grid=(b, nt),
        in_specs=[pl.BlockSpec((per,), lambda bi, ti: (bi * nt + ti,), memory_space=pltpu.SMEM),
                  pl.BlockSpec(memory_space=pl.ANY),
                  pl.BlockSpec((None, TOK_TILE, LANES), tok),
                  pl.BlockSpec((None, TOK_TILE, d), tok),
                  pl.BlockSpec((None, 6, d), lambda bi, ti: (row(bi, ti + t_off), 0, 0))],
        out_specs=pl.BlockSpec((None, TOK_TILE, d), tok),
        scratch_shapes=[pltpu.VMEM((TOP_K, TOK_TILE, y.shape[1]), y.dtype), pltpu.SemaphoreType.DMA(())],
        compiler_params=_cparams(("arbitrary", "arbitrary"), disable_bounds_checks=True),
        name="moe_combine",
    )(dest, y, gates, xs, mods)


def _routing(e_sel, rank, counts, n):
    padded = (counts + MOE_ROWS - 1) // MOE_ROWS * MOE_ROWS
    pend = jnp.cumsum(padded)
    pstart = pend - padded
    dest = jnp.sum(jnp.where(e_sel[..., None] == jnp.arange(N_EXPERTS), pstart, 0), axis=-1) + rank
    n_blocks = n * TOP_K // MOE_ROWS + N_EXPERTS
    first_row = jnp.arange(n_blocks) * MOE_ROWS
    block_e = jnp.minimum(jnp.sum(pend[None, :] <= first_row[:, None], axis=1), N_EXPERTS - 1)
    n_used = (pend[-1] // MOE_ROWS).reshape(1)
    pads = ((pstart + counts).astype(I32), (padded - counts).astype(I32))
    return dest.reshape(-1).astype(I32), block_e.astype(I32), n_used.astype(I32), pads, n_blocks * MOE_ROWS


def _moe_fused(f, route, gates, seen, x_mid, mods, t_off, layer, wgu, bgu, wdn, bdn):
    b, t_out, d = f.shape
    n = b * t_out
    route = route.reshape(n, LANES)
    counts = seen[0, :N_EXPERTS].astype(I32)
    dest, block_e, n_used, _, rows = _routing(route[:, :TOP_K], route[:, TOP_K:2 * TOP_K], counts, n)
    assert rows % IDX_BLOCK == 0
    choice = jnp.arange(n * TOP_K, dtype=I32)
    tok, k = choice // TOP_K, choice % TOP_K
    r = jnp.arange(rows, dtype=I32)
    scratch = TOP_K * n + ((r // MOE_ROWS) % 2) * MOE_ROWS + r % MOE_ROWS
    row_tok = jnp.zeros((rows,), I32).at[dest].set(tok)
    row_tgt = scratch.at[dest].set(k * n + tok)
    lead = TOP_K * n + MOE_ROWS + jnp.arange(MOE_ROWS, dtype=I32)
    tail = jnp.full((IDX_BLOCK - MOE_ROWS,), TOP_K * n, I32)
    row_tgt = jnp.concatenate([lead, row_tgt, tail])
    block_e = jnp.concatenate([block_e, block_e[-1:]])
    y4 = _experts_fused(f.reshape(n, d), row_tok, row_tgt, block_e, n_used, layer, wgu, bgu, wdn, bdn)
    return _sum_choices(y4, gates, x_mid, mods, t_off)


def _moe(f, route, gates, seen, x_mid, mods, t_off, layer, wgu, bgu, wdn, bdn):
    b, t_out, d = f.shape
    n = b * t_out
    route = route.reshape(n, LANES)
    counts = seen[0, :N_EXPERTS].astype(I32)
    dest, block_e, n_used, pads, rows = _routing(route[:, :TOP_K], route[:, TOP_K:2 * TOP_K], counts, n)
    xs = _dispatch(f.reshape(n, d), dest, *pads, n_used, rows)
    y = _experts(xs, block_e, n_used, layer, wgu, bgu, wdn, bdn)
    return _combine(y, dest, gates, x_mid, mods, t_off)


def _axial_angles(n_rows, rot_dim):
    row = np.repeat(np.arange(n_rows, dtype=np.float32), GRID_W)
    col = np.tile(np.arange(GRID_W, dtype=np.float32), n_rows)
    n = rot_dim // 4
    freqs = (np.float32(ROPE_THETA) ** (-np.arange(n, dtype=np.float32) / np.float32(n))).astype(np.float32)
    return np.concatenate([row[:, None] * freqs, col[:, None] * freqs], axis=-1).astype(np.float32)


def _rope_tables(ctx_len, s_len, rot_dim, lane_base, reps):
    ang = _axial_angles(s_len // GRID_W, rot_dim)
    half = rot_dim // 2
    period = LANES // reps
    cos = np.ones((ctx_len + s_len, period), np.float32)
    slo = np.zeros((ctx_len + s_len, period), np.float32)
    shi = np.zeros((ctx_len + s_len, period), np.float32)
    cos[ctx_len:, lane_base:lane_base + rot_dim] = np.tile(np.cos(ang), (1, 2))
    slo[ctx_len:, lane_base:lane_base + half] = -np.sin(ang)
    shi[ctx_len:, lane_base + half:lane_base + rot_dim] = np.sin(ang)
    return [jnp.asarray(np.tile(tb, (1, reps))) for tb in (cos, slo, shi)]


def _segment_mean_matrix(width, period, segs):
    lane = np.arange(width)
    seg_id = np.full((width,), -1)
    seg_w = np.zeros((width,), np.float32)
    for i, (start, length) in enumerate(segs):
        inside = ((lane % period) >= start) & ((lane % period) < start + length)
        seg_id = np.where(inside, (lane // period) * len(segs) + i, seg_id)
        seg_w = np.where(inside, np.float32(1.0 / length), seg_w)
    same = (seg_id[:, None] == seg_id[None, :]) & (seg_id[:, None] >= 0)
    return jnp.asarray(np.where(same, seg_w[None, :], np.float32(0.0)), dtype=BF16)


def _block_diag(w):
    n, c, _ = w.shape
    eye = jnp.eye(n, dtype=w.dtype)
    return (eye[:, None, :, None] * w[:, :, None, :]).reshape(n * c, n * c)


def kernel(x, c, ctx, c_ctx, w_mod, b_mod, norm_mix, norm_ffn, w_in_even, lru_conv_w, lru_conv_b, lru_w_r, lru_b_r, lru_w_i, lru_b_i, lru_lambda, mla_q_a_norm, mla_w_q_b, mla_kv_a_norm, mla_w_kv_b, mla_nope_norm, mla_rope_norm, w_out_even, w_qkv_odd, gqa_qk_norm, gqa_sink, w_out_odd, w_router, b_router, w_gate_up, b_gate_up, w_down, b_down):
    b, s_len, d = x.shape
    ctx_len = ctx.shape[1]
    depth = w_mod.shape[0]
    assert depth == 2 and ctx_len == TOK_TILE and s_len % TOK_TILE == 0 and b + 1 <= SUBLANES
    lru_w = lru_conv_w.shape[-1]
    q_lora = mla_q_a_norm.shape[-1]
    kv_lora = mla_kv_a_norm.shape[-1]

    cvec = jnp.concatenate([c, c_ctx[None], jnp.zeros((SUBLANES - b - 1, d), F32)], axis=0)
    mods = _modulation(cvec, w_mod, b_mod).reshape(depth, SUBLANES, 6, d)

    wr_pad = jnp.zeros((depth, d, LANES), F32).at[:, :, :N_EXPERTS].set(w_router)
    br_pad = jnp.full((depth, 1, LANES), NEG_INF, F32).at[:, 0, :N_EXPERTS].set(b_router)
    experts = (w_gate_up, b_gate_up[:, :, None, :], w_down, b_down[:, :, None, :])

    n_in = w_in_even.shape[-1]
    n_in_pad = -(-n_in // LANES) * LANES
    w_in = jnp.zeros((d, n_in_pad), F32).at[:, :n_in].set(w_in_even[0]).astype(BF16)
    xa, ga, mla_in = _even_in(ctx, x, mods[0], norm_mix[0][None], w_in, lru_w)

    nh = lru_w // (2 * LANES)
    per = LRU_BLOCKS // nh
    blk = lru_w // LRU_BLOCKS
    w_gates = jnp.stack([
        jnp.stack([jnp.concatenate([_block_diag(lru_w_r[0, dd, h * per:(h + 1) * per]),
                                    _block_diag(lru_w_i[0, dd, h * per:(h + 1) * per])], axis=1)
                   for h in range(nh)]) for dd in range(2)]).astype(BF16)
    assert blk * per == 2 * LANES
    ya = _lru(xa, ga, lru_conv_w[0], lru_conv_b[0][:, None, :], w_gates, lru_b_r[0][:, None, :],
              lru_b_i[0][:, None, :], jax.nn.softplus(-lru_lambda[0])[:, None, :], ctx_len)

    qk = MLA_NOPE + MLA_ROPE
    hq = MLA_HEADS * LANES
    wq = jnp.zeros((q_lora, MLA_HEADS, LANES), F32).at[:, :, :qk].set(
        mla_w_q_b[0].reshape(q_lora, MLA_HEADS, qk)).reshape(q_lora, hq).astype(BF16)
    wkv = mla_w_kv_b[0].reshape(kv_lora, MLA_HEADS, MLA_NOPE + MLA_V)
    wk = jnp.zeros((kv_lora, MLA_HEADS, LANES), F32).at[:, :, :MLA_NOPE].set(
        wkv[:, :, :MLA_NOPE]).reshape(kv_lora, hq).astype(BF16)
    wv = jnp.zeros((kv_lora, MLA_HEADS, LANES), F32).at[:, :, :MLA_V].set(
        wkv[:, :, MLA_NOPE:]).reshape(kv_lora, hq).astype(BF16)
    one = jnp.asarray(np.tile(np.arange(LANES) == MLA_V, MLA_HEADS)[None], dtype=F32)
    zpad = jnp.zeros((LANES - qk,), F32)
    gq = jnp.tile(jnp.concatenate([mla_nope_norm[0, 0], mla_rope_norm[0, 0], zpad]), MLA_HEADS)[None]
    gk = jnp.tile(jnp.concatenate([mla_nope_norm[0, 1], jnp.zeros((LANES - MLA_NOPE,), F32)]), MLA_HEADS)[None]
    gkr = jnp.concatenate([mla_rope_norm[0, 1], jnp.zeros((LANES - MLA_ROPE,), F32)])[None]
    cos, slo, shi = _rope_tables(ctx_len, s_len, MLA_ROPE, MLA_NOPE, 1)
    mla_p = dict(qan=mla_q_a_norm[0][None], kvn=mla_kv_a_norm[0][None], wq=wq, wk=wk, wv=wv,
                 mq=_segment_mean_matrix(MXU_TILE,LANES, [(0, MLA_NOPE), (MLA_NOPE, MLA_ROPE)]),
                 mk=_segment_mean_matrix(MXU_TILE,LANES, [(0, MLA_NOPE)]),
                 gq=gq, gk=gk, gkr=gkr, one=one, cos=cos, slo=slo, shi=shi)
    q, k, v = _mla_prep(mla_in, mla_p, qk ** -0.5 * math.log2(math.e))
    yb = _mla_attn(q, k, v, ctx_len)

    x_mid, f, route, gates, seen = _post_mix(ya, yb, 0, (ctx, x), mods[0], norm_ffn[0][None], w_out_even[0].astype(BF16),
                                             wr_pad[0], br_pad[0], 0, 0)
    xs = _moe(f, route, gates, seen, x_mid, mods[0], 0, 0, *experts)

    nq = GQA_HEADS * GQA_DIM
    nkv = GQA_KV_HEADS * GQA_DIM
    cos, slo, shi = _rope_tables(ctx_len, s_len, GQA_DIM, 0, LANES // GQA_DIM)
    odd_p = dict(w=w_qkv_odd[0].astype(BF16),
                 mq=_segment_mean_matrix(MXU_TILE,GQA_DIM, [(0, GQA_DIM)]),
                 mk=_segment_mean_matrix(MXU_TILE,GQA_DIM, [(0, GQA_DIM)]),
                 gq=jnp.tile(gqa_qk_norm[0, 0], GQA_HEADS)[None], gk=jnp.tile(gqa_qk_norm[0, 1], GQA_KV_HEADS)[None],
                 cos=cos, slo=slo, shi=shi)
    q, k, v = _odd_in(xs, mods[1], norm_mix[1][None], odd_p, GQA_DIM ** -0.5)
    o = _win_attn(q, k, v, gqa_sink[0], ctx_len)

    t_off = ctx_len // TOK_TILE
    x_mid, f, route, gates, seen = _post_mix(o, o, 1, xs, mods[1], norm_ffn[1][None], w_out_odd[0].astype(BF16),
                                             wr_pad[1], br_pad[1], t_off, 0)
    return _moe(f, route, gates, seen, x_mid, mods[1], t_off, 1, *experts)
```

```python
import functools
import math

import jax
import jax.numpy as jnp
import numpy as np
from jax import lax
from jax.experimental import pallas as pl
from jax.experimental.pallas import tpu as pltpu

F32 = jnp.float32
BF16 = jnp.bfloat16
I32 = jnp.int32

GRID_W = 64
LRU_BLOCKS = 8
LRU_C = 8.0
CONV_W = 4
MLA_HEADS = 8
MLA_NOPE = 64
MLA_ROPE = 32
MLA_V = 64
GQA_HEADS = 16
GQA_KV_HEADS = 4
GQA_DIM = 64
WINDOW = 128
ROPE_THETA = 10000.0
NEG_INF = -1e30
EPS = 1e-6
N_EXPERTS = 32
TOP_K = 4
SWIGLU_LIMIT = 7.0
SWIGLU_ALPHA = 1.702

LANES = 128
SUBLANES = 8
TOK_TILE = 256
LRU_CHUNK = 128
WIN_BLOCK = 128
MOE_ROWS = 256
MXU_TILE = 256
MLA_HEADS_PER_STEP = 4
VMEM_LIMIT = 48 * 1024 * 1024


def _cparams(sem, **kw):
    return pltpu.CompilerParams(dimension_semantics=sem, vmem_limit_bytes=VMEM_LIMIT, **kw)


def _dot(a, b):
    return jnp.dot(a, b, preferred_element_type=F32)


def _dot_nt(a, b):
    return lax.dot_general(a, b, (((1,), (1,)), ((), ())), preferred_element_type=F32)


def _split_bf16(x):
    hi = x.astype(BF16)
    lo = (x - hi.astype(F32)).astype(BF16)
    return hi, lo


def _dot3(a, w):
    ah, al = _split_bf16(a)
    wh, wl = _split_bf16(w)
    return _dot(ah, wh) + _dot(al, wh) + _dot(ah, wl)


def _rms(x):
    return x * lax.rsqrt(jnp.mean(x * x, axis=-1, keepdims=True) + EPS)


def _prenorm(x, g, scale, shift):
    return (_rms(x) * g) * (1.0 + scale) + shift


def _rope(x, cos, sin_lo, sin_hi, half):
    w = x.shape[-1]
    return x * cos + pltpu.roll(x, w - half, axis=1) * sin_lo + pltpu.roll(x, half, axis=1) * sin_hi


U32 = jnp.uint32
HI_HALF = 0xFFFF0000


def _pack_bf16_pairs(x):
    w = x.shape[1] // 2
    lo = pltpu.bitcast(x[:, :w].astype(BF16).astype(F32), U32) >> 16
    hi = pltpu.bitcast(x[:, w:].astype(BF16).astype(F32), U32) & U32(HI_HALF)
    return lo | hi


def _unpack_bf16_pairs(words):
    lo = pltpu.bitcast(words << 16, F32)
    hi = pltpu.bitcast(words & U32(HI_HALF), F32)
    return jnp.concatenate([lo, hi], axis=1)


def _head_mean_square(y, m_ref):
    w = m_ref.shape[0]
    return jnp.concatenate([_dot((y[:, j:j + w] * y[:, j:j + w]).astype(BF16), m_ref[...])
                            for j in range(0, y.shape[1], w)], axis=1)


def _tile_lanes(t, reps):
    return jnp.concatenate([t] * reps, axis=1) if reps > 1 else t


def _mod_kernel(c_ref, w_ref, b_ref, o_ref):
    c = c_ref[...]
    o_ref[...] = _dot3(c * jax.nn.sigmoid(c), w_ref[...]) + b_ref[...]


def _modulation(cvec, w_mod, b_mod):
    depth, d, n = w_mod.shape
    tn = 1536
    return pl.pallas_call(
        _mod_kernel,
        out_shape=jax.ShapeDtypeStruct((depth, SUBLANES, n), F32),
        grid=(depth, n // tn),
        in_specs=[pl.BlockSpec((SUBLANES, d), lambda l, j: (0, 0)),
                  pl.BlockSpec((None, d, tn), lambda l, j: (l, 0, j)),
                  pl.BlockSpec((None, 1, tn), lambda l, j: (l, 0, j))],
        out_specs=pl.BlockSpec((None, SUBLANES, tn), lambda l, j: (l, 0, j)),
        compiler_params=_cparams(("arbitrary", "arbitrary")),
        name="modulation",
    )(cvec, w_mod, b_mod.reshape(depth, 1, n))


def _mod_row(b, t, ctx_row):
    return jnp.where(t == 0, ctx_row, b)


def _stream_tile(ctx_ref, lat_ref):
    return jnp.where(pl.program_id(1) == 0, ctx_ref[...], lat_ref[...])


def _stream_specs(d):
    return [pl.BlockSpec((None, TOK_TILE, d), lambda bi, ti: (bi, 0, 0)),
            pl.BlockSpec((None, TOK_TILE, d), lambda bi, ti: (bi, jnp.maximum(ti - 1, 0), 0))]


def _even_in_kernel(ctx_ref, x_ref, mod_ref, g_ref, w_ref, xa_ref, ga_ref, mla_ref):
    mod = mod_ref[...]
    h = _prenorm(_stream_tile(ctx_ref, x_ref), g_ref[...], mod[1:2], mod[0:1])
    z = _dot(h.astype(BF16), w_ref[...])
    c = xa_ref.shape[-1]
    xa_ref[...] = z[:, :c]
    ga_ref[...] = z[:, c:2 * c]
    mla_ref[...] = z[:, 2 * c:]


def _even_in(ctx, x, mods, g, w_pad, lru_w):
    b, s_len, d = x.shape
    t = ctx.shape[1] + s_len
    nt = t // TOK_TILE
    n_out = w_pad.shape[1]
    n_mla = n_out - 2 * lru_w
    row = functools.partial(_mod_row, ctx_row=b)
    tok = lambda bi, ti: (bi, ti, 0)
    return pl.pallas_call(
        _even_in_kernel,
        out_shape=(jax.ShapeDtypeStruct((b, t, lru_w), F32),
                   jax.ShapeDtypeStruct((b, t, lru_w), F32),
                   jax.ShapeDtypeStruct((b, t, n_mla), F32)),
        grid=(b, nt),
        in_specs=_stream_specs(d) + [
            pl.BlockSpec((None, 6, d), lambda bi, ti: (row(bi, ti), 0, 0)),
            pl.BlockSpec((1, d), lambda bi, ti: (0, 0)),
            pl.BlockSpec((d, n_out), lambda bi, ti: (0, 0))],
        out_specs=(pl.BlockSpec((None, TOK_TILE, lru_w), tok),
                   pl.BlockSpec((None, TOK_TILE, lru_w), tok),
                   pl.BlockSpec((None, TOK_TILE, n_mla), tok)),
        compiler_params=_cparams(("arbitrary", "arbitrary")),
        name="even_in",
    )(ctx, x, mods, g, w_pad)


def _lru_kernel(xa_ref, ga_ref, cw_ref, cb_ref, wg_ref, br_ref, bi_ref, sp_ref, o_ref, pad_ref, rec_ref, *, ctx_len):
    t, c = xa_ref.shape
    tc = LRU_CHUNK
    halo = SUBLANES
    n_chunks = t // tc
    n_ctx = ctx_len // tc
    groups = tc // SUBLANES

    pad_ref[0:halo, :] = jnp.zeros((halo, c), F32)
    pad_ref[t + halo:t + 2 * halo, :] = jnp.zeros((halo, c), F32)
    pad_ref[halo:t + halo, :] = xa_ref[...]

    rid = lax.broadcasted_iota(I32, (tc, 1), 0)
    sub = rid % SUBLANES

    for d in range(2):
        cw = cw_ref[d]
        cb = cb_ref[d]
        wg = wg_ref[d]
        b_r = br_ref[d]
        b_i = bi_ref[d]
        sp = sp_ref[d]

        def chunk(i, h, d=d, cw=cw, cb=cb, wg=wg, b_r=b_r, b_i=b_i, sp=sp):
            if d == 0:
                ci = i
            else:
                ci = jnp.where(i < n_ctx, n_ctx - 1 - i, n_chunks - 1 - (i - n_ctx))
            r0 = pl.multiple_of(ci * tc, tc)
            win = pad_ref[pl.ds(r0, tc + 2 * halo), :]
            if d == 0:
                past = jnp.where(ci == n_ctx, 0.0, win[0:halo])
                win = jnp.concatenate([past, win[halo:]], axis=0)
            else:
                past = jnp.where(ci == n_ctx - 1, 0.0, win[halo + tc:])
                win = jnp.concatenate([win[:halo + tc], past], axis=0)
            xc = jnp.zeros((tc, c), F32) + cb
            for k in range(CONV_W):
                off = (k - (CONV_W - 1)) if d == 0 else ((CONV_W - 1) - k)
                if off == 0:
                    src = win[halo:halo + tc]
                else:
                    src = pltpu.roll(win, (-off) % (tc + 2 * halo), axis=0)[halo:halo + tc]
                xc = xc + cw[k:k + 1] * src
            gz = _dot(xc.astype(BF16), wg)
            r = 0.5 * jnp.tanh(0.5 * (gz[:, :c] + b_r)) + 0.5
            gi = 0.5 * jnp.tanh(0.5 * (gz[:, c:] + b_i)) + 0.5
            log_a = (-LRU_C) * r * sp
            a = jnp.exp(log_a)
            th = jnp.tanh(log_a)
            bb = jnp.sqrt(-2.0 * th / (1.0 - th)) * (gi * xc)
            for s in (1, 2, 4):
                if d == 0:
                    ok = sub >= s
                    sh = s
                else:
                    ok = sub <= (SUBLANES - 1 - s)
                    sh = tc - s
                a_prev = jnp.where(ok, pltpu.roll(a, sh, axis=0), 1.0)
                b_prev = jnp.where(ok, pltpu.roll(bb, sh, axis=0), 0.0)
                bb = a * b_prev + bb
                a = a * a_prev
            outs = [None] * groups
            order = range(groups) if d == 0 else range(groups - 1, -1, -1)
            for g in order:
                lo = g * SUBLANES
                hg = a[lo:lo + SUBLANES] * h + bb[lo:lo + SUBLANES]
                outs[g] = hg
                h = hg[SUBLANES - 1:SUBLANES] if d == 0 else hg[0:1]
            hs = jnp.concatenate(outs, axis=0)
            if d == 0:
                rec_ref[pl.ds(r0, tc), :] = hs
            else:
                tot = rec_ref[pl.ds(r0, tc), :] + hs
                gate = jax.nn.gelu(ga_ref[pl.ds(r0, tc), :], approximate=True)
                o_ref[pl.ds(r0, tc), :] = (tot * gate).astype(o_ref.dtype)
            return h

        lax.fori_loop(0, n_chunks, chunk, jnp.zeros((1, c), F32))


def _lru(xa, ga, conv_w, conv_b, w_gates, b_r, b_i, sp, ctx_len):
    b, t, w = xa.shape
    c = 2 * LANES
    nh = w // c
    tok = lambda bi, hi: (bi, 0, hi)
    par = lambda bi, hi: (0, 0, hi)
    return pl.pallas_call(
        functools.partial(_lru_kernel, ctx_len=ctx_len),
        out_shape=jax.ShapeDtypeStruct((b, t, w), BF16),
        grid=(b, nh),
        in_specs=[pl.BlockSpec((None, t, c), tok),
                  pl.BlockSpec((None, t, c), tok),
                  pl.BlockSpec((2, CONV_W, c), par),
                  pl.BlockSpec((2, 1, c), par),
                  pl.BlockSpec((2, None, c, 2 * c), lambda bi, hi: (0, hi, 0, 0)),
                  pl.BlockSpec((2, 1, c), par),
                  pl.BlockSpec((2, 1, c), par),
                  pl.BlockSpec((2, 1, c), par)],
        out_specs=pl.BlockSpec((None, t, c), tok),
        scratch_shapes=[pltpu.VMEM((t + 2 * SUBLANES, c), F32), pltpu.VMEM((t, c), F32)],
        compiler_params=_cparams(("arbitrary", "arbitrary")),
        name="rglru",
    )(xa, ga, conv_w, conv_b, w_gates, b_r, b_i, sp)


def _mla_prep_kernel(in_ref, qan_ref, kvn_ref, wq_ref, wk_ref, wv_ref, mq_ref, mk_ref, gq_ref, gk_ref, gkr_ref,
                     one_ref, cos_ref, slo_ref, shi_ref, q_ref, k_ref, v_ref, *, q_lora, kv_lora, scale):
    z = in_ref[...]
    heads = q_ref.shape[-1] // LANES
    cos = cos_ref[...]
    slo = slo_ref[...]
    shi = shi_ref[...]
    half = MLA_ROPE // 2

    qan = (_rms(z[:, :q_lora]) * qan_ref[...]).astype(BF16)
    q = _dot(qan, wq_ref[...])
    q = q * lax.rsqrt(_head_mean_square(q, mq_ref) + EPS) * gq_ref[...]
    q = _rope(q, _tile_lanes(cos, heads), _tile_lanes(slo, heads), _tile_lanes(shi, heads), half)
    q_ref[...] = (q * scale).astype(BF16)

    kvn = (_rms(z[:, q_lora:q_lora + kv_lora]) * kvn_ref[...]).astype(BF16)
    kk = _dot(kvn, wk_ref[...])
    kk = kk * lax.rsqrt(_head_mean_square(kk, mk_ref) + EPS) * gk_ref[...]
    v_ref[...] = (_dot(kvn, wv_ref[...]) + one_ref[...]).astype(BF16)

    kr = z[:, q_lora + kv_lora:]
    kr = kr * lax.rsqrt(jnp.sum(kr * kr, axis=-1, keepdims=True) * (1.0 / MLA_ROPE) + EPS) * gkr_ref[...]
    kr = pltpu.roll(kr, MLA_NOPE, axis=1)
    kr = _rope(kr, cos, slo, shi, half)
    k_ref[...] = (kk + _tile_lanes(kr, heads)).astype(BF16)


def _mla_prep(mla_in, p, scale):
    b, t, w = mla_in.shape
    hq = MLA_HEADS * LANES
    tok = lambda bi, ti: (bi, ti, 0)
    full = lambda a: pl.BlockSpec(a.shape, lambda bi, ti: (0,) * a.ndim)
    pos = pl.BlockSpec((TOK_TILE, LANES), lambda bi, ti: (ti, 0))
    consts = [p["qan"], p["kvn"], p["wq"], p["wk"], p["wv"], p["mq"], p["mk"], p["gq"], p["gk"], p["gkr"], p["one"]]
    out = jax.ShapeDtypeStruct((b, t, hq), BF16)
    return pl.pallas_call(
        functools.partial(_mla_prep_kernel, q_lora=p["qan"].shape[1], kv_lora=p["kvn"].shape[1], scale=scale),
        out_shape=(out, out, out),
        grid=(b, t // TOK_TILE),
        in_specs=[pl.BlockSpec((None, TOK_TILE, w), tok)] + [full(a) for a in consts] + [pos, pos, pos],
        out_specs=(pl.BlockSpec((None, TOK_TILE, hq), tok),) * 3,
        compiler_params=_cparams(("arbitrary", "arbitrary")),
        name="mla_prep",
    )(mla_in, *consts, p["cos"], p["slo"], p["shi"])


def _mla_attn_kernel(q_ref, k_ref, v_ref, o_ref, *, ctx_len):
    tq = q_ref.shape[0]
    t = k_ref.shape[0]
    lane = lax.broadcasted_iota(I32, (tq, LANES), 1)

    def attend(nk):
        outs = []
        for hh in range(q_ref.shape[1] // LANES):
            q = q_ref[:, hh * LANES:(hh + 1) * LANES]
            k = k_ref[0:nk, hh * LANES:(hh + 1) * LANES]
            s = _dot_nt(q, k)
            p = jnp.exp2(s - jnp.max(s, axis=-1, keepdims=True))
            o = _dot(p.astype(BF16), v_ref[0:nk, hh * LANES:(hh + 1) * LANES])
            outs.append(o / o[:, MLA_V:MLA_V + 1])
        pairs = [jnp.where(lane < MLA_V, outs[j], pltpu.roll(outs[j + 1], MLA_V, axis=1))
                 for j in range(0, len(outs), 2)]
        o_ref[...] = jnp.concatenate(pairs, axis=1).astype(o_ref.dtype)

    @pl.when(pl.program_id(2) == 0)
    def _():
        attend(ctx_len)

    @pl.when(pl.program_id(2) > 0)
    def _():
        attend(t)


def _mla_attn(q, k, v, ctx_len):
    b, t, hq = q.shape
    hps = MLA_HEADS_PER_STEP
    groups = hq // (hps * LANES)
    return pl.pallas_call(
        functools.partial(_mla_attn_kernel, ctx_len=ctx_len),
        out_shape=jax.ShapeDtypeStruct((b, t, groups * hps * MLA_V), BF16),
        grid=(b, groups, t // TOK_TILE),
        in_specs=[pl.BlockSpec((None, TOK_TILE, hps * LANES), lambda bi, hi, ti: (bi, ti, hi)),
                  pl.BlockSpec((None, t, hps * LANES), lambda bi, hi, ti: (bi, 0, hi)),
                  pl.BlockSpec((None, t, hps * LANES), lambda bi, hi, ti: (bi, 0, hi))],
        out_specs=pl.BlockSpec((None, TOK_TILE, hps * MLA_V), lambda bi, hi, ti: (bi, ti, hi)),
        compiler_params=_cparams(("arbitrary", "arbitrary", "arbitrary")),
        name="mla_attn",
    )(q, k, v)


def _odd_in_kernel(x_ref, mod_ref, g_ref, w_ref, mq_ref, mk_ref, gq_ref, gk_ref, cos_ref, slo_ref, shi_ref,
                   q_ref, k_ref, v_ref, *, scale):
    mod = mod_ref[...]
    h = _prenorm(x_ref[...], g_ref[...], mod[1:2], mod[0:1]).astype(BF16)
    nq = q_ref.shape[-1]
    nk = k_ref.shape[-1]
    half = GQA_DIM // 2

    def head_norm_rope(y, m_ref, gain_ref):
        reps = y.shape[-1] // LANES
        y = y * lax.rsqrt(_head_mean_square(y, m_ref) + EPS) * gain_ref[...]
        return _rope(y, _tile_lanes(cos_ref[...], reps), _tile_lanes(slo_ref[...], reps),
                     _tile_lanes(shi_ref[...], reps), half)

    q_ref[...] = (head_norm_rope(_dot(h, w_ref[:, 0:nq]), mq_ref, gq_ref) * scale).astype(BF16)
    k_ref[...] = head_norm_rope(_dot(h, w_ref[:, nq:nq + nk]), mk_ref, gk_ref).astype(BF16)
    v_ref[...] = _dot(h, w_ref[:, nq + nk:nq + 2 * nk]).astype(BF16)


def _odd_in(xs, mods, g, p, scale):
    b, t, d = xs.shape
    nq = p["gq"].shape[1]
    nk = p["gk"].shape[1]
    row = functools.partial(_mod_row, ctx_row=b)
    tok = lambda bi, ti: (bi, ti, 0)
    full = lambda a: pl.BlockSpec(a.shape, lambda bi, ti: (0,) * a.ndim)
    pos = pl.BlockSpec((TOK_TILE, LANES), lambda bi, ti: (ti, 0))
    kv = jax.ShapeDtypeStruct((b, t, nk), BF16)
    return pl.pallas_call(
        functools.partial(_odd_in_kernel, scale=scale),
        out_shape=(jax.ShapeDtypeStruct((b, t, nq), BF16), kv, kv),
        grid=(b, t // TOK_TILE),
        in_specs=[pl.BlockSpec((None, TOK_TILE, d), tok),
                  pl.BlockSpec((None, 6, d), lambda bi, ti: (row(bi, ti), 0, 0)),
                  pl.BlockSpec((1, d), lambda bi, ti: (0, 0)),
                  full(p["w"]), full(p["mq"]), full(p["mk"]), full(p["gq"]), full(p["gk"]), pos, pos, pos],
        out_specs=(pl.BlockSpec((None, TOK_TILE, nq), tok),
                   pl.BlockSpec((None, TOK_TILE, nk), tok),
                   pl.BlockSpec((None, TOK_TILE, nk), tok)),
        compiler_params=_cparams(("arbitrary", "arbitrary")),
        name="odd_in",
    )(xs, mods, g, p["w"], p["mq"], p["mk"], p["gq"], p["gk"], p["cos"], p["slo"], p["shi"])


def _win_attn_kernel(sink_ref, q_ref, k_ref, v_ref, o_ref, *, ctx_len):
    t = k_ref.shape[0]
    wb = WIN_BLOCK
    w3 = 3 * wb
    hd = GQA_DIM
    group = GQA_HEADS // GQA_KV_HEADS
    i = pl.program_id(1)
    q0 = ctx_len + i * wb
    ws = pl.multiple_of(jnp.clip(q0 - wb, ctx_len, t - w3), wb)
    nk = w3 + ctx_len
    rows = group * wb
    row = lax.broadcasted_iota(I32, (rows, nk), 0)
    col = lax.broadcasted_iota(I32, (rows, nk), 1)
    valid = (col >= w3) | (jnp.abs(q0 + row % wb - (ws + col)) <= WINDOW)
    head_of_row = lax.broadcasted_iota(I32, (rows, 1), 0) // wb
    q = q_ref[...].astype(F32)
    kcat = jnp.concatenate([k_ref[pl.ds(ws, w3), :], k_ref[0:ctx_len, :]], axis=0).astype(F32)
    vcat = jnp.concatenate([v_ref[pl.ds(ws, w3), :], v_ref[0:ctx_len, :]], axis=0).astype(F32)
    outs = []
    for kh in range(GQA_KV_HEADS):
        qs = jnp.concatenate([q[:, (kh * group + h) * hd:(kh * group + h + 1) * hd] for h in range(group)], axis=0)
        s = _dot_nt(qs.astype(BF16), kcat[:, kh * hd:(kh + 1) * hd].astype(BF16))
        s = jnp.where(valid, s, NEG_INF)
        sink = jnp.zeros((rows, 1), F32)
        for h in range(group):
            sink = jnp.where(head_of_row == h, sink_ref[kh * group + h], sink)
        m = jnp.maximum(jnp.max(s, axis=-1, keepdims=True), sink)
        p = jnp.exp(s - m)
        l = jnp.sum(p, axis=-1, keepdims=True) + jnp.exp(sink - m)
        o = _dot(p.astype(BF16), vcat[:, kh * hd:(kh + 1) * hd].astype(BF16)) / l
        outs.extend(o[h * wb:(h + 1) * wb] for h in range(group))
    o_ref[...] = jnp.concatenate(outs, axis=1).astype(o_ref.dtype)


def _win_attn(q, k, v, sink, ctx_len):
    b, t, n = q.shape
    nkv = k.shape[-1]
    s_len = t - ctx_len
    off = ctx_len // WIN_BLOCK
    return pl.pallas_call(
        functools.partial(_win_attn_kernel, ctx_len=ctx_len),
        out_shape=jax.ShapeDtypeStruct((b, s_len, n), BF16),
        grid=(b, s_len // WIN_BLOCK),
        in_specs=[pl.BlockSpec(memory_space=pltpu.SMEM),
                  pl.BlockSpec((None, WIN_BLOCK, n), lambda bi, ti: (bi, ti + off, 0)),
                  pl.BlockSpec((None, t, nkv), lambda bi, ti: (bi, 0, 0)),
                  pl.BlockSpec((None, t, nkv), lambda bi, ti: (bi, 0, 0))],
        out_specs=pl.BlockSpec((None, WIN_BLOCK, n), lambda bi, ti: (bi, ti, 0)),
        compiler_params=_cparams(("arbitrary", "arbitrary")),
        name="win_attn",
    )(sink, q, k, v)


def _post_mix_kernel(a1_ref, a2_ref, *refs, split_stream):
    x_in = _stream_tile(refs[0], refs[1]) if split_stream else refs[0][...]
    _post_mix_body(a1_ref, a2_ref, x_in, *refs[2 if split_stream else 1:])


def _post_mix_body(a1_ref, a2_ref, x_in, mod_ref, g_ref, w_ref, wr_ref, br_ref,
                   xo_ref, f_ref, route_ref, gate_ref, cnt_ref, seen_ref):
    half = a1_ref.shape[-1]
    mod = mod_ref[...]
    m = _dot(a1_ref[...], w_ref[0:half, :]) + _dot(a2_ref[...], w_ref[half:2 * half, :])
    x = x_in + mod[2:3] * m
    xo_ref[...] = x
    f = _prenorm(x, g_ref[...], mod[4:5], mod[3:4])
    f_ref[...] = _pack_bf16_pairs(f)

    logit = _dot(f.astype(BF16), wr_ref[...].astype(BF16)) + br_ref[...]
    tm = logit.shape[0]
    lane = lax.broadcasted_iota(I32, (tm, LANES), 1)
    lane_f = lane.astype(F32)
    vals, idxs = [], []
    for _ in range(TOP_K):
        mx = jnp.max(logit, axis=-1, keepdims=True)
        ix = jnp.min(jnp.where(logit == mx, lane_f, float(LANES)), axis=-1, keepdims=True)
        vals.append(mx)
        idxs.append(ix)
        logit = jnp.where(lane_f == ix, -jnp.inf, logit)
    exps = [jnp.exp(v - vals[0]) for v in vals]
    den = exps[0]
    for e in exps[1:]:
        den = den + e

    @pl.when((pl.program_id(0) == 0) & (pl.program_id(1) == 0))
    def _():
        seen_ref[...] = jnp.zeros(seen_ref.shape, F32)

    msk = jnp.zeros((tm, LANES), F32)
    for k in range(TOP_K):
        msk = jnp.where(lane_f == idxs[k], 1.0, msk)
    earlier = (lax.broadcasted_iota(I32, (tm, tm), 1) < lax.broadcasted_iota(I32, (tm, tm), 0))
    rank = _dot(jnp.where(earlier, 1.0, 0.0).astype(BF16), msk.astype(BF16)) + seen_ref[0:1, :]
    seen = seen_ref[...] + jnp.sum(msk, axis=0, keepdims=True)
    seen_ref[...] = seen
    cnt_ref[...] = seen

    r_out = jnp.zeros((tm, LANES), F32)
    g_out = jnp.zeros((tm, LANES), F32)
    for k in range(TOP_K):
        rank_k = jnp.sum(jnp.where(lane_f == idxs[k], rank, 0.0), axis=-1, keepdims=True)
        r_out = jnp.where(lane == k, idxs[k], r_out)
        r_out = jnp.where(lane == TOP_K + k, rank_k, r_out)
        g_out = jnp.where(lane == k, exps[k] / den, g_out)
    route_ref[...] = r_out.astype(I32)
    gate_ref[...] = g_out


def _post_mix(a1, a2, lane_blk2, xs, mods, g, w_out, w_router, b_router, t_off, a_off):
    split_stream = isinstance(xs, tuple)
    if split_stream:
        assert t_off == 0
        b, s_len, d = xs[1].shape
        t = xs[0].shape[1] + s_len
        x_specs = _stream_specs(d)
    else:
        b, t, d = xs.shape
        x_specs = [pl.BlockSpec((None, TOK_TILE, d), lambda bi, ti: (bi, ti + t_off, 0))]
        xs = (xs,)
    half = w_out.shape[0] // 2
    nt = t // TOK_TILE - t_off
    t_out = nt * TOK_TILE
    row = functools.partial(_mod_row, ctx_row=b)
    tok = lambda bi, ti: (bi, ti, 0)
    act = jax.ShapeDtypeStruct((b, t_out, d), F32)
    return pl.pallas_call(
        functools.partial(_post_mix_kernel, split_stream=split_stream),
        out_shape=(act, jax.ShapeDtypeStruct((b, t_out, d // 2), U32),
                   jax.ShapeDtypeStruct((b, t_out, LANES), I32),
                   jax.ShapeDtypeStruct((b, t_out, LANES), F32),
                   jax.ShapeDtypeStruct((SUBLANES, LANES), F32)),
        grid=(b, nt),
        in_specs=[pl.BlockSpec((None, TOK_TILE, half), lambda bi, ti: (bi, ti + a_off, 0)),
                  pl.BlockSpec((None, TOK_TILE, half), lambda bi, ti: (bi, ti + a_off, lane_blk2))] + x_specs + [
                  pl.BlockSpec((None, 6, d), lambda bi, ti: (row(bi, ti + t_off), 0, 0)),
                  pl.BlockSpec((1, d), lambda bi, ti: (0, 0)),
                  pl.BlockSpec(w_out.shape, lambda bi, ti: (0, 0)),
                  pl.BlockSpec(w_router.shape, lambda bi, ti: (0, 0)),
                  pl.BlockSpec((1, LANES), lambda bi, ti: (0, 0))],
        out_specs=(pl.BlockSpec((None, TOK_TILE, d), tok),
                   pl.BlockSpec((None, TOK_TILE, d // 2), tok),
                   pl.BlockSpec((None, TOK_TILE, LANES), tok),
                   pl.BlockSpec((None, TOK_TILE, LANES), tok),
                   pl.BlockSpec((SUBLANES, LANES), lambda bi, ti: (0, 0))),
        scratch_shapes=[pltpu.VMEM((SUBLANES, LANES), F32)],
        compiler_params=_cparams(("arbitrary", "arbitrary")),
        name="post_mix",
    )(a1, a2, *xs, mods, g, w_out, w_router, b_router)


DMA_UNROLL = 2


def _row(ref, i):
    return ref.at[pl.ds(i, 1), :]


def _dispatch_kernel(pad_lo_ref, pad_n_ref, nu_ref, dest_ref, f_ref, xs_ref, zero_ref, sem, zsem):
    n = dest_ref.shape[0]

    blk = zero_ref.shape[0]
    n_blk = xs_ref.shape[0] // blk

    def block_copy(i):
        return pltpu.make_async_copy(zero_ref, xs_ref.at[pl.ds(pl.multiple_of(i * blk, blk), blk), :], zsem)

    def pad_rows(e, c, wait):
        lo = pad_lo_ref[e]
        head = jnp.minimum((-lo) & (SUBLANES - 1), pad_n_ref[e])

        def one(r, c2):
            copy = pltpu.make_async_copy(_row(zero_ref, 0), _row(xs_ref, lo + r), zsem)
            copy.wait() if wait else copy.start()
            return c2

        def eight(g, c2):
            r0 = pl.multiple_of(lo + head + g * SUBLANES, SUBLANES)
            copy = pltpu.make_async_copy(zero_ref.at[pl.ds(0, SUBLANES), :], xs_ref.at[pl.ds(r0, SUBLANES), :], zsem)
            copy.wait() if wait else copy.start()
            return c2

        c = lax.fori_loop(0, head, one, c)
        return lax.fori_loop(0, (pad_n_ref[e] - head) // SUBLANES, eight, c)

    @pl.when(pl.program_id(0) == 0)
    def _():
        zero_ref[...] = jnp.zeros(zero_ref.shape, zero_ref.dtype)
        lax.fori_loop(nu_ref[0], n_blk, lambda i, c: (block_copy(i).start(), c)[1], 0)
        lax.fori_loop(0, N_EXPERTS, functools.partial(pad_rows, wait=False), 0)

    @pl.when(pl.program_id(0) == pl.num_programs(0) - 1)
    def _():
        lax.fori_loop(nu_ref[0], n_blk, lambda i, c: (block_copy(i).wait(), c)[1], 0)
        lax.fori_loop(0, N_EXPERTS, functools.partial(pad_rows, wait=True), 0)


    def issue(t, c):
        for k in range(TOP_K):
            pltpu.make_async_copy(_row(f_ref, t), _row(xs_ref, dest_ref[t * TOP_K + k]), sem).start()
        return c

    lax.fori_loop(0, n // TOP_K, issue, 0, unroll=DMA_UNROLL)
    pltpu.make_async_copy(xs_ref.at[pl.ds(0, n), :], xs_ref.at[pl.ds(0, n), :], sem).wait()


def _dispatch(f, dest, pad_lo, pad_n, n_used, rows):
    n, d = f.shape
    per = TOK_TILE * TOP_K
    return pl.pallas_call(
        _dispatch_kernel,
        out_shape=jax.ShapeDtypeStruct((rows, d), f.dtype),
        grid_spec=pltpu.PrefetchScalarGridSpec(
            num_scalar_prefetch=3,
            grid=(n // TOK_TILE,),
            in_specs=[pl.BlockSpec((per,), lambda i, lo, cnt, nu: (i,), memory_space=pltpu.SMEM),
                      pl.BlockSpec((TOK_TILE, d), lambda i, lo, cnt, nu: (i, 0))],
            out_specs=pl.BlockSpec(memory_space=pl.ANY),
            scratch_shapes=[pltpu.VMEM((MOE_ROWS, d), f.dtype), pltpu.SemaphoreType.DMA(()),
                            pltpu.SemaphoreType.DMA(())]),
        compiler_params=_cparams(("arbitrary",), has_side_effects=True, disable_bounds_checks=True),
        name="moe_dispatch",
    )(pad_lo, pad_n, n_used, dest, f)


def _experts_kernel(be_ref, nu_ref, grp_ref, nxt_ref, xs_ref, wgu_hbm, bgu_ref, wdn_hbm, bdn_ref, y_ref,
                    wgu_f32, wdn_f32, wgu_bf, wdn_bf, wsem, *, layer):
    i = pl.program_id(0)

    def fetch(e, slot, wait):
        for src, dst in ((wgu_hbm.at[layer, e], wgu_f32.at[slot]), (wdn_hbm.at[layer, e], wdn_f32.at[slot])):
            c = pltpu.make_async_copy(src, dst, wsem.at[slot])
            c.wait() if wait else c.start()

    @pl.when(i == 0)
    def _():
        fetch(be_ref[0], 0, False)

    @pl.when((i < nu_ref[0]) & ((i == 0) | (grp_ref[i] != grp_ref[jnp.maximum(i - 1, 0)])))
    def _():
        slot = grp_ref[i] % 2
        fetch(be_ref[i], slot, True)
        wgu_bf[...] = wgu_f32[slot].astype(BF16)
        wdn_bf[...] = wdn_f32[slot].astype(BF16)

        @pl.when(nxt_ref[i] >= 0)
        def _():
            fetch(nxt_ref[i], 1 - slot, False)

    @pl.when(i < nu_ref[0])
    def _():
        h = _dot(_unpack_bf16_pairs(xs_ref[...]).astype(BF16), wgu_bf[...]) + bgu_ref[...]
        ff = h.shape[1] // 2
        hg = jnp.minimum(h[:, :ff], SWIGLU_LIMIT)
        hu = jnp.clip(h[:, ff:], -SWIGLU_LIMIT, SWIGLU_LIMIT)
        act = hg * jax.nn.sigmoid(SWIGLU_ALPHA * hg) * (hu + 1.0)
        y_ref[...] = _pack_bf16_pairs(_dot(act.astype(BF16), wdn_bf[...]) + bdn_ref[...])

    @pl.when(i >= nu_ref[0])
    def _():
        y_ref[...] = jnp.zeros(y_ref.shape, y_ref.dtype)


def _experts(xs, block_e, n_used, layer, wgu, bgu, wdn, bdn):
    rows, dp = xs.shape
    d = 2 * dp
    ff2 = wgu.shape[-1]
    nb = rows // MOE_ROWS
    blk = jnp.arange(nb)
    used = blk < n_used[0]
    first = used & ((blk == 0) | (block_e != jnp.roll(block_e, 1)))
    grp = jnp.cumsum(first.astype(I32)) - 1
    later_first = jnp.where(first[None, :] & (blk[None, :] > blk[:, None]), blk[None, :], nb)
    nxt_blk = jnp.min(later_first, axis=1)
    nxt = jnp.where(nxt_blk < nb, block_e[jnp.minimum(nxt_blk, nb - 1)], -1)
    bias = lambda i, be, nu, g, nx: (layer, be[i], 0, 0)
    return pl.pallas_call(
        functools.partial(_experts_kernel, layer=layer),
        out_shape=jax.ShapeDtypeStruct((rows, dp), U32),
        grid_spec=pltpu.PrefetchScalarGridSpec(
            num_scalar_prefetch=4,
            grid=(nb,),
            in_specs=[pl.BlockSpec((MOE_ROWS, dp),
                                   lambda i, be, nu, g, nx: (jnp.maximum(jnp.minimum(i, nu[0] - 1), 0), 0)),
                      pl.BlockSpec(memory_space=pl.ANY),
                      pl.BlockSpec((None, None, 1, ff2), bias),
                      pl.BlockSpec(memory_space=pl.ANY),
                      pl.BlockSpec((None, None, 1, d), bias)],
            out_specs=pl.BlockSpec((MOE_ROWS, dp), lambda i, be, nu, g, nx: (i, 0)),
            scratch_shapes=[pltpu.VMEM((2, d, ff2), F32), pltpu.VMEM((2, ff2 // 2, d), F32),
                            pltpu.VMEM((d, ff2), BF16), pltpu.VMEM((ff2 // 2, d), BF16),
                            pltpu.SemaphoreType.DMA((2,))]),
        compiler_params=_cparams(("arbitrary",), has_side_effects=True),
        name="moe_experts",
    )(block_e, n_used, grp.astype(I32), nxt.astype(I32), xs, wgu, bgu, wdn, bdn)


def _combine_kernel(dest_ref, y_ref, gate_ref, x_ref, mod_ref, o_ref, buf_ref, sem):
    n = dest_ref.shape[0]
    tm = n // TOP_K

    def issue(t, c):
        for k in range(TOP_K):
            pltpu.make_async_copy(_row(y_ref, dest_ref[t * TOP_K + k]), _row(buf_ref.at[k], t), sem).start()
        return c

    lax.fori_loop(0, tm, issue, 0, unroll=DMA_UNROLL)
    for k in range(TOP_K):
        pltpu.make_async_copy(y_ref.at[pl.ds(0, tm), :], buf_ref.at[k], sem).wait()

    gates = gate_ref[...]
    acc = gates[:, 0:1] * _unpack_bf16_pairs(buf_ref[0])
    for k in range(1, TOP_K):
        acc = acc + gates[:, k:k + 1] * _unpack_bf16_pairs(buf_ref[k])
    o_ref[...] = x_ref[...] + mod_ref[5:6, :] * acc


def _combine(y, dest, gates, xs, mods, t_off):
    b, t_out, d = xs.shape
    nt = t_out // TOK_TILE
    per = TOK_TILE * TOP_K
    row = functools.partial(_mod_row, ctx_row=b)
    tok = lambda bi, ti: (bi, ti, 0)
    return pl.pallas_call(
        _combine_kernel,
        out_shape=jax.ShapeDtypeStruct((b, t_out, d), F32),
        grid=(b, nt),
        in_specs=[pl.BlockSpec((per,), lambda bi, ti: (bi * nt + ti,), memory_space=pltpu.SMEM),
                  pl.BlockSpec(memory_space=pl.ANY),
                  pl.BlockSpec((None, TOK_TILE, LANES), tok),
                  pl.BlockSpec((None, TOK_TILE, d), tok),
                  pl.BlockSpec((None, 6, d), lambda bi, ti: (row(bi, ti + t_off), 0, 0))],
        out_specs=pl.BlockSpec((None, TOK_TILE, d), tok),
        scratch_shapes=[pltpu.VMEM((TOP_K, TOK_TILE, y.shape[1]), y.dtype), pltpu.SemaphoreType.DMA(())],
        compiler_params=_cparams(("arbitrary", "arbitrary"), disable_bounds_checks=True),
        name="moe_combine",
    )(dest, y, gates, xs, mods)


def _routing(e_sel, rank, counts, n):
    padded = (counts + MOE_ROWS - 1) // MOE_ROWS * MOE_ROWS
    pend = jnp.cumsum(padded)
    pstart = pend - padded
    dest = jnp.sum(jnp.where(e_sel[..., None] == jnp.arange(N_EXPERTS), pstart, 0), axis=-1) + rank
    n_blocks = n * TOP_K // MOE_ROWS + N_EXPERTS
    first_row = jnp.arange(n_blocks) * MOE_ROWS
    block_e = jnp.minimum(jnp.sum(pend[None, :] <= first_row[:, None], axis=1), N_EXPERTS - 1)
    n_used = (pend[-1] // MOE_ROWS).reshape(1)
    pads = ((pstart + counts).astype(I32), (padded - counts).astype(I32))
    return dest.reshape(-1).astype(I32), block_e.astype(I32), n_used.astype(I32), pads, n_blocks * MOE_ROWS


def _moe(f, route, gates, seen, x_mid, mods, t_off, layer, wgu, bgu, wdn, bdn):
    b, t_out, d = f.shape
    n = b * t_out
    route = route.reshape(n, LANES)
    counts = seen[0, :N_EXPERTS].astype(I32)
    dest, block_e, n_used, pads, rows = _routing(route[:, :TOP_K], route[:, TOP_K:2 * TOP_K], counts, n)
    xs = _dispatch(f.reshape(n, d), dest, *pads, n_used, rows)
    y = _experts(xs, block_e, n_used, layer, wgu, bgu, wdn, bdn)
    return _combine(y, dest, gates, x_mid, mods, t_off)


def _axial_angles(n_rows, rot_dim):
    row = np.repeat(np.arange(n_rows, dtype=np.float32), GRID_W)
    col = np.tile(np.arange(GRID_W, dtype=np.float32), n_rows)
    n = rot_dim // 4
    freqs = (np.float32(ROPE_THETA) ** (-np.arange(n, dtype=np.float32) / np.float32(n))).astype(np.float32)
    return np.concatenate([row[:, None] * freqs, col[:, None] * freqs], axis=-1).astype(np.float32)


def _rope_tables(ctx_len, s_len, rot_dim, lane_base, reps):
    ang = _axial_angles(s_len // GRID_W, rot_dim)
    half = rot_dim // 2
    period = LANES // reps
    cos = np.ones((ctx_len + s_len, period), np.float32)
    slo = np.zeros((ctx_len + s_len, period), np.float32)
    shi = np.zeros((ctx_len + s_len, period), np.float32)
    cos[ctx_len:, lane_base:lane_base + rot_dim] = np.tile(np.cos(ang), (1, 2))
    slo[ctx_len:, lane_base:lane_base + half] = -np.sin(ang)
    shi[ctx_len:, lane_base + half:lane_base + rot_dim] = np.sin(ang)
    return [jnp.asarray(np.tile(tb, (1, reps))) for tb in (cos, slo, shi)]


def _segment_mean_matrix(width, period, segs):
    lane = np.arange(width)
    seg_id = np.full((width,), -1)
    seg_w = np.zeros((width,), np.float32)
    for i, (start, length) in enumerate(segs):
        inside = ((lane % period) >= start) & ((lane % period) < start + length)
        seg_id = np.where(inside, (lane // period) * len(segs) + i, seg_id)
        seg_w = np.where(inside, np.float32(1.0 / length), seg_w)
    same = (seg_id[:, None] == seg_id[None, :]) & (seg_id[:, None] >= 0)
    return jnp.asarray(np.where(same, seg_w[None, :], np.float32(0.0)), dtype=BF16)


def _block_diag(w):
    n, c, _ = w.shape
    eye = jnp.eye(n, dtype=w.dtype)
    return (eye[:, None, :, None] * w[:, :, None, :]).reshape(n * c, n * c)


def kernel(x, c, ctx, c_ctx, w_mod, b_mod, norm_mix, norm_ffn, w_in_even, lru_conv_w, lru_conv_b, lru_w_r, lru_b_r, lru_w_i, lru_b_i, lru_lambda, mla_q_a_norm, mla_w_q_b, mla_kv_a_norm, mla_w_kv_b, mla_nope_norm, mla_rope_norm, w_out_even, w_qkv_odd, gqa_qk_norm, gqa_sink, w_out_odd, w_router, b_router, w_gate_up, b_gate_up, w_down, b_down):
    b, s_len, d = x.shape
    ctx_len = ctx.shape[1]
    depth = w_mod.shape[0]
    assert depth == 2 and ctx_len == TOK_TILE and s_len % TOK_TILE == 0 and b + 1 <= SUBLANES
    lru_w = lru_conv_w.shape[-1]
    q_lora = mla_q_a_norm.shape[-1]
    kv_lora = mla_kv_a_norm.shape[-1]

    cvec = jnp.concatenate([c, c_ctx[None], jnp.zeros((SUBLANES - b - 1, d), F32)], axis=0)
    mods = _modulation(cvec, w_mod, b_mod).reshape(depth, SUBLANES, 6, d)

    wr_pad = jnp.zeros((depth, d, LANES), F32).at[:, :, :N_EXPERTS].set(w_router)
    br_pad = jnp.full((depth, 1, LANES), NEG_INF, F32).at[:, 0, :N_EXPERTS].set(b_router)
    experts = (w_gate_up, b_gate_up[:, :, None, :], w_down, b_down[:, :, None, :])

    n_in = w_in_even.shape[-1]
    n_in_pad = -(-n_in // LANES) * LANES
    w_in = jnp.zeros((d, n_in_pad), F32).at[:, :n_in].set(w_in_even[0]).astype(BF16)
    xa, ga, mla_in = _even_in(ctx, x, mods[0], norm_mix[0][None], w_in, lru_w)

    nh = lru_w // (2 * LANES)
    per = LRU_BLOCKS // nh
    blk = lru_w // LRU_BLOCKS
    w_gates = jnp.stack([
        jnp.stack([jnp.concatenate([_block_diag(lru_w_r[0, dd, h * per:(h + 1) * per]),
                                    _block_diag(lru_w_i[0, dd, h * per:(h + 1) * per])], axis=1)
                   for h in range(nh)]) for dd in range(2)]).astype(BF16)
    assert blk * per == 2 * LANES
    ya = _lru(xa, ga, lru_conv_w[0], lru_conv_b[0][:, None, :], w_gates, lru_b_r[0][:, None, :],
              lru_b_i[0][:, None, :], jax.nn.softplus(-lru_lambda[0])[:, None, :], ctx_len)

    qk = MLA_NOPE + MLA_ROPE
    hq = MLA_HEADS * LANES
    wq = jnp.zeros((q_lora, MLA_HEADS, LANES), F32).at[:, :, :qk].set(
        mla_w_q_b[0].reshape(q_lora, MLA_HEADS, qk)).reshape(q_lora, hq).astype(BF16)
    wkv = mla_w_kv_b[0].reshape(kv_lora, MLA_HEADS, MLA_NOPE + MLA_V)
    wk = jnp.zeros((kv_lora, MLA_HEADS, LANES), F32).at[:, :, :MLA_NOPE].set(
        wkv[:, :, :MLA_NOPE]).reshape(kv_lora, hq).astype(BF16)
    wv = jnp.zeros((kv_lora, MLA_HEADS, LANES), F32).at[:, :, :MLA_V].set(
        wkv[:, :, MLA_NOPE:]).reshape(kv_lora, hq).astype(BF16)
    one = jnp.asarray(np.tile(np.arange(LANES) == MLA_V, MLA_HEADS)[None], dtype=F32)
    zpad = jnp.zeros((LANES - qk,), F32)
    gq = jnp.tile(jnp.concatenate([mla_nope_norm[0, 0], mla_rope_norm[0, 0], zpad]), MLA_HEADS)[None]
    gk = jnp.tile(jnp.concatenate([mla_nope_norm[0, 1], jnp.zeros((LANES - MLA_NOPE,), F32)]), MLA_HEADS)[None]
    gkr = jnp.concatenate([mla_rope_norm[0, 1], jnp.zeros((LANES - MLA_ROPE,), F32)])[None]
    cos, slo, shi = _rope_tables(ctx_len, s_len, MLA_ROPE, MLA_NOPE, 1)
    mla_p = dict(qan=mla_q_a_norm[0][None], kvn=mla_kv_a_norm[0][None], wq=wq, wk=wk, wv=wv,
                 mq=_segment_mean_matrix(MXU_TILE,LANES, [(0, MLA_NOPE), (MLA_NOPE, MLA_ROPE)]),
                 mk=_segment_mean_matrix(MXU_TILE,LANES, [(0, MLA_NOPE)]),
                 gq=gq, gk=gk, gkr=gkr, one=one, cos=cos, slo=slo, shi=shi)
    q, k, v = _mla_prep(mla_in, mla_p, qk ** -0.5 * math.log2(math.e))
    yb = _mla_attn(q, k, v, ctx_len)

    x_mid, f, route, gates, seen = _post_mix(ya, yb, 0, (ctx, x), mods[0], norm_ffn[0][None], w_out_even[0].astype(BF16),
                                             wr_pad[0], br_pad[0], 0, 0)
    xs = _moe(f, route, gates, seen, x_mid, mods[0], 0, 0, *experts)

    nq = GQA_HEADS * GQA_DIM
    nkv = GQA_KV_HEADS * GQA_DIM
    cos, slo, shi = _rope_tables(ctx_len, s_len, GQA_DIM, 0, LANES // GQA_DIM)
    odd_p = dict(w=w_qkv_odd[0].astype(BF16),
                 mq=_segment_mean_matrix(MXU_TILE,GQA_DIM, [(0, GQA_DIM)]),
                 mk=_segment_mean_matrix(MXU_TILE,GQA_DIM, [(0, GQA_DIM)]),
                 gq=jnp.tile(gqa_qk_norm[0, 0], GQA_HEADS)[None], gk=jnp.tile(gqa_qk_norm[0, 1], GQA_KV_HEADS)[None],
                 cos=cos, slo=slo, shi=shi)
    q, k, v = _odd_in(xs, mods[1], norm_mix[1][None], odd_p, GQA_DIM ** -0.5)
    o = _win_attn(q, k, v, gqa_sink[0], ctx_len)

    t_off = ctx_len // TOK_TILE
    x_mid, f, route, gates, seen = _post_mix(o, o, 1, xs, mods[1], norm_ffn[1][None], w_out_odd[0].astype(BF16),
                                             wr_pad[1], br_pad[1], t_off, 0)
    return _moe(f, route, gates, seen, x_mid, mods[1], t_off, 1, *experts)
```

```python
import functools
import math

import jax
import jax.numpy as jnp
import numpy as np
from jax import lax
from jax.experimental import pallas as pl
from jax.experimental.pallas import tpu as pltpu

F32 = jnp.float32
BF16 = jnp.bfloat16
I32 = jnp.int32

GRID_W = 64
LRU_BLOCKS = 8
LRU_C = 8.0
CONV_W = 4
MLA_HEADS = 8
MLA_NOPE = 64
MLA_ROPE = 32
MLA_V = 64
GQA_HEADS = 16
GQA_KV_HEADS = 4
GQA_DIM = 64
WINDOW = 128
ROPE_THETA = 10000.0
NEG_INF = -1e30
EPS = 1e-6
N_EXPERTS = 32
TOP_K = 4
SWIGLU_LIMIT = 7.0
SWIGLU_ALPHA = 1.702

LANES = 128
SUBLANES = 8
TOK_TILE = 256
LRU_CHUNK = 128
WIN_BLOCK = 128
MOE_ROWS = 256
MXU_TILE = 256
MLA_HEADS_PER_STEP = 4
VMEM_LIMIT = 48 * 1024 * 1024


def _cparams(sem, **kw):
    return pltpu.CompilerParams(dimension_semantics=sem, vmem_limit_bytes=VMEM_LIMIT, **kw)


def _dot(a, b):
    return jnp.dot(a, b, preferred_element_type=F32)


def _dot_nt(a, b):
    return lax.dot_general(a, b, (((1,), (1,)), ((), ())), preferred_element_type=F32)


def _split_bf16(x):
    hi = x.astype(BF16)
    lo = (x - hi.astype(F32)).astype(BF16)
    return hi, lo


def _dot3(a, w):
    ah, al = _split_bf16(a)
    wh, wl = _split_bf16(w)
    return _dot(ah, wh) + _dot(al, wh) + _dot(ah, wl)


def _rms(x):
    return x * lax.rsqrt(jnp.mean(x * x, axis=-1, keepdims=True) + EPS)


def _prenorm(x, g, scale, shift):
    return (_rms(x) * g) * (1.0 + scale) + shift


def _rope(x, cos, sin_lo, sin_hi, half):
    w = x.shape[-1]
    return x * cos + pltpu.roll(x, w - half, axis=1) * sin_lo + pltpu.roll(x, half, axis=1) * sin_hi


U32 = jnp.uint32
HI_HALF = 0xFFFF0000


def _pack_bf16_pairs(x):
    w = x.shape[1] // 2
    lo = pltpu.bitcast(x[:, :w].astype(BF16).astype(F32), U32) >> 16
    hi = pltpu.bitcast(x[:, w:].astype(BF16).astype(F32), U32) & U32(HI_HALF)
    return lo | hi


def _unpack_bf16_pairs(words):
    lo = pltpu.bitcast(words << 16, F32)
    hi = pltpu.bitcast(words & U32(HI_HALF), F32)
    return jnp.concatenate([lo, hi], axis=1)


def _head_mean_square(y, m_ref):
    w = m_ref.shape[0]
    return jnp.concatenate([_dot((y[:, j:j + w] * y[:, j:j + w]).astype(BF16), m_ref[...])
                            for j in range(0, y.shape[1], w)], axis=1)


def _tile_lanes(t, reps):
    return jnp.concatenate([t] * reps, axis=1) if reps > 1 else t


def _mod_kernel(c_ref, w_ref, b_ref, o_ref):
    c = c_ref[...]
    o_ref[...] = _dot3(c * jax.nn.sigmoid(c), w_ref[...]) + b_ref[...]


def _modulation(cvec, w_mod, b_mod):
    depth, d, n = w_mod.shape
    tn = 1536
    return pl.pallas_call(
        _mod_kernel,
        out_shape=jax.ShapeDtypeStruct((depth, SUBLANES, n), F32),
        grid=(depth, n // tn),
        in_specs=[pl.BlockSpec((SUBLANES, d), lambda l, j: (0, 0)),
                  pl.BlockSpec((None, d, tn), lambda l, j: (l, 0, j)),
                  pl.BlockSpec((None, 1, tn), lambda l, j: (l, 0, j))],
        out_specs=pl.BlockSpec((None, SUBLANES, tn), lambda l, j: (l, 0, j)),
        compiler_params=_cparams(("arbitrary", "arbitrary")),
        name="modulation",
    )(cvec, w_mod, b_mod.reshape(depth, 1, n))


def _mod_row(b, t, ctx_row):
    return jnp.where(t == 0, ctx_row, b)


def _stream_tile(ctx_ref, lat_ref):
    return jnp.where(pl.program_id(1) == 0, ctx_ref[...], lat_ref[...])


def _stream_specs(d):
    return [pl.BlockSpec((None, TOK_TILE, d), lambda bi, ti: (bi, 0, 0)),
            pl.BlockSpec((None, TOK_TILE, d), lambda bi, ti: (bi, jnp.maximum(ti - 1, 0), 0))]


def _even_in_kernel(ctx_ref, x_ref, mod_ref, g_ref, w_ref, xa_ref, ga_ref, mla_ref):
    mod = mod_ref[...]
    h = _prenorm(_stream_tile(ctx_ref, x_ref), g_ref[...], mod[1:2], mod[0:1])
    z = _dot(h.astype(BF16), w_ref[...])
    c = xa_ref.shape[-1]
    xa_ref[...] = z[:, :c]
    ga_ref[...] = z[:, c:2 * c]
    mla_ref[...] = z[:, 2 * c:]


def _even_in(ctx, x, mods, g, w_pad, lru_w):
    b, s_len, d = x.shape
    t = ctx.shape[1] + s_len
    nt = t // TOK_TILE
    n_out = w_pad.shape[1]
    n_mla = n_out - 2 * lru_w
    row = functools.partial(_mod_row, ctx_row=b)
    tok = lambda bi, ti: (bi, ti, 0)
    return pl.pallas_call(
        _even_in_kernel,
        out_shape=(jax.ShapeDtypeStruct((b, t, lru_w), F32),
                   jax.ShapeDtypeStruct((b, t, lru_w), F32),
                   jax.ShapeDtypeStruct((b, t, n_mla), F32)),
        grid=(b, nt),
        in_specs=_stream_specs(d) + [
            pl.BlockSpec((None, 6, d), lambda bi, ti: (row(bi, ti), 0, 0)),
            pl.BlockSpec((1, d), lambda bi, ti: (0, 0)),
            pl.BlockSpec((d, n_out), lambda bi, ti: (0, 0))],
        out_specs=(pl.BlockSpec((None, TOK_TILE, lru_w), tok),
                   pl.BlockSpec((None, TOK_TILE, lru_w), tok),
                   pl.BlockSpec((None, TOK_TILE, n_mla), tok)),
        compiler_params=_cparams(("arbitrary", "arbitrary")),
        name="even_in",
    )(ctx, x, mods, g, w_pad)


def _lru_kernel(xa_ref, ga_ref, cw_ref, cb_ref, wg_ref, br_ref, bi_ref, sp_ref, o_ref, pad_ref, rec_ref, *, ctx_len):
    t, c = xa_ref.shape
    tc = LRU_CHUNK
    halo = SUBLANES
    n_chunks = t // tc
    n_ctx = ctx_len // tc
    groups = tc // SUBLANES

    pad_ref[0:halo, :] = jnp.zeros((halo, c), F32)
    pad_ref[t + halo:t + 2 * halo, :] = jnp.zeros((halo, c), F32)
    pad_ref[halo:t + halo, :] = xa_ref[...]

    rid = lax.broadcasted_iota(I32, (tc, 1), 0)
    sub = rid % SUBLANES

    for d in range(2):
        cw = cw_ref[d]
        cb = cb_ref[d]
        wg = wg_ref[d]
        b_r = br_ref[d]
        b_i = bi_ref[d]
        sp = sp_ref[d]

        def chunk(i, h, d=d, cw=cw, cb=cb, wg=wg, b_r=b_r, b_i=b_i, sp=sp):
            if d == 0:
                ci = i
            else:
                ci = jnp.where(i < n_ctx, n_ctx - 1 - i, n_chunks - 1 - (i - n_ctx))
            r0 = pl.multiple_of(ci * tc, tc)
            win = pad_ref[pl.ds(r0, tc + 2 * halo), :]
            if d == 0:
                past = jnp.where(ci == n_ctx, 0.0, win[0:halo])
                win = jnp.concatenate([past, win[halo:]], axis=0)
            else:
                past = jnp.where(ci == n_ctx - 1, 0.0, win[halo + tc:])
                win = jnp.concatenate([win[:halo + tc], past], axis=0)
            xc = jnp.zeros((tc, c), F32) + cb
            for k in range(CONV_W):
                off = (k - (CONV_W - 1)) if d == 0 else ((CONV_W - 1) - k)
                if off == 0:
                    src = win[halo:halo + tc]
                else:
                    src = pltpu.roll(win, (-off) % (tc + 2 * halo), axis=0)[halo:halo + tc]
                xc = xc + cw[k:k + 1] * src
            gz = _dot(xc.astype(BF16), wg)
            r = 0.5 * jnp.tanh(0.5 * (gz[:, :c] + b_r)) + 0.5
            gi = 0.5 * jnp.tanh(0.5 * (gz[:, c:] + b_i)) + 0.5
            log_a = (-LRU_C) * r * sp
            a = jnp.exp(log_a)
            th = jnp.tanh(log_a)
            bb = jnp.sqrt(-2.0 * th / (1.0 - th)) * (gi * xc)
            for s in (1, 2, 4):
                if d == 0:
                    ok = sub >= s
                    sh = s
                else:
                    ok = sub <= (SUBLANES - 1 - s)
                    sh = tc - s
                a_prev = jnp.where(ok, pltpu.roll(a, sh, axis=0), 1.0)
                b_prev = jnp.where(ok, pltpu.roll(bb, sh, axis=0), 0.0)
                bb = a * b_prev + bb
                a = a * a_prev
            outs = [None] * groups
            order = range(groups) if d == 0 else range(groups - 1, -1, -1)
            for g in order:
                lo = g * SUBLANES
                hg = a[lo:lo + SUBLANES] * h + bb[lo:lo + SUBLANES]
                outs[g] = hg
                h = hg[SUBLANES - 1:SUBLANES] if d == 0 else hg[0:1]
            hs = jnp.concatenate(outs, axis=0)
            if d == 0:
                rec_ref[pl.ds(r0, tc), :] = hs
            else:
                tot = rec_ref[pl.ds(r0, tc), :] + hs
                gate = jax.nn.gelu(ga_ref[pl.ds(r0, tc), :], approximate=True)
                o_ref[pl.ds(r0, tc), :] = (tot * gate).astype(o_ref.dtype)
            return h

        lax.fori_loop(0, n_chunks, chunk, jnp.zeros((1, c), F32))


def _lru(xa, ga, conv_w, conv_b, w_gates, b_r, b_i, sp, ctx_len):
    b, t, w = xa.shape
    c = 2 * LANES
    nh = w // c
    tok = lambda bi, hi: (bi, 0, hi)
    par = lambda bi, hi: (0, 0, hi)
    return pl.pallas_call(
        functools.partial(_lru_kernel, ctx_len=ctx_len),
        out_shape=jax.ShapeDtypeStruct((b, t, w), BF16),
        grid=(b, nh),
        in_specs=[pl.BlockSpec((None, t, c), tok),
                  pl.BlockSpec((None, t, c), tok),
                  pl.BlockSpec((2, CONV_W, c), par),
                  pl.BlockSpec((2, 1, c), par),
                  pl.BlockSpec((2, None, c, 2 * c), lambda bi, hi: (0, hi, 0, 0)),
                  pl.BlockSpec((2, 1, c), par),
                  pl.BlockSpec((2, 1, c), par),
                  pl.BlockSpec((2, 1, c), par)],
        out_specs=pl.BlockSpec((None, t, c), tok),
        scratch_shapes=[pltpu.VMEM((t + 2 * SUBLANES, c), F32), pltpu.VMEM((t, c), F32)],
        compiler_params=_cparams(("arbitrary", "arbitrary")),
        name="rglru",
    )(xa, ga, conv_w, conv_b, w_gates, b_r, b_i, sp)


def _mla_prep_kernel(in_ref, qan_ref, kvn_ref, wq_ref, wk_ref, wv_ref, mq_ref, mk_ref, gq_ref, gk_ref, gkr_ref,
                     one_ref, cos_ref, slo_ref, shi_ref, q_ref, k_ref, v_ref, *, q_lora, kv_lora, scale):
    z = in_ref[...]
    heads = q_ref.shape[-1] // LANES
    cos = cos_ref[...]
    slo = slo_ref[...]
    shi = shi_ref[...]
    half = MLA_ROPE // 2

    qan = (_rms(z[:, :q_lora]) * qan_ref[...]).astype(BF16)
    q = _dot(qan, wq_ref[...])
    q = q * lax.rsqrt(_head_mean_square(q, mq_ref) + EPS) * gq_ref[...]
    q = _rope(q, _tile_lanes(cos, heads), _tile_lanes(slo, heads), _tile_lanes(shi, heads), half)
    q_ref[...] = (q * scale).astype(BF16)

    kvn = (_rms(z[:, q_lora:q_lora + kv_lora]) * kvn_ref[...]).astype(BF16)
    kk = _dot(kvn, wk_ref[...])
    kk = kk * lax.rsqrt(_head_mean_square(kk, mk_ref) + EPS) * gk_ref[...]
    v_ref[...] = (_dot(kvn, wv_ref[...]) + one_ref[...]).astype(BF16)

    kr = z[:, q_lora + kv_lora:]
    kr = kr * lax.rsqrt(jnp.sum(kr * kr, axis=-1, keepdims=True) * (1.0 / MLA_ROPE) + EPS) * gkr_ref[...]
    kr = pltpu.roll(kr, MLA_NOPE, axis=1)
    kr = _rope(kr, cos, slo, shi, half)
    k_ref[...] = (kk + _tile_lanes(kr, heads)).astype(BF16)


def _mla_prep(mla_in, p, scale):
    b, t, w = mla_in.shape
    hq = MLA_HEADS * LANES
    tok = lambda bi, ti: (bi, ti, 0)
    full = lambda a: pl.BlockSpec(a.shape, lambda bi, ti: (0,) * a.ndim)
    pos = pl.BlockSpec((TOK_TILE, LANES), lambda bi, ti: (ti, 0))
    consts = [p["qan"], p["kvn"], p["wq"], p["wk"], p["wv"], p["mq"], p["mk"], p["gq"], p["gk"], p["gkr"], p["one"]]
    out = jax.ShapeDtypeStruct((b, t, hq), BF16)
    return pl.pallas_call(
        functools.partial(_mla_prep_kernel, q_lora=p["qan"].shape[1], kv_lora=p["kvn"].shape[1], scale=scale),
        out_shape=(out, out, out),
        grid=(b, t // TOK_TILE),
        in_specs=[pl.BlockSpec((None, TOK_TILE, w), tok)] + [full(a) for a in consts] + [pos, pos, pos],
        out_specs=(pl.BlockSpec((None, TOK_TILE, hq), tok),) * 3,
        compiler_params=_cparams(("arbitrary", "arbitrary")),
        name="mla_prep",
    )(mla_in, *consts, p["cos"], p["slo"], p["shi"])


def _mla_attn_kernel(q_ref, k_ref, v_ref, o_ref, *, ctx_len):
    tq = q_ref.shape[0]
    t = k_ref.shape[0]
    lane = lax.broadcasted_iota(I32, (tq, LANES), 1)

    def attend(nk):
        outs = []
        for hh in range(q_ref.shape[1] // LANES):
            q = q_ref[:, hh * LANES:(hh + 1) * LANES]
            k = k_ref[0:nk, hh * LANES:(hh + 1) * LANES]
            s = _dot_nt(q, k)
            p = jnp.exp2(s - jnp.max(s, axis=-1, keepdims=True))
            o = _dot(p.astype(BF16), v_ref[0:nk, hh * LANES:(hh + 1) * LANES])
            outs.append(o / o[:, MLA_V:MLA_V + 1])
        pairs = [jnp.where(lane < MLA_V, outs[j], pltpu.roll(outs[j + 1], MLA_V, axis=1))
                 for j in range(0, len(outs), 2)]
        o_ref[...] = jnp.concatenate(pairs, axis=1).astype(o_ref.dtype)

    @pl.when(pl.program_id(2) == 0)
    def _():
        attend(ctx_len)

    @pl.when(pl.program_id(2) > 0)
    def _():
        attend(t)


def _mla_attn(q, k, v, ctx_len):
    b, t, hq = q.shape
    hps = MLA_HEADS_PER_STEP
    groups = hq // (hps * LANES)
    return pl.pallas_call(
        functools.partial(_mla_attn_kernel, ctx_len=ctx_len),
        out_shape=jax.ShapeDtypeStruct((b, t, groups * hps * MLA_V), BF16),
        grid=(b, groups, t // TOK_TILE),
        in_specs=[pl.BlockSpec((None, TOK_TILE, hps * LANES), lambda bi, hi, ti: (bi, ti, hi)),
                  pl.BlockSpec((None, t, hps * LANES), lambda bi, hi, ti: (bi, 0, hi)),
                  pl.BlockSpec((None, t, hps * LANES), lambda bi, hi, ti: (bi, 0, hi))],
        out_specs=pl.BlockSpec((None, TOK_TILE, hps * MLA_V), lambda bi, hi, ti: (bi, ti, hi)),
        compiler_params=_cparams(("arbitrary", "arbitrary", "arbitrary")),
        name="mla_attn",
    )(q, k, v)


def _odd_in_kernel(x_ref, mod_ref, g_ref, w_ref, mq_ref, mk_ref, gq_ref, gk_ref, cos_ref, slo_ref, shi_ref,
                   q_ref, k_ref, v_ref, *, scale):
    mod = mod_ref[...]
    h = _prenorm(x_ref[...], g_ref[...], mod[1:2], mod[0:1]).astype(BF16)
    nq = q_ref.shape[-1]
    nk = k_ref.shape[-1]
    half = GQA_DIM // 2

    def head_norm_rope(y, m_ref, gain_ref):
        reps = y.shape[-1] // LANES
        y = y * lax.rsqrt(_head_mean_square(y, m_ref) + EPS) * gain_ref[...]
        return _rope(y, _tile_lanes(cos_ref[...], reps), _tile_lanes(slo_ref[...], reps),
                     _tile_lanes(shi_ref[...], reps), half)

    q_ref[...] = (head_norm_rope(_dot(h, w_ref[:, 0:nq]), mq_ref, gq_ref) * scale).astype(BF16)
    k_ref[...] = head_norm_rope(_dot(h, w_ref[:, nq:nq + nk]), mk_ref, gk_ref).astype(BF16)
    v_ref[...] = _dot(h, w_ref[:, nq + nk:nq + 2 * nk]).astype(BF16)


def _odd_in(xs, mods, g, p, scale):
    b, t, d = xs.shape
    nq = p["gq"].shape[1]
    nk = p["gk"].shape[1]
    row = functools.partial(_mod_row, ctx_row=b)
    tok = lambda bi, ti: (bi, ti, 0)
    full = lambda a: pl.BlockSpec(a.shape, lambda bi, ti: (0,) * a.ndim)
    pos = pl.BlockSpec((TOK_TILE, LANES), lambda bi, ti: (ti, 0))
    kv = jax.ShapeDtypeStruct((b, t, nk), BF16)
    return pl.pallas_call(
        functools.partial(_odd_in_kernel, scale=scale),
        out_shape=(jax.ShapeDtypeStruct((b, t, nq), BF16), kv, kv),
        grid=(b, t // TOK_TILE),
        in_specs=[pl.BlockSpec((None, TOK_TILE, d), tok),
                  pl.BlockSpec((None, 6, d), lambda bi, ti: (row(bi, ti), 0, 0)),
                  pl.BlockSpec((1, d), lambda bi, ti: (0, 0)),
                  full(p["w"]), full(p["mq"]), full(p["mk"]), full(p["gq"]), full(p["gk"]), pos, pos, pos],
        out_specs=(pl.BlockSpec((None, TOK_TILE, nq), tok),
                   pl.BlockSpec((None, TOK_TILE, nk), tok),
                   pl.BlockSpec((None, TOK_TILE, nk), tok)),
        compiler_params=_cparams(("arbitrary", "arbitrary")),
        name="odd_in",
    )(xs, mods, g, p["w"], p["mq"], p["mk"], p["gq"], p["gk"], p["cos"], p["slo"], p["shi"])


def _win_attn_kernel(sink_ref, q_ref, k_ref, v_ref, o_ref, *, ctx_len):
    t = k_ref.shape[0]
    wb = WIN_BLOCK
    w3 = 3 * wb
    hd = GQA_DIM
    group = GQA_HEADS // GQA_KV_HEADS
    i = pl.program_id(1)
    q0 = ctx_len + i * wb
    ws = pl.multiple_of(jnp.clip(q0 - wb, ctx_len, t - w3), wb)
    nk = w3 + ctx_len
    rows = group * wb
    row = lax.broadcasted_iota(I32, (rows, nk), 0)
    col = lax.broadcasted_iota(I32, (rows, nk), 1)
    valid = (col >= w3) | (jnp.abs(q0 + row % wb - (ws + col)) <= WINDOW)
    head_of_row = lax.broadcasted_iota(I32, (rows, 1), 0) // wb
    q = q_ref[...].astype(F32)
    kcat = jnp.concatenate([k_ref[pl.ds(ws, w3), :], k_ref[0:ctx_len, :]], axis=0).astype(F32)
    vcat = jnp.concatenate([v_ref[pl.ds(ws, w3), :], v_ref[0:ctx_len, :]], axis=0).astype(F32)
    outs = []
    for kh in range(GQA_KV_HEADS):
        qs = jnp.concatenate([q[:, (kh * group + h) * hd:(kh * group + h + 1) * hd] for h in range(group)], axis=0)
        s = _dot_nt(qs.astype(BF16), kcat[:, kh * hd:(kh + 1) * hd].astype(BF16))
        s = jnp.where(valid, s, NEG_INF)
        sink = jnp.zeros((rows, 1), F32)
        for h in range(group):
            sink = jnp.where(head_of_row == h, sink_ref[kh * group + h], sink)
        m = jnp.maximum(jnp.max(s, axis=-1, keepdims=True), sink)
        p = jnp.exp(s - m)
        l = jnp.sum(p, axis=-1, keepdims=True) + jnp.exp(sink - m)
        o = _dot(p.astype(BF16), vcat[:, kh * hd:(kh + 1) * hd].astype(BF16)) / l
        outs.extend(o[h * wb:(h + 1) * wb] for h in range(group))
    o_ref[...] = jnp.concatenate(outs, axis=1).astype(o_ref.dtype)


def _win_attn(q, k, v, sink, ctx_len):
    b, t, n = q.shape
    nkv = k.shape[-1]
    s_len = t - ctx_len
    off = ctx_len // WIN_BLOCK
    return pl.pallas_call(
        functools.partial(_win_attn_kernel, ctx_len=ctx_len),
        out_shape=jax.ShapeDtypeStruct((b, s_len, n), BF16),
        grid=(b, s_len // WIN_BLOCK),
        in_specs=[pl.BlockSpec(memory_space=pltpu.SMEM),
                  pl.BlockSpec((None, WIN_BLOCK, n), lambda bi, ti: (bi, ti + off, 0)),
                  pl.BlockSpec((None, t, nkv), lambda bi, ti: (bi, 0, 0)),
                  pl.BlockSpec((None, t, nkv), lambda bi, ti: (bi, 0, 0))],
        out_specs=pl.BlockSpec((None, WIN_BLOCK, n), lambda bi, ti: (bi, ti, 0)),
        compiler_params=_cparams(("arbitrary", "arbitrary")),
        name="win_attn",
    )(sink, q, k, v)


def _post_mix_kernel(a1_ref, a2_ref, *refs, split_stream):
    x_in = _stream_tile(refs[0], refs[1]) if split_stream else refs[0][...]
    _post_mix_body(a1_ref, a2_ref, x_in, *refs[2 if split_stream else 1:])


def _post_mix_body(a1_ref, a2_ref, x_in, mod_ref, g_ref, w_ref, wr_ref, br_ref,
                   xo_ref, f_ref, route_ref, gate_ref, cnt_ref, seen_ref):
    half = a1_ref.shape[-1]
    mod = mod_ref[...]
    m = _dot(a1_ref[...], w_ref[0:half, :]) + _dot(a2_ref[...], w_ref[half:2 * half, :])
    x = x_in + mod[2:3] * m
    xo_ref[...] = x
    f = _prenorm(x, g_ref[...], mod[4:5], mod[3:4])
    f_ref[...] = _pack_bf16_pairs(f)

    logit = _dot(f.astype(BF16), wr_ref[...].astype(BF16)) + br_ref[...]
    tm = logit.shape[0]
    lane = lax.broadcasted_iota(I32, (tm, LANES), 1)
    lane_f = lane.astype(F32)
    vals, idxs = [], []
    for _ in range(TOP_K):
        mx = jnp.max(logit, axis=-1, keepdims=True)
        ix = jnp.min(jnp.where(logit == mx, lane_f, float(LANES)), axis=-1, keepdims=True)
        vals.append(mx)
        idxs.append(ix)
        logit = jnp.where(lane_f == ix, -jnp.inf, logit)
    exps = [jnp.exp(v - vals[0]) for v in vals]
    den = exps[0]
    for e in exps[1:]:
        den = den + e

    @pl.when((pl.program_id(0) == 0) & (pl.program_id(1) == 0))
    def _():
        seen_ref[...] = jnp.zeros(seen_ref.shape, F32)

    msk = jnp.zeros((tm, LANES), F32)
    for k in range(TOP_K):
        msk = jnp.where(lane_f == idxs[k], 1.0, msk)
    earlier = (lax.broadcasted_iota(I32, (tm, tm), 1) < lax.broadcasted_iota(I32, (tm, tm), 0))
    rank = _dot(jnp.where(earlier, 1.0, 0.0).astype(BF16), msk.astype(BF16)) + seen_ref[0:1, :]
    seen = seen_ref[...] + jnp.sum(msk, axis=0, keepdims=True)
    seen_ref[...] = seen
    cnt_ref[...] = seen

    r_out = jnp.zeros((tm, LANES), F32)
    g_out = jnp.zeros((tm, LANES), F32)
    for k in range(TOP_K):
        rank_k = jnp.sum(jnp.where(lane_f == idxs[k], rank, 0.0), axis=-1, keepdims=True)
        r_out = jnp.where(lane == k, idxs[k], r_out)
        r_out = jnp.where(lane == TOP_K + k, rank_k, r_out)
        g_out = jnp.where(lane == k, exps[k] / den, g_out)
    route_ref[...] = r_out.astype(I32)
    gate_ref[...] = g_out


def _post_mix(a1, a2, lane_blk2, xs, mods, g, w_out, w_router, b_router, t_off, a_off):
    split_stream = isinstance(xs, tuple)
    if split_stream:
        assert t_off == 0
        b, s_len, d = xs[1].shape
        t = xs[0].shape[1] + s_len
        x_specs = _stream_specs(d)
    else:
        b, t, d = xs.shape
        x_specs = [pl.BlockSpec((None, TOK_TILE, d), lambda bi, ti: (bi, ti + t_off, 0))]
        xs = (xs,)
    half = w_out.shape[0] // 2
    nt = t // TOK_TILE - t_off
    t_out = nt * TOK_TILE
    row = functools.partial(_mod_row, ctx_row=b)
    tok = lambda bi, ti: (bi, ti, 0)
    act = jax.ShapeDtypeStruct((b, t_out, d), F32)
    return pl.pallas_call(
        functools.partial(_post_mix_kernel, split_stream=split_stream),
        out_shape=(act, jax.ShapeDtypeStruct((b, t_out, d // 2), U32),
                   jax.ShapeDtypeStruct((b, t_out, LANES), I32),
                   jax.ShapeDtypeStruct((b, t_out, LANES), F32),
                   jax.ShapeDtypeStruct((SUBLANES, LANES), F32)),
        grid=(b, nt),
        in_specs=[pl.BlockSpec((None, TOK_TILE, half), lambda bi, ti: (bi, ti + a_off, 0)),
                  pl.BlockSpec((None, TOK_TILE, half), lambda bi, ti: (bi, ti + a_off, lane_blk2))] + x_specs + [
                  pl.BlockSpec((None, 6, d), lambda bi, ti: (row(bi, ti + t_off), 0, 0)),
                  pl.BlockSpec((1, d), lambda bi, ti: (0, 0)),
                  pl.BlockSpec(w_out.shape, lambda bi, ti: (0, 0)),
                  pl.BlockSpec(w_router.shape, lambda bi, ti: (0, 0)),
                  pl.BlockSpec((1, LANES), lambda bi, ti: (0, 0))],
        out_specs=(pl.BlockSpec((None, TOK_TILE, d), tok),
                   pl.BlockSpec((None, TOK_TILE, d // 2), tok),
                   pl.BlockSpec((None, TOK_TILE, LANES), tok),
                   pl.BlockSpec((None, TOK_TILE, LANES), tok),
                   pl.BlockSpec((SUBLANES, LANES), lambda bi, ti: (0, 0))),
        scratch_shapes=[pltpu.VMEM((SUBLANES, LANES), F32)],
        compiler_params=_cparams(("arbitrary", "arbitrary")),
        name="post_mix",
    )(a1, a2, *xs, mods, g, w_out, w_router, b_router)


DMA_UNROLL = 2
DISPATCH_TILE = 1024


def _row(ref, i):
    return ref.at[pl.ds(i, 1), :]


def _dispatch_kernel(pad_lo_ref, pad_n_ref, nu_ref, dest_ref, f_ref, xs_ref, zero_ref, sem, zsem):
    n = dest_ref.shape[0]

    blk = zero_ref.shape[0]
    n_blk = xs_ref.shape[0] // blk

    def block_copy(i):
        return pltpu.make_async_copy(zero_ref, xs_ref.at[pl.ds(pl.multiple_of(i * blk, blk), blk), :], zsem)

    def pad_rows(e, c, wait):
        lo = pad_lo_ref[e]
        head = jnp.minimum((-lo) & (SUBLANES - 1), pad_n_ref[e])

        def one(r, c2):
            copy = pltpu.make_async_copy(_row(zero_ref, 0), _row(xs_ref, lo + r), zsem)
            copy.wait() if wait else copy.start()
            return c2

        def eight(g, c2):
            r0 = pl.multiple_of(lo + head + g * SUBLANES, SUBLANES)
            copy = pltpu.make_async_copy(zero_ref.at[pl.ds(0, SUBLANES), :], xs_ref.at[pl.ds(r0, SUBLANES), :], zsem)
            copy.wait() if wait else copy.start()
            return c2

        c = lax.fori_loop(0, head, one, c)
        return lax.fori_loop(0, (pad_n_ref[e] - head) // SUBLANES, eight, c)

    @pl.when(pl.program_id(0) == 0)
    def _():
        zero_ref[...] = jnp.zeros(zero_ref.shape, zero_ref.dtype)
        lax.fori_loop(nu_ref[0], n_blk, lambda i, c: (block_copy(i).start(), c)[1], 0)
        lax.fori_loop(0, N_EXPERTS, functools.partial(pad_rows, wait=False), 0)

    @pl.when(pl.program_id(0) == pl.num_programs(0) - 1)
    def _():
        lax.fori_loop(nu_ref[0], n_blk, lambda i, c: (block_copy(i).wait(), c)[1], 0)
        lax.fori_loop(0, N_EXPERTS, functools.partial(pad_rows, wait=True), 0)


    def issue(t, c):
        for k in range(TOP_K):
            pltpu.make_async_copy(_row(f_ref, t), _row(xs_ref, dest_ref[t * TOP_K + k]), sem).start()
        return c

    lax.fori_loop(0, n // TOP_K, issue, 0, unroll=DMA_UNROLL)
    pltpu.make_async_copy(xs_ref.at[pl.ds(0, n), :], xs_ref.at[pl.ds(0, n), :], sem).wait()


def _dispatch(f, dest, pad_lo, pad_n, n_used, rows):
    n, d = f.shape
    tile = DISPATCH_TILE
    assert n % tile == 0
    return pl.pallas_call(
        _dispatch_kernel,
        out_shape=jax.ShapeDtypeStruct((rows, d), f.dtype),
        grid_spec=pltpu.PrefetchScalarGridSpec(
            num_scalar_prefetch=3,
            grid=(n // tile,),
            in_specs=[pl.BlockSpec((tile * TOP_K,), lambda i, lo, cnt, nu: (i,), memory_space=pltpu.SMEM),
                      pl.BlockSpec((tile, d), lambda i, lo, cnt, nu: (i, 0))],
            out_specs=pl.BlockSpec(memory_space=pl.ANY),
            scratch_shapes=[pltpu.VMEM((MOE_ROWS, d), f.dtype), pltpu.SemaphoreType.DMA(()),
                            pltpu.SemaphoreType.DMA(())]),
        compiler_params=_cparams(("arbitrary",), has_side_effects=True, disable_bounds_checks=True),
        name="moe_dispatch",
    )(pad_lo, pad_n, n_used, dest, f)


def _experts_kernel(be_ref, nu_ref, grp_ref, nxt_ref, xs_ref, wgu_hbm, bgu_ref, wdn_hbm, bdn_ref, y_ref,
                    wgu_f32, wdn_f32, wgu_bf, wdn_bf, wsem, *, layer):
    i = pl.program_id(0)

    def fetch(e, slot, wait):
        for src, dst in ((wgu_hbm.at[layer, e], wgu_f32.at[slot]), (wdn_hbm.at[layer, e], wdn_f32.at[slot])):
            c = pltpu.make_async_copy(src, dst, wsem.at[slot])
            c.wait() if wait else c.start()

    @pl.when(i == 0)
    def _():
        fetch(be_ref[0], 0, False)

    @pl.when((i < nu_ref[0]) & ((i == 0) | (grp_ref[i] != grp_ref[jnp.maximum(i - 1, 0)])))
    def _():
        slot = grp_ref[i] % 2
        fetch(be_ref[i], slot, True)
        wgu_bf[...] = wgu_f32[slot].astype(BF16)
        wdn_bf[...] = wdn_f32[slot].astype(BF16)

        @pl.when(nxt_ref[i] >= 0)
        def _():
            fetch(nxt_ref[i], 1 - slot, False)

    @pl.when(i < nu_ref[0])
    def _():
        h = _dot(_unpack_bf16_pairs(xs_ref[...]).astype(BF16), wgu_bf[...]) + bgu_ref[...]
        ff = h.shape[1] // 2
        hg = jnp.minimum(h[:, :ff], SWIGLU_LIMIT)
        hu = jnp.clip(h[:, ff:], -SWIGLU_LIMIT, SWIGLU_LIMIT)
        act = hg * jax.nn.sigmoid(SWIGLU_ALPHA * hg) * (hu + 1.0)
        y_ref[...] = _pack_bf16_pairs(_dot(act.astype(BF16), wdn_bf[...]) + bdn_ref[...])

    @pl.when(i >= nu_ref[0])
    def _():
        y_ref[...] = jnp.zeros(y_ref.shape, y_ref.dtype)


def _experts(xs, block_e, n_used, layer, wgu, bgu, wdn, bdn):
    rows, dp = xs.shape
    d = 2 * dp
    ff2 = wgu.shape[-1]
    nb = rows // MOE_ROWS
    blk = jnp.arange(nb)
    used = blk < n_used[0]
    first = used & ((blk == 0) | (block_e != jnp.roll(block_e, 1)))
    grp = jnp.cumsum(first.astype(I32)) - 1
    later_first = jnp.where(first[None, :] & (blk[None, :] > blk[:, None]), blk[None, :], nb)
    nxt_blk = jnp.min(later_first, axis=1)
    nxt = jnp.where(nxt_blk < nb, block_e[jnp.minimum(nxt_blk, nb - 1)], -1)
    bias = lambda i, be, nu, g, nx: (layer, be[i], 0, 0)
    return pl.pallas_call(
        functools.partial(_experts_kernel, layer=layer),
        out_shape=jax.ShapeDtypeStruct((rows, dp), U32),
        grid_spec=pltpu.PrefetchScalarGridSpec(
            num_scalar_prefetch=4,
            grid=(nb,),
            in_specs=[pl.BlockSpec((MOE_ROWS, dp),
                                   lambda i, be, nu, g, nx: (jnp.maximum(jnp.minimum(i, nu[0] - 1), 0), 0)),
                      pl.BlockSpec(memory_space=pl.ANY),
                      pl.BlockSpec((None, None, 1, ff2), bias),
                      pl.BlockSpec(memory_space=pl.ANY),
                      pl.BlockSpec((None, None, 1, d), bias)],
            out_specs=pl.BlockSpec((MOE_ROWS, dp), lambda i, be, nu, g, nx: (i, 0)),
            scratch_shapes=[pltpu.VMEM((2, d, ff2), F32), pltpu.VMEM((2, ff2 // 2, d), F32),
                            pltpu.VMEM((d, ff2), BF16), pltpu.VMEM((ff2 // 2, d), BF16),
                            pltpu.SemaphoreType.DMA((2,))]),
        compiler_params=_cparams(("arbitrary",), has_side_effects=True),
        name="moe_experts",
    )(block_e, n_used, grp.astype(I32), nxt.astype(I32), xs, wgu, bgu, wdn, bdn)


def _combine_kernel(dest_ref, y_ref, gate_ref, x_ref, mod_ref, o_ref, buf_ref, sem):
    n = dest_ref.shape[0]
    tm = n // TOP_K

    def issue(t, c):
        for k in range(TOP_K):
            pltpu.make_async_copy(_row(y_ref, dest_ref[t * TOP_K + k]), _row(buf_ref.at[k], t), sem).start()
        return c

    lax.fori_loop(0, tm, issue, 0, unroll=DMA_UNROLL)
    for k in range(TOP_K):
        pltpu.make_async_copy(y_ref.at[pl.ds(0, tm), :], buf_ref.at[k], sem).wait()

    gates = gate_ref[...]
    acc = gates[:, 0:1] * _unpack_bf16_pairs(buf_ref[0])
    for k in range(1, TOP_K):
        acc = acc + gates[:, k:k + 1] * _unpack_bf16_pairs(buf_ref[k])
    o_ref[...] = x_ref[...] + mod_ref[5:6, :] * acc


def _combine(y, dest, gates, xs, mods, t_off):
    b, t_out, d = xs.shape
    nt = t_out // TOK_TILE
    per = TOK_TILE * TOP_K
    row = functools.partial(_mod_row, ctx_row=b)
    tok = lambda bi, ti: (bi, ti, 0)
    return pl.pallas_call(
        _combine_kernel,
        out_shape=jax.ShapeDtypeStruct((b, t_out, d), F32),
        grid=(b, nt),
        in_specs=[pl.BlockSpec((per,), lambda bi, ti: (bi * nt + ti,), memory_space=pltpu.SMEM),
                  pl.BlockSpec(memory_space=pl.ANY),
                  pl.BlockSpec((None, TOK_TILE, LANES), tok),
                  pl.BlockSpec((None, TOK_TILE, d), tok),
                  pl.BlockSpec((None, 6, d), lambda bi, ti: (row(bi, ti + t_off), 0, 0))],
        out_specs=pl.BlockSpec((None, TOK_TILE, d), tok),
        scratch_shapes=[pltpu.VMEM((TOP_K, TOK_TILE, y.shape[1]), y.dtype), pltpu.SemaphoreType.DMA(())],
        compiler_params=_cparams(("arbitrary", "arbitrary"), disable_bounds_checks=True),
        name="moe_combine",
    )(dest, y, gates, xs, mods)


def _routing(e_sel, rank, counts, n):
    padded = (counts + MOE_ROWS - 1) // MOE_ROWS * MOE_ROWS
    pend = jnp.cumsum(padded)
    pstart = pend - padded
    dest = jnp.sum(jnp.where(e_sel[..., None] == jnp.arange(N_EXPERTS), pstart, 0), axis=-1) + rank
    n_blocks = n * TOP_K // MOE_ROWS + N_EXPERTS
    first_row = jnp.arange(n_blocks) * MOE_ROWS
    block_e = jnp.minimum(jnp.sum(pend[None, :] <= first_row[:, None], axis=1), N_EXPERTS - 1)
    n_used = (pend[-1] // MOE_ROWS).reshape(1)
    pads = ((pstart + counts).astype(I32), (padded - counts).astype(I32))
    return dest.reshape(-1).astype(I32), block_e.astype(I32), n_used.astype(I32), pads, n_blocks * MOE_ROWS


def _moe(f, route, gates, seen, x_mid, mods, t_off, layer, wgu, bgu, wdn, bdn):
    b, t_out, d = f.shape
    n = b * t_out
    route = route.reshape(n, LANES)
    counts = seen[0, :N_EXPERTS].astype(I32)
    dest, block_e, n_used, pads, rows = _routing(route[:, :TOP_K], route[:, TOP_K:2 * TOP_K], counts, n)
    xs = _dispatch(f.reshape(n, d), dest, *pads, n_used, rows)
    y = _experts(xs, block_e, n_used, layer, wgu, bgu, wdn, bdn)
    return _combine(y, dest, gates, x_mid, mods, t_off)


def _axial_angles(n_rows, rot_dim):
    row = np.repeat(np.arange(n_rows, dtype=np.float32), GRID_W)
    col = np.tile(np.arange(GRID_W, dtype=np.float32), n_rows)
    n = rot_dim // 4
    freqs = (np.float32(ROPE_THETA) ** (-np.arange(n, dtype=np.float32) / np.float32(n))).astype(np.float32)
    return np.concatenate([row[:, None] * freqs, col[:, None] * freqs], axis=-1).astype(np.float32)


def _rope_tables(ctx_len, s_len, rot_dim, lane_base, reps):
    ang = _axial_angles(s_len // GRID_W, rot_dim)
    half = rot_dim // 2
    period = LANES // reps
    cos = np.ones((ctx_len + s_len, period), np.float32)
    slo = np.zeros((ctx_len + s_len, period), np.float32)
    shi = np.zeros((ctx_len + s_len, period), np.float32)
    cos[ctx_len:, lane_base:lane_base + rot_dim] = np.tile(np.cos(ang), (1, 2))
    slo[ctx_len:, lane_base:lane_base + half] = -np.sin(ang)
    shi[ctx_len:, lane_base + half:lane_base + rot_dim] = np.sin(ang)
    return [jnp.asarray(np.tile(tb, (1, reps))) for tb in (cos, slo, shi)]


def _segment_mean_matrix(width, period, segs):
    lane = np.arange(width)
    seg_id = np.full((width,), -1)
    seg_w = np.zeros((width,), np.float32)
    for i, (start, length) in enumerate(segs):
        inside = ((lane % period) >= start) & ((lane % period) < start + length)
        seg_id = np.where(inside, (lane // period) * len(segs) + i, seg_id)
        seg_w = np.where(inside, np.float32(1.0 / length), seg_w)
    same = (seg_id[:, None] == seg_id[None, :]) & (seg_id[:, None] >= 0)
    return jnp.asarray(np.where(same, seg_w[None, :], np.float32(0.0)), dtype=BF16)


def _block_diag(w):
    n, c, _ = w.shape
    eye = jnp.eye(n, dtype=w.dtype)
    return (eye[:, None, :, None] * w[:, :, None, :]).reshape(n * c, n * c)


def kernel(x, c, ctx, c_ctx, w_mod, b_mod, norm_mix, norm_ffn, w_in_even, lru_conv_w, lru_conv_b, lru_w_r, lru_b_r, lru_w_i, lru_b_i, lru_lambda, mla_q_a_norm, mla_w_q_b, mla_kv_a_norm, mla_w_kv_b, mla_nope_norm, mla_rope_norm, w_out_even, w_qkv_odd, gqa_qk_norm, gqa_sink, w_out_odd, w_router, b_router, w_gate_up, b_gate_up, w_down, b_down):
    b, s_len, d = x.shape
    ctx_len = ctx.shape[1]
    depth = w_mod.shape[0]
    assert depth == 2 and ctx_len == TOK_TILE and s_len % TOK_TILE == 0 and b + 1 <= SUBLANES
    lru_w = lru_conv_w.shape[-1]
    q_lora = mla_q_a_norm.shape[-1]
    kv_lora = mla_kv_a_norm.shape[-1]

    cvec = jnp.concatenate([c, c_ctx[None], jnp.zeros((SUBLANES - b - 1, d), F32)], axis=0)
    mods = _modulation(cvec, w_mod, b_mod).reshape(depth, SUBLANES, 6, d)

    wr_pad = jnp.zeros((depth, d, LANES), F32).at[:, :, :N_EXPERTS].set(w_router)
    br_pad = jnp.full((depth, 1, LANES), NEG_INF, F32).at[:, 0, :N_EXPERTS].set(b_router)
    experts = (w_gate_up, b_gate_up[:, :, None, :], w_down, b_down[:, :, None, :])

    n_in = w_in_even.shape[-1]
    n_in_pad = -(-n_in // LANES) * LANES
    w_in = jnp.zeros((d, n_in_pad), F32).at[:, :n_in].set(w_in_even[0]).astype(BF16)
    xa, ga, mla_in = _even_in(ctx, x, mods[0], norm_mix[0][None], w_in, lru_w)

    nh = lru_w // (2 * LANES)
    per = LRU_BLOCKS // nh
    blk = lru_w // LRU_BLOCKS
    w_gates = jnp.stack([
        jnp.stack([jnp.concatenate([_block_diag(lru_w_r[0, dd, h * per:(h + 1) * per]),
                                    _block_diag(lru_w_i[0, dd, h * per:(h + 1) * per])], axis=1)
                   for h in range(nh)]) for dd in range(2)]).astype(BF16)
    assert blk * per == 2 * LANES
    ya = _lru(xa, ga, lru_conv_w[0], lru_conv_b[0][:, None, :], w_gates, lru_b_r[0][:, None, :],
              lru_b_i[0][:, None, :], jax.nn.softplus(-lru_lambda[0])[:, None, :], ctx_len)

    qk = MLA_NOPE + MLA_ROPE
    hq = MLA_HEADS * LANES
    wq = jnp.zeros((q_lora, MLA_HEADS, LANES), F32).at[:, :, :qk].set(
        mla_w_q_b[0].reshape(q_lora, MLA_HEADS, qk)).reshape(q_lora, hq).astype(BF16)
    wkv = mla_w_kv_b[0].reshape(kv_lora, MLA_HEADS, MLA_NOPE + MLA_V)
    wk = jnp.zeros((kv_lora, MLA_HEADS, LANES), F32).at[:, :, :MLA_NOPE].set(
        wkv[:, :, :MLA_NOPE]).reshape(kv_lora, hq).astype(BF16)
    wv = jnp.zeros((kv_lora, MLA_HEADS, LANES), F32).at[:, :, :MLA_V].set(
        wkv[:, :, MLA_NOPE:]).reshape(kv_lora, hq).astype(BF16)
    one = jnp.asarray(np.tile(np.arange(LANES) == MLA_V, MLA_HEADS)[None], dtype=F32)
    zpad = jnp.zeros((LANES - qk,), F32)
    gq = jnp.tile(jnp.concatenate([mla_nope_norm[0, 0], mla_rope_norm[0, 0], zpad]), MLA_HEADS)[None]
    gk = jnp.tile(jnp.concatenate([mla_nope_norm[0, 1], jnp.zeros((LANES - MLA_NOPE,), F32)]), MLA_HEADS)[None]
    gkr = jnp.concatenate([mla_rope_norm[0, 1], jnp.zeros((LANES - MLA_ROPE,), F32)])[None]
    cos, slo, shi = _rope_tables(ctx_len, s_len, MLA_ROPE, MLA_NOPE, 1)
    mla_p = dict(qan=mla_q_a_norm[0][None], kvn=mla_kv_a_norm[0][None], wq=wq, wk=wk, wv=wv,
                 mq=_segment_mean_matrix(MXU_TILE,LANES, [(0, MLA_NOPE), (MLA_NOPE, MLA_ROPE)]),
                 mk=_segment_mean_matrix(MXU_TILE,LANES, [(0, MLA_NOPE)]),
                 gq=gq, gk=gk, gkr=gkr, one=one, cos=cos, slo=slo, shi=shi)
    q, k, v = _mla_prep(mla_in, mla_p, qk ** -0.5 * math.log2(math.e))
    yb = _mla_attn(q, k, v, ctx_len)

    x_mid, f, route, gates, seen = _post_mix(ya, yb, 0, (ctx, x), mods[0], norm_ffn[0][None], w_out_even[0].astype(BF16),
                                             wr_pad[0], br_pad[0], 0, 0)
    xs = _moe(f, route, gates, seen, x_mid, mods[0], 0, 0, *experts)

    nq = GQA_HEADS * GQA_DIM
    nkv = GQA_KV_HEADS * GQA_DIM
    cos, slo, shi = _rope_tables(ctx_len, s_len, GQA_DIM, 0, LANES // GQA_DIM)
    odd_p = dict(w=w_qkv_odd[0].astype(BF16),
                 mq=_segment_mean_matrix(MXU_TILE,GQA_DIM, [(0, GQA_DIM)]),
                 mk=_segment_mean_matrix(MXU_TILE,GQA_DIM, [(0, GQA_DIM)]),
                 gq=jnp.tile(gqa_qk_norm[0, 0], GQA_HEADS)[None], gk=jnp.tile(gqa_qk_norm[0, 1], GQA_KV_HEADS)[None],
                 cos=cos, slo=slo, shi=shi)
    q, k, v = _odd_in(xs, mods[1], norm_mix[1][None], odd_p, GQA_DIM ** -0.5)
    o = _win_attn(q, k, v, gqa_sink[0], ctx_len)

    t_off = ctx_len // TOK_TILE
    x_mid, f, route, gates, seen = _post_mix(o, o, 1, xs, mods[1], norm_ffn[1][None], w_out_odd[0].astype(BF16),
                                             wr_pad[1], br_pad[1], t_off, 0)
    return _moe(f, route, gates, seen, x_mid, mods[1], t_off, 1, *experts)
```

```python
import functools
import math

import jax
import jax.numpy as jnp
import numpy as np
from jax import lax
from jax.experimental import pallas as pl
from jax.experimental.pallas import tpu as pltpu

F32 = jnp.float32
BF16 = jnp.bfloat16
I32 = jnp.int32

GRID_W = 64
LRU_BLOCKS = 8
LRU_C = 8.0
CONV_W = 4
MLA_HEADS = 8
MLA_NOPE = 64
MLA_ROPE = 32
MLA_V = 64
GQA_HEADS = 16
GQA_KV_HEADS = 4
GQA_DIM = 64
WINDOW = 128
ROPE_THETA = 10000.0
NEG_INF = -1e30
EPS = 1e-6
N_EXPERTS = 32
TOP_K = 4
SWIGLU_LIMIT = 7.0
SWIGLU_ALPHA = 1.702

LANES = 128
SUBLANES = 8
TOK_TILE = 256
LRU_CHUNK = 128
WIN_BLOCK = 128
WIN_BLOCKS_PER_STEP = 2
MOE_ROWS = 256
MXU_TILE = 256
MLA_HEADS_PER_STEP = 4
VMEM_LIMIT = 48 * 1024 * 1024


def _cparams(sem, **kw):
    return pltpu.CompilerParams(dimension_semantics=sem, vmem_limit_bytes=VMEM_LIMIT, **kw)


def _dot(a, b):
    return jnp.dot(a, b, preferred_element_type=F32)


def _dot_nt(a, b):
    return lax.dot_general(a, b, (((1,), (1,)), ((), ())), preferred_element_type=F32)


def _split_bf16(x):
    hi = x.astype(BF16)
    lo = (x - hi.astype(F32)).astype(BF16)
    return hi, lo


def _dot3(a, w):
    ah, al = _split_bf16(a)
    wh, wl = _split_bf16(w)
    return _dot(ah, wh) + _dot(al, wh) + _dot(ah, wl)


def _rms(x):
    return x * lax.rsqrt(jnp.mean(x * x, axis=-1, keepdims=True) + EPS)


def _prenorm(x, g, scale, shift):
    return (_rms(x) * g) * (1.0 + scale) + shift


def _rope(x, cos, sin_lo, sin_hi, half):
    w = x.shape[-1]
    return x * cos + pltpu.roll(x, w - half, axis=1) * sin_lo + pltpu.roll(x, half, axis=1) * sin_hi


U32 = jnp.uint32
HI_HALF = 0xFFFF0000


def _pack_bf16_pairs(x):
    w = x.shape[1] // 2
    lo = pltpu.bitcast(x[:, :w].astype(BF16).astype(F32), U32) >> 16
    hi = pltpu.bitcast(x[:, w:].astype(BF16).astype(F32), U32) & U32(HI_HALF)
    return lo | hi


def _unpack_bf16_pairs(words):
    lo = pltpu.bitcast(words << 16, F32)
    hi = pltpu.bitcast(words & U32(HI_HALF), F32)
    return jnp.concatenate([lo, hi], axis=1)


def _head_mean_square(y, m_ref):
    w = m_ref.shape[0]
    return jnp.concatenate([_dot((y[:, j:j + w] * y[:, j:j + w]).astype(BF16), m_ref[...])
                            for j in range(0, y.shape[1], w)], axis=1)


def _tile_lanes(t, reps):
    return jnp.concatenate([t] * reps, axis=1) if reps > 1 else t


def _mod_kernel(c_ref, w_ref, b_ref, o_ref):
    c = c_ref[...]
    o_ref[...] = _dot3(c * jax.nn.sigmoid(c), w_ref[...]) + b_ref[...]


def _modulation(cvec, w_mod, b_mod):
    depth, d, n = w_mod.shape
    tn = 1536
    return pl.pallas_call(
        _mod_kernel,
        out_shape=jax.ShapeDtypeStruct((depth, SUBLANES, n), F32),
        grid=(depth, n // tn),
        in_specs=[pl.BlockSpec((SUBLANES, d), lambda l, j: (0, 0)),
                  pl.BlockSpec((None, d, tn), lambda l, j: (l, 0, j)),
                  pl.BlockSpec((None, 1, tn), lambda l, j: (l, 0, j))],
        out_specs=pl.BlockSpec((None, SUBLANES, tn), lambda l, j: (l, 0, j)),
        compiler_params=_cparams(("arbitrary", "arbitrary")),
        name="modulation",
    )(cvec, w_mod, b_mod.reshape(depth, 1, n))


def _mod_row(b, t, ctx_row):
    return jnp.where(t == 0, ctx_row, b)


def _stream_tile(ctx_ref, lat_ref):
    return jnp.where(pl.program_id(1) == 0, ctx_ref[...], lat_ref[...])


def _stream_specs(d):
    return [pl.BlockSpec((None, TOK_TILE, d), lambda bi, ti: (bi, 0, 0)),
            pl.BlockSpec((None, TOK_TILE, d), lambda bi, ti: (bi, jnp.maximum(ti - 1, 0), 0))]


def _even_in_kernel(ctx_ref, x_ref, mod_ref, g_ref, w_ref, xa_ref, ga_ref, mla_ref):
    mod = mod_ref[...]
    h = _prenorm(_stream_tile(ctx_ref, x_ref), g_ref[...], mod[1:2], mod[0:1])
    z = _dot(h.astype(BF16), w_ref[...])
    c = xa_ref.shape[-1]
    xa_ref[...] = z[:, :c]
    ga_ref[...] = z[:, c:2 * c]
    mla_ref[...] = z[:, 2 * c:]


def _even_in(ctx, x, mods, g, w_pad, lru_w):
    b, s_len, d = x.shape
    t = ctx.shape[1] + s_len
    nt = t // TOK_TILE
    n_out = w_pad.shape[1]
    n_mla = n_out - 2 * lru_w
    row = functools.partial(_mod_row, ctx_row=b)
    tok = lambda bi, ti: (bi, ti, 0)
    return pl.pallas_call(
        _even_in_kernel,
        out_shape=(jax.ShapeDtypeStruct((b, t, lru_w), F32),
                   jax.ShapeDtypeStruct((b, t, lru_w), F32),
                   jax.ShapeDtypeStruct((b, t, n_mla), F32)),
        grid=(b, nt),
        in_specs=_stream_specs(d) + [
            pl.BlockSpec((None, 6, d), lambda bi, ti: (row(bi, ti), 0, 0)),
            pl.BlockSpec((1, d), lambda bi, ti: (0, 0)),
            pl.BlockSpec((d, n_out), lambda bi, ti: (0, 0))],
        out_specs=(pl.BlockSpec((None, TOK_TILE, lru_w), tok),
                   pl.BlockSpec((None, TOK_TILE, lru_w), tok),
                   pl.BlockSpec((None, TOK_TILE, n_mla), tok)),
        compiler_params=_cparams(("arbitrary", "arbitrary")),
        name="even_in",
    )(ctx, x, mods, g, w_pad)


def _lru_kernel(xa_ref, ga_ref, cw_ref, cb_ref, wg_ref, br_ref, bi_ref, sp_ref, o_ref, pad_ref, rec_ref, *, ctx_len):
    t, c = xa_ref.shape
    tc = LRU_CHUNK
    halo = SUBLANES
    n_chunks = t // tc
    n_ctx = ctx_len // tc
    groups = tc // SUBLANES

    pad_ref[0:halo, :] = jnp.zeros((halo, c), F32)
    pad_ref[t + halo:t + 2 * halo, :] = jnp.zeros((halo, c), F32)
    pad_ref[halo:t + halo, :] = xa_ref[...]

    rid = lax.broadcasted_iota(I32, (tc, 1), 0)
    sub = rid % SUBLANES

    for d in range(2):
        cw = cw_ref[d]
        cb = cb_ref[d]
        wg = wg_ref[d]
        b_r = br_ref[d]
        b_i = bi_ref[d]
        sp = sp_ref[d]

        def chunk(i, h, d=d, cw=cw, cb=cb, wg=wg, b_r=b_r, b_i=b_i, sp=sp):
            if d == 0:
                ci = i
            else:
                ci = jnp.where(i < n_ctx, n_ctx - 1 - i, n_chunks - 1 - (i - n_ctx))
            r0 = pl.multiple_of(ci * tc, tc)
            win = pad_ref[pl.ds(r0, tc + 2 * halo), :]
            if d == 0:
                past = jnp.where(ci == n_ctx, 0.0, win[0:halo])
                win = jnp.concatenate([past, win[halo:]], axis=0)
            else:
                past = jnp.where(ci == n_ctx - 1, 0.0, win[halo + tc:])
                win = jnp.concatenate([win[:halo + tc], past], axis=0)
            xc = jnp.zeros((tc, c), F32) + cb
            for k in range(CONV_W):
                off = (k - (CONV_W - 1)) if d == 0 else ((CONV_W - 1) - k)
                if off == 0:
                    src = win[halo:halo + tc]
                else:
                    src = pltpu.roll(win, (-off) % (tc + 2 * halo), axis=0)[halo:halo + tc]
                xc = xc + cw[k:k + 1] * src
            gz = _dot(xc.astype(BF16), wg)
            r = 0.5 * jnp.tanh(0.5 * (gz[:, :c] + b_r)) + 0.5
            gi = 0.5 * jnp.tanh(0.5 * (gz[:, c:] + b_i)) + 0.5
            log_a = (-LRU_C) * r * sp
            a = jnp.exp(log_a)
            th = jnp.tanh(log_a)
            bb = jnp.sqrt(-2.0 * th / (1.0 - th)) * (gi * xc)
            for s in (1, 2, 4):
                if d == 0:
                    ok = sub >= s
                    sh = s
                else:
                    ok = sub <= (SUBLANES - 1 - s)
                    sh = tc - s
                a_prev = jnp.where(ok, pltpu.roll(a, sh, axis=0), 1.0)
                b_prev = jnp.where(ok, pltpu.roll(bb, sh, axis=0), 0.0)
                bb = a * b_prev + bb
                a = a * a_prev
            outs = [None] * groups
            order = range(groups) if d == 0 else range(groups - 1, -1, -1)
            for g in order:
                lo = g * SUBLANES
                hg = a[lo:lo + SUBLANES] * h + bb[lo:lo + SUBLANES]
                outs[g] = hg
                h = hg[SUBLANES - 1:SUBLANES] if d == 0 else hg[0:1]
            hs = jnp.concatenate(outs, axis=0)
            if d == 0:
                rec_ref[pl.ds(r0, tc), :] = hs
            else:
                tot = rec_ref[pl.ds(r0, tc), :] + hs
                gate = jax.nn.gelu(ga_ref[pl.ds(r0, tc), :], approximate=True)
                o_ref[pl.ds(r0, tc), :] = (tot * gate).astype(o_ref.dtype)
            return h

        lax.fori_loop(0, n_chunks, chunk, jnp.zeros((1, c), F32))


def _lru(xa, ga, conv_w, conv_b, w_gates, b_r, b_i, sp, ctx_len):
    b, t, w = xa.shape
    c = 2 * LANES
    nh = w // c
    tok = lambda bi, hi: (bi, 0, hi)
    par = lambda bi, hi: (0, 0, hi)
    return pl.pallas_call(
        functools.partial(_lru_kernel, ctx_len=ctx_len),
        out_shape=jax.ShapeDtypeStruct((b, t, w), BF16),
        grid=(b, nh),
        in_specs=[pl.BlockSpec((None, t, c), tok),
                  pl.BlockSpec((None, t, c), tok),
                  pl.BlockSpec((2, CONV_W, c), par),
                  pl.BlockSpec((2, 1, c), par),
                  pl.BlockSpec((2, None, c, 2 * c), lambda bi, hi: (0, hi, 0, 0)),
                  pl.BlockSpec((2, 1, c), par),
                  pl.BlockSpec((2, 1, c), par),
                  pl.BlockSpec((2, 1, c), par)],
        out_specs=pl.BlockSpec((None, t, c), tok),
        scratch_shapes=[pltpu.VMEM((t + 2 * SUBLANES, c), F32), pltpu.VMEM((t, c), F32)],
        compiler_params=_cparams(("arbitrary", "arbitrary")),
        name="rglru",
    )(xa, ga, conv_w, conv_b, w_gates, b_r, b_i, sp)


def _mla_prep_kernel(in_ref, qan_ref, kvn_ref, wq_ref, wk_ref, wv_ref, mq_ref, mk_ref, gq_ref, gk_ref, gkr_ref,
                     one_ref, cos_ref, slo_ref, shi_ref, q_ref, k_ref, v_ref, *, q_lora, kv_lora, scale):
    z = in_ref[...]
    heads = q_ref.shape[-1] // LANES
    cos = cos_ref[...]
    slo = slo_ref[...]
    shi = shi_ref[...]
    half = MLA_ROPE // 2

    qan = (_rms(z[:, :q_lora]) * qan_ref[...]).astype(BF16)
    q = _dot(qan, wq_ref[...])
    q = q * lax.rsqrt(_head_mean_square(q, mq_ref) + EPS) * gq_ref[...]
    q = _rope(q, _tile_lanes(cos, heads), _tile_lanes(slo, heads), _tile_lanes(shi, heads), half)
    q_ref[...] = (q * scale).astype(BF16)

    kvn = (_rms(z[:, q_lora:q_lora + kv_lora]) * kvn_ref[...]).astype(BF16)
    kk = _dot(kvn, wk_ref[...])
    kk = kk * lax.rsqrt(_head_mean_square(kk, mk_ref) + EPS) * gk_ref[...]
    v_ref[...] = (_dot(kvn, wv_ref[...]) + one_ref[...]).astype(BF16)

    kr = z[:, q_lora + kv_lora:]
    kr = kr * lax.rsqrt(jnp.sum(kr * kr, axis=-1, keepdims=True) * (1.0 / MLA_ROPE) + EPS) * gkr_ref[...]
    kr = pltpu.roll(kr, MLA_NOPE, axis=1)
    kr = _rope(kr, cos, slo, shi, half)
    k_ref[...] = (kk + _tile_lanes(kr, heads)).astype(BF16)


def _mla_prep(mla_in, p, scale):
    b, t, w = mla_in.shape
    hq = MLA_HEADS * LANES
    tok = lambda bi, ti: (bi, ti, 0)
    full = lambda a: pl.BlockSpec(a.shape, lambda bi, ti: (0,) * a.ndim)
    pos = pl.BlockSpec((TOK_TILE, LANES), lambda bi, ti: (ti, 0))
    consts = [p["qan"], p["kvn"], p["wq"], p["wk"], p["wv"], p["mq"], p["mk"], p["gq"], p["gk"], p["gkr"], p["one"]]
    out = jax.ShapeDtypeStruct((b, t, hq), BF16)
    return pl.pallas_call(
        functools.partial(_mla_prep_kernel, q_lora=p["qan"].shape[1], kv_lora=p["kvn"].shape[1], scale=scale),
        out_shape=(out, out, out),
        grid=(b, t // TOK_TILE),
        in_specs=[pl.BlockSpec((None, TOK_TILE, w), tok)] + [full(a) for a in consts] + [pos, pos, pos],
        out_specs=(pl.BlockSpec((None, TOK_TILE, hq), tok),) * 3,
        compiler_params=_cparams(("arbitrary", "arbitrary")),
        name="mla_prep",
    )(mla_in, *consts, p["cos"], p["slo"], p["shi"])


def _mla_attn_kernel(q_ref, k_ref, v_ref, o_ref, *, ctx_len):
    tq = q_ref.shape[0]
    t = k_ref.shape[0]
    lane = lax.broadcasted_iota(I32, (tq, LANES), 1)

    def attend(nk):
        outs = []
        for hh in range(q_ref.shape[1] // LANES):
            q = q_ref[:, hh * LANES:(hh + 1) * LANES]
            k = k_ref[0:nk, hh * LANES:(hh + 1) * LANES]
            s = _dot_nt(q, k)
            p = jnp.exp2(s - jnp.max(s, axis=-1, keepdims=True))
            o = _dot(p.astype(BF16), v_ref[0:nk, hh * LANES:(hh + 1) * LANES])
            outs.append(o / o[:, MLA_V:MLA_V + 1])
        pairs = [jnp.where(lane < MLA_V, outs[j], pltpu.roll(outs[j + 1], MLA_V, axis=1))
                 for j in range(0, len(outs), 2)]
        o_ref[...] = jnp.concatenate(pairs, axis=1).astype(o_ref.dtype)

    @pl.when(pl.program_id(2) == 0)
    def _():
        attend(ctx_len)

    @pl.when(pl.program_id(2) > 0)
    def _():
        attend(t)


def _mla_attn(q, k, v, ctx_len):
    b, t, hq = q.shape
    hps = MLA_HEADS_PER_STEP
    groups = hq // (hps * LANES)
    return pl.pallas_call(
        functools.partial(_mla_attn_kernel, ctx_len=ctx_len),
        out_shape=jax.ShapeDtypeStruct((b, t, groups * hps * MLA_V), BF16),
        grid=(b, groups, t // TOK_TILE),
        in_specs=[pl.BlockSpec((None, TOK_TILE, hps * LANES), lambda bi, hi, ti: (bi, ti, hi)),
                  pl.BlockSpec((None, t, hps * LANES), lambda bi, hi, ti: (bi, 0, hi)),
                  pl.BlockSpec((None, t, hps * LANES), lambda bi, hi, ti: (bi, 0, hi))],
        out_specs=pl.BlockSpec((None, TOK_TILE, hps * MLA_V), lambda bi, hi, ti: (bi, ti, hi)),
        compiler_params=_cparams(("arbitrary", "arbitrary", "arbitrary")),
        name="mla_attn",
    )(q, k, v)


def _odd_in_kernel(x_ref, mod_ref, g_ref, w_ref, mq_ref, mk_ref, gq_ref, gk_ref, cos_ref, slo_ref, shi_ref,
                   q_ref, k_ref, v_ref, *, scale):
    mod = mod_ref[...]
    h = _prenorm(x_ref[...], g_ref[...], mod[1:2], mod[0:1]).astype(BF16)
    nq = q_ref.shape[-1]
    nk = k_ref.shape[-1]
    half = GQA_DIM // 2

    def head_norm_rope(y, m_ref, gain_ref):
        reps = y.shape[-1] // LANES
        y = y * lax.rsqrt(_head_mean_square(y, m_ref) + EPS) * gain_ref[...]
        return _rope(y, _tile_lanes(cos_ref[...], reps), _tile_lanes(slo_ref[...], reps),
                     _tile_lanes(shi_ref[...], reps), half)

    q_ref[...] = (head_norm_rope(_dot(h, w_ref[:, 0:nq]), mq_ref, gq_ref) * scale).astype(BF16)
    k_ref[...] = head_norm_rope(_dot(h, w_ref[:, nq:nq + nk]), mk_ref, gk_ref).astype(BF16)
    v_ref[...] = _dot(h, w_ref[:, nq + nk:nq + 2 * nk]).astype(BF16)


def _odd_in(xs, mods, g, p, scale):
    b, t, d = xs.shape
    nq = p["gq"].shape[1]
    nk = p["gk"].shape[1]
    row = functools.partial(_mod_row, ctx_row=b)
    tok = lambda bi, ti: (bi, ti, 0)
    full = lambda a: pl.BlockSpec(a.shape, lambda bi, ti: (0,) * a.ndim)
    pos = pl.BlockSpec((TOK_TILE, LANES), lambda bi, ti: (ti, 0))
    kv = jax.ShapeDtypeStruct((b, t, nk), BF16)
    return pl.pallas_call(
        functools.partial(_odd_in_kernel, scale=scale),
        out_shape=(jax.ShapeDtypeStruct((b, t, nq), BF16), kv, kv),
        grid=(b, t // TOK_TILE),
        in_specs=[pl.BlockSpec((None, TOK_TILE, d), tok),
                  pl.BlockSpec((None, 6, d), lambda bi, ti: (row(bi, ti), 0, 0)),
                  pl.BlockSpec((1, d), lambda bi, ti: (0, 0)),
                  full(p["w"]), full(p["mq"]), full(p["mk"]), full(p["gq"]), full(p["gk"]), pos, pos, pos],
        out_specs=(pl.BlockSpec((None, TOK_TILE, nq), tok),
                   pl.BlockSpec((None, TOK_TILE, nk), tok),
                   pl.BlockSpec((None, TOK_TILE, nk), tok)),
        compiler_params=_cparams(("arbitrary", "arbitrary")),
        name="odd_in",
    )(xs, mods, g, p["w"], p["mq"], p["mk"], p["gq"], p["gk"], p["cos"], p["slo"], p["shi"])


def _win_attn_kernel(sink_ref, q_ref, k_ref, v_ref, o_ref, *, ctx_len):
    t = k_ref.shape[0]
    wb = WIN_BLOCK
    w3 = 3 * wb
    hd = GQA_DIM
    group = GQA_HEADS // GQA_KV_HEADS
    nk = w3 + ctx_len
    rows = group * wb
    row = lax.broadcasted_iota(I32, (rows, nk), 0)
    col = lax.broadcasted_iota(I32, (rows, nk), 1)
    head_of_row = lax.broadcasted_iota(I32, (rows, 1), 0) // wb
    kctx = k_ref[0:ctx_len, :]
    vctx = v_ref[0:ctx_len, :]
    for sub in range(WIN_BLOCKS_PER_STEP):
        i = pl.program_id(1) * WIN_BLOCKS_PER_STEP + sub
        q0 = ctx_len + i * wb
        ws = pl.multiple_of(jnp.clip(q0 - wb, ctx_len, t - w3), wb)
        valid = (col >= w3) | (jnp.abs(q0 + row % wb - (ws + col)) <= WINDOW)
        q = q_ref[sub * wb:(sub + 1) * wb, :].astype(F32)
        kcat = jnp.concatenate([k_ref[pl.ds(ws, w3), :], kctx], axis=0).astype(F32)
        vcat = jnp.concatenate([v_ref[pl.ds(ws, w3), :], vctx], axis=0).astype(F32)
        outs = []
        for kh in range(GQA_KV_HEADS):
            qs = jnp.concatenate([q[:, (kh * group + h) * hd:(kh * group + h + 1) * hd] for h in range(group)],
                                 axis=0)
            s = _dot_nt(qs.astype(BF16), kcat[:, kh * hd:(kh + 1) * hd].astype(BF16))
            s = jnp.where(valid, s, NEG_INF)
            sink = jnp.zeros((rows, 1), F32)
            for h in range(group):
                sink = jnp.where(head_of_row == h, sink_ref[kh * group + h], sink)
            m = jnp.maximum(jnp.max(s, axis=-1, keepdims=True), sink)
            p = jnp.exp(s - m)
            l = jnp.sum(p, axis=-1, keepdims=True) + jnp.exp(sink - m)
            o = _dot(p.astype(BF16), vcat[:, kh * hd:(kh + 1) * hd].astype(BF16)) / l
            outs.extend(o[h * wb:(h + 1) * wb] for h in range(group))
        o_ref[sub * wb:(sub + 1) * wb, :] = jnp.concatenate(outs, axis=1).astype(o_ref.dtype)


def _win_attn(q, k, v, sink, ctx_len):
    b, t, n = q.shape
    nkv = k.shape[-1]
    s_len = t - ctx_len
    step = WIN_BLOCK * WIN_BLOCKS_PER_STEP
    assert ctx_len % step == 0 and s_len % step == 0
    off = ctx_len // step
    return pl.pallas_call(
        functools.partial(_win_attn_kernel, ctx_len=ctx_len),
        out_shape=jax.ShapeDtypeStruct((b, s_len, n), BF16),
        grid=(b, s_len // step),
        in_specs=[pl.BlockSpec(memory_space=pltpu.SMEM),
                  pl.BlockSpec((None, step, n), lambda bi, ti: (bi, ti + off, 0)),
                  pl.BlockSpec((None, t, nkv), lambda bi, ti: (bi, 0, 0)),
                  pl.BlockSpec((None, t, nkv), lambda bi, ti: (bi, 0, 0))],
        out_specs=pl.BlockSpec((None, step, n), lambda bi, ti: (bi, ti, 0)),
        compiler_params=_cparams(("arbitrary", "arbitrary")),
        name="win_attn",
    )(sink, q, k, v)


def _post_mix_kernel(a1_ref, a2_ref, *refs, split_stream):
    x_in = _stream_tile(refs[0], refs[1]) if split_stream else refs[0][...]
    _post_mix_body(a1_ref, a2_ref, x_in, *refs[2 if split_stream else 1:])


def _post_mix_body(a1_ref, a2_ref, x_in, mod_ref, g_ref, w_ref, wr_ref, br_ref,
                   xo_ref, f_ref, route_ref, gate_ref, cnt_ref, seen_ref):
    half = a1_ref.shape[-1]
    mod = mod_ref[...]
    m = _dot(a1_ref[...], w_ref[0:half, :]) + _dot(a2_ref[...], w_ref[half:2 * half, :])
    x = x_in + mod[2:3] * m
    xo_ref[...] = x
    f = _prenorm(x, g_ref[...], mod[4:5], mod[3:4])
    f_ref[...] = _pack_bf16_pairs(f)

    logit = _dot(f.astype(BF16), wr_ref[...].astype(BF16)) + br_ref[...]
    tm = logit.shape[0]
    lane = lax.broadcasted_iota(I32, (tm, LANES), 1)
    lane_f = lane.astype(F32)
    vals, idxs = [], []
    for _ in range(TOP_K):
        mx = jnp.max(logit, axis=-1, keepdims=True)
        ix = jnp.min(jnp.where(logit == mx, lane_f, float(LANES)), axis=-1, keepdims=True)
        vals.append(mx)
        idxs.append(ix)
        logit = jnp.where(lane_f == ix, -jnp.inf, logit)
    exps = [jnp.exp(v - vals[0]) for v in vals]
    den = exps[0]
    for e in exps[1:]:
        den = den + e

    @pl.when((pl.program_id(0) == 0) & (pl.program_id(1) == 0))
    def _():
        seen_ref[...] = jnp.zeros(seen_ref.shape, F32)

    msk = jnp.zeros((tm, LANES), F32)
    for k in range(TOP_K):
        msk = jnp.where(lane_f == idxs[k], 1.0, msk)
    earlier = (lax.broadcasted_iota(I32, (tm, tm), 1) < lax.broadcasted_iota(I32, (tm, tm), 0))
    rank = _dot(jnp.where(earlier, 1.0, 0.0).astype(BF16), msk.astype(BF16)) + seen_ref[0:1, :]
    seen = seen_ref[...] + jnp.sum(msk, axis=0, keepdims=True)
    seen_ref[...] = seen
    cnt_ref[...] = seen

    r_out = jnp.zeros((tm, LANES), F32)
    g_out = jnp.zeros((tm, LANES), F32)
    for k in range(TOP_K):
        rank_k = jnp.sum(jnp.where(lane_f == idxs[k], rank, 0.0), axis=-1, keepdims=True)
        r_out = jnp.where(lane == k, idxs[k], r_out)
        r_out = jnp.where(lane == TOP_K + k, rank_k, r_out)
        g_out = jnp.where(lane == k, exps[k] / den, g_out)
    route_ref[...] = r_out.astype(I32)
    gate_ref[...] = g_out


def _post_mix(a1, a2, lane_blk2, xs, mods, g, w_out, w_router, b_router, t_off, a_off):
    split_stream = isinstance(xs, tuple)
    if split_stream:
        assert t_off == 0
        b, s_len, d = xs[1].shape
        t = xs[0].shape[1] + s_len
        x_specs = _stream_specs(d)
    else:
        b, t, d = xs.shape
        x_specs = [pl.BlockSpec((None, TOK_TILE, d), lambda bi, ti: (bi, ti + t_off, 0))]
        xs = (xs,)
    half = w_out.shape[0] // 2
    nt = t // TOK_TILE - t_off
    t_out = nt * TOK_TILE
    row = functools.partial(_mod_row, ctx_row=b)
    tok = lambda bi, ti: (bi, ti, 0)
    act = jax.ShapeDtypeStruct((b, t_out, d), F32)
    return pl.pallas_call(
        functools.partial(_post_mix_kernel, split_stream=split_stream),
        out_shape=(act, jax.ShapeDtypeStruct((b, t_out, d // 2), U32),
                   jax.ShapeDtypeStruct((b, t_out, LANES), I32),
                   jax.ShapeDtypeStruct((b, t_out, LANES), F32),
                   jax.ShapeDtypeStruct((SUBLANES, LANES), F32)),
        grid=(b, nt),
        in_specs=[pl.BlockSpec((None, TOK_TILE, half), lambda bi, ti: (bi, ti + a_off, 0)),
                  pl.BlockSpec((None, TOK_TILE, half), lambda bi, ti: (bi, ti + a_off, lane_blk2))] + x_specs + [
                  pl.BlockSpec((None, 6, d), lambda bi, ti: (row(bi, ti + t_off), 0, 0)),
                  pl.BlockSpec((1, d), lambda bi, ti: (0, 0)),
                  pl.BlockSpec(w_out.shape, lambda bi, ti: (0, 0)),
                  pl.BlockSpec(w_router.shape, lambda bi, ti: (0, 0)),
                  pl.BlockSpec((1, LANES), lambda bi, ti: (0, 0))],
        out_specs=(pl.BlockSpec((None, TOK_TILE, d), tok),
                   pl.BlockSpec((None, TOK_TILE, d // 2), tok),
                   pl.BlockSpec((None, TOK_TILE, LANES), tok),
                   pl.BlockSpec((None, TOK_TILE, LANES), tok),
                   pl.BlockSpec((SUBLANES, LANES), lambda bi, ti: (0, 0))),
        scratch_shapes=[pltpu.VMEM((SUBLANES, LANES), F32)],
        compiler_params=_cparams(("arbitrary", "arbitrary")),
        name="post_mix",
    )(a1, a2, *xs, mods, g, w_out, w_router, b_router)


DMA_UNROLL = 2
DISPATCH_TILE = 1024
COMBINE_TILE_LATENT = 1024


def _row(ref, i):
    return ref.at[pl.ds(i, 1), :]


def _dispatch_kernel(pad_lo_ref, pad_n_ref, nu_ref, dest_ref, f_ref, xs_ref, zero_ref, sem, zsem):
    n = dest_ref.shape[0]

    blk = zero_ref.shape[0]
    n_blk = xs_ref.shape[0] // blk

    def block_copy(i):
        return pltpu.make_async_copy(zero_ref, xs_ref.at[pl.ds(pl.multiple_of(i * blk, blk), blk), :], zsem)

    def pad_rows(e, c, wait):
        lo = pad_lo_ref[e]
        head = jnp.minimum((-lo) & (SUBLANES - 1), pad_n_ref[e])

        def one(r, c2):
            copy = pltpu.make_async_copy(_row(zero_ref, 0), _row(xs_ref, lo + r), zsem)
            copy.wait() if wait else copy.start()
            return c2

        def eight(g, c2):
            r0 = pl.multiple_of(lo + head + g * SUBLANES, SUBLANES)
            copy = pltpu.make_async_copy(zero_ref.at[pl.ds(0, SUBLANES), :], xs_ref.at[pl.ds(r0, SUBLANES), :], zsem)
            copy.wait() if wait else copy.start()
            return c2

        c = lax.fori_loop(0, head, one, c)
        return lax.fori_loop(0, (pad_n_ref[e] - head) // SUBLANES, eight, c)

    @pl.when(pl.program_id(0) == 0)
    def _():
        zero_ref[...] = jnp.zeros(zero_ref.shape, zero_ref.dtype)
        lax.fori_loop(nu_ref[0], n_blk, lambda i, c: (block_copy(i).start(), c)[1], 0)
        lax.fori_loop(0, N_EXPERTS, functools.partial(pad_rows, wait=False), 0)

    @pl.when(pl.program_id(0) == pl.num_programs(0) - 1)
    def _():
        lax.fori_loop(nu_ref[0], n_blk, lambda i, c: (block_copy(i).wait(), c)[1], 0)
        lax.fori_loop(0, N_EXPERTS, functools.partial(pad_rows, wait=True), 0)


    def issue(t, c):
        for k in range(TOP_K):
            pltpu.make_async_copy(_row(f_ref, t), _row(xs_ref, dest_ref[t * TOP_K + k]), sem).start()
        return c

    lax.fori_loop(0, n // TOP_K, issue, 0, unroll=DMA_UNROLL)
    pltpu.make_async_copy(xs_ref.at[pl.ds(0, n), :], xs_ref.at[pl.ds(0, n), :], sem).wait()


def _dispatch(f, dest, pad_lo, pad_n, n_used, rows):
    n, d = f.shape
    tile = DISPATCH_TILE
    assert n % tile == 0
    return pl.pallas_call(
        _dispatch_kernel,
        out_shape=jax.ShapeDtypeStruct((rows, d), f.dtype),
        grid_spec=pltpu.PrefetchScalarGridSpec(
            num_scalar_prefetch=3,
            grid=(n // tile,),
            in_specs=[pl.BlockSpec((tile * TOP_K,), lambda i, lo, cnt, nu: (i,), memory_space=pltpu.SMEM),
                      pl.BlockSpec((tile, d), lambda i, lo, cnt, nu: (i, 0))],
            out_specs=pl.BlockSpec(memory_space=pl.ANY),
            scratch_shapes=[pltpu.VMEM((MOE_ROWS, d), f.dtype), pltpu.SemaphoreType.DMA(()),
                            pltpu.SemaphoreType.DMA(())]),
        compiler_params=_cparams(("arbitrary",), has_side_effects=True, disable_bounds_checks=True),
        name="moe_dispatch",
    )(pad_lo, pad_n, n_used, dest, f)


def _experts_kernel(be_ref, nu_ref, grp_ref, nxt_ref, xs_ref, wgu_hbm, bgu_ref, wdn_hbm, bdn_ref, y_ref,
                    wgu_f32, wdn_f32, wgu_bf, wdn_bf, wsem, *, layer):
    i = pl.program_id(0)

    def fetch(e, slot, wait):
        for src, dst in ((wgu_hbm.at[layer, e], wgu_f32.at[slot]), (wdn_hbm.at[layer, e], wdn_f32.at[slot])):
            c = pltpu.make_async_copy(src, dst, wsem.at[slot])
            c.wait() if wait else c.start()

    @pl.when(i == 0)
    def _():
        fetch(be_ref[0], 0, False)

    @pl.when((i < nu_ref[0]) & ((i == 0) | (grp_ref[i] != grp_ref[jnp.maximum(i - 1, 0)])))
    def _():
        slot = grp_ref[i] % 2
        fetch(be_ref[i], slot, True)
        wgu_bf[...] = wgu_f32[slot].astype(BF16)
        wdn_bf[...] = wdn_f32[slot].astype(BF16)

        @pl.when(nxt_ref[i] >= 0)
        def _():
            fetch(nxt_ref[i], 1 - slot, False)

    @pl.when(i < nu_ref[0])
    def _():
        h = _dot(_unpack_bf16_pairs(xs_ref[...]).astype(BF16), wgu_bf[...]) + bgu_ref[...]
        ff = h.shape[1] // 2
        hg = jnp.minimum(h[:, :ff], SWIGLU_LIMIT)
        hu = jnp.clip(h[:, ff:], -SWIGLU_LIMIT, SWIGLU_LIMIT)
        act = hg * jax.nn.sigmoid(SWIGLU_ALPHA * hg) * (hu + 1.0)
        y_ref[...] = _pack_bf16_pairs(_dot(act.astype(BF16), wdn_bf[...]) + bdn_ref[...])

    @pl.when(i >= nu_ref[0])
    def _():
        y_ref[...] = jnp.zeros(y_ref.shape, y_ref.dtype)


def _experts(xs, block_e, n_used, layer, wgu, bgu, wdn, bdn):
    rows, dp = xs.shape
    d = 2 * dp
    ff2 = wgu.shape[-1]
    nb = rows // MOE_ROWS
    blk = jnp.arange(nb)
    used = blk < n_used[0]
    first = used & ((blk == 0) | (block_e != jnp.roll(block_e, 1)))
    grp = jnp.cumsum(first.astype(I32)) - 1
    later_first = jnp.where(first[None, :] & (blk[None, :] > blk[:, None]), blk[None, :], nb)
    nxt_blk = jnp.min(later_first, axis=1)
    nxt = jnp.where(nxt_blk < nb, block_e[jnp.minimum(nxt_blk, nb - 1)], -1)
    bias = lambda i, be, nu, g, nx: (layer, be[i], 0, 0)
    return pl.pallas_call(
        functools.partial(_experts_kernel, layer=layer),
        out_shape=jax.ShapeDtypeStruct((rows, dp), U32),
        grid_spec=pltpu.PrefetchScalarGridSpec(
            num_scalar_prefetch=4,
            grid=(nb,),
            in_specs=[pl.BlockSpec((MOE_ROWS, dp),
                                   lambda i, be, nu, g, nx: (jnp.maximum(jnp.minimum(i, nu[0] - 1), 0), 0)),
                      pl.BlockSpec(memory_space=pl.ANY),
                      pl.BlockSpec((None, None, 1, ff2), bias),
                      pl.BlockSpec(memory_space=pl.ANY),
                      pl.BlockSpec((None, None, 1, d), bias)],
            out_specs=pl.BlockSpec((MOE_ROWS, dp), lambda i, be, nu, g, nx: (i, 0)),
            scratch_shapes=[pltpu.VMEM((2, d, ff2), F32), pltpu.VMEM((2, ff2 // 2, d), F32),
                            pltpu.VMEM((d, ff2), BF16), pltpu.VMEM((ff2 // 2, d), BF16),
                            pltpu.SemaphoreType.DMA((2,))]),
        compiler_params=_cparams(("arbitrary",), has_side_effects=True),
        name="moe_experts",
    )(block_e, n_used, grp.astype(I32), nxt.astype(I32), xs, wgu, bgu, wdn, bdn)


def _combine_kernel(dest_ref, y_ref, gate_ref, x_ref, mod_ref, o_ref, buf_ref, sem):
    n = dest_ref.shape[0]
    tm = n // TOP_K

    def issue(t, c):
        for k in range(TOP_K):
            pltpu.make_async_copy(_row(y_ref, dest_ref[t * TOP_K + k]), _row(buf_ref.at[k], t), sem).start()
        return c

    lax.fori_loop(0, tm, issue, 0, unroll=DMA_UNROLL)
    for k in range(TOP_K):
        pltpu.make_async_copy(y_ref.at[pl.ds(0, tm), :], buf_ref.at[k], sem).wait()

    gates = gate_ref[...]
    acc = gates[:, 0:1] * _unpack_bf16_pairs(buf_ref[0])
    for k in range(1, TOP_K):
        acc = acc + gates[:, k:k + 1] * _unpack_bf16_pairs(buf_ref[k])
    o_ref[...] = x_ref[...] + mod_ref[5:6, :] * acc


def _combine(y, dest, gates, xs, mods, t_off):
    b, t_out, d = xs.shape
    tile = COMBINE_TILE_LATENT if (t_off > 0 and t_out % COMBINE_TILE_LATENT == 0) else TOK_TILE
    nt = t_out // tile
    per = tile * TOP_K
    row = functools.partial(_mod_row, ctx_row=b)
    tok = lambda bi, ti: (bi, ti, 0)
    return pl.pallas_call(
        _combine_kernel,
        out_shape=jax.ShapeDtypeStruct((b, t_out, d), F32),
        grid=(b, nt),
        in_specs=[pl.BlockSpec((per,), lambda bi, ti: (bi * nt + ti,), memory_space=pltpu.SMEM),
                  pl.BlockSpec(memory_space=pl.ANY),
                  pl.BlockSpec((None, tile, LANES), tok),
                  pl.BlockSpec((None, tile, d), tok),
                  pl.BlockSpec((None, 6, d), lambda bi, ti: (row(bi, ti + t_off), 0, 0))],
        out_specs=pl.BlockSpec((None, tile, d), tok),
        scratch_shapes=[pltpu.VMEM((TOP_K, tile, y.shape[1]), y.dtype), pltpu.SemaphoreType.DMA(())],
        compiler_params=_cparams(("arbitrary", "arbitrary"), disable_bounds_checks=True),
        name="moe_combine",
    )(dest, y, gates, xs, mods)


def _routing(e_sel, rank, counts, n):
    padded = (counts + MOE_ROWS - 1) // MOE_ROWS * MOE_ROWS
    pend = jnp.cumsum(padded)
    pstart = pend - padded
    dest = jnp.sum(jnp.where(e_sel[..., None] == jnp.arange(N_EXPERTS), pstart, 0), axis=-1) + rank
    n_blocks = n * TOP_K // MOE_ROWS + N_EXPERTS
    first_row = jnp.arange(n_blocks) * MOE_ROWS
    block_e = jnp.minimum(jnp.sum(pend[None, :] <= first_row[:, None], axis=1), N_EXPERTS - 1)
    n_used = (pend[-1] // MOE_ROWS).reshape(1)
    pads = ((pstart + counts).astype(I32), (padded - counts).astype(I32))
    return dest.reshape(-1).astype(I32), block_e.astype(I32), n_used.astype(I32), pads, n_blocks * MOE_ROWS


def _moe(f, route, gates, seen, x_mid, mods, t_off, layer, wgu, bgu, wdn, bdn):
    b, t_out, d = f.shape
    n = b * t_out
    route = route.reshape(n, LANES)
    counts = seen[0, :N_EXPERTS].astype(I32)
    dest, block_e, n_used, pads, rows = _routing(route[:, :TOP_K], route[:, TOP_K:2 * TOP_K], counts, n)
    xs = _dispatch(f.reshape(n, d), dest, *pads, n_used, rows)
    y = _experts(xs, block_e, n_used, layer, wgu, bgu, wdn, bdn)
    return _combine(y, dest, gates, x_mid, mods, t_off)


def _axial_angles(n_rows, rot_dim):
    row = np.repeat(np.arange(n_rows, dtype=np.float32), GRID_W)
    col = np.tile(np.arange(GRID_W, dtype=np.float32), n_rows)
    n = rot_dim // 4
    freqs = (np.float32(ROPE_THETA) ** (-np.arange(n, dtype=np.float32) / np.float32(n))).astype(np.float32)
    return np.concatenate([row[:, None] * freqs, col[:, None] * freqs], axis=-1).astype(np.float32)


def _rope_tables(ctx_len, s_len, rot_dim, lane_base, reps):
    ang = _axial_angles(s_len // GRID_W, rot_dim)
    half = rot_dim // 2
    period = LANES // reps
    cos = np.ones((ctx_len + s_len, period), np.float32)
    slo = np.zeros((ctx_len + s_len, period), np.float32)
    shi = np.zeros((ctx_len + s_len, period), np.float32)
    cos[ctx_len:, lane_base:lane_base + rot_dim] = np.tile(np.cos(ang), (1, 2))
    slo[ctx_len:, lane_base:lane_base + half] = -np.sin(ang)
    shi[ctx_len:, lane_base + half:lane_base + rot_dim] = np.sin(ang)
    return [jnp.asarray(np.tile(tb, (1, reps))) for tb in (cos, slo, shi)]


def _segment_mean_matrix(width, period, segs):
    lane = np.arange(width)
    seg_id = np.full((width,), -1)
    seg_w = np.zeros((width,), np.float32)
    for i, (start, length) in enumerate(segs):
        inside = ((lane % period) >= start) & ((lane % period) < start + length)
        seg_id = np.where(inside, (lane // period) * len(segs) + i, seg_id)
        seg_w = np.where(inside, np.float32(1.0 / length), seg_w)
    same = (seg_id[:, None] == seg_id[None, :]) & (seg_id[:, None] >= 0)
    return jnp.asarray(np.where(same, seg_w[None, :], np.float32(0.0)), dtype=BF16)


def _block_diag(w):
    n, c, _ = w.shape
    eye = jnp.eye(n, dtype=w.dtype)
    return (eye[:, None, :, None] * w[:, :, None, :]).reshape(n * c, n * c)


def kernel(x, c, ctx, c_ctx, w_mod, b_mod, norm_mix, norm_ffn, w_in_even, lru_conv_w, lru_conv_b, lru_w_r, lru_b_r, lru_w_i, lru_b_i, lru_lambda, mla_q_a_norm, mla_w_q_b, mla_kv_a_norm, mla_w_kv_b, mla_nope_norm, mla_rope_norm, w_out_even, w_qkv_odd, gqa_qk_norm, gqa_sink, w_out_odd, w_router, b_router, w_gate_up, b_gate_up, w_down, b_down):
    b, s_len, d = x.shape
    ctx_len = ctx.shape[1]
    depth = w_mod.shape[0]
    assert depth == 2 and ctx_len == TOK_TILE and s_len % TOK_TILE == 0 and b + 1 <= SUBLANES
    lru_w = lru_conv_w.shape[-1]
    q_lora = mla_q_a_norm.shape[-1]
    kv_lora = mla_kv_a_norm.shape[-1]

    cvec = jnp.concatenate([c, c_ctx[None], jnp.zeros((SUBLANES - b - 1, d), F32)], axis=0)
    mods = _modulation(cvec, w_mod, b_mod).reshape(depth, SUBLANES, 6, d)

    wr_pad = jnp.zeros((depth, d, LANES), F32).at[:, :, :N_EXPERTS].set(w_router)
    br_pad = jnp.full((depth, 1, LANES), NEG_INF, F32).at[:, 0, :N_EXPERTS].set(b_router)
    experts = (w_gate_up, b_gate_up[:, :, None, :], w_down, b_down[:, :, None, :])

    n_in = w_in_even.shape[-1]
    n_in_pad = -(-n_in // LANES) * LANES
    w_in = jnp.zeros((d, n_in_pad), F32).at[:, :n_in].set(w_in_even[0]).astype(BF16)
    xa, ga, mla_in = _even_in(ctx, x, mods[0], norm_mix[0][None], w_in, lru_w)

    nh = lru_w // (2 * LANES)
    per = LRU_BLOCKS // nh
    blk = lru_w // LRU_BLOCKS
    w_gates = jnp.stack([
        jnp.stack([jnp.concatenate([_block_diag(lru_w_r[0, dd, h * per:(h + 1) * per]),
                                    _block_diag(lru_w_i[0, dd, h * per:(h + 1) * per])], axis=1)
                   for h in range(nh)]) for dd in range(2)]).astype(BF16)
    assert blk * per == 2 * LANES
    ya = _lru(xa, ga, lru_conv_w[0], lru_conv_b[0][:, None, :], w_gates, lru_b_r[0][:, None, :],
              lru_b_i[0][:, None, :], jax.nn.softplus(-lru_lambda[0])[:, None, :], ctx_len)

    qk = MLA_NOPE + MLA_ROPE
    hq = MLA_HEADS * LANES
    wq = jnp.zeros((q_lora, MLA_HEADS, LANES), F32).at[:, :, :qk].set(
        mla_w_q_b[0].reshape(q_lora, MLA_HEADS, qk)).reshape(q_lora, hq).astype(BF16)
    wkv = mla_w_kv_b[0].reshape(kv_lora, MLA_HEADS, MLA_NOPE + MLA_V)
    wk = jnp.zeros((kv_lora, MLA_HEADS, LANES), F32).at[:, :, :MLA_NOPE].set(
        wkv[:, :, :MLA_NOPE]).reshape(kv_lora, hq).astype(BF16)
    wv = jnp.zeros((kv_lora, MLA_HEADS, LANES), F32).at[:, :, :MLA_V].set(
        wkv[:, :, MLA_NOPE:]).reshape(kv_lora, hq).astype(BF16)
    one = jnp.asarray(np.tile(np.arange(LANES) == MLA_V, MLA_HEADS)[None], dtype=F32)
    zpad = jnp.zeros((LANES - qk,), F32)
    gq = jnp.tile(jnp.concatenate([mla_nope_norm[0, 0], mla_rope_norm[0, 0], zpad]), MLA_HEADS)[None]
    gk = jnp.tile(jnp.concatenate([mla_nope_norm[0, 1], jnp.zeros((LANES - MLA_NOPE,), F32)]), MLA_HEADS)[None]
    gkr = jnp.concatenate([mla_rope_norm[0, 1], jnp.zeros((LANES - MLA_ROPE,), F32)])[None]
    cos, slo, shi = _rope_tables(ctx_len, s_len, MLA_ROPE, MLA_NOPE, 1)
    mla_p = dict(qan=mla_q_a_norm[0][None], kvn=mla_kv_a_norm[0][None], wq=wq, wk=wk, wv=wv,
                 mq=_segment_mean_matrix(MXU_TILE,LANES, [(0, MLA_NOPE), (MLA_NOPE, MLA_ROPE)]),
                 mk=_segment_mean_matrix(MXU_TILE,LANES, [(0, MLA_NOPE)]),
                 gq=gq, gk=gk, gkr=gkr, one=one, cos=cos, slo=slo, shi=shi)
    q, k, v = _mla_prep(mla_in, mla_p, qk ** -0.5 * math.log2(math.e))
    yb = _mla_attn(q, k, v, ctx_len)

    x_mid, f, route, gates, seen = _post_mix(ya, yb, 0, (ctx, x), mods[0], norm_ffn[0][None], w_out_even[0].astype(BF16),
                                             wr_pad[0], br_pad[0], 0, 0)
    xs = _moe(f, route, gates, seen, x_mid, mods[0], 0, 0, *experts)

    nq = GQA_HEADS * GQA_DIM
    nkv = GQA_KV_HEADS * GQA_DIM
    cos, slo, shi = _rope_tables(ctx_len, s_len, GQA_DIM, 0, LANES // GQA_DIM)
    odd_p = dict(w=w_qkv_odd[0].astype(BF16),
                 mq=_segment_mean_matrix(MXU_TILE,GQA_DIM, [(0, GQA_DIM)]),
                 mk=_segment_mean_matrix(MXU_TILE,GQA_DIM, [(0, GQA_DIM)]),
                 gq=jnp.tile(gqa_qk_norm[0, 0], GQA_HEADS)[None], gk=jnp.tile(gqa_qk_norm[0, 1], GQA_KV_HEADS)[None],
                 cos=cos, slo=slo, shi=shi)
    q, k, v = _odd_in(xs, mods[1], norm_mix[1][None], odd_p, GQA_DIM ** -0.5)
    o = _win_attn(q, k, v, gqa_sink[0], ctx_len)

    t_off = ctx_len // TOK_TILE
    x_mid, f, route, gates, seen = _post_mix(o, o, 1, xs, mods[1], norm_ffn[1][None], w_out_odd[0].astype(BF16),
                                             wr_pad[1], br_pad[1], t_off, 0)
    return _moe(f, route, gates, seen, x_mid, mods[1], t_off, 1, *experts)
```

```python
import functools
import math

import jax
import jax.numpy as jnp
import numpy as np
from jax import lax
from jax.experimental import pallas as pl
from jax.experimental.pallas import tpu as pltpu

F32 = jnp.float32
BF16 = jnp.bfloat16
I32 = jnp.int32

GRID_W = 64
LRU_BLOCKS = 8
LRU_C = 8.0
CONV_W = 4
MLA_HEADS = 8
MLA_NOPE = 64
MLA_ROPE = 32
MLA_V = 64
GQA_HEADS = 16
GQA_KV_HEADS = 4
GQA_DIM = 64
WINDOW = 128
ROPE_THETA = 10000.0
NEG_INF = -1e30
EPS = 1e-6
N_EXPERTS = 32
TOP_K = 4
SWIGLU_LIMIT = 7.0
SWIGLU_ALPHA = 1.702

LANES = 128
SUBLANES = 8
TOK_TILE = 256
LRU_CHUNK = 256
WIN_BLOCK = 128
WIN_BLOCKS_PER_STEP = 2
MOE_ROWS = 256
MXU_TILE = 256
MLA_HEADS_PER_STEP = 4
VMEM_LIMIT = 48 * 1024 * 1024


def _cparams(sem, **kw):
    return pltpu.CompilerParams(dimension_semantics=sem, vmem_limit_bytes=VMEM_LIMIT, **kw)


def _dot(a, b):
    return jnp.dot(a, b, preferred_element_type=F32)


def _dot_nt(a, b):
    return lax.dot_general(a, b, (((1,), (1,)), ((), ())), preferred_element_type=F32)


def _split_bf16(x):
    hi = x.astype(BF16)
    lo = (x - hi.astype(F32)).astype(BF16)
    return hi, lo


def _dot3(a, w):
    ah, al = _split_bf16(a)
    wh, wl = _split_bf16(w)
    return _dot(ah, wh) + _dot(al, wh) + _dot(ah, wl)


def _rms(x):
    return x * lax.rsqrt(jnp.mean(x * x, axis=-1, keepdims=True) + EPS)


def _prenorm(x, g, scale, shift):
    return (_rms(x) * g) * (1.0 + scale) + shift


def _rope(x, cos, sin_lo, sin_hi, half):
    w = x.shape[-1]
    return x * cos + pltpu.roll(x, w - half, axis=1) * sin_lo + pltpu.roll(x, half, axis=1) * sin_hi


U32 = jnp.uint32
HI_HALF = 0xFFFF0000


def _pack_bf16_pairs(x):
    w = x.shape[1] // 2
    lo = pltpu.bitcast(x[:, :w].astype(BF16).astype(F32), U32) >> 16
    hi = pltpu.bitcast(x[:, w:].astype(BF16).astype(F32), U32) & U32(HI_HALF)
    return lo | hi


def _unpack_bf16_pairs(words):
    lo = pltpu.bitcast(words << 16, F32)
    hi = pltpu.bitcast(words & U32(HI_HALF), F32)
    return jnp.concatenate([lo, hi], axis=1)


def _head_mean_square(y, m_ref):
    w = m_ref.shape[0]
    return jnp.concatenate([_dot((y[:, j:j + w] * y[:, j:j + w]).astype(BF16), m_ref[...])
                            for j in range(0, y.shape[1], w)], axis=1)


def _tile_lanes(t, reps):
    return jnp.concatenate([t] * reps, axis=1) if reps > 1 else t


def _mod_kernel(c_ref, w_ref, b_ref, o_ref):
    c = c_ref[...]
    o_ref[...] = _dot3(c * jax.nn.sigmoid(c), w_ref[...]) + b_ref[...]


def _modulation(cvec, w_mod, b_mod):
    depth, d, n = w_mod.shape
    tn = 1536
    return pl.pallas_call(
        _mod_kernel,
        out_shape=jax.ShapeDtypeStruct((depth, SUBLANES, n), F32),
        grid=(depth, n // tn),
        in_specs=[pl.BlockSpec((SUBLANES, d), lambda l, j: (0, 0)),
                  pl.BlockSpec((None, d, tn), lambda l, j: (l, 0, j)),
                  pl.BlockSpec((None, 1, tn), lambda l, j: (l, 0, j))],
        out_specs=pl.BlockSpec((None, SUBLANES, tn), lambda l, j: (l, 0, j)),
        compiler_params=_cparams(("arbitrary", "arbitrary")),
        name="modulation",
    )(cvec, w_mod, b_mod.reshape(depth, 1, n))


def _mod_row(b, t, ctx_row):
    return jnp.where(t == 0, ctx_row, b)


def _stream_tile(ctx_ref, lat_ref):
    return jnp.where(pl.program_id(1) == 0, ctx_ref[...], lat_ref[...])


def _stream_specs(d):
    return [pl.BlockSpec((None, TOK_TILE, d), lambda bi, ti: (bi, 0, 0)),
            pl.BlockSpec((None, TOK_TILE, d), lambda bi, ti: (bi, jnp.maximum(ti - 1, 0), 0))]


def _even_in_kernel(ctx_ref, x_ref, mod_ref, g_ref, w_ref, xa_ref, ga_ref, mla_ref):
    mod = mod_ref[...]
    h = _prenorm(_stream_tile(ctx_ref, x_ref), g_ref[...], mod[1:2], mod[0:1])
    z = _dot(h.astype(BF16), w_ref[...])
    c = xa_ref.shape[-1]
    xa_ref[...] = z[:, :c]
    ga_ref[...] = z[:, c:2 * c]
    mla_ref[...] = z[:, 2 * c:]


def _even_in(ctx, x, mods, g, w_pad, lru_w):
    b, s_len, d = x.shape
    t = ctx.shape[1] + s_len
    nt = t // TOK_TILE
    n_out = w_pad.shape[1]
    n_mla = n_out - 2 * lru_w
    row = functools.partial(_mod_row, ctx_row=b)
    tok = lambda bi, ti: (bi, ti, 0)
    return pl.pallas_call(
        _even_in_kernel,
        out_shape=(jax.ShapeDtypeStruct((b, t, lru_w), F32),
                   jax.ShapeDtypeStruct((b, t, lru_w), F32),
                   jax.ShapeDtypeStruct((b, t, n_mla), F32)),
        grid=(b, nt),
        in_specs=_stream_specs(d) + [
            pl.BlockSpec((None, 6, d), lambda bi, ti: (row(bi, ti), 0, 0)),
            pl.BlockSpec((1, d), lambda bi, ti: (0, 0)),
            pl.BlockSpec((d, n_out), lambda bi, ti: (0, 0))],
        out_specs=(pl.BlockSpec((None, TOK_TILE, lru_w), tok),
                   pl.BlockSpec((None, TOK_TILE, lru_w), tok),
                   pl.BlockSpec((None, TOK_TILE, n_mla), tok)),
        compiler_params=_cparams(("arbitrary", "arbitrary")),
        name="even_in",
    )(ctx, x, mods, g, w_pad)


def _lru_kernel(xa_ref, ga_ref, cw_ref, cb_ref, wg_ref, br_ref, bi_ref, sp_ref, o_ref, pad_ref, rec_ref, *, ctx_len):
    t, c = xa_ref.shape
    tc = LRU_CHUNK
    halo = SUBLANES
    n_chunks = t // tc
    n_ctx = ctx_len // tc
    groups = tc // SUBLANES

    pad_ref[0:halo, :] = jnp.zeros((halo, c), F32)
    pad_ref[t + halo:t + 2 * halo, :] = jnp.zeros((halo, c), F32)
    pad_ref[halo:t + halo, :] = xa_ref[...]

    rid = lax.broadcasted_iota(I32, (tc, 1), 0)
    sub = rid % SUBLANES

    for d in range(2):
        cw = cw_ref[d]
        cb = cb_ref[d]
        wg = wg_ref[d]
        b_r = br_ref[d]
        b_i = bi_ref[d]
        sp = sp_ref[d]

        def chunk(i, h, d=d, cw=cw, cb=cb, wg=wg, b_r=b_r, b_i=b_i, sp=sp):
            if d == 0:
                ci = i
            else:
                ci = jnp.where(i < n_ctx, n_ctx - 1 - i, n_chunks - 1 - (i - n_ctx))
            r0 = pl.multiple_of(ci * tc, tc)
            win = pad_ref[pl.ds(r0, tc + 2 * halo), :]
            if d == 0:
                past = jnp.where(ci == n_ctx, 0.0, win[0:halo])
                win = jnp.concatenate([past, win[halo:]], axis=0)
            else:
                past = jnp.where(ci == n_ctx - 1, 0.0, win[halo + tc:])
                win = jnp.concatenate([win[:halo + tc], past], axis=0)
            xc = jnp.zeros((tc, c), F32) + cb
            for k in range(CONV_W):
                off = (k - (CONV_W - 1)) if d == 0 else ((CONV_W - 1) - k)
                if off == 0:
                    src = win[halo:halo + tc]
                else:
                    src = pltpu.roll(win, (-off) % (tc + 2 * halo), axis=0)[halo:halo + tc]
                xc = xc + cw[k:k + 1] * src
            gz = _dot(xc.astype(BF16), wg)
            r = 0.5 * jnp.tanh(0.5 * (gz[:, :c] + b_r)) + 0.5
            gi = 0.5 * jnp.tanh(0.5 * (gz[:, c:] + b_i)) + 0.5
            log_a = (-LRU_C) * r * sp
            a = jnp.exp(log_a)
            th = jnp.tanh(log_a)
            bb = jnp.sqrt(-2.0 * th / (1.0 - th)) * (gi * xc)
            for s in (1, 2, 4):
                if d == 0:
                    ok = sub >= s
                    sh = s
                else:
                    ok = sub <= (SUBLANES - 1 - s)
                    sh = tc - s
                a_prev = jnp.where(ok, pltpu.roll(a, sh, axis=0), 1.0)
                b_prev = jnp.where(ok, pltpu.roll(bb, sh, axis=0), 0.0)
                bb = a * b_prev + bb
                a = a * a_prev
            outs = [None] * groups
            order = range(groups) if d == 0 else range(groups - 1, -1, -1)
            for g in order:
                lo = g * SUBLANES
                hg = a[lo:lo + SUBLANES] * h + bb[lo:lo + SUBLANES]
                outs[g] = hg
                h = hg[SUBLANES - 1:SUBLANES] if d == 0 else hg[0:1]
            hs = jnp.concatenate(outs, axis=0)
            if d == 0:
                rec_ref[pl.ds(r0, tc), :] = hs
            else:
                tot = rec_ref[pl.ds(r0, tc), :] + hs
                gate = jax.nn.gelu(ga_ref[pl.ds(r0, tc), :], approximate=True)
                o_ref[pl.ds(r0, tc), :] = (tot * gate).astype(o_ref.dtype)
            return h

        lax.fori_loop(0, n_chunks, chunk, jnp.zeros((1, c), F32))


def _lru(xa, ga, conv_w, conv_b, w_gates, b_r, b_i, sp, ctx_len):
    b, t, w = xa.shape
    c = 2 * LANES
    nh = w // c
    tok = lambda bi, hi: (bi, 0, hi)
    par = lambda bi, hi: (0, 0, hi)
    return pl.pallas_call(
        functools.partial(_lru_kernel, ctx_len=ctx_len),
        out_shape=jax.ShapeDtypeStruct((b, t, w), BF16),
        grid=(b, nh),
        in_specs=[pl.BlockSpec((None, t, c), tok),
                  pl.BlockSpec((None, t, c), tok),
                  pl.BlockSpec((2, CONV_W, c), par),
                  pl.BlockSpec((2, 1, c), par),
                  pl.BlockSpec((2, None, c, 2 * c), lambda bi, hi: (0, hi, 0, 0)),
                  pl.BlockSpec((2, 1, c), par),
                  pl.BlockSpec((2, 1, c), par),
                  pl.BlockSpec((2, 1, c), par)],
        out_specs=pl.BlockSpec((None, t, c), tok),
        scratch_shapes=[pltpu.VMEM((t + 2 * SUBLANES, c), F32), pltpu.VMEM((t, c), F32)],
        compiler_params=_cparams(("arbitrary", "arbitrary")),
        name="rglru",
    )(xa, ga, conv_w, conv_b, w_gates, b_r, b_i, sp)


def _mla_prep_kernel(in_ref, qan_ref, kvn_ref, wq_ref, wk_ref, wv_ref, mq_ref, mk_ref, gq_ref, gk_ref, gkr_ref,
                     one_ref, cos_ref, slo_ref, shi_ref, q_ref, k_ref, v_ref, *, q_lora, kv_lora, scale):
    z = in_ref[...]
    heads = q_ref.shape[-1] // LANES
    cos = cos_ref[...]
    slo = slo_ref[...]
    shi = shi_ref[...]
    half = MLA_ROPE // 2

    qan = (_rms(z[:, :q_lora]) * qan_ref[...]).astype(BF16)
    q = _dot(qan, wq_ref[...])
    q = q * lax.rsqrt(_head_mean_square(q, mq_ref) + EPS) * gq_ref[...]
    q = _rope(q, _tile_lanes(cos, heads), _tile_lanes(slo, heads), _tile_lanes(shi, heads), half)
    q_ref[...] = (q * scale).astype(BF16)

    kvn = (_rms(z[:, q_lora:q_lora + kv_lora]) * kvn_ref[...]).astype(BF16)
    kk = _dot(kvn, wk_ref[...])
    kk = kk * lax.rsqrt(_head_mean_square(kk, mk_ref) + EPS) * gk_ref[...]
    v_ref[...] = (_dot(kvn, wv_ref[...]) + one_ref[...]).astype(BF16)

    kr = z[:, q_lora + kv_lora:]
    kr = kr * lax.rsqrt(jnp.sum(kr * kr, axis=-1, keepdims=True) * (1.0 / MLA_ROPE) + EPS) * gkr_ref[...]
    kr = pltpu.roll(kr, MLA_NOPE, axis=1)
    kr = _rope(kr, cos, slo, shi, half)
    k_ref[...] = (kk + _tile_lanes(kr, heads)).astype(BF16)


def _mla_prep(mla_in, p, scale):
    b, t, w = mla_in.shape
    hq = MLA_HEADS * LANES
    tok = lambda bi, ti: (bi, ti, 0)
    full = lambda a: pl.BlockSpec(a.shape, lambda bi, ti: (0,) * a.ndim)
    pos = pl.BlockSpec((TOK_TILE, LANES), lambda bi, ti: (ti, 0))
    consts = [p["qan"], p["kvn"], p["wq"], p["wk"], p["wv"], p["mq"], p["mk"], p["gq"], p["gk"], p["gkr"], p["one"]]
    out = jax.ShapeDtypeStruct((b, t, hq), BF16)
    return pl.pallas_call(
        functools.partial(_mla_prep_kernel, q_lora=p["qan"].shape[1], kv_lora=p["kvn"].shape[1], scale=scale),
        out_shape=(out, out, out),
        grid=(b, t // TOK_TILE),
        in_specs=[pl.BlockSpec((None, TOK_TILE, w), tok)] + [full(a) for a in consts] + [pos, pos, pos],
        out_specs=(pl.BlockSpec((None, TOK_TILE, hq), tok),) * 3,
        compiler_params=_cparams(("arbitrary", "arbitrary")),
        name="mla_prep",
    )(mla_in, *consts, p["cos"], p["slo"], p["shi"])


def _mla_attn_kernel(q_ref, k_ref, v_ref, o_ref, *, ctx_len):
    tq = q_ref.shape[0]
    t = k_ref.shape[0]
    lane = lax.broadcasted_iota(I32, (tq, LANES), 1)

    def attend(nk):
        outs = []
        for hh in range(q_ref.shape[1] // LANES):
            q = q_ref[:, hh * LANES:(hh + 1) * LANES]
            k = k_ref[0:nk, hh * LANES:(hh + 1) * LANES]
            s = _dot_nt(q, k)
            p = jnp.exp2(s - jnp.max(s, axis=-1, keepdims=True))
            o = _dot(p.astype(BF16), v_ref[0:nk, hh * LANES:(hh + 1) * LANES])
            outs.append(o / o[:, MLA_V:MLA_V + 1])
        pairs = [jnp.where(lane < MLA_V, outs[j], pltpu.roll(outs[j + 1], MLA_V, axis=1))
                 for j in range(0, len(outs), 2)]
        o_ref[...] = jnp.concatenate(pairs, axis=1).astype(o_ref.dtype)

    @pl.when(pl.program_id(2) == 0)
    def _():
        attend(ctx_len)

    @pl.when(pl.program_id(2) > 0)
    def _():
        attend(t)


def _mla_attn(q, k, v, ctx_len):
    b, t, hq = q.shape
    hps = MLA_HEADS_PER_STEP
    groups = hq // (hps * LANES)
    return pl.pallas_call(
        functools.partial(_mla_attn_kernel, ctx_len=ctx_len),
        out_shape=jax.ShapeDtypeStruct((b, t, groups * hps * MLA_V), BF16),
        grid=(b, groups, t // TOK_TILE),
        in_specs=[pl.BlockSpec((None, TOK_TILE, hps * LANES), lambda bi, hi, ti: (bi, ti, hi)),
                  pl.BlockSpec((None, t, hps * LANES), lambda bi, hi, ti: (bi, 0, hi)),
                  pl.BlockSpec((None, t, hps * LANES), lambda bi, hi, ti: (bi, 0, hi))],
        out_specs=pl.BlockSpec((None, TOK_TILE, hps * MLA_V), lambda bi, hi, ti: (bi, ti, hi)),
        compiler_params=_cparams(("arbitrary", "arbitrary", "arbitrary")),
        name="mla_attn",
    )(q, k, v)


def _odd_in_kernel(x_ref, mod_ref, g_ref, w_ref, mq_ref, mk_ref, gq_ref, gk_ref, cos_ref, slo_ref, shi_ref,
                   q_ref, k_ref, v_ref, *, scale):
    mod = mod_ref[...]
    h = _prenorm(x_ref[...], g_ref[...], mod[1:2], mod[0:1]).astype(BF16)
    nq = q_ref.shape[-1]
    nk = k_ref.shape[-1]
    half = GQA_DIM // 2

    def head_norm_rope(y, m_ref, gain_ref):
        reps = y.shape[-1] // LANES
        y = y * lax.rsqrt(_head_mean_square(y, m_ref) + EPS) * gain_ref[...]
        return _rope(y, _tile_lanes(cos_ref[...], reps), _tile_lanes(slo_ref[...], reps),
                     _tile_lanes(shi_ref[...], reps), half)

    q_ref[...] = (head_norm_rope(_dot(h, w_ref[:, 0:nq]), mq_ref, gq_ref) * scale).astype(BF16)
    k_ref[...] = head_norm_rope(_dot(h, w_ref[:, nq:nq + nk]), mk_ref, gk_ref).astype(BF16)
    v_ref[...] = _dot(h, w_ref[:, nq + nk:nq + 2 * nk]).astype(BF16)


def _odd_in(xs, mods, g, p, scale):
    b, t, d = xs.shape
    nq = p["gq"].shape[1]
    nk = p["gk"].shape[1]
    row = functools.partial(_mod_row, ctx_row=b)
    tok = lambda bi, ti: (bi, ti, 0)
    full = lambda a: pl.BlockSpec(a.shape, lambda bi, ti: (0,) * a.ndim)
    pos = pl.BlockSpec((TOK_TILE, LANES), lambda bi, ti: (ti, 0))
    kv = jax.ShapeDtypeStruct((b, t, nk), BF16)
    return pl.pallas_call(
        functools.partial(_odd_in_kernel, scale=scale),
        out_shape=(jax.ShapeDtypeStruct((b, t, nq), BF16), kv, kv),
        grid=(b, t // TOK_TILE),
        in_specs=[pl.BlockSpec((None, TOK_TILE, d), tok),
                  pl.BlockSpec((None, 6, d), lambda bi, ti: (row(bi, ti), 0, 0)),
                  pl.BlockSpec((1, d), lambda bi, ti: (0, 0)),
                  full(p["w"]), full(p["mq"]), full(p["mk"]), full(p["gq"]), full(p["gk"]), pos, pos, pos],
        out_specs=(pl.BlockSpec((None, TOK_TILE, nq), tok),
                   pl.BlockSpec((None, TOK_TILE, nk), tok),
                   pl.BlockSpec((None, TOK_TILE, nk), tok)),
        compiler_params=_cparams(("arbitrary", "arbitrary")),
        name="odd_in",
    )(xs, mods, g, p["w"], p["mq"], p["mk"], p["gq"], p["gk"], p["cos"], p["slo"], p["shi"])


def _win_attn_kernel(sink_ref, q_ref, k_ref, v_ref, o_ref, *, ctx_len):
    t = k_ref.shape[0]
    wb = WIN_BLOCK
    w3 = 3 * wb
    hd = GQA_DIM
    group = GQA_HEADS // GQA_KV_HEADS
    nk = w3 + ctx_len
    rows = group * wb
    row = lax.broadcasted_iota(I32, (rows, nk), 0)
    col = lax.broadcasted_iota(I32, (rows, nk), 1)
    head_of_row = lax.broadcasted_iota(I32, (rows, 1), 0) // wb
    kctx = k_ref[0:ctx_len, :]
    vctx = v_ref[0:ctx_len, :]
    for sub in range(WIN_BLOCKS_PER_STEP):
        i = pl.program_id(1) * WIN_BLOCKS_PER_STEP + sub
        q0 = ctx_len + i * wb
        ws = pl.multiple_of(jnp.clip(q0 - wb, ctx_len, t - w3), wb)
        valid = (col >= w3) | (jnp.abs(q0 + row % wb - (ws + col)) <= WINDOW)
        q = q_ref[sub * wb:(sub + 1) * wb, :].astype(F32)
        kcat = jnp.concatenate([k_ref[pl.ds(ws, w3), :], kctx], axis=0).astype(F32)
        vcat = jnp.concatenate([v_ref[pl.ds(ws, w3), :], vctx], axis=0).astype(F32)
        outs = []
        for kh in range(GQA_KV_HEADS):
            qs = jnp.concatenate([q[:, (kh * group + h) * hd:(kh * group + h + 1) * hd] for h in range(group)],
                                 axis=0)
            s = _dot_nt(qs.astype(BF16), kcat[:, kh * hd:(kh + 1) * hd].astype(BF16))
            s = jnp.where(valid, s, NEG_INF)
            sink = jnp.zeros((rows, 1), F32)
            for h in range(group):
                sink = jnp.where(head_of_row == h, sink_ref[kh * group + h], sink)
            m = jnp.maximum(jnp.max(s, axis=-1, keepdims=True), sink)
            p = jnp.exp(s - m)
            l = jnp.sum(p, axis=-1, keepdims=True) + jnp.exp(sink - m)
            o = _dot(p.astype(BF16), vcat[:, kh * hd:(kh + 1) * hd].astype(BF16)) / l
            outs.extend(o[h * wb:(h + 1) * wb] for h in range(group))
        o_ref[sub * wb:(sub + 1) * wb, :] = jnp.concatenate(outs, axis=1).astype(o_ref.dtype)


def _win_attn(q, k, v, sink, ctx_len):
    b, t, n = q.shape
    nkv = k.shape[-1]
    s_len = t - ctx_len
    step = WIN_BLOCK * WIN_BLOCKS_PER_STEP
    assert ctx_len % step == 0 and s_len % step == 0
    off = ctx_len // step
    return pl.pallas_call(
        functools.partial(_win_attn_kernel, ctx_len=ctx_len),
        out_shape=jax.ShapeDtypeStruct((b, s_len, n), BF16),
        grid=(b, s_len // step),
        in_specs=[pl.BlockSpec(memory_space=pltpu.SMEM),
                  pl.BlockSpec((None, step, n), lambda bi, ti: (bi, ti + off, 0)),
                  pl.BlockSpec((None, t, nkv), lambda bi, ti: (bi, 0, 0)),
                  pl.BlockSpec((None, t, nkv), lambda bi, ti: (bi, 0, 0))],
        out_specs=pl.BlockSpec((None, step, n), lambda bi, ti: (bi, ti, 0)),
        compiler_params=_cparams(("arbitrary", "arbitrary")),
        name="win_attn",
    )(sink, q, k, v)


def _post_mix_kernel(a1_ref, a2_ref, *refs, split_stream):
    x_in = _stream_tile(refs[0], refs[1]) if split_stream else refs[0][...]
    _post_mix_body(a1_ref, a2_ref, x_in, *refs[2 if split_stream else 1:])


def _post_mix_body(a1_ref, a2_ref, x_in, mod_ref, g_ref, w_ref, wr_ref, br_ref,
                   xo_ref, f_ref, route_ref, gate_ref, cnt_ref, seen_ref):
    half = a1_ref.shape[-1]
    mod = mod_ref[...]
    m = _dot(a1_ref[...], w_ref[0:half, :]) + _dot(a2_ref[...], w_ref[half:2 * half, :])
    x = x_in + mod[2:3] * m
    xo_ref[...] = x
    f = _prenorm(x, g_ref[...], mod[4:5], mod[3:4])
    f_ref[...] = _pack_bf16_pairs(f)

    logit = _dot(f.astype(BF16), wr_ref[...].astype(BF16)) + br_ref[...]
    tm = logit.shape[0]
    lane = lax.broadcasted_iota(I32, (tm, LANES), 1)
    lane_f = lane.astype(F32)
    vals, idxs = [], []
    for _ in range(TOP_K):
        mx = jnp.max(logit, axis=-1, keepdims=True)
        ix = jnp.min(jnp.where(logit == mx, lane_f, float(LANES)), axis=-1, keepdims=True)
        vals.append(mx)
        idxs.append(ix)
        logit = jnp.where(lane_f == ix, -jnp.inf, logit)
    exps = [jnp.exp(v - vals[0]) for v in vals]
    den = exps[0]
    for e in exps[1:]:
        den = den + e

    @pl.when((pl.program_id(0) == 0) & (pl.program_id(1) == 0))
    def _():
        seen_ref[...] = jnp.zeros(seen_ref.shape, F32)

    msk = jnp.zeros((tm, LANES), F32)
    for k in range(TOP_K):
        msk = jnp.where(lane_f == idxs[k], 1.0, msk)
    earlier = (lax.broadcasted_iota(I32, (tm, tm), 1) < lax.broadcasted_iota(I32, (tm, tm), 0))
    rank = _dot(jnp.where(earlier, 1.0, 0.0).astype(BF16), msk.astype(BF16)) + seen_ref[0:1, :]
    seen = seen_ref[...] + jnp.sum(msk, axis=0, keepdims=True)
    seen_ref[...] = seen
    cnt_ref[...] = seen

    r_out = jnp.zeros((tm, LANES), F32)
    g_out = jnp.zeros((tm, LANES), F32)
    for k in range(TOP_K):
        rank_k = jnp.sum(jnp.where(lane_f == idxs[k], rank, 0.0), axis=-1, keepdims=True)
        r_out = jnp.where(lane == k, idxs[k], r_out)
        r_out = jnp.where(lane == TOP_K + k, rank_k, r_out)
        g_out = jnp.where(lane == k, exps[k] / den, g_out)
    route_ref[...] = r_out.astype(I32)
    gate_ref[...] = g_out


def _post_mix(a1, a2, lane_blk2, xs, mods, g, w_out, w_router, b_router, t_off, a_off):
    split_stream = isinstance(xs, tuple)
    if split_stream:
        assert t_off == 0
        b, s_len, d = xs[1].shape
        t = xs[0].shape[1] + s_len
        x_specs = _stream_specs(d)
    else:
        b, t, d = xs.shape
        x_specs = [pl.BlockSpec((None, TOK_TILE, d), lambda bi, ti: (bi, ti + t_off, 0))]
        xs = (xs,)
    half = w_out.shape[0] // 2
    nt = t // TOK_TILE - t_off
    t_out = nt * TOK_TILE
    row = functools.partial(_mod_row, ctx_row=b)
    tok = lambda bi, ti: (bi, ti, 0)
    act = jax.ShapeDtypeStruct((b, t_out, d), F32)
    return pl.pallas_call(
        functools.partial(_post_mix_kernel, split_stream=split_stream),
        out_shape=(act, jax.ShapeDtypeStruct((b, t_out, d // 2), U32),
                   jax.ShapeDtypeStruct((b, t_out, LANES), I32),
                   jax.ShapeDtypeStruct((b, t_out, LANES), F32),
                   jax.ShapeDtypeStruct((SUBLANES, LANES), F32)),
        grid=(b, nt),
        in_specs=[pl.BlockSpec((None, TOK_TILE, half), lambda bi, ti: (bi, ti + a_off, 0)),
                  pl.BlockSpec((None, TOK_TILE, half), lambda bi, ti: (bi, ti + a_off, lane_blk2))] + x_specs + [
                  pl.BlockSpec((None, 6, d), lambda bi, ti: (row(bi, ti + t_off), 0, 0)),
                  pl.BlockSpec((1, d), lambda bi, ti: (0, 0)),
                  pl.BlockSpec(w_out.shape, lambda bi, ti: (0, 0)),
                  pl.BlockSpec(w_router.shape, lambda bi, ti: (0, 0)),
                  pl.BlockSpec((1, LANES), lambda bi, ti: (0, 0))],
        out_specs=(pl.BlockSpec((None, TOK_TILE, d), tok),
                   pl.BlockSpec((None, TOK_TILE, d // 2), tok),
                   pl.BlockSpec((None, TOK_TILE, LANES), tok),
                   pl.BlockSpec((None, TOK_TILE, LANES), tok),
                   pl.BlockSpec((SUBLANES, LANES), lambda bi, ti: (0, 0))),
        scratch_shapes=[pltpu.VMEM((SUBLANES, LANES), F32)],
        compiler_params=_cparams(("arbitrary", "arbitrary")),
        name="post_mix",
    )(a1, a2, *xs, mods, g, w_out, w_router, b_router)


DMA_UNROLL = 2
DISPATCH_TILE = 1024
COMBINE_TILE_LATENT = 1024


def _row(ref, i):
    return ref.at[pl.ds(i, 1), :]


def _dispatch_kernel(pad_lo_ref, pad_n_ref, nu_ref, dest_ref, f_ref, xs_ref, zero_ref, sem, zsem):
    n = dest_ref.shape[0]

    blk = zero_ref.shape[0]
    n_blk = xs_ref.shape[0] // blk

    def block_copy(i):
        return pltpu.make_async_copy(zero_ref, xs_ref.at[pl.ds(pl.multiple_of(i * blk, blk), blk), :], zsem)

    def pad_rows(e, c, wait):
        lo = pad_lo_ref[e]
        head = jnp.minimum((-lo) & (SUBLANES - 1), pad_n_ref[e])

        def one(r, c2):
            copy = pltpu.make_async_copy(_row(zero_ref, 0), _row(xs_ref, lo + r), zsem)
            copy.wait() if wait else copy.start()
            return c2

        def eight(g, c2):
            r0 = pl.multiple_of(lo + head + g * SUBLANES, SUBLANES)
            copy = pltpu.make_async_copy(zero_ref.at[pl.ds(0, SUBLANES), :], xs_ref.at[pl.ds(r0, SUBLANES), :], zsem)
            copy.wait() if wait else copy.start()
            return c2

        c = lax.fori_loop(0, head, one, c)
        return lax.fori_loop(0, (pad_n_ref[e] - head) // SUBLANES, eight, c)

    @pl.when(pl.program_id(0) == 0)
    def _():
        zero_ref[...] = jnp.zeros(zero_ref.shape, zero_ref.dtype)
        lax.fori_loop(nu_ref[0], n_blk, lambda i, c: (block_copy(i).start(), c)[1], 0)
        lax.fori_loop(0, N_EXPERTS, functools.partial(pad_rows, wait=False), 0)

    @pl.when(pl.program_id(0) == pl.num_programs(0) - 1)
    def _():
        lax.fori_loop(nu_ref[0], n_blk, lambda i, c: (block_copy(i).wait(), c)[1], 0)
        lax.fori_loop(0, N_EXPERTS, functools.partial(pad_rows, wait=True), 0)


    def issue(t, c):
        for k in range(TOP_K):
            pltpu.make_async_copy(_row(f_ref, t), _row(xs_ref, dest_ref[t * TOP_K + k]), sem).start()
        return c

    lax.fori_loop(0, n // TOP_K, issue, 0, unroll=DMA_UNROLL)
    pltpu.make_async_copy(xs_ref.at[pl.ds(0, n), :], xs_ref.at[pl.ds(0, n), :], sem).wait()


def _dispatch(f, dest, pad_lo, pad_n, n_used, rows):
    n, d = f.shape
    tile = DISPATCH_TILE
    assert n % tile == 0
    return pl.pallas_call(
        _dispatch_kernel,
        out_shape=jax.ShapeDtypeStruct((rows, d), f.dtype),
        grid_spec=pltpu.PrefetchScalarGridSpec(
            num_scalar_prefetch=3,
            grid=(n // tile,),
            in_specs=[pl.BlockSpec((tile * TOP_K,), lambda i, lo, cnt, nu: (i,), memory_space=pltpu.SMEM),
                      pl.BlockSpec((tile, d), lambda i, lo, cnt, nu: (i, 0))],
            out_specs=pl.BlockSpec(memory_space=pl.ANY),
            scratch_shapes=[pltpu.VMEM((MOE_ROWS, d), f.dtype), pltpu.SemaphoreType.DMA(()),
                            pltpu.SemaphoreType.DMA(())]),
        compiler_params=_cparams(("arbitrary",), has_side_effects=True, disable_bounds_checks=True),
        name="moe_dispatch",
    )(pad_lo, pad_n, n_used, dest, f)


def _experts_kernel(be_ref, nu_ref, grp_ref, nxt_ref, xs_ref, wgu_hbm, bgu_ref, wdn_hbm, bdn_ref, y_ref,
                    wgu_f32, wdn_f32, wgu_bf, wdn_bf, wsem, *, layer):
    i = pl.program_id(0)

    def fetch(e, slot, wait):
        for src, dst in ((wgu_hbm.at[layer, e], wgu_f32.at[slot]), (wdn_hbm.at[layer, e], wdn_f32.at[slot])):
            c = pltpu.make_async_copy(src, dst, wsem.at[slot])
            c.wait() if wait else c.start()

    @pl.when(i == 0)
    def _():
        fetch(be_ref[0], 0, False)

    @pl.when((i < nu_ref[0]) & ((i == 0) | (grp_ref[i] != grp_ref[jnp.maximum(i - 1, 0)])))
    def _():
        slot = grp_ref[i] % 2
        fetch(be_ref[i], slot, True)
        wgu_bf[...] = wgu_f32[slot].astype(BF16)
        wdn_bf[...] = wdn_f32[slot].astype(BF16)

        @pl.when(nxt_ref[i] >= 0)
        def _():
            fetch(nxt_ref[i], 1 - slot, False)

    @pl.when(i < nu_ref[0])
    def _():
        h = _dot(_unpack_bf16_pairs(xs_ref[...]).astype(BF16), wgu_bf[...]) + bgu_ref[...]
        ff = h.shape[1] // 2
        hg = jnp.minimum(h[:, :ff], SWIGLU_LIMIT)
        hu = jnp.clip(h[:, ff:], -SWIGLU_LIMIT, SWIGLU_LIMIT)
        act = hg * jax.nn.sigmoid(SWIGLU_ALPHA * hg) * (hu + 1.0)
        y_ref[...] = _pack_bf16_pairs(_dot(act.astype(BF16), wdn_bf[...]) + bdn_ref[...])

    @pl.when(i >= nu_ref[0])
    def _():
        y_ref[...] = jnp.zeros(y_ref.shape, y_ref.dtype)


def _experts(xs, block_e, n_used, layer, wgu, bgu, wdn, bdn):
    rows, dp = xs.shape
    d = 2 * dp
    ff2 = wgu.shape[-1]
    nb = rows // MOE_ROWS
    blk = jnp.arange(nb)
    used = blk < n_used[0]
    first = used & ((blk == 0) | (block_e != jnp.roll(block_e, 1)))
    grp = jnp.cumsum(first.astype(I32)) - 1
    later_first = jnp.where(first[None, :] & (blk[None, :] > blk[:, None]), blk[None, :], nb)
    nxt_blk = jnp.min(later_first, axis=1)
    nxt = jnp.where(nxt_blk < nb, block_e[jnp.minimum(nxt_blk, nb - 1)], -1)
    bias = lambda i, be, nu, g, nx: (layer, be[i], 0, 0)
    return pl.pallas_call(
        functools.partial(_experts_kernel, layer=layer),
        out_shape=jax.ShapeDtypeStruct((rows, dp), U32),
        grid_spec=pltpu.PrefetchScalarGridSpec(
            num_scalar_prefetch=4,
            grid=(nb,),
            in_specs=[pl.BlockSpec((MOE_ROWS, dp),
                                   lambda i, be, nu, g, nx: (jnp.maximum(jnp.minimum(i, nu[0] - 1), 0), 0)),
                      pl.BlockSpec(memory_space=pl.ANY),
                      pl.BlockSpec((None, None, 1, ff2), bias),
                      pl.BlockSpec(memory_space=pl.ANY),
                      pl.BlockSpec((None, None, 1, d), bias)],
            out_specs=pl.BlockSpec((MOE_ROWS, dp), lambda i, be, nu, g, nx: (i, 0)),
            scratch_shapes=[pltpu.VMEM((2, d, ff2), F32), pltpu.VMEM((2, ff2 // 2, d), F32),
                            pltpu.VMEM((d, ff2), BF16), pltpu.VMEM((ff2 // 2, d), BF16),
                            pltpu.SemaphoreType.DMA((2,))]),
        compiler_params=_cparams(("arbitrary",), has_side_effects=True),
        name="moe_experts",
    )(block_e, n_used, grp.astype(I32), nxt.astype(I32), xs, wgu, bgu, wdn, bdn)


def _combine_kernel(dest_ref, y_ref, gate_ref, x_ref, mod_ref, o_ref, buf_ref, sem):
    n = dest_ref.shape[0]
    tm = n // TOP_K

    def issue(t, c):
        for k in range(TOP_K):
            pltpu.make_async_copy(_row(y_ref, dest_ref[t * TOP_K + k]), _row(buf_ref.at[k], t), sem).start()
        return c

    lax.fori_loop(0, tm, issue, 0, unroll=DMA_UNROLL)
    for k in range(TOP_K):
        pltpu.make_async_copy(y_ref.at[pl.ds(0, tm), :], buf_ref.at[k], sem).wait()

    gates = gate_ref[...]
    acc = gates[:, 0:1] * _unpack_bf16_pairs(buf_ref[0])
    for k in range(1, TOP_K):
        acc = acc + gates[:, k:k + 1] * _unpack_bf16_pairs(buf_ref[k])
    o_ref[...] = x_ref[...] + mod_ref[5:6, :] * acc


def _combine(y, dest, gates, xs, mods, t_off):
    b, t_out, d = xs.shape
    tile = COMBINE_TILE_LATENT if (t_off > 0 and t_out % COMBINE_TILE_LATENT == 0) else TOK_TILE
    nt = t_out // tile
    per = tile * TOP_K
    row = functools.partial(_mod_row, ctx_row=b)
    tok = lambda bi, ti: (bi, ti, 0)
    return pl.pallas_call(
        _combine_kernel,
        out_shape=jax.ShapeDtypeStruct((b, t_out, d), F32),
        grid=(b, nt),
        in_specs=[pl.BlockSpec((per,), lambda bi, ti: (bi * nt + ti,), memory_space=pltpu.SMEM),
                  pl.BlockSpec(memory_space=pl.ANY),
                  pl.BlockSpec((None, tile, LANES), tok),
                  pl.BlockSpec((None, tile, d), tok),
                  pl.BlockSpec((None, 6, d), lambda bi, ti: (row(bi, ti + t_off), 0, 0))],
        out_specs=pl.BlockSpec((None, tile, d), tok),
        scratch_shapes=[pltpu.VMEM((TOP_K, tile, y.shape[1]), y.dtype), pltpu.SemaphoreType.DMA(())],
        compiler_params=_cparams(("arbitrary", "arbitrary"), disable_bounds_checks=True),
        name="moe_combine",
    )(dest, y, gates, xs, mods)


def _routing(e_sel, rank, counts, n):
    padded = (counts + MOE_ROWS - 1) // MOE_ROWS * MOE_ROWS
    pend = jnp.cumsum(padded)
    pstart = pend - padded
    dest = jnp.sum(jnp.where(e_sel[..., None] == jnp.arange(N_EXPERTS), pstart, 0), axis=-1) + rank
    n_blocks = n * TOP_K // MOE_ROWS + N_EXPERTS
    first_row = jnp.arange(n_blocks) * MOE_ROWS
    block_e = jnp.minimum(jnp.sum(pend[None, :] <= first_row[:, None], axis=1), N_EXPERTS - 1)
    n_used = (pend[-1] // MOE_ROWS).reshape(1)
    pads = ((pstart + counts).astype(I32), (padded - counts).astype(I32))
    return dest.reshape(-1).astype(I32), block_e.astype(I32), n_used.astype(I32), pads, n_blocks * MOE_ROWS


def _moe(f, route, gates, seen, x_mid, mods, t_off, layer, wgu, bgu, wdn, bdn):
    b, t_out, d = f.shape
    n = b * t_out
    route = route.reshape(n, LANES)
    counts = seen[0, :N_EXPERTS].astype(I32)
    dest, block_e, n_used, pads, rows = _routing(route[:, :TOP_K], route[:, TOP_K:2 * TOP_K], counts, n)
    xs = _dispatch(f.reshape(n, d), dest, *pads, n_used, rows)
    y = _experts(xs, block_e, n_used, layer, wgu, bgu, wdn, bdn)
    return _combine(y, dest, gates, x_mid, mods, t_off)


def _axial_angles(n_rows, rot_dim):
    row = np.repeat(np.arange(n_rows, dtype=np.float32), GRID_W)
    col = np.tile(np.arange(GRID_W, dtype=np.float32), n_rows)
    n = rot_dim // 4
    freqs = (np.float32(ROPE_THETA) ** (-np.arange(n, dtype=np.float32) / np.float32(n))).astype(np.float32)
    return np.concatenate([row[:, None] * freqs, col[:, None] * freqs], axis=-1).astype(np.float32)


def _rope_tables(ctx_len, s_len, rot_dim, lane_base, reps):
    ang = _axial_angles(s_len // GRID_W, rot_dim)
    half = rot_dim // 2
    period = LANES // reps
    cos = np.ones((ctx_len + s_len, period), np.float32)
    slo = np.zeros((ctx_len + s_len, period), np.float32)
    shi = np.zeros((ctx_len + s_len, period), np.float32)
    cos[ctx_len:, lane_base:lane_base + rot_dim] = np.tile(np.cos(ang), (1, 2))
    slo[ctx_len:, lane_base:lane_base + half] = -np.sin(ang)
    shi[ctx_len:, lane_base + half:lane_base + rot_dim] = np.sin(ang)
    return [jnp.asarray(np.tile(tb, (1, reps))) for tb in (cos, slo, shi)]


def _segment_mean_matrix(width, period, segs):
    lane = np.arange(width)
    seg_id = np.full((width,), -1)
    seg_w = np.zeros((width,), np.float32)
    for i, (start, length) in enumerate(segs):
        inside = ((lane % period) >= start) & ((lane % period) < start + length)
        seg_id = np.where(inside, (lane // period) * len(segs) + i, seg_id)
        seg_w = np.where(inside, np.float32(1.0 / length), seg_w)
    same = (seg_id[:, None] == seg_id[None, :]) & (seg_id[:, None] >= 0)
    return jnp.asarray(np.where(same, seg_w[None, :], np.float32(0.0)), dtype=BF16)


def _block_diag(w):
    n, c, _ = w.shape
    eye = jnp.eye(n, dtype=w.dtype)
    return (eye[:, None, :, None] * w[:, :, None, :]).reshape(n * c, n * c)


def kernel(x, c, ctx, c_ctx, w_mod, b_mod, norm_mix, norm_ffn, w_in_even, lru_conv_w, lru_conv_b, lru_w_r, lru_b_r, lru_w_i, lru_b_i, lru_lambda, mla_q_a_norm, mla_w_q_b, mla_kv_a_norm, mla_w_kv_b, mla_nope_norm, mla_rope_norm, w_out_even, w_qkv_odd, gqa_qk_norm, gqa_sink, w_out_odd, w_router, b_router, w_gate_up, b_gate_up, w_down, b_down):
    b, s_len, d = x.shape
    ctx_len = ctx.shape[1]
    depth = w_mod.shape[0]
    assert depth == 2 and ctx_len == TOK_TILE and s_len % TOK_TILE == 0 and b + 1 <= SUBLANES
    lru_w = lru_conv_w.shape[-1]
    q_lora = mla_q_a_norm.shape[-1]
    kv_lora = mla_kv_a_norm.shape[-1]

    cvec = jnp.concatenate([c, c_ctx[None], jnp.zeros((SUBLANES - b - 1, d), F32)], axis=0)
    mods = _modulation(cvec, w_mod, b_mod).reshape(depth, SUBLANES, 6, d)

    wr_pad = jnp.zeros((depth, d, LANES), F32).at[:, :, :N_EXPERTS].set(w_router)
    br_pad = jnp.full((depth, 1, LANES), NEG_INF, F32).at[:, 0, :N_EXPERTS].set(b_router)
    experts = (w_gate_up, b_gate_up[:, :, None, :], w_down, b_down[:, :, None, :])

    n_in = w_in_even.shape[-1]
    n_in_pad = -(-n_in // LANES) * LANES
    w_in = jnp.zeros((d, n_in_pad), F32).at[:, :n_in].set(w_in_even[0]).astype(BF16)
    xa, ga, mla_in = _even_in(ctx, x, mods[0], norm_mix[0][None], w_in, lru_w)

    nh = lru_w // (2 * LANES)
    per = LRU_BLOCKS // nh
    blk = lru_w // LRU_BLOCKS
    w_gates = jnp.stack([
        jnp.stack([jnp.concatenate([_block_diag(lru_w_r[0, dd, h * per:(h + 1) * per]),
                                    _block_diag(lru_w_i[0, dd, h * per:(h + 1) * per])], axis=1)
                   for h in range(nh)]) for dd in range(2)]).astype(BF16)
    assert blk * per == 2 * LANES
    ya = _lru(xa, ga, lru_conv_w[0], lru_conv_b[0][:, None, :], w_gates, lru_b_r[0][:, None, :],
              lru_b_i[0][:, None, :], jax.nn.softplus(-lru_lambda[0])[:, None, :], ctx_len)

    qk = MLA_NOPE + MLA_ROPE
    hq = MLA_HEADS * LANES
    wq = jnp.zeros((q_lora, MLA_HEADS, LANES), F32).at[:, :, :qk].set(
        mla_w_q_b[0].reshape(q_lora, MLA_HEADS, qk)).reshape(q_lora, hq).astype(BF16)
    wkv = mla_w_kv_b[0].reshape(kv_lora, MLA_HEADS, MLA_NOPE + MLA_V)
    wk = jnp.zeros((kv_lora, MLA_HEADS, LANES), F32).at[:, :, :MLA_NOPE].set(
        wkv[:, :, :MLA_NOPE]).reshape(kv_lora, hq).astype(BF16)
    wv = jnp.zeros((kv_lora, MLA_HEADS, LANES), F32).at[:, :, :MLA_V].set(
        wkv[:, :, MLA_NOPE:]).reshape(kv_lora, hq).astype(BF16)
    one = jnp.asarray(np.tile(np.arange(LANES) == MLA_V, MLA_HEADS)[None], dtype=F32)
    zpad = jnp.zeros((LANES - qk,), F32)
    gq = jnp.tile(jnp.concatenate([mla_nope_norm[0, 0], mla_rope_norm[0, 0], zpad]), MLA_HEADS)[None]
    gk = jnp.tile(jnp.concatenate([mla_nope_norm[0, 1], jnp.zeros((LANES - MLA_NOPE,), F32)]), MLA_HEADS)[None]
    gkr = jnp.concatenate([mla_rope_norm[0, 1], jnp.zeros((LANES - MLA_ROPE,), F32)])[None]
    cos, slo, shi = _rope_tables(ctx_len, s_len, MLA_ROPE, MLA_NOPE, 1)
    mla_p = dict(qan=mla_q_a_norm[0][None], kvn=mla_kv_a_norm[0][None], wq=wq, wk=wk, wv=wv,
                 mq=_segment_mean_matrix(MXU_TILE,LANES, [(0, MLA_NOPE), (MLA_NOPE, MLA_ROPE)]),
                 mk=_segment_mean_matrix(MXU_TILE,LANES, [(0, MLA_NOPE)]),
                 gq=gq, gk=gk, gkr=gkr, one=one, cos=cos, slo=slo, shi=shi)
    q, k, v = _mla_prep(mla_in, mla_p, qk ** -0.5 * math.log2(math.e))
    yb = _mla_attn(q, k, v, ctx_len)

    x_mid, f, route, gates, seen = _post_mix(ya, yb, 0, (ctx, x), mods[0], norm_ffn[0][None], w_out_even[0].astype(BF16),
                                             wr_pad[0], br_pad[0], 0, 0)
    xs = _moe(f, route, gates, seen, x_mid, mods[0], 0, 0, *experts)

    nq = GQA_HEADS * GQA_DIM
    nkv = GQA_KV_HEADS * GQA_DIM
    cos, slo, shi = _rope_tables(ctx_len, s_len, GQA_DIM, 0, LANES // GQA_DIM)
    odd_p = dict(w=w_qkv_odd[0].astype(BF16),
                 mq=_segment_mean_matrix(MXU_TILE,GQA_DIM, [(0, GQA_DIM)]),
                 mk=_segment_mean_matrix(MXU_TILE,GQA_DIM, [(0, GQA_DIM)]),
                 gq=jnp.tile(gqa_qk_norm[0, 0], GQA_HEADS)[None], gk=jnp.tile(gqa_qk_norm[0, 1], GQA_KV_HEADS)[None],
                 cos=cos, slo=slo, shi=shi)
    q, k, v = _odd_in(xs, mods[1], norm_mix[1][None], odd_p, GQA_DIM ** -0.5)
    o = _win_attn(q, k, v, gqa_sink[0], ctx_len)

    t_off = ctx_len // TOK_TILE
    x_mid, f, route, gates, seen = _post_mix(o, o, 1, xs, mods[1], norm_ffn[1][None], w_out_odd[0].astype(BF16),
                                             wr_pad[1], br_pad[1], t_off, 0)
    return _moe(f, route, gates, seen, x_mid, mods[1], t_off, 1, *experts)
```

```python
import functools
import math

import jax
import jax.numpy as jnp
import numpy as np
from jax import lax
from jax.experimental import pallas as pl
from jax.experimental.pallas import tpu as pltpu

F32 = jnp.float32
BF16 = jnp.bfloat16
I32 = jnp.int32

GRID_W = 64
LRU_BLOCKS = 8
LRU_C = 8.0
CONV_W = 4
MLA_HEADS = 8
MLA_NOPE = 64
MLA_ROPE = 32
MLA_V = 64
GQA_HEADS = 16
GQA_KV_HEADS = 4
GQA_DIM = 64
WINDOW = 128
ROPE_THETA = 10000.0
NEG_INF = -1e30
EPS = 1e-6
N_EXPERTS = 32
TOP_K = 4
SWIGLU_LIMIT = 7.0
SWIGLU_ALPHA = 1.702

LANES = 128
SUBLANES = 8
TOK_TILE = 256
LRU_CHUNK = 256
WIN_BLOCK = 128
WIN_BLOCKS_PER_STEP = 2
MOE_ROWS = 512
MXU_TILE = 256
MLA_HEADS_PER_STEP = 4
VMEM_LIMIT = 48 * 1024 * 1024


def _cparams(sem, **kw):
    return pltpu.CompilerParams(dimension_semantics=sem, vmem_limit_bytes=VMEM_LIMIT, **kw)


def _dot(a, b):
    return jnp.dot(a, b, preferred_element_type=F32)


def _dot_nt(a, b):
    return lax.dot_general(a, b, (((1,), (1,)), ((), ())), preferred_element_type=F32)


def _split_bf16(x):
    hi = x.astype(BF16)
    lo = (x - hi.astype(F32)).astype(BF16)
    return hi, lo


def _dot3(a, w):
    ah, al = _split_bf16(a)
    wh, wl = _split_bf16(w)
    return _dot(ah, wh) + _dot(al, wh) + _dot(ah, wl)


def _rms(x):
    return x * lax.rsqrt(jnp.mean(x * x, axis=-1, keepdims=True) + EPS)


def _prenorm(x, g, scale, shift):
    return (_rms(x) * g) * (1.0 + scale) + shift


def _rope(x, cos, sin_lo, sin_hi, half):
    w = x.shape[-1]
    return x * cos + pltpu.roll(x, w - half, axis=1) * sin_lo + pltpu.roll(x, half, axis=1) * sin_hi


U32 = jnp.uint32
HI_HALF = 0xFFFF0000


def _pack_bf16_pairs(x):
    w = x.shape[1] // 2
    lo = pltpu.bitcast(x[:, :w].astype(BF16).astype(F32), U32) >> 16
    hi = pltpu.bitcast(x[:, w:].astype(BF16).astype(F32), U32) & U32(HI_HALF)
    return lo | hi


def _unpack_bf16_pairs(words):
    lo = pltpu.bitcast(words << 16, F32)
    hi = pltpu.bitcast(words & U32(HI_HALF), F32)
    return jnp.concatenate([lo, hi], axis=1)


def _head_mean_square(y, m_ref):
    w = m_ref.shape[0]
    return jnp.concatenate([_dot((y[:, j:j + w] * y[:, j:j + w]).astype(BF16), m_ref[...])
                            for j in range(0, y.shape[1], w)], axis=1)


def _tile_lanes(t, reps):
    return jnp.concatenate([t] * reps, axis=1) if reps > 1 else t


def _mod_kernel(c_ref, w_ref, b_ref, o_ref):
    c = c_ref[...]
    o_ref[...] = _dot3(c * jax.nn.sigmoid(c), w_ref[...]) + b_ref[...]


def _modulation(cvec, w_mod, b_mod):
    depth, d, n = w_mod.shape
    tn = 1536
    return pl.pallas_call(
        _mod_kernel,
        out_shape=jax.ShapeDtypeStruct((depth, SUBLANES, n), F32),
        grid=(depth, n // tn),
        in_specs=[pl.BlockSpec((SUBLANES, d), lambda l, j: (0, 0)),
                  pl.BlockSpec((None, d, tn), lambda l, j: (l, 0, j)),
                  pl.BlockSpec((None, 1, tn), lambda l, j: (l, 0, j))],
        out_specs=pl.BlockSpec((None, SUBLANES, tn), lambda l, j: (l, 0, j)),
        compiler_params=_cparams(("arbitrary", "arbitrary")),
        name="modulation",
    )(cvec, w_mod, b_mod.reshape(depth, 1, n))


def _mod_row(b, t, ctx_row):
    return jnp.where(t == 0, ctx_row, b)


def _stream_tile(ctx_ref, lat_ref):
    return jnp.where(pl.program_id(1) == 0, ctx_ref[...], lat_ref[...])


def _stream_specs(d):
    return [pl.BlockSpec((None, TOK_TILE, d), lambda bi, ti: (bi, 0, 0)),
            pl.BlockSpec((None, TOK_TILE, d), lambda bi, ti: (bi, jnp.maximum(ti - 1, 0), 0))]


def _even_in_kernel(ctx_ref, x_ref, mod_ref, g_ref, w_ref, xa_ref, ga_ref, mla_ref):
    mod = mod_ref[...]
    h = _prenorm(_stream_tile(ctx_ref, x_ref), g_ref[...], mod[1:2], mod[0:1])
    z = _dot(h.astype(BF16), w_ref[...])
    c = xa_ref.shape[-1]
    xa_ref[...] = z[:, :c]
    ga_ref[...] = z[:, c:2 * c]
    mla_ref[...] = z[:, 2 * c:]


def _even_in(ctx, x, mods, g, w_pad, lru_w):
    b, s_len, d = x.shape
    t = ctx.shape[1] + s_len
    nt = t // TOK_TILE
    n_out = w_pad.shape[1]
    n_mla = n_out - 2 * lru_w
    row = functools.partial(_mod_row, ctx_row=b)
    tok = lambda bi, ti: (bi, ti, 0)
    return pl.pallas_call(
        _even_in_kernel,
        out_shape=(jax.ShapeDtypeStruct((b, t, lru_w), F32),
                   jax.ShapeDtypeStruct((b, t, lru_w), F32),
                   jax.ShapeDtypeStruct((b, t, n_mla), F32)),
        grid=(b, nt),
        in_specs=_stream_specs(d) + [
            pl.BlockSpec((None, 6, d), lambda bi, ti: (row(bi, ti), 0, 0)),
            pl.BlockSpec((1, d), lambda bi, ti: (0, 0)),
            pl.BlockSpec((d, n_out), lambda bi, ti: (0, 0))],
        out_specs=(pl.BlockSpec((None, TOK_TILE, lru_w), tok),
                   pl.BlockSpec((None, TOK_TILE, lru_w), tok),
                   pl.BlockSpec((None, TOK_TILE, n_mla), tok)),
        compiler_params=_cparams(("arbitrary", "arbitrary")),
        name="even_in",
    )(ctx, x, mods, g, w_pad)


def _lru_kernel(xa_ref, ga_ref, cw_ref, cb_ref, wg_ref, br_ref, bi_ref, sp_ref, o_ref, pad_ref, rec_ref, *, ctx_len):
    t, c = xa_ref.shape
    tc = LRU_CHUNK
    halo = SUBLANES
    n_chunks = t // tc
    n_ctx = ctx_len // tc
    groups = tc // SUBLANES

    pad_ref[0:halo, :] = jnp.zeros((halo, c), F32)
    pad_ref[t + halo:t + 2 * halo, :] = jnp.zeros((halo, c), F32)
    pad_ref[halo:t + halo, :] = xa_ref[...]

    rid = lax.broadcasted_iota(I32, (tc, 1), 0)
    sub = rid % SUBLANES

    for d in range(2):
        cw = cw_ref[d]
        cb = cb_ref[d]
        wg = wg_ref[d]
        b_r = br_ref[d]
        b_i = bi_ref[d]
        sp = sp_ref[d]

        def chunk(i, h, d=d, cw=cw, cb=cb, wg=wg, b_r=b_r, b_i=b_i, sp=sp):
            if d == 0:
                ci = i
            else:
                ci = jnp.where(i < n_ctx, n_ctx - 1 - i, n_chunks - 1 - (i - n_ctx))
            r0 = pl.multiple_of(ci * tc, tc)
            win = pad_ref[pl.ds(r0, tc + 2 * halo), :]
            if d == 0:
                past = jnp.where(ci == n_ctx, 0.0, win[0:halo])
                win = jnp.concatenate([past, win[halo:]], axis=0)
            else:
                past = jnp.where(ci == n_ctx - 1, 0.0, win[halo + tc:])
                win = jnp.concatenate([win[:halo + tc], past], axis=0)
            xc = jnp.zeros((tc, c), F32) + cb
            for k in range(CONV_W):
                off = (k - (CONV_W - 1)) if d == 0 else ((CONV_W - 1) - k)
                if off == 0:
                    src = win[halo:halo + tc]
                else:
                    src = pltpu.roll(win, (-off) % (tc + 2 * halo), axis=0)[halo:halo + tc]
                xc = xc + cw[k:k + 1] * src
            gz = _dot(xc.astype(BF16), wg)
            r = 0.5 * jnp.tanh(0.5 * (gz[:, :c] + b_r)) + 0.5
            gi = 0.5 * jnp.tanh(0.5 * (gz[:, c:] + b_i)) + 0.5
            log_a = (-LRU_C) * r * sp
            a = jnp.exp(log_a)
            th = jnp.tanh(log_a)
            bb = jnp.sqrt(-2.0 * th / (1.0 - th)) * (gi * xc)
            for s in (1, 2, 4):
                if d == 0:
                    ok = sub >= s
                    sh = s
                else:
                    ok = sub <= (SUBLANES - 1 - s)
                    sh = tc - s
                a_prev = jnp.where(ok, pltpu.roll(a, sh, axis=0), 1.0)
                b_prev = jnp.where(ok, pltpu.roll(bb, sh, axis=0), 0.0)
                bb = a * b_prev + bb
                a = a * a_prev
            outs = [None] * groups
            order = range(groups) if d == 0 else range(groups - 1, -1, -1)
            for g in order:
                lo = g * SUBLANES
                hg = a[lo:lo + SUBLANES] * h + bb[lo:lo + SUBLANES]
                outs[g] = hg
                h = hg[SUBLANES - 1:SUBLANES] if d == 0 else hg[0:1]
            hs = jnp.concatenate(outs, axis=0)
            if d == 0:
                rec_ref[pl.ds(r0, tc), :] = hs
            else:
                tot = rec_ref[pl.ds(r0, tc), :] + hs
                gate = jax.nn.gelu(ga_ref[pl.ds(r0, tc), :], approximate=True)
                o_ref[pl.ds(r0, tc), :] = (tot * gate).astype(o_ref.dtype)
            return h

        lax.fori_loop(0, n_chunks, chunk, jnp.zeros((1, c), F32))


def _lru(xa, ga, conv_w, conv_b, w_gates, b_r, b_i, sp, ctx_len):
    b, t, w = xa.shape
    c = 2 * LANES
    nh = w // c
    tok = lambda bi, hi: (bi, 0, hi)
    par = lambda bi, hi: (0, 0, hi)
    return pl.pallas_call(
        functools.partial(_lru_kernel, ctx_len=ctx_len),
        out_shape=jax.ShapeDtypeStruct((b, t, w), BF16),
        grid=(b, nh),
        in_specs=[pl.BlockSpec((None, t, c), tok),
                  pl.BlockSpec((None, t, c), tok),
                  pl.BlockSpec((2, CONV_W, c), par),
                  pl.BlockSpec((2, 1, c), par),
                  pl.BlockSpec((2, None, c, 2 * c), lambda bi, hi: (0, hi, 0, 0)),
                  pl.BlockSpec((2, 1, c), par),
                  pl.BlockSpec((2, 1, c), par),
                  pl.BlockSpec((2, 1, c), par)],
        out_specs=pl.BlockSpec((None, t, c), tok),
        scratch_shapes=[pltpu.VMEM((t + 2 * SUBLANES, c), F32), pltpu.VMEM((t, c), F32)],
        compiler_params=_cparams(("arbitrary", "arbitrary")),
        name="rglru",
    )(xa, ga, conv_w, conv_b, w_gates, b_r, b_i, sp)


def _mla_prep_kernel(in_ref, qan_ref, kvn_ref, wq_ref, wk_ref, wv_ref, mq_ref, mk_ref, gq_ref, gk_ref, gkr_ref,
                     one_ref, cos_ref, slo_ref, shi_ref, q_ref, k_ref, v_ref, *, q_lora, kv_lora, scale):
    z = in_ref[...]
    heads = q_ref.shape[-1] // LANES
    cos = cos_ref[...]
    slo = slo_ref[...]
    shi = shi_ref[...]
    half = MLA_ROPE // 2

    qan = (_rms(z[:, :q_lora]) * qan_ref[...]).astype(BF16)
    q = _dot(qan, wq_ref[...])
    q = q * lax.rsqrt(_head_mean_square(q, mq_ref) + EPS) * gq_ref[...]
    q = _rope(q, _tile_lanes(cos, heads), _tile_lanes(slo, heads), _tile_lanes(shi, heads), half)
    q_ref[...] = (q * scale).astype(BF16)

    kvn = (_rms(z[:, q_lora:q_lora + kv_lora]) * kvn_ref[...]).astype(BF16)
    kk = _dot(kvn, wk_ref[...])
    kk = kk * lax.rsqrt(_head_mean_square(kk, mk_ref) + EPS) * gk_ref[...]
    v_ref[...] = (_dot(kvn, wv_ref[...]) + one_ref[...]).astype(BF16)

    kr = z[:, q_lora + kv_lora:]
    kr = kr * lax.rsqrt(jnp.sum(kr * kr, axis=-1, keepdims=True) * (1.0 / MLA_ROPE) + EPS) * gkr_ref[...]
    kr = pltpu.roll(kr, MLA_NOPE, axis=1)
    kr = _rope(kr, cos, slo, shi, half)
    k_ref[...] = (kk + _tile_lanes(kr, heads)).astype(BF16)


def _mla_prep(mla_in, p, scale):
    b, t, w = mla_in.shape
    hq = MLA_HEADS * LANES
    tok = lambda bi, ti: (bi, ti, 0)
    full = lambda a: pl.BlockSpec(a.shape, lambda bi, ti: (0,) * a.ndim)
    pos = pl.BlockSpec((TOK_TILE, LANES), lambda bi, ti: (ti, 0))
    consts = [p["qan"], p["kvn"], p["wq"], p["wk"], p["wv"], p["mq"], p["mk"], p["gq"], p["gk"], p["gkr"], p["one"]]
    out = jax.ShapeDtypeStruct((b, t, hq), BF16)
    return pl.pallas_call(
        functools.partial(_mla_prep_kernel, q_lora=p["qan"].shape[1], kv_lora=p["kvn"].shape[1], scale=scale),
        out_shape=(out, out, out),
        grid=(b, t // TOK_TILE),
        in_specs=[pl.BlockSpec((None, TOK_TILE, w), tok)] + [full(a) for a in consts] + [pos, pos, pos],
        out_specs=(pl.BlockSpec((None, TOK_TILE, hq), tok),) * 3,
        compiler_params=_cparams(("arbitrary", "arbitrary")),
        name="mla_prep",
    )(mla_in, *consts, p["cos"], p["slo"], p["shi"])


def _mla_attn_kernel(q_ref, k_ref, v_ref, o_ref, *, ctx_len):
    tq = q_ref.shape[0]
    t = k_ref.shape[0]
    lane = lax.broadcasted_iota(I32, (tq, LANES), 1)

    def attend(nk):
        outs = []
        for hh in range(q_ref.shape[1] // LANES):
            q = q_ref[:, hh * LANES:(hh + 1) * LANES]
            k = k_ref[0:nk, hh * LANES:(hh + 1) * LANES]
            s = _dot_nt(q, k)
            p = jnp.exp2(s - jnp.max(s, axis=-1, keepdims=True))
            o = _dot(p.astype(BF16), v_ref[0:nk, hh * LANES:(hh + 1) * LANES])
            outs.append(o / o[:, MLA_V:MLA_V + 1])
        pairs = [jnp.where(lane < MLA_V, outs[j], pltpu.roll(outs[j + 1], MLA_V, axis=1))
                 for j in range(0, len(outs), 2)]
        o_ref[...] = jnp.concatenate(pairs, axis=1).astype(o_ref.dtype)

    @pl.when(pl.program_id(2) == 0)
    def _():
        attend(ctx_len)

    @pl.when(pl.program_id(2) > 0)
    def _():
        attend(t)


def _mla_attn(q, k, v, ctx_len):
    b, t, hq = q.shape
    hps = MLA_HEADS_PER_STEP
    groups = hq // (hps * LANES)
    return pl.pallas_call(
        functools.partial(_mla_attn_kernel, ctx_len=ctx_len),
        out_shape=jax.ShapeDtypeStruct((b, t, groups * hps * MLA_V), BF16),
        grid=(b, groups, t // TOK_TILE),
        in_specs=[pl.BlockSpec((None, TOK_TILE, hps * LANES), lambda bi, hi, ti: (bi, ti, hi)),
                  pl.BlockSpec((None, t, hps * LANES), lambda bi, hi, ti: (bi, 0, hi)),
                  pl.BlockSpec((None, t, hps * LANES), lambda bi, hi, ti: (bi, 0, hi))],
        out_specs=pl.BlockSpec((None, TOK_TILE, hps * MLA_V), lambda bi, hi, ti: (bi, ti, hi)),
        compiler_params=_cparams(("arbitrary", "arbitrary", "arbitrary")),
        name="mla_attn",
    )(q, k, v)


def _odd_in_kernel(x_ref, mod_ref, g_ref, w_ref, mq_ref, mk_ref, gq_ref, gk_ref, cos_ref, slo_ref, shi_ref,
                   q_ref, k_ref, v_ref, *, scale):
    mod = mod_ref[...]
    h = _prenorm(x_ref[...], g_ref[...], mod[1:2], mod[0:1]).astype(BF16)
    nq = q_ref.shape[-1]
    nk = k_ref.shape[-1]
    half = GQA_DIM // 2

    def head_norm_rope(y, m_ref, gain_ref):
        reps = y.shape[-1] // LANES
        y = y * lax.rsqrt(_head_mean_square(y, m_ref) + EPS) * gain_ref[...]
        return _rope(y, _tile_lanes(cos_ref[...], reps), _tile_lanes(slo_ref[...], reps),
                     _tile_lanes(shi_ref[...], reps), half)

    q_ref[...] = (head_norm_rope(_dot(h, w_ref[:, 0:nq]), mq_ref, gq_ref) * scale).astype(BF16)
    k_ref[...] = head_norm_rope(_dot(h, w_ref[:, nq:nq + nk]), mk_ref, gk_ref).astype(BF16)
    v_ref[...] = _dot(h, w_ref[:, nq + nk:nq + 2 * nk]).astype(BF16)


def _odd_in(xs, mods, g, p, scale):
    b, t, d = xs.shape
    nq = p["gq"].shape[1]
    nk = p["gk"].shape[1]
    row = functools.partial(_mod_row, ctx_row=b)
    tok = lambda bi, ti: (bi, ti, 0)
    full = lambda a: pl.BlockSpec(a.shape, lambda bi, ti: (0,) * a.ndim)
    pos = pl.BlockSpec((TOK_TILE, LANES), lambda bi, ti: (ti, 0))
    kv = jax.ShapeDtypeStruct((b, t, nk), BF16)
    return pl.pallas_call(
        functools.partial(_odd_in_kernel, scale=scale),
        out_shape=(jax.ShapeDtypeStruct((b, t, nq), BF16), kv, kv),
        grid=(b, t // TOK_TILE),
        in_specs=[pl.BlockSpec((None, TOK_TILE, d), tok),
                  pl.BlockSpec((None, 6, d), lambda bi, ti: (row(bi, ti), 0, 0)),
                  pl.BlockSpec((1, d), lambda bi, ti: (0, 0)),
                  full(p["w"]), full(p["mq"]), full(p["mk"]), full(p["gq"]), full(p["gk"]), pos, pos, pos],
        out_specs=(pl.BlockSpec((None, TOK_TILE, nq), tok),
                   pl.BlockSpec((None, TOK_TILE, nk), tok),
                   pl.BlockSpec((None, TOK_TILE, nk), tok)),
        compiler_params=_cparams(("arbitrary", "arbitrary")),
        name="odd_in",
    )(xs, mods, g, p["w"], p["mq"], p["mk"], p["gq"], p["gk"], p["cos"], p["slo"], p["shi"])


def _win_attn_kernel(sink_ref, q_ref, k_ref, v_ref, o_ref, *, ctx_len):
    t = k_ref.shape[0]
    wb = WIN_BLOCK
    w3 = 3 * wb
    hd = GQA_DIM
    group = GQA_HEADS // GQA_KV_HEADS
    nk = w3 + ctx_len
    rows = group * wb
    row = lax.broadcasted_iota(I32, (rows, nk), 0)
    col = lax.broadcasted_iota(I32, (rows, nk), 1)
    head_of_row = lax.broadcasted_iota(I32, (rows, 1), 0) // wb
    kctx = k_ref[0:ctx_len, :]
    vctx = v_ref[0:ctx_len, :]
    for sub in range(WIN_BLOCKS_PER_STEP):
        i = pl.program_id(1) * WIN_BLOCKS_PER_STEP + sub
        q0 = ctx_len + i * wb
        ws = pl.multiple_of(jnp.clip(q0 - wb, ctx_len, t - w3), wb)
        valid = (col >= w3) | (jnp.abs(q0 + row % wb - (ws + col)) <= WINDOW)
        q = q_ref[sub * wb:(sub + 1) * wb, :].astype(F32)
        kcat = jnp.concatenate([k_ref[pl.ds(ws, w3), :], kctx], axis=0).astype(F32)
        vcat = jnp.concatenate([v_ref[pl.ds(ws, w3), :], vctx], axis=0).astype(F32)
        outs = []
        for kh in range(GQA_KV_HEADS):
            qs = jnp.concatenate([q[:, (kh * group + h) * hd:(kh * group + h + 1) * hd] for h in range(group)],
                                 axis=0)
            s = _dot_nt(qs.astype(BF16), kcat[:, kh * hd:(kh + 1) * hd].astype(BF16))
            s = jnp.where(valid, s, NEG_INF)
            sink = jnp.zeros((rows, 1), F32)
            for h in range(group):
                sink = jnp.where(head_of_row == h, sink_ref[kh * group + h], sink)
            m = jnp.maximum(jnp.max(s, axis=-1, keepdims=True), sink)
            p = jnp.exp(s - m)
            l = jnp.sum(p, axis=-1, keepdims=True) + jnp.exp(sink - m)
            o = _dot(p.astype(BF16), vcat[:, kh * hd:(kh + 1) * hd].astype(BF16)) / l
            outs.extend(o[h * wb:(h + 1) * wb] for h in range(group))
        o_ref[sub * wb:(sub + 1) * wb, :] = jnp.concatenate(outs, axis=1).astype(o_ref.dtype)


def _win_attn(q, k, v, sink, ctx_len):
    b, t, n = q.shape
    nkv = k.shape[-1]
    s_len = t - ctx_len
    step = WIN_BLOCK * WIN_BLOCKS_PER_STEP
    assert ctx_len % step == 0 and s_len % step == 0
    off = ctx_len // step
    return pl.pallas_call(
        functools.partial(_win_attn_kernel, ctx_len=ctx_len),
        out_shape=jax.ShapeDtypeStruct((b, s_len, n), BF16),
        grid=(b, s_len // step),
        in_specs=[pl.BlockSpec(memory_space=pltpu.SMEM),
                  pl.BlockSpec((None, step, n), lambda bi, ti: (bi, ti + off, 0)),
                  pl.BlockSpec((None, t, nkv), lambda bi, ti: (bi, 0, 0)),
                  pl.BlockSpec((None, t, nkv), lambda bi, ti: (bi, 0, 0))],
        out_specs=pl.BlockSpec((None, step, n), lambda bi, ti: (bi, ti, 0)),
        compiler_params=_cparams(("arbitrary", "arbitrary")),
        name="win_attn",
    )(sink, q, k, v)


def _post_mix_kernel(a1_ref, a2_ref, *refs, split_stream):
    x_in = _stream_tile(refs[0], refs[1]) if split_stream else refs[0][...]
    _post_mix_body(a1_ref, a2_ref, x_in, *refs[2 if split_stream else 1:])


def _post_mix_body(a1_ref, a2_ref, x_in, mod_ref, g_ref, w_ref, wr_ref, br_ref,
                   xo_ref, f_ref, route_ref, gate_ref, cnt_ref, seen_ref):
    half = a1_ref.shape[-1]
    mod = mod_ref[...]
    m = _dot(a1_ref[...], w_ref[0:half, :]) + _dot(a2_ref[...], w_ref[half:2 * half, :])
    x = x_in + mod[2:3] * m
    xo_ref[...] = x
    f = _prenorm(x, g_ref[...], mod[4:5], mod[3:4])
    f_ref[...] = _pack_bf16_pairs(f)

    logit = _dot(f.astype(BF16), wr_ref[...].astype(BF16)) + br_ref[...]
    tm = logit.shape[0]
    lane = lax.broadcasted_iota(I32, (tm, LANES), 1)
    lane_f = lane.astype(F32)
    vals, idxs = [], []
    for _ in range(TOP_K):
        mx = jnp.max(logit, axis=-1, keepdims=True)
        ix = jnp.min(jnp.where(logit == mx, lane_f, float(LANES)), axis=-1, keepdims=True)
        vals.append(mx)
        idxs.append(ix)
        logit = jnp.where(lane_f == ix, -jnp.inf, logit)
    exps = [jnp.exp(v - vals[0]) for v in vals]
    den = exps[0]
    for e in exps[1:]:
        den = den + e

    @pl.when((pl.program_id(0) == 0) & (pl.program_id(1) == 0))
    def _():
        seen_ref[...] = jnp.zeros(seen_ref.shape, F32)

    msk = jnp.zeros((tm, LANES), F32)
    for k in range(TOP_K):
        msk = jnp.where(lane_f == idxs[k], 1.0, msk)
    earlier = (lax.broadcasted_iota(I32, (tm, tm), 1) < lax.broadcasted_iota(I32, (tm, tm), 0))
    rank = _dot(jnp.where(earlier, 1.0, 0.0).astype(BF16), msk.astype(BF16)) + seen_ref[0:1, :]
    seen = seen_ref[...] + jnp.sum(msk, axis=0, keepdims=True)
    seen_ref[...] = seen
    cnt_ref[...] = seen

    r_out = jnp.zeros((tm, LANES), F32)
    g_out = jnp.zeros((tm, LANES), F32)
    for k in range(TOP_K):
        rank_k = jnp.sum(jnp.where(lane_f == idxs[k], rank, 0.0), axis=-1, keepdims=True)
        r_out = jnp.where(lane == k, idxs[k], r_out)
        r_out = jnp.where(lane == TOP_K + k, rank_k, r_out)
        g_out = jnp.where(lane == k, exps[k] / den, g_out)
    route_ref[...] = r_out.astype(I32)
    gate_ref[...] = g_out


def _post_mix(a1, a2, lane_blk2, xs, mods, g, w_out, w_router, b_router, t_off, a_off):
    split_stream = isinstance(xs, tuple)
    if split_stream:
        assert t_off == 0
        b, s_len, d = xs[1].shape
        t = xs[0].shape[1] + s_len
        x_specs = _stream_specs(d)
    else:
        b, t, d = xs.shape
        x_specs = [pl.BlockSpec((None, TOK_TILE, d), lambda bi, ti: (bi, ti + t_off, 0))]
        xs = (xs,)
    half = w_out.shape[0] // 2
    nt = t // TOK_TILE - t_off
    t_out = nt * TOK_TILE
    row = functools.partial(_mod_row, ctx_row=b)
    tok = lambda bi, ti: (bi, ti, 0)
    act = jax.ShapeDtypeStruct((b, t_out, d), F32)
    return pl.pallas_call(
        functools.partial(_post_mix_kernel, split_stream=split_stream),
        out_shape=(act, jax.ShapeDtypeStruct((b, t_out, d // 2), U32),
                   jax.ShapeDtypeStruct((b, t_out, LANES), I32),
                   jax.ShapeDtypeStruct((b, t_out, LANES), F32),
                   jax.ShapeDtypeStruct((SUBLANES, LANES), F32)),
        grid=(b, nt),
        in_specs=[pl.BlockSpec((None, TOK_TILE, half), lambda bi, ti: (bi, ti + a_off, 0)),
                  pl.BlockSpec((None, TOK_TILE, half), lambda bi, ti: (bi, ti + a_off, lane_blk2))] + x_specs + [
                  pl.BlockSpec((None, 6, d), lambda bi, ti: (row(bi, ti + t_off), 0, 0)),
                  pl.BlockSpec((1, d), lambda bi, ti: (0, 0)),
                  pl.BlockSpec(w_out.shape, lambda bi, ti: (0, 0)),
                  pl.BlockSpec(w_router.shape, lambda bi, ti: (0, 0)),
                  pl.BlockSpec((1, LANES), lambda bi, ti: (0, 0))],
        out_specs=(pl.BlockSpec((None, TOK_TILE, d), tok),
                   pl.BlockSpec((None, TOK_TILE, d // 2), tok),
                   pl.BlockSpec((None, TOK_TILE, LANES), tok),
                   pl.BlockSpec((None, TOK_TILE, LANES), tok),
                   pl.BlockSpec((SUBLANES, LANES), lambda bi, ti: (0, 0))),
        scratch_shapes=[pltpu.VMEM((SUBLANES, LANES), F32)],
        compiler_params=_cparams(("arbitrary", "arbitrary")),
        name="post_mix",
    )(a1, a2, *xs, mods, g, w_out, w_router, b_router)


DMA_UNROLL = 2
DISPATCH_TILE = 1024
COMBINE_TILE_LATENT = 1024


def _row(ref, i):
    return ref.at[pl.ds(i, 1), :]


def _dispatch_kernel(pad_lo_ref, pad_n_ref, nu_ref, dest_ref, f_ref, xs_ref, zero_ref, sem, zsem):
    n = dest_ref.shape[0]

    blk = zero_ref.shape[0]
    n_blk = xs_ref.shape[0] // blk

    def block_copy(i):
        return pltpu.make_async_copy(zero_ref, xs_ref.at[pl.ds(pl.multiple_of(i * blk, blk), blk), :], zsem)

    def pad_rows(e, c, wait):
        lo = pad_lo_ref[e]
        head = jnp.minimum((-lo) & (SUBLANES - 1), pad_n_ref[e])

        def one(r, c2):
            copy = pltpu.make_async_copy(_row(zero_ref, 0), _row(xs_ref, lo + r), zsem)
            copy.wait() if wait else copy.start()
            return c2

        def eight(g, c2):
            r0 = pl.multiple_of(lo + head + g * SUBLANES, SUBLANES)
            copy = pltpu.make_async_copy(zero_ref.at[pl.ds(0, SUBLANES), :], xs_ref.at[pl.ds(r0, SUBLANES), :], zsem)
            copy.wait() if wait else copy.start()
            return c2

        c = lax.fori_loop(0, head, one, c)
        return lax.fori_loop(0, (pad_n_ref[e] - head) // SUBLANES, eight, c)

    @pl.when(pl.program_id(0) == 0)
    def _():
        zero_ref[...] = jnp.zeros(zero_ref.shape, zero_ref.dtype)
        lax.fori_loop(nu_ref[0], n_blk, lambda i, c: (block_copy(i).start(), c)[1], 0)
        lax.fori_loop(0, N_EXPERTS, functools.partial(pad_rows, wait=False), 0)

    @pl.when(pl.program_id(0) == pl.num_programs(0) - 1)
    def _():
        lax.fori_loop(nu_ref[0], n_blk, lambda i, c: (block_copy(i).wait(), c)[1], 0)
        lax.fori_loop(0, N_EXPERTS, functools.partial(pad_rows, wait=True), 0)


    def issue(t, c):
        for k in range(TOP_K):
            pltpu.make_async_copy(_row(f_ref, t), _row(xs_ref, dest_ref[t * TOP_K + k]), sem).start()
        return c

    lax.fori_loop(0, n // TOP_K, issue, 0, unroll=DMA_UNROLL)
    pltpu.make_async_copy(xs_ref.at[pl.ds(0, n), :], xs_ref.at[pl.ds(0, n), :], sem).wait()


def _dispatch(f, dest, pad_lo, pad_n, n_used, rows):
    n, d = f.shape
    tile = DISPATCH_TILE
    assert n % tile == 0
    return pl.pallas_call(
        _dispatch_kernel,
        out_shape=jax.ShapeDtypeStruct((rows, d), f.dtype),
        grid_spec=pltpu.PrefetchScalarGridSpec(
            num_scalar_prefetch=3,
            grid=(n // tile,),
            in_specs=[pl.BlockSpec((tile * TOP_K,), lambda i, lo, cnt, nu: (i,), memory_space=pltpu.SMEM),
                      pl.BlockSpec((tile, d), lambda i, lo, cnt, nu: (i, 0))],
            out_specs=pl.BlockSpec(memory_space=pl.ANY),
            scratch_shapes=[pltpu.VMEM((MOE_ROWS, d), f.dtype), pltpu.SemaphoreType.DMA(()),
                            pltpu.SemaphoreType.DMA(())]),
        compiler_params=_cparams(("arbitrary",), has_side_effects=True, disable_bounds_checks=True),
        name="moe_dispatch",
    )(pad_lo, pad_n, n_used, dest, f)


def _experts_kernel(be_ref, nu_ref, grp_ref, nxt_ref, xs_ref, wgu_hbm, bgu_ref, wdn_hbm, bdn_ref, y_ref,
                    wgu_f32, wdn_f32, wgu_bf, wdn_bf, wsem, *, layer):
    i = pl.program_id(0)

    def fetch(e, slot, wait):
        for src, dst in ((wgu_hbm.at[layer, e], wgu_f32.at[slot]), (wdn_hbm.at[layer, e], wdn_f32.at[slot])):
            c = pltpu.make_async_copy(src, dst, wsem.at[slot])
            c.wait() if wait else c.start()

    @pl.when(i == 0)
    def _():
        fetch(be_ref[0], 0, False)

    @pl.when((i < nu_ref[0]) & ((i == 0) | (grp_ref[i] != grp_ref[jnp.maximum(i - 1, 0)])))
    def _():
        slot = grp_ref[i] % 2
        fetch(be_ref[i], slot, True)
        wgu_bf[...] = wgu_f32[slot].astype(BF16)
        wdn_bf[...] = wdn_f32[slot].astype(BF16)

        @pl.when(nxt_ref[i] >= 0)
        def _():
            fetch(nxt_ref[i], 1 - slot, False)

    @pl.when(i < nu_ref[0])
    def _():
        h = _dot(_unpack_bf16_pairs(xs_ref[...]).astype(BF16), wgu_bf[...]) + bgu_ref[...]
        ff = h.shape[1] // 2
        hg = jnp.minimum(h[:, :ff], SWIGLU_LIMIT)
        hu = jnp.clip(h[:, ff:], -SWIGLU_LIMIT, SWIGLU_LIMIT)
        act = hg * jax.nn.sigmoid(SWIGLU_ALPHA * hg) * (hu + 1.0)
        y_ref[...] = _pack_bf16_pairs(_dot(act.astype(BF16), wdn_bf[...]) + bdn_ref[...])

    @pl.when(i >= nu_ref[0])
    def _():
        y_ref[...] = jnp.zeros(y_ref.shape, y_ref.dtype)


def _experts(xs, block_e, n_used, layer, wgu, bgu, wdn, bdn):
    rows, dp = xs.shape
    d = 2 * dp
    ff2 = wgu.shape[-1]
    nb = rows // MOE_ROWS
    blk = jnp.arange(nb)
    used = blk < n_used[0]
    first = used & ((blk == 0) | (block_e != jnp.roll(block_e, 1)))
    grp = jnp.cumsum(first.astype(I32)) - 1
    later_first = jnp.where(first[None, :] & (blk[None, :] > blk[:, None]), blk[None, :], nb)
    nxt_blk = jnp.min(later_first, axis=1)
    nxt = jnp.where(nxt_blk < nb, block_e[jnp.minimum(nxt_blk, nb - 1)], -1)
    bias = lambda i, be, nu, g, nx: (layer, be[i], 0, 0)
    return pl.pallas_call(
        functools.partial(_experts_kernel, layer=layer),
        out_shape=jax.ShapeDtypeStruct((rows, dp), U32),
        grid_spec=pltpu.PrefetchScalarGridSpec(
            num_scalar_prefetch=4,
            grid=(nb,),
            in_specs=[pl.BlockSpec((MOE_ROWS, dp),
                                   lambda i, be, nu, g, nx: (jnp.maximum(jnp.minimum(i, nu[0] - 1), 0), 0)),
                      pl.BlockSpec(memory_space=pl.ANY),
                      pl.BlockSpec((None, None, 1, ff2), bias),
                      pl.BlockSpec(memory_space=pl.ANY),
                      pl.BlockSpec((None, None, 1, d), bias)],
            out_specs=pl.BlockSpec((MOE_ROWS, dp), lambda i, be, nu, g, nx: (i, 0)),
            scratch_shapes=[pltpu.VMEM((2, d, ff2), F32), pltpu.VMEM((2, ff2 // 2, d), F32),
                            pltpu.VMEM((d, ff2), BF16), pltpu.VMEM((ff2 // 2, d), BF16),
                            pltpu.SemaphoreType.DMA((2,))]),
        compiler_params=_cparams(("arbitrary",), has_side_effects=True),
        name="moe_experts",
    )(block_e, n_used, grp.astype(I32), nxt.astype(I32), xs, wgu, bgu, wdn, bdn)


def _combine_kernel(dest_ref, y_ref, gate_ref, x_ref, mod_ref, o_ref, buf_ref, sem):
    n = dest_ref.shape[0]
    tm = n // TOP_K

    def issue(t, c):
        for k in range(TOP_K):
            pltpu.make_async_copy(_row(y_ref, dest_ref[t * TOP_K + k]), _row(buf_ref.at[k], t), sem).start()
        return c

    lax.fori_loop(0, tm, issue, 0, unroll=DMA_UNROLL)
    for k in range(TOP_K):
        pltpu.make_async_copy(y_ref.at[pl.ds(0, tm), :], buf_ref.at[k], sem).wait()

    gates = gate_ref[...]
    acc = gates[:, 0:1] * _unpack_bf16_pairs(buf_ref[0])
    for k in range(1, TOP_K):
        acc = acc + gates[:, k:k + 1] * _unpack_bf16_pairs(buf_ref[k])
    o_ref[...] = x_ref[...] + mod_ref[5:6, :] * acc


def _combine(y, dest, gates, xs, mods, t_off):
    b, t_out, d = xs.shape
    tile = COMBINE_TILE_LATENT if (t_off > 0 and t_out % COMBINE_TILE_LATENT == 0) else TOK_TILE
    nt = t_out // tile
    per = tile * TOP_K
    row = functools.partial(_mod_row, ctx_row=b)
    tok = lambda bi, ti: (bi, ti, 0)
    return pl.pallas_call(
        _combine_kernel,
        out_shape=jax.ShapeDtypeStruct((b, t_out, d), F32),
        grid=(b, nt),
        in_specs=[pl.BlockSpec((per,), lambda bi, ti: (bi * nt + ti,), memory_space=pltpu.SMEM),
                  pl.BlockSpec(memory_space=pl.ANY),
                  pl.BlockSpec((None, tile, LANES), tok),
                  pl.BlockSpec((None, tile, d), tok),
                  pl.BlockSpec((None, 6, d), lambda bi, ti: (row(bi, ti + t_off), 0, 0))],
        out_specs=pl.BlockSpec((None, tile, d), tok),
        scratch_shapes=[pltpu.VMEM((TOP_K, tile, y.shape[1]), y.dtype), pltpu.SemaphoreType.DMA(())],
        compiler_params=_cparams(("arbitrary", "arbitrary"), disable_bounds_checks=True),
        name="moe_combine",
    )(dest, y, gates, xs, mods)


def _routing(e_sel, rank, counts, n):
    padded = (counts + MOE_ROWS - 1) // MOE_ROWS * MOE_ROWS
    pend = jnp.cumsum(padded)
    pstart = pend - padded
    dest = jnp.sum(jnp.where(e_sel[..., None] == jnp.arange(N_EXPERTS), pstart, 0), axis=-1) + rank
    n_blocks = n * TOP_K // MOE_ROWS + N_EXPERTS
    first_row = jnp.arange(n_blocks) * MOE_ROWS
    block_e = jnp.minimum(jnp.sum(pend[None, :] <= first_row[:, None], axis=1), N_EXPERTS - 1)
    n_used = (pend[-1] // MOE_ROWS).reshape(1)
    pads = ((pstart + counts).astype(I32), (padded - counts).astype(I32))
    return dest.reshape(-1).astype(I32), block_e.astype(I32), n_used.astype(I32), pads, n_blocks * MOE_ROWS


def _moe(f, route, gates, seen, x_mid, mods, t_off, layer, wgu, bgu, wdn, bdn):
    b, t_out, d = f.shape
    n = b * t_out
    route = route.reshape(n, LANES)
    counts = seen[0, :N_EXPERTS].astype(I32)
    dest, block_e, n_used, pads, rows = _routing(route[:, :TOP_K], route[:, TOP_K:2 * TOP_K], counts, n)
    xs = _dispatch(f.reshape(n, d), dest, *pads, n_used, rows)
    y = _experts(xs, block_e, n_used, layer, wgu, bgu, wdn, bdn)
    return _combine(y, dest, gates, x_mid, mods, t_off)


def _axial_angles(n_rows, rot_dim):
    row = np.repeat(np.arange(n_rows, dtype=np.float32), GRID_W)
    col = np.tile(np.arange(GRID_W, dtype=np.float32), n_rows)
    n = rot_dim // 4
    freqs = (np.float32(ROPE_THETA) ** (-np.arange(n, dtype=np.float32) / np.float32(n))).astype(np.float32)
    return np.concatenate([row[:, None] * freqs, col[:, None] * freqs], axis=-1).astype(np.float32)


def _rope_tables(ctx_len, s_len, rot_dim, lane_base, reps):
    ang = _axial_angles(s_len // GRID_W, rot_dim)
    half = rot_dim // 2
    period = LANES // reps
    cos = np.ones((ctx_len + s_len, period), np.float32)
    slo = np.zeros((ctx_len + s_len, period), np.float32)
    shi = np.zeros((ctx_len + s_len, period), np.float32)
    cos[ctx_len:, lane_base:lane_base + rot_dim] = np.tile(np.cos(ang), (1, 2))
    slo[ctx_len:, lane_base:lane_base + half] = -np.sin(ang)
    shi[ctx_len:, lane_base + half:lane_base + rot_dim] = np.sin(ang)
    return [jnp.asarray(np.tile(tb, (1, reps))) for tb in (cos, slo, shi)]


def _segment_mean_matrix(width, period, segs):
    lane = np.arange(width)
    seg_id = np.full((width,), -1)
    seg_w = np.zeros((width,), np.float32)
    for i, (start, length) in enumerate(segs):
        inside = ((lane % period) >= start) & ((lane % period) < start + length)
        seg_id = np.where(inside, (lane // period) * len(segs) + i, seg_id)
        seg_w = np.where(inside, np.float32(1.0 / length), seg_w)
    same = (seg_id[:, None] == seg_id[None, :]) & (seg_id[:, None] >= 0)
    return jnp.asarray(np.where(same, seg_w[None, :], np.float32(0.0)), dtype=BF16)


def _block_diag(w):
    n, c, _ = w.shape
    eye = jnp.eye(n, dtype=w.dtype)
    return (eye[:, None, :, None] * w[:, :, None, :]).reshape(n * c, n * c)


def kernel(x, c, ctx, c_ctx, w_mod, b_mod, norm_mix, norm_ffn, w_in_even, lru_conv_w, lru_conv_b, lru_w_r, lru_b_r, lru_w_i, lru_b_i, lru_lambda, mla_q_a_norm, mla_w_q_b, mla_kv_a_norm, mla_w_kv_b, mla_nope_norm, mla_rope_norm, w_out_even, w_qkv_odd, gqa_qk_norm, gqa_sink, w_out_odd, w_router, b_router, w_gate_up, b_gate_up, w_down, b_down):
    b, s_len, d = x.shape
    ctx_len = ctx.shape[1]
    depth = w_mod.shape[0]
    assert depth == 2 and ctx_len == TOK_TILE and s_len % TOK_TILE == 0 and b + 1 <= SUBLANES
    lru_w = lru_conv_w.shape[-1]
    q_lora = mla_q_a_norm.shape[-1]
    kv_lora = mla_kv_a_norm.shape[-1]

    cvec = jnp.concatenate([c, c_ctx[None], jnp.zeros((SUBLANES - b - 1, d), F32)], axis=0)
    mods = _modulation(cvec, w_mod, b_mod).reshape(depth, SUBLANES, 6, d)

    wr_pad = jnp.zeros((depth, d, LANES), F32).at[:, :, :N_EXPERTS].set(w_router)
    br_pad = jnp.full((depth, 1, LANES), NEG_INF, F32).at[:, 0, :N_EXPERTS].set(b_router)
    experts = (w_gate_up, b_gate_up[:, :, None, :], w_down, b_down[:, :, None, :])

    n_in = w_in_even.shape[-1]
    n_in_pad = -(-n_in // LANES) * LANES
    w_in = jnp.zeros((d, n_in_pad), F32).at[:, :n_in].set(w_in_even[0]).astype(BF16)
    xa, ga, mla_in = _even_in(ctx, x, mods[0], norm_mix[0][None], w_in, lru_w)

    nh = lru_w // (2 * LANES)
    per = LRU_BLOCKS // nh
    blk = lru_w // LRU_BLOCKS
    w_gates = jnp.stack([
        jnp.stack([jnp.concatenate([_block_diag(lru_w_r[0, dd, h * per:(h + 1) * per]),
                                    _block_diag(lru_w_i[0, dd, h * per:(h + 1) * per])], axis=1)
                   for h in range(nh)]) for dd in range(2)]).astype(BF16)
    assert blk * per == 2 * LANES
    ya = _lru(xa, ga, lru_conv_w[0], lru_conv_b[0][:, None, :], w_gates, lru_b_r[0][:, None, :],
              lru_b_i[0][:, None, :], jax.nn.softplus(-lru_lambda[0])[:, None, :], ctx_len)

    qk = MLA_NOPE + MLA_ROPE
    hq = MLA_HEADS * LANES
    wq = jnp.zeros((q_lora, MLA_HEADS, LANES), F32).at[:, :, :qk].set(
        mla_w_q_b[0].reshape(q_lora, MLA_HEADS, qk)).reshape(q_lora, hq).astype(BF16)
    wkv = mla_w_kv_b[0].reshape(kv_lora, MLA_HEADS, MLA_NOPE + MLA_V)
    wk = jnp.zeros((kv_lora, MLA_HEADS, LANES), F32).at[:, :, :MLA_NOPE].set(
        wkv[:, :, :MLA_NOPE]).reshape(kv_lora, hq).astype(BF16)
    wv = jnp.zeros((kv_lora, MLA_HEADS, LANES), F32).at[:, :, :MLA_V].set(
        wkv[:, :, MLA_NOPE:]).reshape(kv_lora, hq).astype(BF16)
    one = jnp.asarray(np.tile(np.arange(LANES) == MLA_V, MLA_HEADS)[None], dtype=F32)
    zpad = jnp.zeros((LANES - qk,), F32)
    gq = jnp.tile(jnp.concatenate([mla_nope_norm[0, 0], mla_rope_norm[0, 0], zpad]), MLA_HEADS)[None]
    gk = jnp.tile(jnp.concatenate([mla_nope_norm[0, 1], jnp.zeros((LANES - MLA_NOPE,), F32)]), MLA_HEADS)[None]
    gkr = jnp.concatenate([mla_rope_norm[0, 1], jnp.zeros((LANES - MLA_ROPE,), F32)])[None]
    cos, slo, shi = _rope_tables(ctx_len, s_len, MLA_ROPE, MLA_NOPE, 1)
    mla_p = dict(qan=mla_q_a_norm[0][None], kvn=mla_kv_a_norm[0][None], wq=wq, wk=wk, wv=wv,
                 mq=_segment_mean_matrix(MXU_TILE,LANES, [(0, MLA_NOPE), (MLA_NOPE, MLA_ROPE)]),
                 mk=_segment_mean_matrix(MXU_TILE,LANES, [(0, MLA_NOPE)]),
                 gq=gq, gk=gk, gkr=gkr, one=one, cos=cos, slo=slo, shi=shi)
    q, k, v = _mla_prep(mla_in, mla_p, qk ** -0.5 * math.log2(math.e))
    yb = _mla_attn(q, k, v, ctx_len)

    x_mid, f, route, gates, seen = _post_mix(ya, yb, 0, (ctx, x), mods[0], norm_ffn[0][None], w_out_even[0].astype(BF16),
                                             wr_pad[0], br_pad[0], 0, 0)
    xs = _moe(f, route, gates, seen, x_mid, mods[0], 0, 0, *experts)

    nq = GQA_HEADS * GQA_DIM
    nkv = GQA_KV_HEADS * GQA_DIM
    cos, slo, shi = _rope_tables(ctx_len, s_len, GQA_DIM, 0, LANES // GQA_DIM)
    odd_p = dict(w=w_qkv_odd[0].astype(BF16),
                 mq=_segment_mean_matrix(MXU_TILE,GQA_DIM, [(0, GQA_DIM)]),
                 mk=_segment_mean_matrix(MXU_TILE,GQA_DIM, [(0, GQA_DIM)]),
                 gq=jnp.tile(gqa_qk_norm[0, 0], GQA_HEADS)[None], gk=jnp.tile(gqa_qk_norm[0, 1], GQA_KV_HEADS)[None],
                 cos=cos, slo=slo, shi=shi)
    q, k, v = _odd_in(xs, mods[1], norm_mix[1][None], odd_p, GQA_DIM ** -0.5)
    o = _win_attn(q, k, v, gqa_sink[0], ctx_len)

    t_off = ctx_len // TOK_TILE
    x_mid, f, route, gates, seen = _post_mix(o, o, 1, xs, mods[1], norm_ffn[1][None], w_out_odd[0].astype(BF16),
                                             wr_pad[1], br_pad[1], t_off, 0)
    return _moe(f, route, gates, seen, x_mid, mods[1], t_off, 1, *experts)
```
